```python
import math
import jax, jax.numpy as jnp
from jax import lax
import numpy as np

D_MODEL = 4096
BATCH = 8
SEQ = 4096
DEPTH = 2

N_MIXERS = 2
N_LAYERS_A = (DEPTH + 1) // 2
N_LAYERS_B = DEPTH // 2

A_DK = 128
A_HEADS = D_MODEL // A_DK
A_DV = D_MODEL // A_HEADS
A_KEY_WIDTH = A_HEADS * A_DK
A_VAL_WIDTH = A_HEADS * A_DV
A_IN_WIDTH = 2 * A_KEY_WIDTH + 2 * A_VAL_WIDTH

B_DK = 128
B_DV = 128
B_HEADS = D_MODEL // B_DV
B_KEY_WIDTH = B_HEADS * B_DK
B_VAL_WIDTH = B_HEADS * B_DV
B_CONV_WIDTH = 4
B_CONV_CH = 2 * B_KEY_WIDTH + B_VAL_WIDTH
B_IN_WIDTH = B_CONV_CH + B_VAL_WIDTH + 2 * B_HEADS

CHUNK = 64
DEEPNORM_ALPHA = (2.0 * DEPTH) ** 0.25
DEEPNORM_BETA = (8.0 * DEPTH) ** -0.25
LN_EPS = 1e-5
RMS_EPS = 1e-6
L2_EPS = 1e-6

kernel_name = "hybrid_hgrn2_gated_deltanet_deepnorm"


def layer_norm(x, g, b):
    xf = x.astype(jnp.float32)
    mu = jnp.mean(xf, axis=-1, keepdims=True)
    var = jnp.mean(jnp.square(xf - mu), axis=-1, keepdims=True)
    return ((xf - mu) * lax.rsqrt(var + LN_EPS) * g.astype(jnp.float32) + b.astype(jnp.float32)).astype(x.dtype)


def gated_rmsnorm(o, z, w):
    b, s, h, dv = o.shape
    on = o * lax.rsqrt(jnp.mean(o * o, axis=-1, keepdims=True) + RMS_EPS) * w.astype(jnp.float32)
    return on.reshape(b, s, h * dv) * jax.nn.silu(z.astype(jnp.float32))


def to_chunks(x):
    b, s, h, d = x.shape
    return x.reshape(b, s // CHUNK, CHUNK, h, d).transpose(1, 0, 3, 2, 4)


def from_chunks(y):
    n, b, h, c, d = y.shape
    return y.transpose(1, 0, 3, 2, 4).reshape(b, n * c, h, d)


def causal_depthwise_conv(x, w):
    k = w.shape[0]
    return lax.conv_general_dilated(x, w[:, None, :], window_strides=(1,), padding=[(k - 1, 0)],
                                    dimension_numbers=("NWC", "WIO", "NWC"),
                                    feature_group_count=x.shape[-1])


def hgrn2_chunk_scan(q, k, v, log_f):
    n, b, h, c, dk = q.shape
    dv = v.shape[-1]
    causal = jnp.tril(jnp.ones((c, c), dtype=bool))

    def step(state, xs):
        q_c, k_c, v_c, g_c = xs
        cum = jnp.cumsum(g_c, axis=-2)
        diff = cum[..., :, None, :] - cum[..., None, :, :]
        decay = jnp.exp(jnp.where(causal[:, :, None], diff, -jnp.inf))
        attn = jnp.einsum("bhtd,bhsd,bhtsd->bhts", q_c, k_c, decay)
        o = (jnp.einsum("bhts,bhsv->bhtv", attn, v_c)
             + jnp.einsum("bhtd,bhdv->bhtv", q_c * jnp.exp(cum), state))
        last = cum[..., -1:, :]
        state = (jnp.exp(last[..., 0, :])[..., None] * state
                 + jnp.einsum("bhsd,bhsv->bhdv", k_c * jnp.exp(last - cum), v_c))
        return state, o

    state0 = jnp.zeros((b, h, dk, dv), jnp.float32)
    _, o = lax.scan(step, state0, (q, k, v, log_f))
    return o


def hgrn2_mixer(x, w_in, lb, norm_w, w_out):
    b, s, _ = x.shape
    hproj = x @ w_in
    q, f, i, z = jnp.split(hproj, [A_KEY_WIDTH, 2 * A_KEY_WIDTH, 2 * A_KEY_WIDTH + A_VAL_WIDTH], axis=-1)
    q = jax.nn.silu(q.astype(jnp.float32))
    forget = lb + (1.0 - lb) * jax.nn.sigmoid(f.astype(jnp.float32))
    k = 1.0 - forget
    log_f = jnp.log(forget)
    heads_k = lambda t: to_chunks(t.reshape(b, s, A_HEADS, A_DK))
    o = hgrn2_chunk_scan(heads_k(q), heads_k(k),
                         to_chunks(i.astype(jnp.float32).reshape(b, s, A_HEADS, A_DV)),
                         heads_k(log_f))
    y = gated_rmsnorm(from_chunks(o), z, norm_w)
    return y.astype(x.dtype) @ w_out


def gated_delta_chunk(q, k, v, beta, g):
    c = q.shape[-2]
    dv = v.shape[-1]
    strict = jnp.tril(jnp.ones((c, c), dtype=bool), -1)
    incl = jnp.tril(jnp.ones((c, c), dtype=bool))
    cum = jnp.cumsum(g, axis=-1)
    diff = cum[..., :, None] - cum[..., None, :]
    kk = jnp.einsum("nbhtd,nbhsd->nbhts", k, k)
    lower = beta[..., :, None] * kk * jnp.exp(jnp.where(strict, diff, -jnp.inf))
    eye = jnp.eye(c, dtype=jnp.float32)
    rhs = jnp.concatenate([beta[..., None] * v, (beta * jnp.exp(cum))[..., None] * k], axis=-1)
    sol = lax.linalg.triangular_solve(eye + lower, rhs, left_side=True, lower=True, unit_diagonal=True)
    u0, w = sol[..., :dv], sol[..., dv:]
    qk = jnp.einsum("nbhtd,nbhsd->nbhts", q, k) * jnp.exp(jnp.where(incl, diff, -jnp.inf))
    q_decay = q * jnp.exp(cum)[..., None]
    k_decay = k * jnp.exp(cum[..., -1:] - cum)[..., None]
    last_decay = jnp.exp(cum[..., -1])

    def step(state, xs):
        u0_c, w_c, qk_c, qd_c, kd_c, ld_c = xs
        u = u0_c - jnp.einsum("bhcd,bhdv->bhcv", w_c, state)
        o = jnp.einsum("bhcd,bhdv->bhcv", qd_c, state) + jnp.einsum("bhts,bhsv->bhtv", qk_c, u)
        state = ld_c[..., None, None] * state + jnp.einsum("bhcd,bhcv->bhdv", kd_c, u)
        return state, o

    n, b, h, _, dk = q.shape
    state0 = jnp.zeros((b, h, dk, dv), jnp.float32)
    _, o = lax.scan(step, state0, (u0, w, qk, q_decay, k_decay, last_decay))
    return o


def l2_normalize(t):
    return t * lax.rsqrt(jnp.sum(t * t, axis=-1, keepdims=True) + L2_EPS)


def gated_deltanet_mixer(x, w_in, conv_w, a_log, dt_bias, norm_w, w_out):
    b, s, _ = x.shape
    hproj = x @ w_in
    qkv, z, beta_logit, a = jnp.split(
        hproj, [B_CONV_CH, B_CONV_CH + B_VAL_WIDTH, B_CONV_CH + B_VAL_WIDTH + B_HEADS], axis=-1)
    qkv = jax.nn.silu(causal_depthwise_conv(qkv, conv_w).astype(jnp.float32))
    q, k, v = jnp.split(qkv, [B_KEY_WIDTH, 2 * B_KEY_WIDTH], axis=-1)
    q = l2_normalize(q.reshape(b, s, B_HEADS, B_DK)) * (B_DK ** -0.5)
    k = l2_normalize(k.reshape(b, s, B_HEADS, B_DK))
    v = v.reshape(b, s, B_HEADS, B_DV)
    beta = jax.nn.sigmoid(beta_logit.astype(jnp.float32))
    g = -jnp.exp(a_log.astype(jnp.float32)) * jax.nn.softplus(
        a.astype(jnp.float32) + dt_bias.astype(jnp.float32))
    o = gated_delta_chunk(to_chunks(q), to_chunks(k), to_chunks(v),
                          to_chunks(beta[..., None])[..., 0], to_chunks(g[..., None])[..., 0])
    y = gated_rmsnorm(from_chunks(o), z, norm_w)
    return y.astype(x.dtype) @ w_out


def _fwd_setup_inputs(seed: int = 0) -> dict:
    key = jax.random.key(seed)
    ks = jax.random.split(key, 16)
    f32 = jnp.float32
    D = D_MODEL
    x = jax.random.normal(ks[0], (BATCH, SEQ, D), f32)
    a_w_in = jax.random.normal(ks[1], (N_LAYERS_A, D, A_IN_WIDTH), f32) * (D ** -0.5)
    a_lower_bounds = jax.random.normal(ks[2], (DEPTH + 1, A_KEY_WIDTH), f32) * 0.5
    a_norm_w = 1.0 + 0.02 * jax.random.normal(ks[3], (N_LAYERS_A, A_DV), f32)
    a_w_out = jax.random.normal(ks[4], (N_LAYERS_A, A_VAL_WIDTH, D), f32) * (A_VAL_WIDTH ** -0.5) * DEEPNORM_BETA
    b_w_in = jax.random.normal(ks[5], (N_LAYERS_B, D, B_IN_WIDTH), f32) * (D ** -0.5)
    b_conv_w = jax.random.normal(ks[6], (N_LAYERS_B, B_CONV_WIDTH, B_CONV_CH), f32) * (B_CONV_WIDTH ** -0.5)
    b_a_log = jnp.log(jax.random.uniform(ks[7], (N_LAYERS_B, B_HEADS), f32, 1.0, 16.0))
    dt = jnp.exp(jax.random.uniform(ks[8], (N_LAYERS_B, B_HEADS), f32, math.log(1e-3), math.log(1e-1)))
    b_dt_bias = dt + jnp.log(-jnp.expm1(-dt))
    b_norm_w = 1.0 + 0.02 * jax.random.normal(ks[9], (N_LAYERS_B, B_DV), f32)
    b_w_out = jax.random.normal(ks[10], (N_LAYERS_B, B_VAL_WIDTH, D), f32) * (B_VAL_WIDTH ** -0.5) * DEEPNORM_BETA
    ln_g = 1.0 + 0.02 * jax.random.normal(ks[11], (DEPTH, D), f32)
    ln_b = 0.02 * jax.random.normal(ks[12], (DEPTH, D), f32)
    return {"x": x, "a_w_in": a_w_in, "a_lower_bounds": a_lower_bounds, "a_norm_w": a_norm_w,
            "a_w_out": a_w_out, "b_w_in": b_w_in, "b_conv_w": b_conv_w, "b_a_log": b_a_log,
            "b_dt_bias": b_dt_bias, "b_norm_w": b_norm_w, "b_w_out": b_w_out,
            "ln_g": ln_g, "ln_b": ln_b}


def _fwd_reference(x, a_w_in, a_lower_bounds, a_norm_w, a_w_out, b_w_in, b_conv_w, b_a_log,
              b_dt_bias, b_norm_w, b_w_out, ln_g, ln_b):
    lb_all = jnp.cumsum(jax.nn.softmax(a_lower_bounds.astype(jnp.float32), axis=0), axis=0)
    for i in range(DEPTH):
        j = i // N_MIXERS
        if i % N_MIXERS == 0:
            y = hgrn2_mixer(x, a_w_in[j], lb_all[i], a_norm_w[j], a_w_out[j])
        else:
            y = gated_deltanet_mixer(x, b_w_in[j], b_conv_w[j], b_a_log[j], b_dt_bias[j],
                                     b_norm_w[j], b_w_out[j])
        x = layer_norm(DEEPNORM_ALPHA * x + y, ln_g[i], ln_b[i])
    return x


import jax as _jax
import jax.numpy as _jnp

TWIN_FORMAT = 'train_step'
FWD_PARAMS = ['x', 'a_w_in', 'a_lower_bounds', 'a_norm_w', 'a_w_out', 'b_w_in', 'b_conv_w', 'b_a_log', 'b_dt_bias', 'b_norm_w', 'b_w_out', 'ln_g', 'ln_b']
TWIN_WEIGHTS = ['a_w_in', 'a_lower_bounds', 'a_norm_w', 'a_w_out', 'b_w_in', 'b_conv_w', 'b_a_log', 'b_dt_bias', 'b_norm_w', 'b_w_out', 'ln_g', 'ln_b']
TWIN_DIFF_INPUT = 'x'
TWIN_INPUTS = ['x', 'a_w_in', 'a_lower_bounds', 'a_norm_w', 'a_w_out', 'b_w_in', 'b_conv_w', 'b_a_log', 'b_dt_bias', 'b_norm_w', 'b_w_out', 'ln_g', 'ln_b', 'loss_target', 'm_a_w_in', 'm_a_lower_bounds', 'm_a_norm_w', 'm_a_w_out', 'm_b_w_in', 'm_b_conv_w', 'm_b_a_log', 'm_b_dt_bias', 'm_b_norm_w', 'm_b_w_out', 'm_ln_g', 'm_ln_b', 'v_a_w_in', 'v_a_lower_bounds', 'v_a_norm_w', 'v_a_w_out', 'v_b_w_in', 'v_b_conv_w', 'v_b_a_log', 'v_b_dt_bias', 'v_b_norm_w', 'v_b_w_out', 'v_ln_g', 'v_ln_b']
TWIN_OUTPUTS = ['loss', 'grad_x', 'grad_a_w_in', 'grad_a_lower_bounds', 'grad_a_norm_w', 'grad_a_w_out', 'grad_b_w_in', 'grad_b_conv_w', 'grad_b_a_log', 'grad_b_dt_bias', 'grad_b_norm_w', 'grad_b_w_out', 'grad_ln_g', 'grad_ln_b', 'delta_a_w_in', 'delta_a_lower_bounds', 'delta_a_norm_w', 'delta_a_w_out', 'delta_b_w_in', 'delta_b_conv_w', 'delta_b_a_log', 'delta_b_dt_bias', 'delta_b_norm_w', 'delta_b_w_out', 'delta_ln_g', 'delta_ln_b', 'new_m_a_w_in', 'new_m_a_lower_bounds', 'new_m_a_norm_w', 'new_m_a_w_out', 'new_m_b_w_in', 'new_m_b_conv_w', 'new_m_b_a_log', 'new_m_b_dt_bias', 'new_m_b_norm_w', 'new_m_b_w_out', 'new_m_ln_g', 'new_m_ln_b', 'new_v_a_w_in', 'new_v_a_lower_bounds', 'new_v_a_norm_w', 'new_v_a_w_out', 'new_v_b_w_in', 'new_v_b_conv_w', 'new_v_b_a_log', 'new_v_b_dt_bias', 'new_v_b_norm_w', 'new_v_b_w_out', 'new_v_ln_g', 'new_v_ln_b']
TWIN_LEAF_KINDS = {'loss': 'loss', 'grad_x': 'grad_x', 'grad_a_w_in': 'grad_w', 'grad_a_lower_bounds': 'grad_w', 'grad_a_norm_w': 'grad_w', 'grad_a_w_out': 'grad_w', 'grad_b_w_in': 'grad_w', 'grad_b_conv_w': 'grad_w', 'grad_b_a_log': 'grad_w', 'grad_b_dt_bias': 'grad_w', 'grad_b_norm_w': 'grad_w', 'grad_b_w_out': 'grad_w', 'grad_ln_g': 'grad_w', 'grad_ln_b': 'grad_w', 'delta_a_w_in': 'delta_w', 'delta_a_lower_bounds': 'delta_w', 'delta_a_norm_w': 'delta_w', 'delta_a_w_out': 'delta_w', 'delta_b_w_in': 'delta_w', 'delta_b_conv_w': 'delta_w', 'delta_b_a_log': 'delta_w', 'delta_b_dt_bias': 'delta_w', 'delta_b_norm_w': 'delta_w', 'delta_b_w_out': 'delta_w', 'delta_ln_g': 'delta_w', 'delta_ln_b': 'delta_w', 'new_m_a_w_in': 'new_m', 'new_m_a_lower_bounds': 'new_m', 'new_m_a_norm_w': 'new_m', 'new_m_a_w_out': 'new_m', 'new_m_b_w_in': 'new_m', 'new_m_b_conv_w': 'new_m', 'new_m_b_a_log': 'new_m', 'new_m_b_dt_bias': 'new_m', 'new_m_b_norm_w': 'new_m', 'new_m_b_w_out': 'new_m', 'new_m_ln_g': 'new_m', 'new_m_ln_b': 'new_m', 'new_v_a_w_in': 'new_v', 'new_v_a_lower_bounds': 'new_v', 'new_v_a_norm_w': 'new_v', 'new_v_a_w_out': 'new_v', 'new_v_b_w_in': 'new_v', 'new_v_b_conv_w': 'new_v', 'new_v_b_a_log': 'new_v', 'new_v_b_dt_bias': 'new_v', 'new_v_b_norm_w': 'new_v', 'new_v_b_w_out': 'new_v', 'new_v_ln_g': 'new_v', 'new_v_ln_b': 'new_v'}


def _forward(args):
    return _fwd_reference(*[args[k] for k in FWD_PARAMS])


def _output_shape():
    out = _jax.eval_shape(lambda: _forward(_fwd_setup_inputs(0)))
    return out.shape, out.dtype

N_MICROBATCH = 1
ADAM_LR = 0.001
ADAM_B1 = 0.9
ADAM_B2 = 0.999
ADAM_EPS = 1e-08
ADAM_WD = 0.01
ADAM_STEP = 10
PER_EXAMPLE_BATCH_AXIS = {'x': 0, 'loss_target': 0}
SHARED_INPUTS = []
_WEIGHT_DTYPES = {'a_w_in': _jnp.float32, 'a_lower_bounds': _jnp.float32, 'a_norm_w': _jnp.float32, 'a_w_out': _jnp.float32, 'b_w_in': _jnp.float32, 'b_conv_w': _jnp.float32, 'b_a_log': _jnp.float32, 'b_dt_bias': _jnp.float32, 'b_norm_w': _jnp.float32, 'b_w_out': _jnp.float32, 'ln_g': _jnp.float32, 'ln_b': _jnp.float32}
MOMENT_SCALE = {'a_w_in': 6.867744e-03, 'a_lower_bounds': 4.266213e-04, 'a_norm_w': 7.400672e-02, 'a_w_out': 1.894989e-02, 'b_w_in': 7.467798e-03, 'b_conv_w': 7.029525e-03, 'b_a_log': 4.294821e-02, 'b_dt_bias': 4.276117e-02, 'b_norm_w': 5.313196e-02, 'b_w_out': 1.962824e-02, 'ln_g': 5.650631e+00, 'ln_b': 2.581402e-01}


def _to_microbatches(a, axis):
    t = _jnp.moveaxis(a, axis, 0)
    t = t.reshape((N_MICROBATCH, t.shape[0] // N_MICROBATCH) + t.shape[1:])
    return _jnp.moveaxis(t, 1, axis + 1)


def setup_inputs(seed: int = 0) -> dict:
    inp = _fwd_setup_inputs(seed)
    key = _jax.random.fold_in(_jax.random.key(seed), 7919)
    shape, _ = _output_shape()
    out = dict(inp)
    out["loss_target"] = _jax.random.normal(_jax.random.fold_in(key, 0), shape, _jnp.float32)
    for i, name in enumerate(TWIN_WEIGHTS):
        w = inp[name].astype(_jnp.float32)
        if MOMENT_SCALE is None:
            s = _jnp.sqrt(_jnp.mean(_jnp.square(w)) + 1e-30)
        else:
            s = MOMENT_SCALE[name]
        km, kv = _jax.random.split(_jax.random.fold_in(key, i + 1))
        out[name] = w
        out["m_" + name] = s * _jax.random.normal(km, w.shape, _jnp.float32)
        out["v_" + name] = (s * s) * _jax.random.uniform(kv, w.shape, _jnp.float32, 0.5, 1.5)
    if N_MICROBATCH > 1:
        for name, axis in PER_EXAMPLE_BATCH_AXIS.items():
            out[name] = _to_microbatches(out[name], axis)
    return {'x': out['x'], 'a_w_in': out['a_w_in'], 'a_lower_bounds': out['a_lower_bounds'], 'a_norm_w': out['a_norm_w'], 'a_w_out': out['a_w_out'], 'b_w_in': out['b_w_in'], 'b_conv_w': out['b_conv_w'], 'b_a_log': out['b_a_log'], 'b_dt_bias': out['b_dt_bias'], 'b_norm_w': out['b_norm_w'], 'b_w_out': out['b_w_out'], 'ln_g': out['ln_g'], 'ln_b': out['ln_b'], 'loss_target': out['loss_target'], 'm_a_w_in': out['m_a_w_in'], 'm_a_lower_bounds': out['m_a_lower_bounds'], 'm_a_norm_w': out['m_a_norm_w'], 'm_a_w_out': out['m_a_w_out'], 'm_b_w_in': out['m_b_w_in'], 'm_b_conv_w': out['m_b_conv_w'], 'm_b_a_log': out['m_b_a_log'], 'm_b_dt_bias': out['m_b_dt_bias'], 'm_b_norm_w': out['m_b_norm_w'], 'm_b_w_out': out['m_b_w_out'], 'm_ln_g': out['m_ln_g'], 'm_ln_b': out['m_ln_b'], 'v_a_w_in': out['v_a_w_in'], 'v_a_lower_bounds': out['v_a_lower_bounds'], 'v_a_norm_w': out['v_a_norm_w'], 'v_a_w_out': out['v_a_w_out'], 'v_b_w_in': out['v_b_w_in'], 'v_b_conv_w': out['v_b_conv_w'], 'v_b_a_log': out['v_b_a_log'], 'v_b_dt_bias': out['v_b_dt_bias'], 'v_b_norm_w': out['v_b_norm_w'], 'v_b_w_out': out['v_b_w_out'], 'v_ln_g': out['v_ln_g'], 'v_ln_b': out['v_ln_b']}


def _loss(weights, diff, rest, loss_target):
    with _jax.named_scope("forward"):
        args = {**rest, TWIN_DIFF_INPUT: diff, **{k: w.astype(_WEIGHT_DTYPES[k]) for k, w in weights.items()}}
        y = _forward(args)
    with _jax.named_scope("loss_head"):
        err = _jnp.square(y.astype(_jnp.float32) - loss_target)
        return 0.5 * _jnp.sum(_jnp.mean(err, axis=-1)) if err.ndim else 0.5 * err


def _adamw(w, g, m, v):
    m = ADAM_B1 * m + (1.0 - ADAM_B1) * g
    v = ADAM_B2 * v + (1.0 - ADAM_B2) * _jnp.square(g)
    m_hat = m / (1.0 - ADAM_B1 ** ADAM_STEP)
    v_hat = v / (1.0 - ADAM_B2 ** ADAM_STEP)
    delta = -ADAM_LR * (m_hat / (_jnp.sqrt(v_hat) + ADAM_EPS) + ADAM_WD * w)
    return delta, m, v


def reference(x, a_w_in, a_lower_bounds, a_norm_w, a_w_out, b_w_in, b_conv_w, b_a_log, b_dt_bias, b_norm_w, b_w_out, ln_g, ln_b, loss_target, m_a_w_in, m_a_lower_bounds, m_a_norm_w, m_a_w_out, m_b_w_in, m_b_conv_w, m_b_a_log, m_b_dt_bias, m_b_norm_w, m_b_w_out, m_ln_g, m_ln_b, v_a_w_in, v_a_lower_bounds, v_a_norm_w, v_a_w_out, v_b_w_in, v_b_conv_w, v_b_a_log, v_b_dt_bias, v_b_norm_w, v_b_w_out, v_ln_g, v_ln_b):
    given = dict(x=x, a_w_in=a_w_in, a_lower_bounds=a_lower_bounds, a_norm_w=a_norm_w, a_w_out=a_w_out, b_w_in=b_w_in, b_conv_w=b_conv_w, b_a_log=b_a_log, b_dt_bias=b_dt_bias, b_norm_w=b_norm_w, b_w_out=b_w_out, ln_g=ln_g, ln_b=ln_b, loss_target=loss_target, m_a_w_in=m_a_w_in, m_a_lower_bounds=m_a_lower_bounds, m_a_norm_w=m_a_norm_w, m_a_w_out=m_a_w_out, m_b_w_in=m_b_w_in, m_b_conv_w=m_b_conv_w, m_b_a_log=m_b_a_log, m_b_dt_bias=m_b_dt_bias, m_b_norm_w=m_b_norm_w, m_b_w_out=m_b_w_out, m_ln_g=m_ln_g, m_ln_b=m_ln_b, v_a_w_in=v_a_w_in, v_a_lower_bounds=v_a_lower_bounds, v_a_norm_w=v_a_norm_w, v_a_w_out=v_a_w_out, v_b_w_in=v_b_w_in, v_b_conv_w=v_b_conv_w, v_b_a_log=v_b_a_log, v_b_dt_bias=v_b_dt_bias, v_b_norm_w=v_b_norm_w, v_b_w_out=v_b_w_out, v_ln_g=v_ln_g, v_ln_b=v_ln_b)
    weights = {n: given[n] for n in TWIN_WEIGHTS}
    shared = {n: given[n] for n in SHARED_INPUTS}
    per_example = {n: given[n] for n in ['x']}
    grad_fn = _jax.value_and_grad(_loss, argnums=(0, 1))

    def one_microbatch(ex, loss_target):
        ex = dict(ex)
        diff = ex.pop(TWIN_DIFF_INPUT)
        return grad_fn(weights, diff, {**shared, **ex}, loss_target)

    if N_MICROBATCH == 1:
        loss, (grad_w, grad_x) = one_microbatch(per_example, given["loss_target"])
    else:
        def body(carry, xs):
            loss_sum, grad_sum = carry
            l_k, (gw_k, gx_k) = one_microbatch(xs[0], xs[1])
            with _jax.named_scope("update"):
                return (loss_sum + l_k, _jax.tree.map(_jnp.add, grad_sum, gw_k)), gx_k

        init = (_jnp.zeros((), _jnp.float32), _jax.tree.map(_jnp.zeros_like, weights))
        (loss, grad_w), grad_x = _jax.lax.scan(body, init, (per_example, given["loss_target"]))
    with _jax.named_scope("update"):
        delta_w, new_m, new_v = {}, {}, {}
        for n in TWIN_WEIGHTS:
            delta_w[n], new_m[n], new_v[n] = _adamw(weights[n], grad_w[n], given["m_" + n], given["v_" + n])
    return (loss, grad_x, *[grad_w[n] for n in TWIN_WEIGHTS], *[delta_w[n] for n in TWIN_WEIGHTS],
            *[new_m[n] for n in TWIN_WEIGHTS], *[new_v[n] for n in TWIN_WEIGHTS])
```

```python
import functools
import math

import jax
import jax.numpy as jnp
from jax import lax
from jax.experimental import pallas as pl
from jax.experimental.pallas import tpu as pltpu

F32 = jnp.float32
BF16 = jnp.bfloat16
HIGHEST = lax.Precision.HIGHEST

N_DEV = 8
LANES = 128
CHUNK = 64
SUB = 16
HALO = 8
CONV_K = 4
DEPTH = 2
DEEPNORM_ALPHA = (2.0 * DEPTH) ** 0.25
LN_EPS = 1e-5
RMS_EPS = 1e-6
L2_EPS = 1e-6
EXP_CLAMP = 60.0
ADAM_LR = 0.001
ADAM_B1 = 0.9
ADAM_B2 = 0.999
ADAM_EPS = 1e-08
ADAM_WD = 0.01
ADAM_STEP = 10
VMEM_LIMIT = 56 * 1024 * 1024

NN = ((1,), (0,))
NT = ((1,), (1,))
TN = ((0,), (0,))


def _dg(a, b, dims, precision=None):
    return lax.dot_general(a, b, (dims, ((), ())), precision=precision, preferred_element_type=F32)


def _bdot(a, b, dims):
    return _dg(a.astype(BF16), b.astype(BF16), dims)


def _hdot(a, b, dims):
    return _dg(a, b, dims, HIGHEST)


def _silu(t):
    return t * jax.nn.sigmoid(t)


def _softplus(t):
    return jnp.where(t > 20.0, t, jnp.log1p(jnp.exp(jnp.minimum(t, 20.0))))


def _cparams(sem=None):
    kw = dict(vmem_limit_bytes=VMEM_LIMIT)
    if sem is not None:
        kw["dimension_semantics"] = sem
    return pltpu.CompilerParams(**kw)


def matmul(a, b, *, name, nt=False, addend=None, alpha=1.0, out_dtype=F32, out_split=1, tm=1024, tn=1024, tk=512):
    m, k = a.shape
    b_split = b.shape[0] if b.ndim == 3 else 1
    b_rows, b_cols = b.shape[-2], b.shape[-1] * b_split
    n = b_rows if nt else b_cols
    tm, tn, tk = min(tm, m), min(tn, n), min(tk, k)
    if b_split > 1:
        part = b_cols // b_split
        tn, tk = (tn, min(tk, part)) if nt else (min(tn, part), tk)
    if out_split > 1:
        tn = min(tn, n // out_split)
    assert m % tm == 0 and n % tn == 0 and k % tk == 0, (a.shape, b.shape, nt)
    nk = k // tk
    dims = NT if nt else NN

    def body(*refs):
        if addend is None:
            a_ref, b_ref, o_ref, acc_ref = refs
            add_ref = None
        else:
            a_ref, b_ref, add_ref, o_ref, acc_ref = refs
        kk = pl.program_id(2)

        @pl.when(kk == 0)
        def _():
            acc_ref[...] = jnp.zeros_like(acc_ref)

        acc_ref[...] += _dg(a_ref[...], b_ref[...], dims)

        @pl.when(kk == nk - 1)
        def _():
            r = acc_ref[...]
            if add_ref is not None:
                r = r + alpha * add_ref[...].astype(F32)
            o_ref[...] = r.astype(o_ref.dtype)

    if b_split == 1:
        b_spec = (pl.BlockSpec((tn, tk), lambda i, j, kk: (j, kk)) if nt
                  else pl.BlockSpec((tk, tn), lambda i, j, kk: (kk, j)))
    elif nt:
        per = (b_cols // b_split) // tk
        b_spec = pl.BlockSpec((None, tn, tk), lambda i, j, kk: (kk // per, j, kk % per))
    else:
        per = (b_cols // b_split) // tn
        b_spec = pl.BlockSpec((None, tk, tn), lambda i, j, kk: (j // per, kk, j % per))
    in_specs = [pl.BlockSpec((tm, tk), lambda i, j, kk: (i, kk)), b_spec]
    args = [a, b]
    if addend is not None:
        in_specs.append(pl.BlockSpec((tm, tn), lambda i, j, kk: (i, j)))
        args.append(addend)
    if out_split == 1:
        out_spec = pl.BlockSpec((tm, tn), lambda i, j, kk: (i, j))
        out_shape = jax.ShapeDtypeStruct((m, n), out_dtype)
    else:
        per_o = (n // out_split) // tn
        out_spec = pl.BlockSpec((None, tm, tn), lambda i, j, kk: (j // per_o, i, j % per_o))
        out_shape = jax.ShapeDtypeStruct((out_split, m, n // out_split), out_dtype)
    return pl.pallas_call(
        body, name=name, grid=(m // tm, n // tn, nk),
        in_specs=in_specs,
        out_specs=out_spec,
        out_shape=out_shape,
        scratch_shapes=[pltpu.VMEM((tm, tn), F32)],
        compiler_params=_cparams(("parallel", "parallel", "arbitrary")),
    )(*args)


def _gated_rmsnorm(o, z, nw):
    r = lax.rsqrt(jnp.mean(o * o, axis=1, keepdims=True) + RMS_EPS)
    return o * r * nw * _silu(z)


def _chunk_consts():
    row = lax.broadcasted_iota(jnp.int32, (CHUNK, CHUNK), 0)
    col = lax.broadcasted_iota(jnp.int32, (CHUNK, CHUNK), 1)
    return row, col


def _hgrn2_chunk(qr, fr, iv, z, a0, a1, a2, nw, st):
    row, col = _chunk_consts()
    rowl = lax.broadcasted_iota(jnp.int32, (CHUNK, LANES), 0)
    amax = jnp.maximum(jnp.maximum(a0, a1), a2)
    e0, e1, e2 = jnp.exp(a0 - amax), jnp.exp(a1 - amax), jnp.exp(a2 - amax)
    lb = e0 / (e0 + e1 + e2)
    q = _silu(qr)
    forget = lb + (1.0 - lb) * jax.nn.sigmoid(fr)
    k = 1.0 - forget
    g = jnp.log(forget)
    tril = (col <= row).astype(F32)
    cum = _hdot(tril, g, NN)
    cum_last = jnp.sum(jnp.where(rowl == CHUNK - 1, cum, 0.0), axis=0, keepdims=True)
    refs = [jnp.zeros((1, LANES), F32)]
    for i in range(1, CHUNK // SUB):
        refs.append(jnp.sum(jnp.where(rowl == SUB * i - 1, cum, 0.0), axis=0, keepdims=True))
    ref_rows = jnp.zeros((CHUNK, LANES), F32)
    for i, r in enumerate(refs):
        ref_rows = ref_rows + jnp.where(rowl // SUB == i, r, 0.0)
    qt = q * jnp.exp(cum - ref_rows)
    att = jnp.zeros((CHUNK, CHUNK), F32)
    for i, r in enumerate(refs):
        kt = k * jnp.exp(jnp.minimum(r - cum, EXP_CLAMP))
        att = att + jnp.where((row // SUB == i) & (col <= row), _bdot(qt, kt, NT), 0.0)
    o = _bdot(att, iv, NN) + _bdot(q * jnp.exp(cum), st, NT)
    kd = k * jnp.exp(cum_last - cum)
    st_new = st * jnp.exp(cum_last) + _bdot(iv, kd, TN)
    return _gated_rmsnorm(o, z, nw), st_new


def _hgrn2_specs(tb, heads):
    def col(j):
        return pl.BlockSpec((tb, LANES), lambda b, h, j=j: (b, j * heads + h))
    return [col(0), col(1), col(2), col(3)]


def hgrn2_fwd(hproj, alb, nw, *, tb):
    t, d4 = hproj.shape
    d = d4 // 4
    heads = d // LANES
    nb, nc = t // tb, tb // CHUNK

    def body(q_ref, f_ref, i_ref, z_ref, alb_ref, nw_ref, y_ref, states_ref, st_ref):
        b, h = pl.program_id(0), pl.program_id(1)
        lane0 = pl.multiple_of(h * LANES, LANES)

        @pl.when(b == 0)
        def _():
            st_ref[h] = jnp.zeros((LANES, LANES), F32)

        a0 = alb_ref[0:1, pl.ds(lane0, LANES)]
        a1 = alb_ref[1:2, pl.ds(lane0, LANES)]
        a2 = alb_ref[2:3, pl.ds(lane0, LANES)]
        nwv = nw_ref[...]

        def step(c, carry):
            rows = pl.ds(pl.multiple_of(c * CHUNK, CHUNK), CHUNK)
            st = st_ref[h]
            states_ref[c, 0] = st
            y, st_new = _hgrn2_chunk(q_ref[rows, :], f_ref[rows, :], i_ref[rows, :], z_ref[rows, :],
                                     a0, a1, a2, nwv, st)
            y_ref[rows, :] = y.astype(y_ref.dtype)
            st_ref[h] = st_new
            return carry

        lax.fori_loop(0, nc, step, 0)

    return pl.pallas_call(
        body, name="hgrn2_fwd", grid=(nb, heads),
        in_specs=_hgrn2_specs(tb, heads) + [
            pl.BlockSpec((3, d), lambda b, h: (0, 0)),
            pl.BlockSpec((1, LANES), lambda b, h: (0, 0))],
        out_specs=[pl.BlockSpec((tb, LANES), lambda b, h: (b, h)),
                   pl.BlockSpec((nc, 1, LANES, LANES), lambda b, h: (b, h, 0, 0))],
        out_shape=[jax.ShapeDtypeStruct((t, d), BF16),
                   jax.ShapeDtypeStruct((t // CHUNK, heads, LANES, LANES), F32)],
        scratch_shapes=[pltpu.VMEM((heads, LANES, LANES), F32)],
        compiler_params=_cparams(("arbitrary", "arbitrary")),
    )(hproj, hproj, hproj, hproj, alb, nw)


def hgrn2_bwd(hproj, alb, nw, states, dy, *, tb):
    t, d4 = hproj.shape
    d = d4 // 4
    heads = d // LANES
    nb, nc = t // tb, tb // CHUNK

    def rev(spec_fn):
        return lambda b, h: spec_fn(nb - 1 - b, h)

    def body(q_ref, f_ref, i_ref, z_ref, alb_ref, nw_ref, states_ref, dy_ref,
             dh_ref, dalb_ref, dnw_ref, dst_ref):
        b, h = pl.program_id(0), pl.program_id(1)
        lane0 = pl.multiple_of(h * LANES, LANES)

        @pl.when(b == 0)
        def _():
            dst_ref[h] = jnp.zeros((LANES, LANES), F32)

        @pl.when((b == 0) & (h == 0))
        def _():
            dalb_ref[...] = jnp.zeros_like(dalb_ref)
            dnw_ref[...] = jnp.zeros_like(dnw_ref)

        a0 = alb_ref[0:1, pl.ds(lane0, LANES)]
        a1 = alb_ref[1:2, pl.ds(lane0, LANES)]
        a2 = alb_ref[2:3, pl.ds(lane0, LANES)]
        nwv = nw_ref[...]

        def step(i, carry):
            c = nc - 1 - i
            rows = pl.ds(pl.multiple_of(c * CHUNK, CHUNK), CHUNK)
            _, vjp = jax.vjp(_hgrn2_chunk, q_ref[rows, :], f_ref[rows, :], i_ref[rows, :], z_ref[rows, :],
                             a0, a1, a2, nwv, states_ref[c, 0])
            dq, df, di, dz, da0, da1, da2, dnw, dst = vjp((dy_ref[rows, :].astype(F32), dst_ref[h]))
            for j, val in enumerate((dq, df, di, dz)):
                dh_ref[rows, pl.ds(pl.multiple_of(j * d + h * LANES, LANES), LANES)] = val.astype(dh_ref.dtype)
            dalb_ref[0:1, pl.ds(lane0, LANES)] += da0
            dalb_ref[1:2, pl.ds(lane0, LANES)] += da1
            dalb_ref[2:3, pl.ds(lane0, LANES)] += da2
            dnw_ref[0:1, :] += dnw
            dst_ref[h] = dst
            return carry

        lax.fori_loop(0, nc, step, 0)

    def col(j):
        return pl.BlockSpec((tb, LANES), lambda b, h, j=j: (nb - 1 - b, j * heads + h))

    return pl.pallas_call(
        body, name="hgrn2_bwd", grid=(nb, heads),
        in_specs=[col(0), col(1), col(2), col(3),
                  pl.BlockSpec((3, d), lambda b, h: (0, 0)),
                  pl.BlockSpec((1, LANES), lambda b, h: (0, 0)),
                  pl.BlockSpec((nc, 1, LANES, LANES), lambda b, h: (nb - 1 - b, h, 0, 0)),
                  pl.BlockSpec((tb, LANES), lambda b, h: (nb - 1 - b, h))],
        out_specs=[pl.BlockSpec((tb, d4), lambda b, h: (nb - 1 - b, 0)),
                   pl.BlockSpec((3, d), lambda b, h: (0, 0)),
                   pl.BlockSpec((8, LANES), lambda b, h: (0, 0))],
        out_shape=[jax.ShapeDtypeStruct((t, d4), BF16),
                   jax.ShapeDtypeStruct((3, d), F32),
                   jax.ShapeDtypeStruct((8, LANES), F32)],
        scratch_shapes=[pltpu.VMEM((heads, LANES, LANES), F32)],
        compiler_params=_cparams(("arbitrary", "arbitrary")),
    )(hproj, hproj, hproj, hproj, alb, nw, states, dy)


def _conv_silu(xm, xh, w):
    rm = lax.broadcasted_iota(jnp.int32, (CHUNK, CHUNK), 0)
    cm = lax.broadcasted_iota(jnp.int32, (CHUNK, CHUNK), 1)
    rh = lax.broadcasted_iota(jnp.int32, (CHUNK, HALO), 0)
    ch = lax.broadcasted_iota(jnp.int32, (CHUNK, HALO), 1)
    acc = w[CONV_K - 1] * xm
    for j in range(CONV_K - 1):
        back = CONV_K - 1 - j
        sel_m = (cm == rm - back).astype(F32)
        sel_h = (ch == rh - back + HALO).astype(F32)
        acc = acc + w[j] * (_hdot(sel_m, xm, NN) + _hdot(sel_h, xh, NN))
    return _silu(acc)


def _l2norm(t):
    return t * lax.rsqrt(jnp.sum(t * t, axis=1, keepdims=True) + L2_EPS)


def _gdn_chunk(head, hh, qm, qh, km, kh, vm, vh, z, tail, wq, wk, wv, alog_row, dtb_row, nw, s):
    row, col = _chunk_consts()
    heads_lane = lax.broadcasted_iota(jnp.int32, (1, LANES), 1)
    q = _l2norm(_conv_silu(qm, qh, wq)) * (LANES ** -0.5)
    k = _l2norm(_conv_silu(km, kh, wk))
    v = _conv_silu(vm, vh, wv)
    lane = lax.broadcasted_iota(jnp.int32, (CHUNK, LANES), 1)
    beta = jax.nn.sigmoid(jnp.sum(jnp.where(lane == head, tail, 0.0), axis=1, keepdims=True))
    a_t = jnp.sum(jnp.where(lane == hh + head, tail, 0.0), axis=1, keepdims=True)
    alog = jnp.sum(jnp.where(heads_lane == head, alog_row, 0.0), axis=1, keepdims=True)
    dtb = jnp.sum(jnp.where(heads_lane == head, dtb_row, 0.0), axis=1, keepdims=True)
    g = -jnp.exp(alog) * _softplus(a_t + dtb)
    tril = (col <= row).astype(F32)
    eye = (col == row).astype(F32)
    cum = _hdot(tril, g + jnp.zeros((CHUNK, CHUNK), F32), NN)
    cum_r = _hdot(jnp.ones((CHUNK, CHUNK), F32), cum * eye, NN)
    diff = cum - cum_r
    strict = col < row
    incl = col <= row
    dec_strict = jnp.where(strict, jnp.exp(jnp.where(strict, diff, 0.0)), 0.0)
    dec_incl = jnp.where(incl, jnp.exp(jnp.where(incl, diff, 0.0)), 0.0)
    cum_c = jnp.sum(cum * (col == 0).astype(F32), axis=1, keepdims=True)
    cum_last = jnp.sum(jnp.where(row[:, 0:1] == CHUNK - 1, cum_c, 0.0), axis=0, keepdims=True)
    ecum = jnp.exp(cum_c)
    m = -(beta * _bdot(k, k, NT) * dec_strict)
    inv = eye + m
    mp = m
    for _ in range(int(math.log2(CHUNK)) - 1):
        mp = _hdot(mp, mp, NN)
        inv = inv + _hdot(inv, mp, NN)
    u0 = _hdot(inv, beta * v, NN)
    w = _hdot(inv, (beta * ecum) * k, NN)
    qk = _bdot(q, k, NT) * dec_incl
    u = u0 - _bdot(w, s, NN)
    o = _bdot(q * ecum, s, NN) + _bdot(qk, u, NN)
    kd = k * jnp.exp(cum_last - cum_c)
    s_new = jnp.exp(cum_last) * s + _bdot(kd, u, TN)
    return _gated_rmsnorm(o, z, nw), s_new


def _gdn_in_specs(tb, heads, rev_nb=None):
    def bb(b):
        return b if rev_nb is None else rev_nb - 1 - b

    def main(j):
        return pl.BlockSpec((tb, LANES), lambda b, h, j=j: (bb(b), j * heads + h))

    def halo(j):
        return pl.BlockSpec((HALO, LANES),
                            lambda b, h, j=j: (jnp.maximum(bb(b) * (tb // HALO) - 1, 0), j * heads + h))

    return [main(0), halo(0), main(1), halo(1), main(2), halo(2), main(3),
            pl.BlockSpec((tb, LANES), lambda b, h: (bb(b), 0)),
            pl.BlockSpec((CONV_K, 3 * heads * LANES), lambda b, h: (0, 0)),
            pl.BlockSpec((8, LANES), lambda b, h: (0, 0)),
            pl.BlockSpec((1, LANES), lambda b, h: (0, 0))]


def _gdn_chunk_args(c, h, blk, heads, q_ref, qh_ref, k_ref, kh_ref, v_ref, vh_ref, z_ref, tail_ref, cw_ref, prm_ref, nw_ref):
    d = heads * LANES
    rows = pl.ds(pl.multiple_of(c * CHUNK, CHUNK), CHUNK)
    prev = pl.ds(pl.multiple_of(jnp.maximum(c * CHUNK - HALO, 0), HALO), HALO)
    first = c == 0
    live = jnp.where(first & (blk == 0), 0.0, 1.0)

    def halo_of(ref, href):
        return jnp.where(first, href[...], ref[prev, :]) * live

    def cw(j):
        lanes = pl.ds(pl.multiple_of(j * d + h * LANES, LANES), LANES)
        return tuple(cw_ref[r:r + 1, lanes] for r in range(CONV_K))

    return (q_ref[rows, :], halo_of(q_ref, qh_ref), k_ref[rows, :], halo_of(k_ref, kh_ref),
            v_ref[rows, :], halo_of(v_ref, vh_ref), z_ref[rows, :], tail_ref[rows, :],
            cw(0), cw(1), cw(2), prm_ref[0:1, :], prm_ref[1:2, :], nw_ref[...])


def gdn_fwd(hmain, tail, conv_w, prm, nw, *, tb):
    t, d4 = hmain.shape
    d = d4 // 4
    heads = d // LANES
    nb, nc = t // tb, tb // CHUNK

    def body(q_ref, qh_ref, k_ref, kh_ref, v_ref, vh_ref, z_ref, tail_ref, cw_ref, prm_ref, nw_ref,
             y_ref, states_ref, s_ref):
        b, h = pl.program_id(0), pl.program_id(1)

        @pl.when(b == 0)
        def _():
            s_ref[h] = jnp.zeros((LANES, LANES), F32)

        def step(c, carry):
            rows = pl.ds(pl.multiple_of(c * CHUNK, CHUNK), CHUNK)
            s = s_ref[h]
            states_ref[c, 0] = s
            args = _gdn_chunk_args(c, h, b, heads, q_ref, qh_ref, k_ref, kh_ref, v_ref, vh_ref, z_ref,
                                   tail_ref, cw_ref, prm_ref, nw_ref)
            y, s_new = _gdn_chunk(h, heads, *args, s)
            y_ref[rows, :] = y.astype(y_ref.dtype)
            s_ref[h] = s_new
            return carry

        lax.fori_loop(0, nc, step, 0)

    return pl.pallas_call(
        body, name="gdn_fwd", grid=(nb, heads),
        in_specs=_gdn_in_specs(tb, heads),
        out_specs=[pl.BlockSpec((tb, LANES), lambda b, h: (b, h)),
                   pl.BlockSpec((nc, 1, LANES, LANES), lambda b, h: (b, h, 0, 0))],
        out_shape=[jax.ShapeDtypeStruct((t, d), BF16),
                   jax.ShapeDtypeStruct((t // CHUNK, heads, LANES, LANES), F32)],
        scratch_shapes=[pltpu.VMEM((heads, LANES, LANES), F32)],
        compiler_params=_cparams(("arbitrary", "arbitrary")),
    )(hmain, hmain, hmain, hmain, hmain, hmain, hmain, tail, conv_w, prm, nw)


def gdn_bwd(hmain, tail, conv_w, prm, nw, states, dy, *, tb):
    t, d4 = hmain.shape
    d = d4 // 4
    heads = d // LANES
    nb, nc = t // tb, tb // CHUNK

    def body(q_ref, qh_ref, k_ref, kh_ref, v_ref, vh_ref, z_ref, tail_ref, cw_ref, prm_ref, nw_ref,
             states_ref, dy_ref, dh_ref, dtail_ref, dcw_ref, dprm_ref, ds_ref, pend_ref):
        b, h = pl.program_id(0), pl.program_id(1)
        blk = nb - 1 - b

        @pl.when(b == 0)
        def _():
            ds_ref[h] = jnp.zeros((LANES, LANES), F32)
            pend_ref[h] = jnp.zeros((3, HALO, LANES), F32)

        @pl.when((b == 0) & (h == 0))
        def _():
            dcw_ref[...] = jnp.zeros_like(dcw_ref)
            dprm_ref[...] = jnp.zeros_like(dprm_ref)

        @pl.when(h == 0)
        def _():
            dtail_ref[...] = jnp.zeros_like(dtail_ref)

        def step(i, carry):
            c = nc - 1 - i
            rows = pl.ds(pl.multiple_of(c * CHUNK, CHUNK), CHUNK)
            args = _gdn_chunk_args(c, h, blk, heads, q_ref, qh_ref, k_ref, kh_ref, v_ref, vh_ref, z_ref,
                                   tail_ref, cw_ref, prm_ref, nw_ref)
            _, vjp = jax.vjp(functools.partial(_gdn_chunk, h, heads), *args, states_ref[c, 0])
            (dqm, dqh, dkm, dkh, dvm, dvh, dz, dtl, dwq, dwk, dwv, dalog, ddtb, dnw, ds) = vjp(
                (dy_ref[rows, :].astype(F32), ds_ref[h]))
            pend = pend_ref[h]
            zpad = jnp.zeros((CHUNK - HALO, LANES), F32)
            for j, (dm, dhalo) in enumerate(((dqm, dqh), (dkm, dkh), (dvm, dvh))):
                full = dm + jnp.concatenate([zpad, pend[j]], axis=0)
                dh_ref[rows, pl.ds(pl.multiple_of(j * d + h * LANES, LANES), LANES)] = full.astype(dh_ref.dtype)
                pend_ref[h, j] = dhalo
            dh_ref[rows, pl.ds(pl.multiple_of(3 * d + h * LANES, LANES), LANES)] = dz.astype(dh_ref.dtype)
            dtail_ref[rows, :] += dtl
            for j, dw in enumerate((dwq, dwk, dwv)):
                lanes = pl.ds(pl.multiple_of(j * d + h * LANES, LANES), LANES)
                for r in range(CONV_K):
                    dcw_ref[r:r + 1, lanes] += dw[r]
            dprm_ref[0:1, :] += dalog
            dprm_ref[1:2, :] += ddtb
            dprm_ref[2:3, :] += dnw
            ds_ref[h] = ds
            return carry

        lax.fori_loop(0, nc, step, 0)

    return pl.pallas_call(
        body, name="gdn_bwd", grid=(nb, heads),
        in_specs=_gdn_in_specs(tb, heads, rev_nb=nb) + [
            pl.BlockSpec((nc, 1, LANES, LANES), lambda b, h: (nb - 1 - b, h, 0, 0)),
            pl.BlockSpec((tb, LANES), lambda b, h: (nb - 1 - b, h))],
        out_specs=[pl.BlockSpec((tb, d4), lambda b, h: (nb - 1 - b, 0)),
                   pl.BlockSpec((tb, LANES), lambda b, h: (nb - 1 - b, 0)),
                   pl.BlockSpec((CONV_K, 3 * d), lambda b, h: (0, 0)),
                   pl.BlockSpec((8, LANES), lambda b, h: (0, 0))],
        out_shape=[jax.ShapeDtypeStruct((t, d4), BF16),
                   jax.ShapeDtypeStruct((t, LANES), F32),
                   jax.ShapeDtypeStruct((CONV_K, 3 * d), F32),
                   jax.ShapeDtypeStruct((8, LANES), F32)],
        scratch_shapes=[pltpu.VMEM((heads, LANES, LANES), F32),
                        pltpu.VMEM((heads, 3, HALO, LANES), F32)],
        compiler_params=_cparams(("arbitrary", "arbitrary")),
    )(hmain, hmain, hmain, hmain, hmain, hmain, hmain, tail, conv_w, prm, nw, states, dy)


def _layer_norm(u, g, b):
    mu = jnp.mean(u, axis=1, keepdims=True)
    var = jnp.mean(jnp.square(u - mu), axis=1, keepdims=True)
    return (u - mu) * lax.rsqrt(var + LN_EPS) * g + b


def ln_fwd(u, g, b, *, tr):
    t, d = u.shape

    def body(u_ref, g_ref, b_ref, x_ref, xb_ref):
        x = _layer_norm(u_ref[...], g_ref[...], b_ref[...])
        x_ref[...] = x
        xb_ref[...] = x.astype(BF16)

    row = pl.BlockSpec((tr, d), lambda i: (i, 0))
    vec = pl.BlockSpec((1, d), lambda i: (0, 0))
    return pl.pallas_call(
        body, name="ln_fwd", grid=(t // tr,), in_specs=[row, vec, vec], out_specs=[row, row],
        out_shape=[jax.ShapeDtypeStruct((t, d), F32), jax.ShapeDtypeStruct((t, d), BF16)],
        compiler_params=_cparams(("parallel",)),
    )(u, g, b)


def ln_bwd(u, g, b, dout, *, tr):
    t, d = u.shape

    def body(u_ref, g_ref, b_ref, dout_ref, du_ref, dub_ref, dg_ref, db_ref):
        @pl.when(pl.program_id(0) == 0)
        def _():
            dg_ref[...] = jnp.zeros_like(dg_ref)
            db_ref[...] = jnp.zeros_like(db_ref)

        _, vjp = jax.vjp(_layer_norm, u_ref[...], g_ref[...], b_ref[...])
        du, dg, db = vjp(dout_ref[...])
        du_ref[...] = du
        dub_ref[...] = du.astype(BF16)
        dg_ref[0:1, :] += dg
        db_ref[0:1, :] += db

    row = pl.BlockSpec((tr, d), lambda i: (i, 0))
    vec = pl.BlockSpec((1, d), lambda i: (0, 0))
    acc = pl.BlockSpec((8, d), lambda i: (0, 0))
    return pl.pallas_call(
        body, name="ln_bwd", grid=(t // tr,), in_specs=[row, vec, vec, row], out_specs=[row, row, acc, acc],
        out_shape=[jax.ShapeDtypeStruct((t, d), F32), jax.ShapeDtypeStruct((t, d), BF16),
                   jax.ShapeDtypeStruct((8, d), F32), jax.ShapeDtypeStruct((8, d), F32)],
        compiler_params=_cparams(("arbitrary",)),
    )(u, g, b, dout)


def ln_loss_bwd(u, g, b, target, *, tr):
    t, d = u.shape

    def loss_of(uu, gg, bb, tgt):
        err = jnp.square(_layer_norm(uu, gg, bb) - tgt)
        return 0.5 * jnp.sum(jnp.mean(err, axis=1, keepdims=True), axis=0, keepdims=True)

    def body(u_ref, g_ref, b_ref, t_ref, loss_ref, du_ref, dub_ref, dg_ref, db_ref):
        @pl.when(pl.program_id(0) == 0)
        def _():
            loss_ref[...] = jnp.zeros_like(loss_ref)
            dg_ref[...] = jnp.zeros_like(dg_ref)
            db_ref[...] = jnp.zeros_like(db_ref)

        tgt = t_ref[...]
        val, vjp = jax.vjp(lambda uu, gg, bb: loss_of(uu, gg, bb, tgt), u_ref[...], g_ref[...], b_ref[...])
        du, dg, db = vjp(jnp.ones((1, 1), F32))
        loss_ref[...] += val
        du_ref[...] = du
        dub_ref[...] = du.astype(BF16)
        dg_ref[0:1, :] += dg
        db_ref[0:1, :] += db

    row = pl.BlockSpec((tr, d), lambda i: (i, 0))
    vec = pl.BlockSpec((1, d), lambda i: (0, 0))
    acc = pl.BlockSpec((8, d), lambda i: (0, 0))
    return pl.pallas_call(
        body, name="ln_loss_bwd", grid=(t // tr,), in_specs=[row, vec, vec, row],
        out_specs=[pl.BlockSpec((8, LANES), lambda i: (0, 0)), row, row, acc, acc],
        out_shape=[jax.ShapeDtypeStruct((8, LANES), F32),
                   jax.ShapeDtypeStruct((t, d), F32), jax.ShapeDtypeStruct((t, d), BF16),
                   jax.ShapeDtypeStruct((8, d), F32), jax.ShapeDtypeStruct((8, d), F32)],
        compiler_params=_cparams(("arbitrary",)),
    )(u, g, b, target)


def local_step(x, target, wa_in, alb, a_nw, wa_out, wb_main, wb_tail, conv_w, a_log, dt_bias, b_nw, wb_out,
               ln_g, ln_b, *, tb=256, tr=256):
    t, d = x.shape
    heads = d // LANES
    tb, tr = min(tb, t), min(tr, t)
    xb = x.astype(BF16)
    prm = jnp.zeros((8, LANES), F32).at[0, :heads].set(a_log[0]).at[1, :heads].set(dt_bias[0])

    ha = matmul(xb, wa_in, name="mm_a_in")
    ya, st_a = hgrn2_fwd(ha, alb, a_nw, tb=tb)
    u1 = matmul(ya, wa_out, name="mm_a_out", addend=x, alpha=DEEPNORM_ALPHA)
    x1, x1b = ln_fwd(u1, ln_g[0:1], ln_b[0:1], tr=tr)
    hb = matmul(x1b, wb_main, name="mm_b_in")
    tl = matmul(x1b, wb_tail, name="mm_b_tail")
    yb, st_b = gdn_fwd(hb, tl, conv_w, prm, b_nw, tb=tb)
    u2 = matmul(yb, wb_out, name="mm_b_out", addend=x1, alpha=DEEPNORM_ALPHA)

    loss, du2, du2b, dg2, db2 = ln_loss_bwd(u2, ln_g[1:2], ln_b[1:2], target, tr=tr)
    d_wb_out = matmul(yb.T, du2b, name="mm_dwb_out", out_dtype=BF16)
    dyb = matmul(du2b, wb_out, name="mm_dyb", nt=True)
    dhb, dtl, d_conv, dprm = gdn_bwd(hb, tl, conv_w, prm, b_nw, st_b, dyb, tb=tb)
    dtlb = dtl.astype(BF16)
    x1t = x1b.T
    d_wb_main = matmul(x1t, dhb, name="mm_dwb_main", out_dtype=BF16)
    d_wb_tail = matmul(x1t, dtlb, name="mm_dwb_tail", out_dtype=BF16)
    dx1_tail = matmul(dtlb, wb_tail, name="mm_dx1_tail", nt=True, addend=du2, alpha=DEEPNORM_ALPHA)
    dx1 = matmul(dhb, wb_main, name="mm_dx1", nt=True, addend=dx1_tail, alpha=1.0)
    du1, du1b, dg1, db1 = ln_bwd(u1, ln_g[0:1], ln_b[0:1], dx1, tr=tr)
    d_wa_out = matmul(ya.T, du1b, name="mm_dwa_out", out_dtype=BF16)
    dya = matmul(du1b, wa_out, name="mm_dya", nt=True)
    dha, d_alb, d_anw = hgrn2_bwd(ha, alb, a_nw, st_a, dya, tb=tb)
    d_wa_in = matmul(xb.T, dha, name="mm_dwa_in", out_dtype=BF16, out_split=wa_in.shape[0] if wa_in.ndim == 3 else 1)
    grad_x = matmul(dha, wa_in, name="mm_dx", nt=True, addend=du1, alpha=DEEPNORM_ALPHA)

    small = dict(
        a_lower_bounds=d_alb, a_norm_w=d_anw[0:1], b_conv_w=d_conv,
        b_a_log=dprm[0:1, :heads], b_dt_bias=dprm[1:2, :heads], b_norm_w=dprm[2:3],
        ln_g=jnp.concatenate([dg1[0:1], dg2[0:1]], axis=0), ln_b=jnp.concatenate([db1[0:1], db2[0:1]], axis=0))
    big = dict(a_w_in=d_wa_in, a_w_out=d_wa_out, b_w_main=d_wb_main, b_w_tail=d_wb_tail, b_w_out=d_wb_out)
    return loss, grad_x, big, small


MESH_ID = pl.DeviceIdType.MESH


def _place():
    return lax.axis_index("x"), lax.axis_index("y"), lax.axis_index("c")


def _index_of(p):
    return 4 * p[0] + 2 * p[1] + p[2]


def all_gather(shards, *, name, space):
    n = len(shards)

    def body(*refs):
        ins, outs = refs[:n], refs[n:2 * n]
        send_sems, recv_sems, local_sems = refs[2 * n:]
        x, y, c = _place()
        me, sibling = (x, y, c), (x, y, 1 - c)
        chips = [(1 - x, y), (x, 1 - y), (1 - x, 1 - y)]

        def copy(a, k, block, to, own=False):
            dst = outs[a].at[_index_of(block)]
            return pltpu.make_async_remote_copy(
                src_ref=ins[a] if own else dst, dst_ref=dst,
                send_sem=send_sems.at[7 * a + k], recv_sem=recv_sems.at[7 * a + k],
                device_id=to, device_id_type=MESH_ID)

        mine = [pltpu.make_async_copy(ins[a], outs[a].at[_index_of(me)], local_sems.at[a]) for a in range(n)]
        for cp in mine:
            cp.start()
        first = []
        for a in range(n):
            first.append(copy(a, 0, me, sibling, own=True))
            first += [copy(a, 1 + j, me, (*chip, c), own=True) for j, chip in enumerate(chips)]
        for cp in first:
            cp.start()
        passed = []
        for j, chip in enumerate(chips):
            for a in range(n):
                copy(a, 1 + j, (*chip, c), me).wait_recv()
                fwd = copy(a, 4 + j, (*chip, c), sibling)
                fwd.start()
                passed.append(fwd)
        for a in range(n):
            copy(a, 0, sibling, me).wait_recv()
            for j, chip in enumerate(chips):
                copy(a, 4 + j, (*chip, 1 - c), me).wait_recv()
        for cp in first + passed:
            cp.wait_send()
        for cp in mine:
            cp.wait()

    spec = pl.BlockSpec(memory_space=space)
    return pl.pallas_call(
        body, name=name,
        in_specs=[spec] * n, out_specs=[spec] * n,
        out_shape=[jax.ShapeDtypeStruct((N_DEV, *s.shape), s.dtype) for s in shards],
        scratch_shapes=[pltpu.SemaphoreType.DMA((7 * n,)), pltpu.SemaphoreType.DMA((7 * n,)),
                        pltpu.SemaphoreType.DMA((n,))],
        compiler_params=pltpu.CompilerParams(vmem_limit_bytes=VMEM_LIMIT),
    )(*shards)


def exchange(parts, *, name):
    n = len(parts)

    def body(*refs):
        ins, outs = refs[:n], refs[n:2 * n]
        send_sems, recv_sems, local_sems = refs[2 * n:]
        x, y, c = _place()
        me = (x, y, c)
        copies = []
        for a in range(n):
            own = pltpu.make_async_copy(ins[a].at[_index_of(me)], outs[a].at[_index_of(me)], local_sems.at[a])
            own.start()
            copies.append(own)
            for r in range(1, N_DEV):
                fx, fy, fc = (r >> 2) & 1, (r >> 1) & 1, r & 1
                peer = (x ^ fx, y ^ fy, c ^ fc)
                cp = pltpu.make_async_remote_copy(
                    src_ref=ins[a].at[_index_of(peer)], dst_ref=outs[a].at[_index_of(me)],
                    send_sem=send_sems.at[7 * a + r - 1], recv_sem=recv_sems.at[7 * a + r - 1],
                    device_id=peer, device_id_type=MESH_ID)
                cp.start()
                copies.append(cp)
        for cp in copies:
            cp.wait()

    spec = pl.BlockSpec(memory_space=pltpu.HBM)
    return pl.pallas_call(
        body, name=name,
        in_specs=[spec] * n, out_specs=[spec] * n,
        out_shape=[jax.ShapeDtypeStruct(p.shape, p.dtype) for p in parts],
        scratch_shapes=[pltpu.SemaphoreType.DMA((7 * n,)), pltpu.SemaphoreType.DMA((7 * n,)),
                        pltpu.SemaphoreType.DMA((n,))],
        compiler_params=pltpu.CompilerParams(vmem_limit_bytes=VMEM_LIMIT),
    )(*parts)


def adamw(parts, w, m, v, *, name, tr=64):
    p, r, c = parts.shape
    tr = min(tr, r)
    assert r % tr == 0
    c1 = 1.0 / (1.0 - ADAM_B1 ** ADAM_STEP)
    c2 = 1.0 / (1.0 - ADAM_B2 ** ADAM_STEP)

    def body(p_ref, w_ref, m_ref, v_ref, g_ref, d_ref, nm_ref, nv_ref):
        g = p_ref[0].astype(F32)
        for i in range(1, p):
            g = g + p_ref[i].astype(F32)
        nm = ADAM_B1 * m_ref[...] + (1.0 - ADAM_B1) * g
        nv = ADAM_B2 * v_ref[...] + (1.0 - ADAM_B2) * jnp.square(g)
        g_ref[...] = g
        nm_ref[...] = nm
        nv_ref[...] = nv
        d_ref[...] = -ADAM_LR * ((nm * c1) / (jnp.sqrt(nv * c2) + ADAM_EPS) + ADAM_WD * w_ref[...])

    blk = pl.BlockSpec((tr, c), lambda i: (i, 0))
    out = jax.ShapeDtypeStruct((r, c), F32)
    return pl.pallas_call(
        body, name=name, grid=(r // tr,),
        in_specs=[pl.BlockSpec((p, tr, c), lambda i: (0, i, 0)), blk, blk, blk],
        out_specs=[blk] * 4, out_shape=[out] * 4,
        compiler_params=_cparams(("parallel",)),
    )(parts, w, m, v)


def _pack(d, vals):
    heads = d // LANES
    vecs = jnp.zeros((8, LANES), F32)
    vecs = vecs.at[0:1].set(vals["a_norm_w"]).at[1:2, :heads].set(vals["b_a_log"])
    vecs = vecs.at[2:3, :heads].set(vals["b_dt_bias"]).at[3:4].set(vals["b_norm_w"])
    rows = [vals["a_lower_bounds"].reshape(-1, LANES), vals["ln_g"].reshape(-1, LANES),
            vals["ln_b"].reshape(-1, LANES), vecs]
    return jnp.concatenate(rows, axis=0)


def _unpack(d, packed):
    heads = d // LANES
    n3, n2 = 3 * heads, 2 * heads
    o = 0
    out = {}
    out["a_lower_bounds"] = packed[o:o + n3].reshape(3, d); o += n3
    out["ln_g"] = packed[o:o + n2].reshape(2, d); o += n2
    out["ln_b"] = packed[o:o + n2].reshape(2, d); o += n2
    out["a_norm_w"] = packed[o:o + 1]
    out["b_a_log"] = packed[o + 1:o + 2, :heads]
    out["b_dt_bias"] = packed[o + 2:o + 3, :heads]
    out["b_norm_w"] = packed[o + 3:o + 4]
    return out


SMALL = ("a_lower_bounds", "a_norm_w", "b_a_log", "b_dt_bias", "b_norm_w", "ln_g", "ln_b")
ORDER = ("a_w_in", "a_lower_bounds", "a_norm_w", "a_w_out", "b_w_in", "b_conv_w", "b_a_log", "b_dt_bias", "b_norm_w",
         "b_w_out", "ln_g", "ln_b")


def kernel(x, a_w_in, a_lower_bounds, a_norm_w, a_w_out, b_w_in, b_conv_w, b_a_log, b_dt_bias, b_norm_w, b_w_out, ln_g, ln_b, loss_target, m_a_w_in, m_a_lower_bounds, m_a_norm_w, m_a_w_out, m_b_w_in, m_b_conv_w, m_b_a_log, m_b_dt_bias, m_b_norm_w, m_b_w_out, m_ln_g, m_ln_b, v_a_w_in, v_a_lower_bounds, v_a_norm_w, v_a_w_out, v_b_w_in, v_b_conv_w, v_b_a_log, v_b_dt_bias, v_b_norm_w, v_b_w_out, v_ln_g, v_ln_b):
    w = dict(a_w_in=a_w_in, a_lower_bounds=a_lower_bounds, a_norm_w=a_norm_w, a_w_out=a_w_out, b_w_in=b_w_in,
             b_conv_w=b_conv_w, b_a_log=b_a_log, b_dt_bias=b_dt_bias, b_norm_w=b_norm_w, b_w_out=b_w_out, ln_g=ln_g, ln_b=ln_b)
    m = dict(a_w_in=m_a_w_in, a_lower_bounds=m_a_lower_bounds, a_norm_w=m_a_norm_w, a_w_out=m_a_w_out, b_w_in=m_b_w_in,
             b_conv_w=m_b_conv_w, b_a_log=m_b_a_log, b_dt_bias=m_b_dt_bias, b_norm_w=m_b_norm_w, b_w_out=m_b_w_out,
             ln_g=m_ln_g, ln_b=m_ln_b)
    v = dict(a_w_in=v_a_w_in, a_lower_bounds=v_a_lower_bounds, a_norm_w=v_a_norm_w, a_w_out=v_a_w_out, b_w_in=v_b_w_in,
             b_conv_w=v_b_conv_w, b_a_log=v_b_a_log, b_dt_bias=v_b_dt_bias, b_norm_w=v_b_norm_w, b_w_out=v_b_w_out,
             ln_g=v_ln_g, ln_b=v_ln_b)
    t, d = x.shape[1], x.shape[2]
    heads = d // LANES
    n_tail = 2 * heads
    me = _index_of(_place())

    ga_in, ga_out, gb_in, gb_out, g_conv = all_gather(
        [a_w_in[0].astype(BF16), a_w_out[0].astype(BF16), b_w_in[0].astype(BF16), b_w_out[0].astype(BF16), b_conv_w[0]],
        name="gather_weights", space=pltpu.HBM)
    wa_out = ga_out.reshape(d, d)
    wb_out = gb_out.reshape(d, d)
    wb_full = jnp.transpose(gb_in, (1, 0, 2)).reshape(d, 4 * d + n_tail)
    wb_main = wb_full[:, :4 * d]
    wb_tail = jnp.concatenate([wb_full[:, 4 * d:], jnp.zeros((d, LANES - n_tail), BF16)], axis=1)
    conv_w = jnp.transpose(g_conv, (1, 0, 2)).reshape(CONV_K, 3 * d)

    loss, grad_x, big, small = local_step(
        x[0], loss_target[0], ga_in, a_lower_bounds, a_norm_w, wa_out, wb_main, wb_tail, conv_w, b_a_log, b_dt_bias,
        b_norm_w, wb_out, ln_g, ln_b)
    loss = lax.psum(loss[0, 0], ("x", "y", "c"))

    conv_rows = small["b_conv_w"].reshape(-1, LANES)
    n_conv = conv_rows.shape[0]
    sent = jnp.concatenate([conv_rows, _pack(d, small)], axis=0)
    (got,) = all_gather([sent], name="gather_small", space=pltpu.VMEM)
    res = {}
    packed = adamw(got[:, n_conv:], _pack(d, w), _pack(d, m), _pack(d, v), name="adamw_small", tr=4096)
    for k, vals in zip(("grad", "delta", "new_m", "new_v"), packed):
        res[k] = _unpack(d, vals)
    shard_ch = 3 * d // N_DEV
    conv_parts = lax.dynamic_slice_in_dim(got[:, :n_conv].reshape(N_DEV, CONV_K, 3 * d), me * shard_ch, shard_ch, axis=2)
    conv_out = adamw(conv_parts, b_conv_w[0], m_b_conv_w[0], v_b_conv_w[0], name="adamw_conv")
    for k, vals in zip(("grad", "delta", "new_m", "new_v"), conv_out):
        res[k]["b_conv_w"] = vals[None]

    d_b_in = jnp.concatenate([big["b_w_main"], big["b_w_tail"][:, :n_tail]], axis=1)
    d_b_in = jnp.transpose(d_b_in.reshape(d, N_DEV, -1), (1, 0, 2))
    parts = [big["a_w_in"], big["a_w_out"].reshape(N_DEV, d // N_DEV, d), d_b_in, big["b_w_out"].reshape(N_DEV, d // N_DEV, d)]
    recv = exchange(parts, name="exchange_grads")
    for name, got_parts in zip(("a_w_in", "a_w_out", "b_w_in", "b_w_out"), recv):
        outs = adamw(got_parts, w[name][0], m[name][0], v[name][0], name="adamw_" + name)
        for k, vals in zip(("grad", "delta", "new_m", "new_v"), outs):
            res[k][name] = vals[None]

    return (loss, grad_x[None], *[res["grad"][k] for k in ORDER], *[res["delta"][k] for k in ORDER],
            *[res["new_m"][k] for k in ORDER], *[res["new_v"][k] for k in ORDER])
```

```python
import functools
import math

import jax
import jax.numpy as jnp
from jax import lax
from jax.experimental import pallas as pl
from jax.experimental.pallas import tpu as pltpu

F32 = jnp.float32
BF16 = jnp.bfloat16

N_DEV = 8
LANES = 128
CHUNK = 64
SUB = 16
HALO = 8
CONV_K = 4
DEPTH = 2
DEEPNORM_ALPHA = (2.0 * DEPTH) ** 0.25
LN_EPS = 1e-5
RMS_EPS = 1e-6
L2_EPS = 1e-6
EXP_CLAMP = 60.0
ADAM_LR = 0.001
ADAM_B1 = 0.9
ADAM_B2 = 0.999
ADAM_EPS = 1e-08
ADAM_WD = 0.01
ADAM_STEP = 10
VMEM_LIMIT = 56 * 1024 * 1024
HEADS_PER_STEP = 2

NN = ((1,), (0,))
NT = ((1,), (1,))
TN = ((0,), (0,))


def _dg(a, b, dims, precision=None):
    return lax.dot_general(a, b, (dims, ((), ())), precision=precision, preferred_element_type=F32)


def _bdot(a, b, dims):
    return _dg(a.astype(BF16), b.astype(BF16), dims)


def _split2(t):
    hi = t.astype(BF16)
    return hi, (t - hi.astype(F32)).astype(BF16)


def _split3(t):
    hi = t.astype(BF16)
    r = t - hi.astype(F32)
    mid = r.astype(BF16)
    return hi, mid, (r - mid.astype(F32)).astype(BF16)


@jax.custom_vjp
def _sdot(sel, t):
    sel = sel.astype(BF16)
    hi, mid, lo = _split3(t)
    return (_dg(sel, lo, NN) + _dg(sel, mid, NN)) + _dg(sel, hi, NN)


def _sdot_fwd(sel, t):
    return _sdot(sel, t), sel


def _sdot_bwd(sel, ct):
    selb = sel.astype(BF16)
    hi, mid, lo = _split3(ct)
    return jnp.zeros_like(sel), (_dg(selb, lo, TN) + _dg(selb, mid, TN)) + _dg(selb, hi, TN)


_sdot.defvjp(_sdot_fwd, _sdot_bwd)


def _dot3_raw(a, b, dims):
    ah, al = _split2(a)
    bh, bl = _split2(b)
    return (_dg(al, bh, dims) + _dg(ah, bl, dims)) + _dg(ah, bh, dims)


@jax.custom_vjp
def _dot3(a, b):
    return _dot3_raw(a, b, NN)


def _dot3_fwd(a, b):
    return _dot3_raw(a, b, NN), (a, b)


def _dot3_bwd(res, ct):
    a, b = res
    return _bdot(ct, b, NT), _bdot(a, ct, TN)


_dot3.defvjp(_dot3_fwd, _dot3_bwd)


@jax.custom_vjp
def _tri_inv(m):
    row = lax.broadcasted_iota(jnp.int32, (CHUNK, CHUNK), 0)
    col = lax.broadcasted_iota(jnp.int32, (CHUNK, CHUNK), 1)
    inv = (col == row).astype(F32) + m
    mp = m
    for _ in range(int(math.log2(CHUNK)) - 1):
        mp = _dot3_raw(mp, mp, NN)
        inv = inv + _dot3_raw(inv, mp, NN)
    return inv


def _tri_inv_fwd(m):
    inv = _tri_inv(m)
    return inv, inv


def _tri_inv_bwd(inv, ct):
    return (_dot3_raw(_dot3_raw(inv, ct, TN), inv, NT),)


_tri_inv.defvjp(_tri_inv_fwd, _tri_inv_bwd)


@functools.partial(jax.custom_vjp, nondiff_argnums=(2,))
def _shift_rows(xm, xh, back):
    r = pltpu.roll(xm, back, 0)
    row = lax.broadcasted_iota(jnp.int32, xh.shape, 0)
    top = jnp.where(row < back, pltpu.roll(xh, back, 0), r[0:HALO])
    return jnp.concatenate([top, r[HALO:]], axis=0)


def _shift_rows_fwd(xm, xh, back):
    return _shift_rows(xm, xh, back), None


def _shift_rows_bwd(back, _, ct):
    row = lax.broadcasted_iota(jnp.int32, ct.shape, 0)
    dxm = jnp.where(row < CHUNK - back, pltpu.roll(ct, CHUNK - back, 0), 0.0)
    rowh = lax.broadcasted_iota(jnp.int32, (HALO, ct.shape[1]), 0)
    dxh = jnp.where(rowh >= HALO - back, pltpu.roll(ct[0:HALO], HALO - back, 0), 0.0)
    return dxm, dxh


_shift_rows.defvjp(_shift_rows_fwd, _shift_rows_bwd)


def _silu(t):
    return t * jax.nn.sigmoid(t)


def _softplus(t):
    return jnp.where(t > 20.0, t, jnp.log1p(jnp.exp(jnp.minimum(t, 20.0))))


def _cparams(sem=None):
    kw = dict(vmem_limit_bytes=VMEM_LIMIT)
    if sem is not None:
        kw["dimension_semantics"] = sem
    return pltpu.CompilerParams(**kw)


def matmul(a, b, *, name, nt=False, addend=None, alpha=1.0, out_dtype=F32, out_split=1, tm=1024, tn=1024, tk=512):
    m, k = a.shape
    b_split = b.shape[0] if b.ndim == 3 else 1
    b_rows, b_cols = b.shape[-2], b.shape[-1] * b_split
    n = b_rows if nt else b_cols
    tm, tn, tk = min(tm, m), min(tn, n), min(tk, k)
    if b_split > 1:
        part = b_cols // b_split
        tn, tk = (tn, min(tk, part)) if nt else (min(tn, part), tk)
    if out_split > 1:
        tn = min(tn, n // out_split)
    assert m % tm == 0 and n % tn == 0 and k % tk == 0, (a.shape, b.shape, nt)
    nk = k // tk
    dims = NT if nt else NN

    def body(*refs):
        if addend is None:
            a_ref, b_ref, o_ref, acc_ref = refs
            add_ref = None
        else:
            a_ref, b_ref, add_ref, o_ref, acc_ref = refs
        kk = pl.program_id(2)

        @pl.when(kk == 0)
        def _():
            acc_ref[...] = jnp.zeros_like(acc_ref)

        acc_ref[...] += _dg(a_ref[...], b_ref[...], dims)

        @pl.when(kk == nk - 1)
        def _():
            r = acc_ref[...]
            if add_ref is not None:
                r = r + alpha * add_ref[...].astype(F32)
            o_ref[...] = r.astype(o_ref.dtype)

    if b_split == 1:
        b_spec = (pl.BlockSpec((tn, tk), lambda i, j, kk: (j, kk)) if nt
                  else pl.BlockSpec((tk, tn), lambda i, j, kk: (kk, j)))
    elif nt:
        per = (b_cols // b_split) // tk
        b_spec = pl.BlockSpec((None, tn, tk), lambda i, j, kk: (kk // per, j, kk % per))
    else:
        per = (b_cols // b_split) // tn
        b_spec = pl.BlockSpec((None, tk, tn), lambda i, j, kk: (j // per, kk, j % per))
    in_specs = [pl.BlockSpec((tm, tk), lambda i, j, kk: (i, kk)), b_spec]
    args = [a, b]
    if addend is not None:
        in_specs.append(pl.BlockSpec((tm, tn), lambda i, j, kk: (i, j)))
        args.append(addend)
    if out_split == 1:
        out_spec = pl.BlockSpec((tm, tn), lambda i, j, kk: (i, j))
        out_shape = jax.ShapeDtypeStruct((m, n), out_dtype)
    else:
        per_o = (n // out_split) // tn
        out_spec = pl.BlockSpec((None, tm, tn), lambda i, j, kk: (j // per_o, i, j % per_o))
        out_shape = jax.ShapeDtypeStruct((out_split, m, n // out_split), out_dtype)
    return pl.pallas_call(
        body, name=name, grid=(m // tm, n // tn, nk),
        in_specs=in_specs,
        out_specs=out_spec,
        out_shape=out_shape,
        scratch_shapes=[pltpu.VMEM((tm, tn), F32)],
        compiler_params=_cparams(("parallel", "parallel", "arbitrary")),
    )(*args)


def _gated_rmsnorm(o, z, nw):
    r = lax.rsqrt(jnp.mean(o * o, axis=1, keepdims=True) + RMS_EPS)
    return o * r * nw * _silu(z)


def _chunk_consts():
    row = lax.broadcasted_iota(jnp.int32, (CHUNK, CHUNK), 0)
    col = lax.broadcasted_iota(jnp.int32, (CHUNK, CHUNK), 1)
    return row, col


def _head_cols(tb, groups, hp, j, rev_nb=None):
    def bb(b):
        return b if rev_nb is None else rev_nb - 1 - b
    return pl.BlockSpec((tb, hp * LANES), lambda b, h: (bb(b), j * groups + h))


def _hgrn2_chunk(qr, fr, iv, z, a0, a1, a2, nw, st):
    row, col = _chunk_consts()
    rowl = lax.broadcasted_iota(jnp.int32, (CHUNK, LANES), 0)
    amax = jnp.maximum(jnp.maximum(a0, a1), a2)
    e0, e1, e2 = jnp.exp(a0 - amax), jnp.exp(a1 - amax), jnp.exp(a2 - amax)
    lb = e0 / (e0 + e1 + e2)
    q = _silu(qr)
    forget = lb + (1.0 - lb) * jax.nn.sigmoid(fr)
    k = 1.0 - forget
    g = jnp.log(forget)
    tril = (col <= row).astype(F32)
    cum = _sdot(tril, g)
    cum_last = jnp.sum(jnp.where(rowl == CHUNK - 1, cum, 0.0), axis=0, keepdims=True)
    refs = [jnp.zeros((1, LANES), F32)]
    for i in range(1, CHUNK // SUB):
        refs.append(jnp.sum(jnp.where(rowl == SUB * i - 1, cum, 0.0), axis=0, keepdims=True))
    ref_rows = jnp.zeros((CHUNK, LANES), F32)
    for i, r in enumerate(refs):
        ref_rows = ref_rows + jnp.where(rowl // SUB == i, r, 0.0)
    qt = q * jnp.exp(cum - ref_rows)
    att = jnp.zeros((CHUNK, CHUNK), F32)
    for i, r in enumerate(refs):
        kt = k * jnp.exp(jnp.minimum(r - cum, EXP_CLAMP))
        att = att + jnp.where((row // SUB == i) & (col <= row), _bdot(qt, kt, NT), 0.0)
    o = _bdot(att, iv, NN) + _bdot(q * jnp.exp(cum), st, NT)
    kd = k * jnp.exp(cum_last - cum)
    st_new = st * jnp.exp(cum_last) + _bdot(iv, kd, TN)
    return _gated_rmsnorm(o, z, nw), st_new


def hgrn2_fwd(hproj, alb, nw, *, tb, hp):
    t, d4 = hproj.shape
    d = d4 // 4
    heads = d // LANES
    groups = heads // hp
    nb, nc = t // tb, tb // CHUNK

    def body(q_ref, f_ref, i_ref, z_ref, alb_ref, nw_ref, y_ref, states_ref, st_ref):
        b, hg = pl.program_id(0), pl.program_id(1)

        @pl.when(b == 0)
        def _():
            for p in range(hp):
                st_ref[hg * hp + p] = jnp.zeros((LANES, LANES), F32)

        nwv = nw_ref[...]
        albs = []
        for p in range(hp):
            lanes = pl.ds(pl.multiple_of((hg * hp + p) * LANES, LANES), LANES)
            albs.append((alb_ref[0:1, lanes], alb_ref[1:2, lanes], alb_ref[2:3, lanes]))

        def step(c, carry):
            rows = pl.ds(pl.multiple_of(c * CHUNK, CHUNK), CHUNK)
            lns = [slice(p * LANES, (p + 1) * LANES) for p in range(hp)]
            olds = [st_ref[hg * hp + p] for p in range(hp)]
            ins = [(q_ref[rows, ln], f_ref[rows, ln], i_ref[rows, ln], z_ref[rows, ln]) for ln in lns]
            outs = [_hgrn2_chunk(*ins[p], *albs[p], nwv, olds[p]) for p in range(hp)]
            for p in range(hp):
                states_ref[c, p] = olds[p]
                y_ref[rows, lns[p]] = outs[p][0].astype(y_ref.dtype)
                st_ref[hg * hp + p] = outs[p][1]
            return carry

        lax.fori_loop(0, nc, step, 0)

    return pl.pallas_call(
        body, name="hgrn2_fwd", grid=(nb, groups),
        in_specs=[_head_cols(tb, groups, hp, j) for j in range(4)] + [
            pl.BlockSpec((3, d), lambda b, h: (0, 0)),
            pl.BlockSpec((1, LANES), lambda b, h: (0, 0))],
        out_specs=[pl.BlockSpec((tb, hp * LANES), lambda b, h: (b, h)),
                   pl.BlockSpec((nc, hp, LANES, LANES), lambda b, h: (b, h, 0, 0))],
        out_shape=[jax.ShapeDtypeStruct((t, d), BF16),
                   jax.ShapeDtypeStruct((t // CHUNK, heads, LANES, LANES), F32)],
        scratch_shapes=[pltpu.VMEM((heads, LANES, LANES), F32)],
        compiler_params=_cparams(("arbitrary", "arbitrary")),
    )(hproj, hproj, hproj, hproj, alb, nw)


def hgrn2_bwd(hproj, alb, nw, states, dy, *, tb, hp):
    t, d4 = hproj.shape
    d = d4 // 4
    heads = d // LANES
    groups = heads // hp
    nb, nc = t // tb, tb // CHUNK

    def body(q_ref, f_ref, i_ref, z_ref, alb_ref, nw_ref, states_ref, dy_ref,
             dh_ref, dalb_ref, dnw_ref, dst_ref):
        b, hg = pl.program_id(0), pl.program_id(1)

        @pl.when(b == 0)
        def _():
            for p in range(hp):
                dst_ref[hg * hp + p] = jnp.zeros((LANES, LANES), F32)

        @pl.when((b == 0) & (hg == 0))
        def _():
            dalb_ref[...] = jnp.zeros_like(dalb_ref)
            dnw_ref[...] = jnp.zeros_like(dnw_ref)

        nwv = nw_ref[...]
        lanes_of, albs = [], []
        for p in range(hp):
            lanes = pl.ds(pl.multiple_of((hg * hp + p) * LANES, LANES), LANES)
            lanes_of.append(lanes)
            albs.append((alb_ref[0:1, lanes], alb_ref[1:2, lanes], alb_ref[2:3, lanes]))

        def step(i, carry):
            c = nc - 1 - i
            rows = pl.ds(pl.multiple_of(c * CHUNK, CHUNK), CHUNK)
            grads = []
            for p in range(hp):
                ln = slice(p * LANES, (p + 1) * LANES)
                cts = (dy_ref[rows, ln].astype(F32), dst_ref[hg * hp + p])
                _, vjp = jax.vjp(_hgrn2_chunk, q_ref[rows, ln], f_ref[rows, ln], i_ref[rows, ln], z_ref[rows, ln],
                                 *albs[p], nwv, states_ref[c, p])
                grads.append(vjp(cts))
            for p in range(hp):
                h = hg * hp + p
                dq, df, di, dz, da0, da1, da2, dnw, dst = grads[p]
                for j, val in enumerate((dq, df, di, dz)):
                    dh_ref[rows, pl.ds(pl.multiple_of(j * d + h * LANES, LANES), LANES)] = val.astype(dh_ref.dtype)
                dalb_ref[0:1, lanes_of[p]] += da0
                dalb_ref[1:2, lanes_of[p]] += da1
                dalb_ref[2:3, lanes_of[p]] += da2
                dnw_ref[0:1, :] += dnw
                dst_ref[h] = dst
            return carry

        lax.fori_loop(0, nc, step, 0)

    return pl.pallas_call(
        body, name="hgrn2_bwd", grid=(nb, groups),
        in_specs=[_head_cols(tb, groups, hp, j, rev_nb=nb) for j in range(4)] + [
            pl.BlockSpec((3, d), lambda b, h: (0, 0)),
            pl.BlockSpec((1, LANES), lambda b, h: (0, 0)),
            pl.BlockSpec((nc, hp, LANES, LANES), lambda b, h: (nb - 1 - b, h, 0, 0)),
            pl.BlockSpec((tb, hp * LANES), lambda b, h: (nb - 1 - b, h))],
        out_specs=[pl.BlockSpec((tb, d4), lambda b, h: (nb - 1 - b, 0)),
                   pl.BlockSpec((3, d), lambda b, h: (0, 0)),
                   pl.BlockSpec((8, LANES), lambda b, h: (0, 0))],
        out_shape=[jax.ShapeDtypeStruct((t, d4), BF16),
                   jax.ShapeDtypeStruct((3, d), F32),
                   jax.ShapeDtypeStruct((8, LANES), F32)],
        scratch_shapes=[pltpu.VMEM((heads, LANES, LANES), F32)],
        compiler_params=_cparams(("arbitrary", "arbitrary")),
    )(hproj, hproj, hproj, hproj, alb, nw, states, dy)


def _conv_silu(xm, xh, w):
    acc = w[CONV_K - 1] * xm
    for j in range(CONV_K - 1):
        acc = acc + w[j] * _shift_rows(xm, xh, CONV_K - 1 - j)
    return _silu(acc)


def _l2norm(t):
    return t * lax.rsqrt(jnp.sum(t * t, axis=1, keepdims=True) + L2_EPS)


def _gdn_chunk(head, hh, qm, qh, km, kh, vm, vh, z, tail, wq, wk, wv, alog_row, dtb_row, nw, s):
    row, col = _chunk_consts()
    heads_lane = lax.broadcasted_iota(jnp.int32, (1, LANES), 1)
    q = _l2norm(_conv_silu(qm, qh, wq)) * (LANES ** -0.5)
    k = _l2norm(_conv_silu(km, kh, wk))
    v = _conv_silu(vm, vh, wv)
    lane = lax.broadcasted_iota(jnp.int32, (CHUNK, LANES), 1)
    beta = jax.nn.sigmoid(jnp.sum(jnp.where(lane == head, tail, 0.0), axis=1, keepdims=True))
    a_t = jnp.sum(jnp.where(lane == hh + head, tail, 0.0), axis=1, keepdims=True)
    alog = jnp.sum(jnp.where(heads_lane == head, alog_row, 0.0), axis=1, keepdims=True)
    dtb = jnp.sum(jnp.where(heads_lane == head, dtb_row, 0.0), axis=1, keepdims=True)
    g = -jnp.exp(alog) * _softplus(a_t + dtb)
    tril = (col <= row).astype(F32)
    eye = (col == row).astype(F32)
    cum = _sdot(tril, g + jnp.zeros((CHUNK, CHUNK), F32))
    cum_r = _sdot(jnp.ones((CHUNK, CHUNK), F32), cum * eye)
    diff = cum - cum_r
    strict = col < row
    incl = col <= row
    dec_strict = jnp.where(strict, jnp.exp(jnp.where(strict, diff, 0.0)), 0.0)
    dec_incl = jnp.where(incl, jnp.exp(jnp.where(incl, diff, 0.0)), 0.0)
    cum_c = jnp.sum(cum * (col == 0).astype(F32), axis=1, keepdims=True)
    cum_last = jnp.sum(jnp.where(row[:, 0:1] == CHUNK - 1, cum_c, 0.0), axis=0, keepdims=True)
    ecum = jnp.exp(cum_c)
    inv = _tri_inv(-(beta * _bdot(k, k, NT) * dec_strict))
    u0 = _dot3(inv, beta * v)
    w = _dot3(inv, (beta * ecum) * k)
    qk = _bdot(q, k, NT) * dec_incl
    u = u0 - _bdot(w, s, NN)
    o = _bdot(q * ecum, s, NN) + _bdot(qk, u, NN)
    kd = k * jnp.exp(cum_last - cum_c)
    s_new = jnp.exp(cum_last) * s + _bdot(kd, u, TN)
    return _gated_rmsnorm(o, z, nw), s_new


def _gdn_in_specs(tb, heads, hp, rev_nb=None):
    groups = heads // hp

    def bb(b):
        return b if rev_nb is None else rev_nb - 1 - b

    def halo(j):
        return pl.BlockSpec((HALO, hp * LANES),
                            lambda b, h, j=j: (jnp.maximum(bb(b) * (tb // HALO) - 1, 0), j * groups + h))

    def main(j):
        return _head_cols(tb, groups, hp, j, rev_nb=rev_nb)

    return [main(0), halo(0), main(1), halo(1), main(2), halo(2), main(3),
            pl.BlockSpec((tb, LANES), lambda b, h: (bb(b), 0)),
            pl.BlockSpec((CONV_K, 3 * heads * LANES), lambda b, h: (0, 0)),
            pl.BlockSpec((8, LANES), lambda b, h: (0, 0)),
            pl.BlockSpec((1, LANES), lambda b, h: (0, 0))]


def _gdn_chunk_args(c, h, p, blk, heads, q_ref, qh_ref, k_ref, kh_ref, v_ref, vh_ref, z_ref, tail_ref, cw_ref, prm_ref, nw_ref):
    d = heads * LANES
    ln = slice(p * LANES, (p + 1) * LANES)
    rows = pl.ds(pl.multiple_of(c * CHUNK, CHUNK), CHUNK)
    prev = pl.ds(pl.multiple_of(jnp.maximum(c * CHUNK - HALO, 0), HALO), HALO)
    first = c == 0
    live = jnp.where(first & (blk == 0), 0.0, 1.0)

    def halo_of(ref, href):
        return jnp.where(first, href[:, ln], ref[prev, ln]) * live

    def cw(j):
        lanes = pl.ds(pl.multiple_of(j * d + h * LANES, LANES), LANES)
        return tuple(cw_ref[r:r + 1, lanes] for r in range(CONV_K))

    return (q_ref[rows, ln], halo_of(q_ref, qh_ref), k_ref[rows, ln], halo_of(k_ref, kh_ref),
            v_ref[rows, ln], halo_of(v_ref, vh_ref), z_ref[rows, ln], tail_ref[rows, :],
            cw(0), cw(1), cw(2), prm_ref[0:1, :], prm_ref[1:2, :], nw_ref[...])


def gdn_fwd(hmain, tail, conv_w, prm, nw, *, tb, hp):
    t, d4 = hmain.shape
    d = d4 // 4
    heads = d // LANES
    groups = heads // hp
    nb, nc = t // tb, tb // CHUNK

    def body(q_ref, qh_ref, k_ref, kh_ref, v_ref, vh_ref, z_ref, tail_ref, cw_ref, prm_ref, nw_ref,
             y_ref, states_ref, s_ref):
        b, hg = pl.program_id(0), pl.program_id(1)

        @pl.when(b == 0)
        def _():
            for p in range(hp):
                s_ref[hg * hp + p] = jnp.zeros((LANES, LANES), F32)

        def step(c, carry):
            rows = pl.ds(pl.multiple_of(c * CHUNK, CHUNK), CHUNK)
            olds = [s_ref[hg * hp + p] for p in range(hp)]
            args = [_gdn_chunk_args(c, hg * hp + p, p, b, heads, q_ref, qh_ref, k_ref, kh_ref, v_ref, vh_ref, z_ref,
                                    tail_ref, cw_ref, prm_ref, nw_ref) for p in range(hp)]
            outs = [_gdn_chunk(hg * hp + p, heads, *args[p], olds[p]) for p in range(hp)]
            for p in range(hp):
                states_ref[c, p] = olds[p]
                y_ref[rows, p * LANES:(p + 1) * LANES] = outs[p][0].astype(y_ref.dtype)
                s_ref[hg * hp + p] = outs[p][1]
            return carry

        lax.fori_loop(0, nc, step, 0)

    return pl.pallas_call(
        body, name="gdn_fwd", grid=(nb, groups),
        in_specs=_gdn_in_specs(tb, heads, hp),
        out_specs=[pl.BlockSpec((tb, hp * LANES), lambda b, h: (b, h)),
                   pl.BlockSpec((nc, hp, LANES, LANES), lambda b, h: (b, h, 0, 0))],
        out_shape=[jax.ShapeDtypeStruct((t, d), BF16),
                   jax.ShapeDtypeStruct((t // CHUNK, heads, LANES, LANES), F32)],
        scratch_shapes=[pltpu.VMEM((heads, LANES, LANES), F32)],
        compiler_params=_cparams(("arbitrary", "arbitrary")),
    )(hmain, hmain, hmain, hmain, hmain, hmain, hmain, tail, conv_w, prm, nw)


def gdn_bwd(hmain, tail, conv_w, prm, nw, states, dy, *, tb, hp):
    t, d4 = hmain.shape
    d = d4 // 4
    heads = d // LANES
    groups = heads // hp
    nb, nc = t // tb, tb // CHUNK

    def body(q_ref, qh_ref, k_ref, kh_ref, v_ref, vh_ref, z_ref, tail_ref, cw_ref, prm_ref, nw_ref,
             states_ref, dy_ref, dh_ref, dtail_ref, dcw_ref, dprm_ref, ds_ref, pend_ref):
        b, hg = pl.program_id(0), pl.program_id(1)
        blk = nb - 1 - b

        @pl.when(b == 0)
        def _():
            for p in range(hp):
                ds_ref[hg * hp + p] = jnp.zeros((LANES, LANES), F32)
                pend_ref[hg * hp + p] = jnp.zeros((3, HALO, LANES), F32)

        @pl.when((b == 0) & (hg == 0))
        def _():
            dcw_ref[...] = jnp.zeros_like(dcw_ref)
            dprm_ref[...] = jnp.zeros_like(dprm_ref)

        @pl.when(hg == 0)
        def _():
            dtail_ref[...] = jnp.zeros_like(dtail_ref)

        def step(i, carry):
            c = nc - 1 - i
            rows = pl.ds(pl.multiple_of(c * CHUNK, CHUNK), CHUNK)
            zpad = jnp.zeros((CHUNK - HALO, LANES), F32)
            grads, pends = [], []
            for p in range(hp):
                h = hg * hp + p
                args = _gdn_chunk_args(c, h, p, blk, heads, q_ref, qh_ref, k_ref, kh_ref, v_ref, vh_ref, z_ref,
                                       tail_ref, cw_ref, prm_ref, nw_ref)
                cts = (dy_ref[rows, p * LANES:(p + 1) * LANES].astype(F32), ds_ref[h])
                pends.append(pend_ref[h])
                _, vjp = jax.vjp(functools.partial(_gdn_chunk, h, heads), *args, states_ref[c, p])
                grads.append(vjp(cts))
            for p in range(hp):
                h = hg * hp + p
                (dqm, dqh, dkm, dkh, dvm, dvh, dz, dtl, dwq, dwk, dwv, dalog, ddtb, dnw, ds) = grads[p]
                pend = pends[p]
                for j, (dm, dhalo) in enumerate(((dqm, dqh), (dkm, dkh), (dvm, dvh))):
                    full = dm + jnp.concatenate([zpad, pend[j]], axis=0)
                    dh_ref[rows, pl.ds(pl.multiple_of(j * d + h * LANES, LANES), LANES)] = full.astype(dh_ref.dtype)
                    pend_ref[h, j] = dhalo
                dh_ref[rows, pl.ds(pl.multiple_of(3 * d + h * LANES, LANES), LANES)] = dz.astype(dh_ref.dtype)
                dtail_ref[rows, :] += dtl
                for j, dw in enumerate((dwq, dwk, dwv)):
                    lanes = pl.ds(pl.multiple_of(j * d + h * LANES, LANES), LANES)
                    for r in range(CONV_K):
                        dcw_ref[r:r + 1, lanes] += dw[r]
                dprm_ref[0:1, :] += dalog
                dprm_ref[1:2, :] += ddtb
                dprm_ref[2:3, :] += dnw
                ds_ref[h] = ds
            return carry

        lax.fori_loop(0, nc, step, 0)

    return pl.pallas_call(
        body, name="gdn_bwd", grid=(nb, groups),
        in_specs=_gdn_in_specs(tb, heads, hp, rev_nb=nb) + [
            pl.BlockSpec((nc, hp, LANES, LANES), lambda b, h: (nb - 1 - b, h, 0, 0)),
            pl.BlockSpec((tb, hp * LANES), lambda b, h: (nb - 1 - b, h))],
        out_specs=[pl.BlockSpec((tb, d4), lambda b, h: (nb - 1 - b, 0)),
                   pl.BlockSpec((tb, LANES), lambda b, h: (nb - 1 - b, 0)),
                   pl.BlockSpec((CONV_K, 3 * d), lambda b, h: (0, 0)),
                   pl.BlockSpec((8, LANES), lambda b, h: (0, 0))],
        out_shape=[jax.ShapeDtypeStruct((t, d4), BF16),
                   jax.ShapeDtypeStruct((t, LANES), F32),
                   jax.ShapeDtypeStruct((CONV_K, 3 * d), F32),
                   jax.ShapeDtypeStruct((8, LANES), F32)],
        scratch_shapes=[pltpu.VMEM((heads, LANES, LANES), F32),
                        pltpu.VMEM((heads, 3, HALO, LANES), F32)],
        compiler_params=_cparams(("arbitrary", "arbitrary")),
    )(hmain, hmain, hmain, hmain, hmain, hmain, hmain, tail, conv_w, prm, nw, states, dy)


def _layer_norm(u, g, b):
    mu = jnp.mean(u, axis=1, keepdims=True)
    var = jnp.mean(jnp.square(u - mu), axis=1, keepdims=True)
    return (u - mu) * lax.rsqrt(var + LN_EPS) * g + b


def ln_fwd(u, g, b, *, tr):
    t, d = u.shape

    def body(u_ref, g_ref, b_ref, x_ref, xb_ref):
        x = _layer_norm(u_ref[...], g_ref[...], b_ref[...])
        x_ref[...] = x
        xb_ref[...] = x.astype(BF16)

    row = pl.BlockSpec((tr, d), lambda i: (i, 0))
    vec = pl.BlockSpec((1, d), lambda i: (0, 0))
    return pl.pallas_call(
        body, name="ln_fwd", grid=(t // tr,), in_specs=[row, vec, vec], out_specs=[row, row],
        out_shape=[jax.ShapeDtypeStruct((t, d), F32), jax.ShapeDtypeStruct((t, d), BF16)],
        compiler_params=_cparams(("parallel",)),
    )(u, g, b)


def ln_bwd(u, g, b, dout, *, tr):
    t, d = u.shape

    def body(u_ref, g_ref, b_ref, dout_ref, du_ref, dub_ref, dg_ref, db_ref):
        @pl.when(pl.program_id(0) == 0)
        def _():
            dg_ref[...] = jnp.zeros_like(dg_ref)
            db_ref[...] = jnp.zeros_like(db_ref)

        _, vjp = jax.vjp(_layer_norm, u_ref[...], g_ref[...], b_ref[...])
        du, dg, db = vjp(dout_ref[...])
        du_ref[...] = du
        dub_ref[...] = du.astype(BF16)
        dg_ref[0:1, :] += dg
        db_ref[0:1, :] += db

    row = pl.BlockSpec((tr, d), lambda i: (i, 0))
    vec = pl.BlockSpec((1, d), lambda i: (0, 0))
    acc = pl.BlockSpec((8, d), lambda i: (0, 0))
    return pl.pallas_call(
        body, name="ln_bwd", grid=(t // tr,), in_specs=[row, vec, vec, row], out_specs=[row, row, acc, acc],
        out_shape=[jax.ShapeDtypeStruct((t, d), F32), jax.ShapeDtypeStruct((t, d), BF16),
                   jax.ShapeDtypeStruct((8, d), F32), jax.ShapeDtypeStruct((8, d), F32)],
        compiler_params=_cparams(("arbitrary",)),
    )(u, g, b, dout)


def ln_loss_bwd(u, g, b, target, *, tr):
    t, d = u.shape

    def loss_of(uu, gg, bb, tgt):
        err = jnp.square(_layer_norm(uu, gg, bb) - tgt)
        return 0.5 * jnp.sum(jnp.mean(err, axis=1, keepdims=True), axis=0, keepdims=True)

    def body(u_ref, g_ref, b_ref, t_ref, loss_ref, du_ref, dub_ref, dg_ref, db_ref):
        @pl.when(pl.program_id(0) == 0)
        def _():
            loss_ref[...] = jnp.zeros_like(loss_ref)
            dg_ref[...] = jnp.zeros_like(dg_ref)
            db_ref[...] = jnp.zeros_like(db_ref)

        tgt = t_ref[...]
        val, vjp = jax.vjp(lambda uu, gg, bb: loss_of(uu, gg, bb, tgt), u_ref[...], g_ref[...], b_ref[...])
        du, dg, db = vjp(jnp.ones((1, 1), F32))
        loss_ref[...] += val
        du_ref[...] = du
        dub_ref[...] = du.astype(BF16)
        dg_ref[0:1, :] += dg
        db_ref[0:1, :] += db

    row = pl.BlockSpec((tr, d), lambda i: (i, 0))
    vec = pl.BlockSpec((1, d), lambda i: (0, 0))
    acc = pl.BlockSpec((8, d), lambda i: (0, 0))
    return pl.pallas_call(
        body, name="ln_loss_bwd", grid=(t // tr,), in_specs=[row, vec, vec, row],
        out_specs=[pl.BlockSpec((8, LANES), lambda i: (0, 0)), row, row, acc, acc],
        out_shape=[jax.ShapeDtypeStruct((8, LANES), F32),
                   jax.ShapeDtypeStruct((t, d), F32), jax.ShapeDtypeStruct((t, d), BF16),
                   jax.ShapeDtypeStruct((8, d), F32), jax.ShapeDtypeStruct((8, d), F32)],
        compiler_params=_cparams(("arbitrary",)),
    )(u, g, b, target)


def local_step(x, target, wa_in, alb, a_nw, wa_out, wb_main, wb_tail, conv_w, a_log, dt_bias, b_nw, wb_out,
               ln_g, ln_b, *, tb=256, tr=256, hp=HEADS_PER_STEP):
    t, d = x.shape
    heads = d // LANES
    tb, tr = min(tb, t), min(tr, t)
    xb = x.astype(BF16)
    prm = jnp.zeros((8, LANES), F32).at[0, :heads].set(a_log[0]).at[1, :heads].set(dt_bias[0])

    ha = matmul(xb, wa_in, name="mm_a_in")
    ya, st_a = hgrn2_fwd(ha, alb, a_nw, tb=tb, hp=hp)
    u1 = matmul(ya, wa_out, name="mm_a_out", addend=x, alpha=DEEPNORM_ALPHA)
    x1, x1b = ln_fwd(u1, ln_g[0:1], ln_b[0:1], tr=tr)
    hb = matmul(x1b, wb_main, name="mm_b_in")
    tl = matmul(x1b, wb_tail, name="mm_b_tail")
    yb, st_b = gdn_fwd(hb, tl, conv_w, prm, b_nw, tb=tb, hp=hp)
    u2 = matmul(yb, wb_out, name="mm_b_out", addend=x1, alpha=DEEPNORM_ALPHA)

    loss, du2, du2b, dg2, db2 = ln_loss_bwd(u2, ln_g[1:2], ln_b[1:2], target, tr=tr)
    d_wb_out = matmul(yb.T, du2b, name="mm_dwb_out", out_dtype=BF16)
    dyb = matmul(du2b, wb_out, name="mm_dyb", nt=True)
    dhb, dtl, d_conv, dprm = gdn_bwd(hb, tl, conv_w, prm, b_nw, st_b, dyb, tb=tb, hp=hp)
    dtlb = dtl.astype(BF16)
    x1t = x1b.T
    d_wb_main = matmul(x1t, dhb, name="mm_dwb_main", out_dtype=BF16)
    d_wb_tail = matmul(x1t, dtlb, name="mm_dwb_tail", out_dtype=BF16)
    dx1_tail = matmul(dtlb, wb_tail, name="mm_dx1_tail", nt=True, addend=du2, alpha=DEEPNORM_ALPHA)
    dx1 = matmul(dhb, wb_main, name="mm_dx1", nt=True, addend=dx1_tail, alpha=1.0)
    du1, du1b, dg1, db1 = ln_bwd(u1, ln_g[0:1], ln_b[0:1], dx1, tr=tr)
    d_wa_out = matmul(ya.T, du1b, name="mm_dwa_out", out_dtype=BF16)
    dya = matmul(du1b, wa_out, name="mm_dya", nt=True)
    dha, d_alb, d_anw = hgrn2_bwd(ha, alb, a_nw, st_a, dya, tb=tb, hp=hp)
    d_wa_in = matmul(xb.T, dha, name="mm_dwa_in", out_dtype=BF16, out_split=wa_in.shape[0] if wa_in.ndim == 3 else 1)
    grad_x = matmul(dha, wa_in, name="mm_dx", nt=True, addend=du1, alpha=DEEPNORM_ALPHA)

    small = dict(
        a_lower_bounds=d_alb, a_norm_w=d_anw[0:1], b_conv_w=d_conv,
        b_a_log=dprm[0:1, :heads], b_dt_bias=dprm[1:2, :heads], b_norm_w=dprm[2:3],
        ln_g=jnp.concatenate([dg1[0:1], dg2[0:1]], axis=0), ln_b=jnp.concatenate([db1[0:1], db2[0:1]], axis=0))
    big = dict(a_w_in=d_wa_in, a_w_out=d_wa_out, b_w_main=d_wb_main, b_w_tail=d_wb_tail, b_w_out=d_wb_out)
    return loss, grad_x, big, small


MESH_ID = pl.DeviceIdType.MESH


def _place():
    return lax.axis_index("x"), lax.axis_index("y"), lax.axis_index("c")


def _index_of(p):
    return 4 * p[0] + 2 * p[1] + p[2]


def all_gather(shards, *, name, space):
    n = len(shards)

    def body(*refs):
        ins, outs = refs[:n], refs[n:2 * n]
        send_sems, recv_sems, local_sems = refs[2 * n:]
        x, y, c = _place()
        me, sibling = (x, y, c), (x, y, 1 - c)
        chips = [(1 - x, y), (x, 1 - y), (1 - x, 1 - y)]

        def copy(a, k, block, to, own=False):
            dst = outs[a].at[_index_of(block)]
            return pltpu.make_async_remote_copy(
                src_ref=ins[a] if own else dst, dst_ref=dst,
                send_sem=send_sems.at[7 * a + k], recv_sem=recv_sems.at[7 * a + k],
                device_id=to, device_id_type=MESH_ID)

        mine = [pltpu.make_async_copy(ins[a], outs[a].at[_index_of(me)], local_sems.at[a]) for a in range(n)]
        for cp in mine:
            cp.start()
        first = []
        for a in range(n):
            first.append(copy(a, 0, me, sibling, own=True))
            first += [copy(a, 1 + j, me, (*chip, c), own=True) for j, chip in enumerate(chips)]
        for cp in first:
            cp.start()
        passed = []
        for j, chip in enumerate(chips):
            for a in range(n):
                copy(a, 1 + j, (*chip, c), me).wait_recv()
                fwd = copy(a, 4 + j, (*chip, c), sibling)
                fwd.start()
                passed.append(fwd)
        for a in range(n):
            copy(a, 0, sibling, me).wait_recv()
            for j, chip in enumerate(chips):
                copy(a, 4 + j, (*chip, 1 - c), me).wait_recv()
        for cp in first + passed:
            cp.wait_send()
        for cp in mine:
            cp.wait()

    spec = pl.BlockSpec(memory_space=space)
    return pl.pallas_call(
        body, name=name,
        in_specs=[spec] * n, out_specs=[spec] * n,
        out_shape=[jax.ShapeDtypeStruct((N_DEV, *s.shape), s.dtype) for s in shards],
        scratch_shapes=[pltpu.SemaphoreType.DMA((7 * n,)), pltpu.SemaphoreType.DMA((7 * n,)),
                        pltpu.SemaphoreType.DMA((n,))],
        compiler_params=pltpu.CompilerParams(vmem_limit_bytes=VMEM_LIMIT),
    )(*shards)


def exchange(parts, *, name):
    n = len(parts)

    def body(*refs):
        ins, outs = refs[:n], refs[n:2 * n]
        send_sems, recv_sems, local_sems = refs[2 * n:]
        x, y, c = _place()
        me = (x, y, c)
        copies = []
        for a in range(n):
            own = pltpu.make_async_copy(ins[a].at[_index_of(me)], outs[a].at[_index_of(me)], local_sems.at[a])
            own.start()
            copies.append(own)
            for r in range(1, N_DEV):
                fx, fy, fc = (r >> 2) & 1, (r >> 1) & 1, r & 1
                peer = (x ^ fx, y ^ fy, c ^ fc)
                cp = pltpu.make_async_remote_copy(
                    src_ref=ins[a].at[_index_of(peer)], dst_ref=outs[a].at[_index_of(me)],
                    send_sem=send_sems.at[7 * a + r - 1], recv_sem=recv_sems.at[7 * a + r - 1],
                    device_id=peer, device_id_type=MESH_ID)
                cp.start()
                copies.append(cp)
        for cp in copies:
            cp.wait()

    spec = pl.BlockSpec(memory_space=pltpu.HBM)
    return pl.pallas_call(
        body, name=name,
        in_specs=[spec] * n, out_specs=[spec] * n,
        out_shape=[jax.ShapeDtypeStruct(p.shape, p.dtype) for p in parts],
        scratch_shapes=[pltpu.SemaphoreType.DMA((7 * n,)), pltpu.SemaphoreType.DMA((7 * n,)),
                        pltpu.SemaphoreType.DMA((n,))],
        compiler_params=pltpu.CompilerParams(vmem_limit_bytes=VMEM_LIMIT),
    )(*parts)


def adamw(parts, w, m, v, *, name, tr=64):
    p, r, c = parts.shape
    tr = min(tr, r)
    assert r % tr == 0
    c1 = 1.0 / (1.0 - ADAM_B1 ** ADAM_STEP)
    c2 = 1.0 / (1.0 - ADAM_B2 ** ADAM_STEP)

    def body(p_ref, w_ref, m_ref, v_ref, g_ref, d_ref, nm_ref, nv_ref):
        g = p_ref[0].astype(F32)
        for i in range(1, p):
            g = g + p_ref[i].astype(F32)
        nm = ADAM_B1 * m_ref[...] + (1.0 - ADAM_B1) * g
        nv = ADAM_B2 * v_ref[...] + (1.0 - ADAM_B2) * jnp.square(g)
        g_ref[...] = g
        nm_ref[...] = nm
        nv_ref[...] = nv
        d_ref[...] = -ADAM_LR * ((nm * c1) / (jnp.sqrt(nv * c2) + ADAM_EPS) + ADAM_WD * w_ref[...])

    blk = pl.BlockSpec((tr, c), lambda i: (i, 0))
    out = jax.ShapeDtypeStruct((r, c), F32)
    return pl.pallas_call(
        body, name=name, grid=(r // tr,),
        in_specs=[pl.BlockSpec((p, tr, c), lambda i: (0, i, 0)), blk, blk, blk],
        out_specs=[blk] * 4, out_shape=[out] * 4,
        compiler_params=_cparams(("parallel",)),
    )(parts, w, m, v)


def _pack(d, vals):
    heads = d // LANES
    vecs = jnp.zeros((8, LANES), F32)
    vecs = vecs.at[0:1].set(vals["a_norm_w"]).at[1:2, :heads].set(vals["b_a_log"])
    vecs = vecs.at[2:3, :heads].set(vals["b_dt_bias"]).at[3:4].set(vals["b_norm_w"])
    rows = [vals["a_lower_bounds"].reshape(-1, LANES), vals["ln_g"].reshape(-1, LANES),
            vals["ln_b"].reshape(-1, LANES), vecs]
    return jnp.concatenate(rows, axis=0)


def _unpack(d, packed):
    heads = d // LANES
    n3, n2 = 3 * heads, 2 * heads
    o = 0
    out = {}
    out["a_lower_bounds"] = packed[o:o + n3].reshape(3, d); o += n3
    out["ln_g"] = packed[o:o + n2].reshape(2, d); o += n2
    out["ln_b"] = packed[o:o + n2].reshape(2, d); o += n2
    out["a_norm_w"] = packed[o:o + 1]
    out["b_a_log"] = packed[o + 1:o + 2, :heads]
    out["b_dt_bias"] = packed[o + 2:o + 3, :heads]
    out["b_norm_w"] = packed[o + 3:o + 4]
    return out


ORDER = ("a_w_in", "a_lower_bounds", "a_norm_w", "a_w_out", "b_w_in", "b_conv_w", "b_a_log", "b_dt_bias", "b_norm_w",
         "b_w_out", "ln_g", "ln_b")


def kernel(x, a_w_in, a_lower_bounds, a_norm_w, a_w_out, b_w_in, b_conv_w, b_a_log, b_dt_bias, b_norm_w, b_w_out, ln_g, ln_b, loss_target, m_a_w_in, m_a_lower_bounds, m_a_norm_w, m_a_w_out, m_b_w_in, m_b_conv_w, m_b_a_log, m_b_dt_bias, m_b_norm_w, m_b_w_out, m_ln_g, m_ln_b, v_a_w_in, v_a_lower_bounds, v_a_norm_w, v_a_w_out, v_b_w_in, v_b_conv_w, v_b_a_log, v_b_dt_bias, v_b_norm_w, v_b_w_out, v_ln_g, v_ln_b):
    w = dict(a_w_in=a_w_in, a_lower_bounds=a_lower_bounds, a_norm_w=a_norm_w, a_w_out=a_w_out, b_w_in=b_w_in,
             b_conv_w=b_conv_w, b_a_log=b_a_log, b_dt_bias=b_dt_bias, b_norm_w=b_norm_w, b_w_out=b_w_out, ln_g=ln_g, ln_b=ln_b)
    m = dict(a_w_in=m_a_w_in, a_lower_bounds=m_a_lower_bounds, a_norm_w=m_a_norm_w, a_w_out=m_a_w_out, b_w_in=m_b_w_in,
             b_conv_w=m_b_conv_w, b_a_log=m_b_a_log, b_dt_bias=m_b_dt_bias, b_norm_w=m_b_norm_w, b_w_out=m_b_w_out,
             ln_g=m_ln_g, ln_b=m_ln_b)
    v = dict(a_w_in=v_a_w_in, a_lower_bounds=v_a_lower_bounds, a_norm_w=v_a_norm_w, a_w_out=v_a_w_out, b_w_in=v_b_w_in,
             b_conv_w=v_b_conv_w, b_a_log=v_b_a_log, b_dt_bias=v_b_dt_bias, b_norm_w=v_b_norm_w, b_w_out=v_b_w_out,
             ln_g=v_ln_g, ln_b=v_ln_b)
    t, d = x.shape[1], x.shape[2]
    heads = d // LANES
    n_tail = 2 * heads
    me = _index_of(_place())

    ga_in, ga_out, gb_in, gb_out, g_conv = all_gather(
        [a_w_in[0].astype(BF16), a_w_out[0].astype(BF16), b_w_in[0].astype(BF16), b_w_out[0].astype(BF16), b_conv_w[0]],
        name="gather_weights", space=pltpu.HBM)
    wa_out = ga_out.reshape(d, d)
    wb_out = gb_out.reshape(d, d)
    wb_full = jnp.transpose(gb_in, (1, 0, 2)).reshape(d, 4 * d + n_tail)
    wb_main = wb_full[:, :4 * d]
    wb_tail = jnp.concatenate([wb_full[:, 4 * d:], jnp.zeros((d, LANES - n_tail), BF16)], axis=1)
    conv_w = jnp.transpose(g_conv, (1, 0, 2)).reshape(CONV_K, 3 * d)

    loss, grad_x, big, small = local_step(
        x[0], loss_target[0], ga_in, a_lower_bounds, a_norm_w, wa_out, wb_main, wb_tail, conv_w, b_a_log, b_dt_bias,
        b_norm_w, wb_out, ln_g, ln_b)
    loss = lax.psum(loss[0, 0], ("x", "y", "c"))

    conv_rows = small["b_conv_w"].reshape(-1, LANES)
    n_conv = conv_rows.shape[0]
    sent = jnp.concatenate([conv_rows, _pack(d, small)], axis=0)
    (got,) = all_gather([sent], name="gather_small", space=pltpu.VMEM)
    res = {}
    packed = adamw(got[:, n_conv:], _pack(d, w), _pack(d, m), _pack(d, v), name="adamw_small", tr=4096)
    for k, vals in zip(("grad", "delta", "new_m", "new_v"), packed):
        res[k] = _unpack(d, vals)
    shard_ch = 3 * d // N_DEV
    conv_parts = lax.dynamic_slice_in_dim(got[:, :n_conv].reshape(N_DEV, CONV_K, 3 * d), me * shard_ch, shard_ch, axis=2)
    conv_out = adamw(conv_parts, b_conv_w[0], m_b_conv_w[0], v_b_conv_w[0], name="adamw_conv")
    for k, vals in zip(("grad", "delta", "new_m", "new_v"), conv_out):
        res[k]["b_conv_w"] = vals[None]

    d_b_in = jnp.concatenate([big["b_w_main"], big["b_w_tail"][:, :n_tail]], axis=1)
    d_b_in = jnp.transpose(d_b_in.reshape(d, N_DEV, -1), (1, 0, 2))
    parts = [big["a_w_in"], big["a_w_out"].reshape(N_DEV, d // N_DEV, d), d_b_in, big["b_w_out"].reshape(N_DEV, d // N_DEV, d)]
    recv = exchange(parts, name="exchange_grads")
    for name, got_parts in zip(("a_w_in", "a_w_out", "b_w_in", "b_w_out"), recv):
        outs = adamw(got_parts, w[name][0], m[name][0], v[name][0], name="adamw_" + name)
        for k, vals in zip(("grad", "delta", "new_m", "new_v"), outs):
            res[k][name] = vals[None]

    return (loss, grad_x[None], *[res["grad"][k] for k in ORDER], *[res["delta"][k] for k in ORDER],
            *[res["new_m"][k] for k in ORDER], *[res["new_v"][k] for k in ORDER])
```

```python
import functools
import math

import jax
import jax.numpy as jnp
from jax import lax
from jax.experimental import pallas as pl
from jax.experimental.pallas import tpu as pltpu

F32 = jnp.float32
BF16 = jnp.bfloat16

N_DEV = 8
LANES = 128
CHUNK = 64
SUB = 16
HALO = 8
CONV_K = 4
DEPTH = 2
DEEPNORM_ALPHA = (2.0 * DEPTH) ** 0.25
LN_EPS = 1e-5
RMS_EPS = 1e-6
L2_EPS = 1e-6
EXP_CLAMP = 60.0
ADAM_LR = 0.001
ADAM_B1 = 0.9
ADAM_B2 = 0.999
ADAM_EPS = 1e-08
ADAM_WD = 0.01
ADAM_STEP = 10
VMEM_LIMIT = 56 * 1024 * 1024
HEADS_PER_STEP = 4

NN = ((1,), (0,))
NT = ((1,), (1,))
TN = ((0,), (0,))


def _dg(a, b, dims, precision=None):
    return lax.dot_general(a, b, (dims, ((), ())), precision=precision, preferred_element_type=F32)


def _bdot(a, b, dims):
    return _dg(a.astype(BF16), b.astype(BF16), dims)


def _split2(t):
    hi = t.astype(BF16)
    return hi, (t - hi.astype(F32)).astype(BF16)


def _split3(t):
    hi = t.astype(BF16)
    r = t - hi.astype(F32)
    mid = r.astype(BF16)
    return hi, mid, (r - mid.astype(F32)).astype(BF16)


@jax.custom_vjp
def _sdot(sel, t):
    sel = sel.astype(BF16)
    hi, mid, lo = _split3(t)
    return (_dg(sel, lo, NN) + _dg(sel, mid, NN)) + _dg(sel, hi, NN)


def _sdot_fwd(sel, t):
    return _sdot(sel, t), sel


def _sdot_bwd(sel, ct):
    selb = sel.astype(BF16)
    hi, mid, lo = _split3(ct)
    return jnp.zeros_like(sel), (_dg(selb, lo, TN) + _dg(selb, mid, TN)) + _dg(selb, hi, TN)


_sdot.defvjp(_sdot_fwd, _sdot_bwd)


def _dot3_raw(a, b, dims):
    ah, al = _split2(a)
    bh, bl = _split2(b)
    return (_dg(al, bh, dims) + _dg(ah, bl, dims)) + _dg(ah, bh, dims)


@jax.custom_vjp
def _dot3(a, b):
    return _dot3_raw(a, b, NN)


def _dot3_fwd(a, b):
    return _dot3_raw(a, b, NN), (a, b)


def _dot3_bwd(res, ct):
    a, b = res
    return _bdot(ct, b, NT), _bdot(a, ct, TN)


_dot3.defvjp(_dot3_fwd, _dot3_bwd)


@jax.custom_vjp
def _tri_inv(m):
    n = m.shape[0]
    row = lax.broadcasted_iota(jnp.int32, (n, n), 0)
    col = lax.broadcasted_iota(jnp.int32, (n, n), 1)
    inv = (col == row).astype(F32) + m
    mp = m
    for _ in range(int(math.log2(CHUNK)) - 1):
        mp = _dot3_raw(mp, mp, NN)
        inv = inv + _dot3_raw(inv, mp, NN)
    return inv


def _tri_inv_fwd(m):
    inv = _tri_inv(m)
    return inv, inv


def _tri_inv_bwd(inv, ct):
    return (_dot3_raw(_dot3_raw(inv, ct, TN), inv, NT),)


_tri_inv.defvjp(_tri_inv_fwd, _tri_inv_bwd)


@functools.partial(jax.custom_vjp, nondiff_argnums=(2,))
def _shift_rows(xm, xh, back):
    r = pltpu.roll(xm, back, 0)
    row = lax.broadcasted_iota(jnp.int32, xh.shape, 0)
    top = jnp.where(row < back, pltpu.roll(xh, back, 0), r[0:HALO])
    return jnp.concatenate([top, r[HALO:]], axis=0)


def _shift_rows_fwd(xm, xh, back):
    return _shift_rows(xm, xh, back), None


def _shift_rows_bwd(back, _, ct):
    row = lax.broadcasted_iota(jnp.int32, ct.shape, 0)
    dxm = jnp.where(row < CHUNK - back, pltpu.roll(ct, CHUNK - back, 0), 0.0)
    rowh = lax.broadcasted_iota(jnp.int32, (HALO, ct.shape[1]), 0)
    dxh = jnp.where(rowh >= HALO - back, pltpu.roll(ct[0:HALO], HALO - back, 0), 0.0)
    return dxm, dxh


_shift_rows.defvjp(_shift_rows_fwd, _shift_rows_bwd)


def _silu(t):
    return t * jax.nn.sigmoid(t)


def _softplus(t):
    return jnp.where(t > 20.0, t, jnp.log1p(jnp.exp(jnp.minimum(t, 20.0))))


def _cparams(sem=None):
    kw = dict(vmem_limit_bytes=VMEM_LIMIT)
    if sem is not None:
        kw["dimension_semantics"] = sem
    return pltpu.CompilerParams(**kw)


def matmul(a, b, *, name, nt=False, addend=None, alpha=1.0, out_dtype=F32, out_split=1, tm=1024, tn=1024, tk=512):
    m, k = a.shape
    b_split = b.shape[0] if b.ndim == 3 else 1
    b_rows, b_cols = b.shape[-2], b.shape[-1] * b_split
    n = b_rows if nt else b_cols
    tm, tn, tk = min(tm, m), min(tn, n), min(tk, k)
    if b_split > 1:
        part = b_cols // b_split
        tn, tk = (tn, min(tk, part)) if nt else (min(tn, part), tk)
    if out_split > 1:
        tn = min(tn, n // out_split)
    assert m % tm == 0 and n % tn == 0 and k % tk == 0, (a.shape, b.shape, nt)
    nk = k // tk
    dims = NT if nt else NN

    def body(*refs):
        if addend is None:
            a_ref, b_ref, o_ref, acc_ref = refs
            add_ref = None
        else:
            a_ref, b_ref, add_ref, o_ref, acc_ref = refs
        kk = pl.program_id(2)

        @pl.when(kk == 0)
        def _():
            acc_ref[...] = jnp.zeros_like(acc_ref)

        acc_ref[...] += _dg(a_ref[...], b_ref[...], dims)

        @pl.when(kk == nk - 1)
        def _():
            r = acc_ref[...]
            if add_ref is not None:
                r = r + alpha * add_ref[...].astype(F32)
            o_ref[...] = r.astype(o_ref.dtype)

    if b_split == 1:
        b_spec = (pl.BlockSpec((tn, tk), lambda i, j, kk: (j, kk)) if nt
                  else pl.BlockSpec((tk, tn), lambda i, j, kk: (kk, j)))
    elif nt:
        per = (b_cols // b_split) // tk
        b_spec = pl.BlockSpec((None, tn, tk), lambda i, j, kk: (kk // per, j, kk % per))
    else:
        per = (b_cols // b_split) // tn
        b_spec = pl.BlockSpec((None, tk, tn), lambda i, j, kk: (j // per, kk, j % per))
    in_specs = [pl.BlockSpec((tm, tk), lambda i, j, kk: (i, kk)), b_spec]
    args = [a, b]
    if addend is not None:
        in_specs.append(pl.BlockSpec((tm, tn), lambda i, j, kk: (i, j)))
        args.append(addend)
    if out_split == 1:
        out_spec = pl.BlockSpec((tm, tn), lambda i, j, kk: (i, j))
        out_shape = jax.ShapeDtypeStruct((m, n), out_dtype)
    else:
        per_o = (n // out_split) // tn
        out_spec = pl.BlockSpec((None, tm, tn), lambda i, j, kk: (j // per_o, i, j % per_o))
        out_shape = jax.ShapeDtypeStruct((out_split, m, n // out_split), out_dtype)
    return pl.pallas_call(
        body, name=name, grid=(m // tm, n // tn, nk),
        in_specs=in_specs,
        out_specs=out_spec,
        out_shape=out_shape,
        scratch_shapes=[pltpu.VMEM((tm, tn), F32)],
        compiler_params=_cparams(("parallel", "parallel", "arbitrary")),
    )(*args)


def _gated_rmsnorm(o, z, nw):
    r = lax.rsqrt(jnp.mean(o * o, axis=1, keepdims=True) + RMS_EPS)
    return o * r * nw * _silu(z)


def _chunk_consts():
    row = lax.broadcasted_iota(jnp.int32, (CHUNK, CHUNK), 0)
    col = lax.broadcasted_iota(jnp.int32, (CHUNK, CHUNK), 1)
    return row, col


def _head_cols(tb, groups, hp, j, rev_nb=None):
    def bb(b):
        return b if rev_nb is None else rev_nb - 1 - b
    return pl.BlockSpec((tb, hp * LANES), lambda b, h: (bb(b), j * groups + h))


def _stack(pieces):
    return jnp.concatenate(pieces, axis=0) if len(pieces) > 1 else pieces[0]


def _place(t, g):
    if g == 1:
        return t
    head = lax.broadcasted_iota(jnp.int32, t.shape, 0) // CHUNK
    return jnp.concatenate([jnp.where(head == h, t, 0.0) for h in range(g)], axis=1)


def _hgrn2_chunk(qr, fr, iv, z, a0, a1, a2, nw, st):
    g = len(qr)
    n = g * CHUNK
    row = lax.broadcasted_iota(jnp.int32, (n, n), 0)
    col = lax.broadcasted_iota(jnp.int32, (n, n), 1)
    same = (row // CHUNK) == (col // CHUNK)
    qs, ks, gl = [], [], []
    for h in range(g):
        amax = jnp.maximum(jnp.maximum(a0[h], a1[h]), a2[h])
        e0, e1, e2 = jnp.exp(a0[h] - amax), jnp.exp(a1[h] - amax), jnp.exp(a2[h] - amax)
        lb = e0 / (e0 + e1 + e2)
        forget = lb + (1.0 - lb) * jax.nn.sigmoid(fr[h])
        qs.append(_silu(qr[h]))
        ks.append(1.0 - forget)
        gl.append(jnp.log(forget))
    q, k, glog, v, zs = _stack(qs), _stack(ks), _stack(gl), _stack(list(iv)), _stack(list(z))
    cum = _sdot((same & (col <= row)).astype(F32), glog)
    sub = (row % CHUNK) // SUB
    rowl = lax.broadcasted_iota(jnp.int32, (n, LANES), 0)
    headl, subl = rowl // CHUNK, (rowl % CHUNK) // SUB

    def row_of(r):
        picks = [jnp.sum(jnp.where(rowl == h * CHUNK + r, cum, 0.0), axis=0, keepdims=True) for h in range(g)]
        out = jnp.zeros((n, LANES), F32)
        for h in range(g):
            out = out + jnp.where(headl == h, picks[h], 0.0)
        return out, picks

    refs = [jnp.zeros((n, LANES), F32)] + [row_of(i * SUB - 1)[0] for i in range(1, CHUNK // SUB)]
    own = jnp.zeros((n, LANES), F32)
    for i, ref in enumerate(refs):
        own = own + jnp.where(subl == i, ref, 0.0)
    qt = q * jnp.exp(cum - own)
    att = jnp.zeros((n, n), F32)
    for i, ref in enumerate(refs):
        kt = k * jnp.exp(jnp.minimum(ref - cum, EXP_CLAMP))
        att = att + jnp.where(same & (sub == i) & (col <= row), _bdot(qt, kt, NT), 0.0)
    cl_rows, lasts = row_of(CHUNK - 1)
    o = _bdot(att, v, NN) + _bdot(_place(q * jnp.exp(cum), g), st, NT)
    kd = k * jnp.exp(cl_rows - cum)
    cl_wide = jnp.concatenate(lasts, axis=1) if g > 1 else lasts[0]
    st_new = st * jnp.exp(cl_wide) + _bdot(v, _place(kd, g), TN)
    return _gated_rmsnorm(o, zs, nw), st_new


def hgrn2_fwd(hproj, alb, nw, *, tb, hp):
    t, d4 = hproj.shape
    d = d4 // 4
    heads = d // LANES
    groups = heads // hp
    nb, nc = t // tb, tb // CHUNK

    def body(q_ref, f_ref, i_ref, z_ref, alb_ref, nw_ref, y_ref, states_ref, st_ref):
        b, hg = pl.program_id(0), pl.program_id(1)

        @pl.when(b == 0)
        def _():
            st_ref[hg] = jnp.zeros((LANES, hp * LANES), F32)

        nwv = nw_ref[...]
        lns = [slice(p * LANES, (p + 1) * LANES) for p in range(hp)]
        albs = []
        for r in range(3):
            albs.append(tuple(alb_ref[r:r + 1, pl.ds(pl.multiple_of((hg * hp + p) * LANES, LANES), LANES)]
                              for p in range(hp)))

        def step(c, carry):
            rows = pl.ds(pl.multiple_of(c * CHUNK, CHUNK), CHUNK)
            st = st_ref[hg]
            states_ref[c, 0] = st
            ins = [tuple(ref[rows, ln] for ln in lns) for ref in (q_ref, f_ref, i_ref, z_ref)]
            y, st_new = _hgrn2_chunk(*ins, *albs, nwv, st)
            for p in range(hp):
                y_ref[rows, lns[p]] = y[p * CHUNK:(p + 1) * CHUNK].astype(y_ref.dtype)
            st_ref[hg] = st_new
            return carry

        lax.fori_loop(0, nc, step, 0)

    return pl.pallas_call(
        body, name="hgrn2_fwd", grid=(nb, groups),
        in_specs=[_head_cols(tb, groups, hp, j) for j in range(4)] + [
            pl.BlockSpec((3, d), lambda b, h: (0, 0)),
            pl.BlockSpec((1, LANES), lambda b, h: (0, 0))],
        out_specs=[pl.BlockSpec((tb, hp * LANES), lambda b, h: (b, h)),
                   pl.BlockSpec((nc, 1, LANES, hp * LANES), lambda b, h: (b, h, 0, 0))],
        out_shape=[jax.ShapeDtypeStruct((t, d), BF16),
                   jax.ShapeDtypeStruct((t // CHUNK, groups, LANES, hp * LANES), F32)],
        scratch_shapes=[pltpu.VMEM((groups, LANES, hp * LANES), F32)],
        compiler_params=_cparams(("arbitrary", "arbitrary")),
    )(hproj, hproj, hproj, hproj, alb, nw)


def hgrn2_bwd(hproj, alb, nw, states, dy, *, tb, hp):
    t, d4 = hproj.shape
    d = d4 // 4
    heads = d // LANES
    groups = heads // hp
    nb, nc = t // tb, tb // CHUNK

    def body(q_ref, f_ref, i_ref, z_ref, alb_ref, nw_ref, states_ref, dy_ref,
             dh_ref, dalb_ref, dnw_ref, dst_ref):
        b, hg = pl.program_id(0), pl.program_id(1)

        @pl.when(b == 0)
        def _():
            dst_ref[hg] = jnp.zeros((LANES, hp * LANES), F32)

        @pl.when((b == 0) & (hg == 0))
        def _():
            dalb_ref[...] = jnp.zeros_like(dalb_ref)
            dnw_ref[...] = jnp.zeros_like(dnw_ref)

        nwv = nw_ref[...]
        lns = [slice(p * LANES, (p + 1) * LANES) for p in range(hp)]
        lanes_of = [pl.ds(pl.multiple_of((hg * hp + p) * LANES, LANES), LANES) for p in range(hp)]
        albs = [tuple(alb_ref[r:r + 1, lanes_of[p]] for p in range(hp)) for r in range(3)]

        def step(i, carry):
            c = nc - 1 - i
            rows = pl.ds(pl.multiple_of(c * CHUNK, CHUNK), CHUNK)
            ins = [tuple(ref[rows, ln] for ln in lns) for ref in (q_ref, f_ref, i_ref, z_ref)]
            dy = _stack([dy_ref[rows, ln].astype(F32) for ln in lns])
            _, vjp = jax.vjp(_hgrn2_chunk, *ins, *albs, nwv, states_ref[c, 0])
            dq, df, di, dz, da0, da1, da2, dnw, dst = vjp((dy, dst_ref[hg]))
            for p in range(hp):
                h = hg * hp + p
                for j, val in enumerate((dq, df, di, dz)):
                    dh_ref[rows, pl.ds(pl.multiple_of(j * d + h * LANES, LANES), LANES)] = val[p].astype(dh_ref.dtype)
                dalb_ref[0:1, lanes_of[p]] += da0[p]
                dalb_ref[1:2, lanes_of[p]] += da1[p]
                dalb_ref[2:3, lanes_of[p]] += da2[p]
            dnw_ref[0:1, :] += dnw
            dst_ref[hg] = dst
            return carry

        lax.fori_loop(0, nc, step, 0)

    return pl.pallas_call(
        body, name="hgrn2_bwd", grid=(nb, groups),
        in_specs=[_head_cols(tb, groups, hp, j, rev_nb=nb) for j in range(4)] + [
            pl.BlockSpec((3, d), lambda b, h: (0, 0)),
            pl.BlockSpec((1, LANES), lambda b, h: (0, 0)),
            pl.BlockSpec((nc, 1, LANES, hp * LANES), lambda b, h: (nb - 1 - b, h, 0, 0)),
            pl.BlockSpec((tb, hp * LANES), lambda b, h: (nb - 1 - b, h))],
        out_specs=[pl.BlockSpec((tb, d4), lambda b, h: (nb - 1 - b, 0)),
                   pl.BlockSpec((3, d), lambda b, h: (0, 0)),
                   pl.BlockSpec((8, LANES), lambda b, h: (0, 0))],
        out_shape=[jax.ShapeDtypeStruct((t, d4), BF16),
                   jax.ShapeDtypeStruct((3, d), F32),
                   jax.ShapeDtypeStruct((8, LANES), F32)],
        scratch_shapes=[pltpu.VMEM((groups, LANES, hp * LANES), F32)],
        compiler_params=_cparams(("arbitrary", "arbitrary")),
    )(hproj, hproj, hproj, hproj, alb, nw, states, dy)


def _conv_silu(xm, xh, w):
    acc = w[CONV_K - 1] * xm
    for j in range(CONV_K - 1):
        acc = acc + w[j] * _shift_rows(xm, xh, CONV_K - 1 - j)
    return _silu(acc)


def _l2norm(t):
    return t * lax.rsqrt(jnp.sum(t * t, axis=1, keepdims=True) + L2_EPS)


def _gdn_chunk(head0, hh, qm, qh, km, kh, vm, vh, z, tail, wq, wk, wv, alog_row, dtb_row, nw, s):
    g = len(qm)
    n = g * CHUNK
    row = lax.broadcasted_iota(jnp.int32, (n, n), 0)
    col = lax.broadcasted_iota(jnp.int32, (n, n), 1)
    same = (row // CHUNK) == (col // CHUNK)
    lane = lax.broadcasted_iota(jnp.int32, (CHUNK, LANES), 1)
    heads_lane = lax.broadcasted_iota(jnp.int32, (1, LANES), 1)
    q = _l2norm(_stack([_conv_silu(qm[h], qh[h], wq[h]) for h in range(g)])) * (LANES ** -0.5)
    k = _l2norm(_stack([_conv_silu(km[h], kh[h], wk[h]) for h in range(g)]))
    v = _stack([_conv_silu(vm[h], vh[h], wv[h]) for h in range(g)])
    zs = _stack(list(z))
    betas, gs = [], []
    for h in range(g):
        head = head0 + h
        betas.append(jax.nn.sigmoid(jnp.sum(jnp.where(lane == head, tail, 0.0), axis=1, keepdims=True)))
        a_t = jnp.sum(jnp.where(lane == hh + head, tail, 0.0), axis=1, keepdims=True)
        alog = jnp.sum(jnp.where(heads_lane == head, alog_row, 0.0), axis=1, keepdims=True)
        dtb = jnp.sum(jnp.where(heads_lane == head, dtb_row, 0.0), axis=1, keepdims=True)
        gs.append(-jnp.exp(alog) * _softplus(a_t + dtb))
    beta, gcol = _stack(betas), _stack(gs)
    tril = (same & (col <= row)).astype(F32)
    eye = (col == row).astype(F32)
    cum = _sdot(tril, gcol + jnp.zeros((n, n), F32))
    cum_r = _sdot(jnp.ones((n, n), F32), cum * eye)
    diff = cum - cum_r
    strict = same & (col < row)
    incl = same & (col <= row)
    dec_strict = jnp.where(strict, jnp.exp(jnp.where(strict, diff, 0.0)), 0.0)
    dec_incl = jnp.where(incl, jnp.exp(jnp.where(incl, diff, 0.0)), 0.0)
    cum_c = jnp.sum(cum * eye, axis=1, keepdims=True)
    last = same & (col % CHUNK == CHUNK - 1)
    cl_rows = jnp.sum(jnp.where(last, cum_r, 0.0), axis=1, keepdims=True)
    ecum = jnp.exp(cum_c)
    inv = _tri_inv(-(beta * _bdot(k, k, NT) * dec_strict))
    u0 = _dot3(inv, beta * v)
    w = _dot3(inv, (beta * ecum) * k)
    qk = _bdot(q, k, NT) * dec_incl
    u = u0 - _bdot(_place(w, g), s, NN)
    o = _bdot(_place(q * ecum, g), s, NN) + _bdot(qk, u, NN)
    kd = k * jnp.exp(cl_rows - cum_c)
    row1 = lax.broadcasted_iota(jnp.int32, (n, 1), 0)
    decay = []
    for h in range(g):
        cl_h = jnp.sum(jnp.where(row1 == h * CHUNK + CHUNK - 1, cum_c, 0.0), axis=0, keepdims=True)
        decay.append(jnp.exp(cl_h) + jnp.zeros((LANES, 1), F32))
    s_new = _stack(decay) * s + _bdot(_place(kd, g), u, TN)
    return _gated_rmsnorm(o, zs, nw), s_new


def _gdn_in_specs(tb, heads, hp, rev_nb=None):
    groups = heads // hp

    def bb(b):
        return b if rev_nb is None else rev_nb - 1 - b

    def halo(j):
        return pl.BlockSpec((HALO, hp * LANES),
                            lambda b, h, j=j: (jnp.maximum(bb(b) * (tb // HALO) - 1, 0), j * groups + h))

    def main(j):
        return _head_cols(tb, groups, hp, j, rev_nb=rev_nb)

    return [main(0), halo(0), main(1), halo(1), main(2), halo(2), main(3),
            pl.BlockSpec((tb, LANES), lambda b, h: (bb(b), 0)),
            pl.BlockSpec((CONV_K, 3 * heads * LANES), lambda b, h: (0, 0)),
            pl.BlockSpec((8, LANES), lambda b, h: (0, 0)),
            pl.BlockSpec((1, LANES), lambda b, h: (0, 0))]


def _gdn_chunk_args(c, head0, hp, blk, heads, q_ref, qh_ref, k_ref, kh_ref, v_ref, vh_ref, z_ref, tail_ref, cw_ref, prm_ref, nw_ref):
    d = heads * LANES
    lns = [slice(p * LANES, (p + 1) * LANES) for p in range(hp)]
    rows = pl.ds(pl.multiple_of(c * CHUNK, CHUNK), CHUNK)
    prev = pl.ds(pl.multiple_of(jnp.maximum(c * CHUNK - HALO, 0), HALO), HALO)
    first = c == 0
    live = jnp.where(first & (blk == 0), 0.0, 1.0)

    def main_of(ref):
        return tuple(ref[rows, ln] for ln in lns)

    def halo_of(ref, href):
        return tuple(jnp.where(first, href[:, ln], ref[prev, ln]) * live for ln in lns)

    def cw(j):
        out = []
        for p in range(hp):
            lanes = pl.ds(pl.multiple_of(j * d + (head0 + p) * LANES, LANES), LANES)
            out.append(tuple(cw_ref[r:r + 1, lanes] for r in range(CONV_K)))
        return tuple(out)

    return (main_of(q_ref), halo_of(q_ref, qh_ref), main_of(k_ref), halo_of(k_ref, kh_ref),
            main_of(v_ref), halo_of(v_ref, vh_ref), main_of(z_ref), tail_ref[rows, :],
            cw(0), cw(1), cw(2), prm_ref[0:1, :], prm_ref[1:2, :], nw_ref[...])


def gdn_fwd(hmain, tail, conv_w, prm, nw, *, tb, hp):
    t, d4 = hmain.shape
    d = d4 // 4
    heads = d // LANES
    groups = heads // hp
    nb, nc = t // tb, tb // CHUNK

    def body(q_ref, qh_ref, k_ref, kh_ref, v_ref, vh_ref, z_ref, tail_ref, cw_ref, prm_ref, nw_ref,
             y_ref, states_ref, s_ref):
        b, hg = pl.program_id(0), pl.program_id(1)

        @pl.when(b == 0)
        def _():
            s_ref[hg] = jnp.zeros((hp * LANES, LANES), F32)

        def step(c, carry):
            rows = pl.ds(pl.multiple_of(c * CHUNK, CHUNK), CHUNK)
            s = s_ref[hg]
            states_ref[c, 0] = s
            args = _gdn_chunk_args(c, hg * hp, hp, b, heads, q_ref, qh_ref, k_ref, kh_ref, v_ref, vh_ref, z_ref,
                                   tail_ref, cw_ref, prm_ref, nw_ref)
            y, s_new = _gdn_chunk(hg * hp, heads, *args, s)
            for p in range(hp):
                y_ref[rows, p * LANES:(p + 1) * LANES] = y[p * CHUNK:(p + 1) * CHUNK].astype(y_ref.dtype)
            s_ref[hg] = s_new
            return carry

        lax.fori_loop(0, nc, step, 0)

    return pl.pallas_call(
        body, name="gdn_fwd", grid=(nb, groups),
        in_specs=_gdn_in_specs(tb, heads, hp),
        out_specs=[pl.BlockSpec((tb, hp * LANES), lambda b, h: (b, h)),
                   pl.BlockSpec((nc, 1, hp * LANES, LANES), lambda b, h: (b, h, 0, 0))],
        out_shape=[jax.ShapeDtypeStruct((t, d), BF16),
                   jax.ShapeDtypeStruct((t // CHUNK, groups, hp * LANES, LANES), F32)],
        scratch_shapes=[pltpu.VMEM((groups, hp * LANES, LANES), F32)],
        compiler_params=_cparams(("arbitrary", "arbitrary")),
    )(hmain, hmain, hmain, hmain, hmain, hmain, hmain, tail, conv_w, prm, nw)


def gdn_bwd(hmain, tail, conv_w, prm, nw, states, dy, *, tb, hp):
    t, d4 = hmain.shape
    d = d4 // 4
    heads = d // LANES
    groups = heads // hp
    nb, nc = t // tb, tb // CHUNK

    def body(q_ref, qh_ref, k_ref, kh_ref, v_ref, vh_ref, z_ref, tail_ref, cw_ref, prm_ref, nw_ref,
             states_ref, dy_ref, dh_ref, dtail_ref, dcw_ref, dprm_ref, ds_ref, pend_ref):
        b, hg = pl.program_id(0), pl.program_id(1)
        blk = nb - 1 - b

        @pl.when(b == 0)
        def _():
            ds_ref[hg] = jnp.zeros((hp * LANES, LANES), F32)
            for p in range(hp):
                pend_ref[hg * hp + p] = jnp.zeros((3, HALO, LANES), F32)

        @pl.when((b == 0) & (hg == 0))
        def _():
            dcw_ref[...] = jnp.zeros_like(dcw_ref)
            dprm_ref[...] = jnp.zeros_like(dprm_ref)

        @pl.when(hg == 0)
        def _():
            dtail_ref[...] = jnp.zeros_like(dtail_ref)

        def step(i, carry):
            c = nc - 1 - i
            rows = pl.ds(pl.multiple_of(c * CHUNK, CHUNK), CHUNK)
            zpad = jnp.zeros((CHUNK - HALO, LANES), F32)
            args = _gdn_chunk_args(c, hg * hp, hp, blk, heads, q_ref, qh_ref, k_ref, kh_ref, v_ref, vh_ref, z_ref,
                                   tail_ref, cw_ref, prm_ref, nw_ref)
            dy = _stack([dy_ref[rows, p * LANES:(p + 1) * LANES].astype(F32) for p in range(hp)])
            pends = [pend_ref[hg * hp + p] for p in range(hp)]
            _, vjp = jax.vjp(functools.partial(_gdn_chunk, hg * hp, heads), *args, states_ref[c, 0])
            (dqm, dqh, dkm, dkh, dvm, dvh, dz, dtl, dwq, dwk, dwv, dalog, ddtb, dnw, ds) = vjp((dy, ds_ref[hg]))
            for p in range(hp):
                h = hg * hp + p
                for j, (dm, dhalo) in enumerate(((dqm, dqh), (dkm, dkh), (dvm, dvh))):
                    full = dm[p] + jnp.concatenate([zpad, pends[p][j]], axis=0)
                    dh_ref[rows, pl.ds(pl.multiple_of(j * d + h * LANES, LANES), LANES)] = full.astype(dh_ref.dtype)
                    pend_ref[h, j] = dhalo[p]
                dh_ref[rows, pl.ds(pl.multiple_of(3 * d + h * LANES, LANES), LANES)] = dz[p].astype(dh_ref.dtype)
                for j, dw in enumerate((dwq, dwk, dwv)):
                    lanes = pl.ds(pl.multiple_of(j * d + h * LANES, LANES), LANES)
                    for r in range(CONV_K):
                        dcw_ref[r:r + 1, lanes] += dw[p][r]
            dtail_ref[rows, :] += dtl
            dprm_ref[0:1, :] += dalog
            dprm_ref[1:2, :] += ddtb
            dprm_ref[2:3, :] += dnw
            ds_ref[hg] = ds
            return carry

        lax.fori_loop(0, nc, step, 0)

    return pl.pallas_call(
        body, name="gdn_bwd", grid=(nb, groups),
        in_specs=_gdn_in_specs(tb, heads, hp, rev_nb=nb) + [
            pl.BlockSpec((nc, 1, hp * LANES, LANES), lambda b, h: (nb - 1 - b, h, 0, 0)),
            pl.BlockSpec((tb, hp * LANES), lambda b, h: (nb - 1 - b, h))],
        out_specs=[pl.BlockSpec((tb, d4), lambda b, h: (nb - 1 - b, 0)),
                   pl.BlockSpec((tb, LANES), lambda b, h: (nb - 1 - b, 0)),
                   pl.BlockSpec((CONV_K, 3 * d), lambda b, h: (0, 0)),
                   pl.BlockSpec((8, LANES), lambda b, h: (0, 0))],
        out_shape=[jax.ShapeDtypeStruct((t, d4), BF16),
                   jax.ShapeDtypeStruct((t, LANES), F32),
                   jax.ShapeDtypeStruct((CONV_K, 3 * d), F32),
                   jax.ShapeDtypeStruct((8, LANES), F32)],
        scratch_shapes=[pltpu.VMEM((groups, hp * LANES, LANES), F32),
                        pltpu.VMEM((heads, 3, HALO, LANES), F32)],
        compiler_params=_cparams(("arbitrary", "arbitrary")),
    )(hmain, hmain, hmain, hmain, hmain, hmain, hmain, tail, conv_w, prm, nw, states, dy)


def _layer_norm(u, g, b):
    mu = jnp.mean(u, axis=1, keepdims=True)
    var = jnp.mean(jnp.square(u - mu), axis=1, keepdims=True)
    return (u - mu) * lax.rsqrt(var + LN_EPS) * g + b


def ln_fwd(u, g, b, *, tr):
    t, d = u.shape

    def body(u_ref, g_ref, b_ref, x_ref, xb_ref):
        x = _layer_norm(u_ref[...], g_ref[...], b_ref[...])
        x_ref[...] = x
        xb_ref[...] = x.astype(BF16)

    row = pl.BlockSpec((tr, d), lambda i: (i, 0))
    vec = pl.BlockSpec((1, d), lambda i: (0, 0))
    return pl.pallas_call(
        body, name="ln_fwd", grid=(t // tr,), in_specs=[row, vec, vec], out_specs=[row, row],
        out_shape=[jax.ShapeDtypeStruct((t, d), F32), jax.ShapeDtypeStruct((t, d), BF16)],
        compiler_params=_cparams(("parallel",)),
    )(u, g, b)


def ln_bwd(u, g, b, dout, *, tr):
    t, d = u.shape

    def body(u_ref, g_ref, b_ref, dout_ref, du_ref, dub_ref, dg_ref, db_ref):
        @pl.when(pl.program_id(0) == 0)
        def _():
            dg_ref[...] = jnp.zeros_like(dg_ref)
            db_ref[...] = jnp.zeros_like(db_ref)

        _, vjp = jax.vjp(_layer_norm, u_ref[...], g_ref[...], b_ref[...])
        du, dg, db = vjp(dout_ref[...])
        du_ref[...] = du
        dub_ref[...] = du.astype(BF16)
        dg_ref[0:1, :] += dg
        db_ref[0:1, :] += db

    row = pl.BlockSpec((tr, d), lambda i: (i, 0))
    vec = pl.BlockSpec((1, d), lambda i: (0, 0))
    acc = pl.BlockSpec((8, d), lambda i: (0, 0))
    return pl.pallas_call(
        body, name="ln_bwd", grid=(t // tr,), in_specs=[row, vec, vec, row], out_specs=[row, row, acc, acc],
        out_shape=[jax.ShapeDtypeStruct((t, d), F32), jax.ShapeDtypeStruct((t, d), BF16),
                   jax.ShapeDtypeStruct((8, d), F32), jax.ShapeDtypeStruct((8, d), F32)],
        compiler_params=_cparams(("arbitrary",)),
    )(u, g, b, dout)


def ln_loss_bwd(u, g, b, target, *, tr):
    t, d = u.shape

    def loss_of(uu, gg, bb, tgt):
        err = jnp.square(_layer_norm(uu, gg, bb) - tgt)
        return 0.5 * jnp.sum(jnp.mean(err, axis=1, keepdims=True), axis=0, keepdims=True)

    def body(u_ref, g_ref, b_ref, t_ref, loss_ref, du_ref, dub_ref, dg_ref, db_ref):
        @pl.when(pl.program_id(0) == 0)
        def _():
            loss_ref[...] = jnp.zeros_like(loss_ref)
            dg_ref[...] = jnp.zeros_like(dg_ref)
            db_ref[...] = jnp.zeros_like(db_ref)

        tgt = t_ref[...]
        val, vjp = jax.vjp(lambda uu, gg, bb: loss_of(uu, gg, bb, tgt), u_ref[...], g_ref[...], b_ref[...])
        du, dg, db = vjp(jnp.ones((1, 1), F32))
        loss_ref[...] += val
        du_ref[...] = du
        dub_ref[...] = du.astype(BF16)
        dg_ref[0:1, :] += dg
        db_ref[0:1, :] += db

    row = pl.BlockSpec((tr, d), lambda i: (i, 0))
    vec = pl.BlockSpec((1, d), lambda i: (0, 0))
    acc = pl.BlockSpec((8, d), lambda i: (0, 0))
    return pl.pallas_call(
        body, name="ln_loss_bwd", grid=(t // tr,), in_specs=[row, vec, vec, row],
        out_specs=[pl.BlockSpec((8, LANES), lambda i: (0, 0)), row, row, acc, acc],
        out_shape=[jax.ShapeDtypeStruct((8, LANES), F32),
                   jax.ShapeDtypeStruct((t, d), F32), jax.ShapeDtypeStruct((t, d), BF16),
                   jax.ShapeDtypeStruct((8, d), F32), jax.ShapeDtypeStruct((8, d), F32)],
        compiler_params=_cparams(("arbitrary",)),
    )(u, g, b, target)


def local_step(x, target, wa_in, alb, a_nw, wa_out, wb_main, wb_tail, conv_w, a_log, dt_bias, b_nw, wb_out,
               ln_g, ln_b, *, tb=256, tr=256, hp=HEADS_PER_STEP):
    t, d = x.shape
    heads = d // LANES
    tb, tr, hp = min(tb, t), min(tr, t), min(hp, heads)
    xb = x.astype(BF16)
    prm = jnp.zeros((8, LANES), F32).at[0, :heads].set(a_log[0]).at[1, :heads].set(dt_bias[0])

    ha = matmul(xb, wa_in, name="mm_a_in")
    ya, st_a = hgrn2_fwd(ha, alb, a_nw, tb=tb, hp=hp)
    u1 = matmul(ya, wa_out, name="mm_a_out", addend=x, alpha=DEEPNORM_ALPHA)
    x1, x1b = ln_fwd(u1, ln_g[0:1], ln_b[0:1], tr=tr)
    hb = matmul(x1b, wb_main, name="mm_b_in")
    tl = matmul(x1b, wb_tail, name="mm_b_tail")
    yb, st_b = gdn_fwd(hb, tl, conv_w, prm, b_nw, tb=tb, hp=hp)
    u2 = matmul(yb, wb_out, name="mm_b_out", addend=x1, alpha=DEEPNORM_ALPHA)

    loss, du2, du2b, dg2, db2 = ln_loss_bwd(u2, ln_g[1:2], ln_b[1:2], target, tr=tr)
    d_wb_out = matmul(yb.T, du2b, name="mm_dwb_out", out_dtype=BF16)
    dyb = matmul(du2b, wb_out, name="mm_dyb", nt=True)
    dhb, dtl, d_conv, dprm = gdn_bwd(hb, tl, conv_w, prm, b_nw, st_b, dyb, tb=tb, hp=hp)
    dtlb = dtl.astype(BF16)
    x1t = x1b.T
    d_wb_main = matmul(x1t, dhb, name="mm_dwb_main", out_dtype=BF16)
    d_wb_tail = matmul(x1t, dtlb, name="mm_dwb_tail", out_dtype=BF16)
    dx1_tail = matmul(dtlb, wb_tail, name="mm_dx1_tail", nt=True, addend=du2, alpha=DEEPNORM_ALPHA)
    dx1 = matmul(dhb, wb_main, name="mm_dx1", nt=True, addend=dx1_tail, alpha=1.0)
    du1, du1b, dg1, db1 = ln_bwd(u1, ln_g[0:1], ln_b[0:1], dx1, tr=tr)
    d_wa_out = matmul(ya.T, du1b, name="mm_dwa_out", out_dtype=BF16)
    dya = matmul(du1b, wa_out, name="mm_dya", nt=True)
    dha, d_alb, d_anw = hgrn2_bwd(ha, alb, a_nw, st_a, dya, tb=tb, hp=hp)
    d_wa_in = matmul(xb.T, dha, name="mm_dwa_in", out_dtype=BF16, out_split=wa_in.shape[0] if wa_in.ndim == 3 else 1)
    grad_x = matmul(dha, wa_in, name="mm_dx", nt=True, addend=du1, alpha=DEEPNORM_ALPHA)

    small = dict(
        a_lower_bounds=d_alb, a_norm_w=d_anw[0:1], b_conv_w=d_conv,
        b_a_log=dprm[0:1, :heads], b_dt_bias=dprm[1:2, :heads], b_norm_w=dprm[2:3],
        ln_g=jnp.concatenate([dg1[0:1], dg2[0:1]], axis=0), ln_b=jnp.concatenate([db1[0:1], db2[0:1]], axis=0))
    big = dict(a_w_in=d_wa_in, a_w_out=d_wa_out, b_w_main=d_wb_main, b_w_tail=d_wb_tail, b_w_out=d_wb_out)
    return loss, grad_x, big, small


MESH_ID = pl.DeviceIdType.MESH


def _position():
    return lax.axis_index("x"), lax.axis_index("y"), lax.axis_index("c")


def _index_of(p):
    return 4 * p[0] + 2 * p[1] + p[2]


def all_gather(shards, *, name, space):
    n = len(shards)

    def body(*refs):
        ins, outs = refs[:n], refs[n:2 * n]
        send_sems, recv_sems, local_sems = refs[2 * n:]
        x, y, c = _position()
        me, sibling = (x, y, c), (x, y, 1 - c)
        chips = [(1 - x, y), (x, 1 - y), (1 - x, 1 - y)]

        def copy(a, k, block, to, own=False):
            dst = outs[a].at[_index_of(block)]
            return pltpu.make_async_remote_copy(
                src_ref=ins[a] if own else dst, dst_ref=dst,
                send_sem=send_sems.at[7 * a + k], recv_sem=recv_sems.at[7 * a + k],
                device_id=to, device_id_type=MESH_ID)

        mine = [pltpu.make_async_copy(ins[a], outs[a].at[_index_of(me)], local_sems.at[a]) for a in range(n)]
        for cp in mine:
            cp.start()
        first = []
        for a in range(n):
            first.append(copy(a, 0, me, sibling, own=True))
            first += [copy(a, 1 + j, me, (*chip, c), own=True) for j, chip in enumerate(chips)]
        for cp in first:
            cp.start()
        passed = []
        for j, chip in enumerate(chips):
            for a in range(n):
                copy(a, 1 + j, (*chip, c), me).wait_recv()
                fwd = copy(a, 4 + j, (*chip, c), sibling)
                fwd.start()
                passed.append(fwd)
        for a in range(n):
            copy(a, 0, sibling, me).wait_recv()
            for j, chip in enumerate(chips):
                copy(a, 4 + j, (*chip, 1 - c), me).wait_recv()
        for cp in first + passed:
            cp.wait_send()
        for cp in mine:
            cp.wait()

    spec = pl.BlockSpec(memory_space=space)
    return pl.pallas_call(
        body, name=name,
        in_specs=[spec] * n, out_specs=[spec] * n,
        out_shape=[jax.ShapeDtypeStruct((N_DEV, *s.shape), s.dtype) for s in shards],
        scratch_shapes=[pltpu.SemaphoreType.DMA((7 * n,)), pltpu.SemaphoreType.DMA((7 * n,)),
                        pltpu.SemaphoreType.DMA((n,))],
        compiler_params=pltpu.CompilerParams(vmem_limit_bytes=VMEM_LIMIT),
    )(*shards)


def exchange(parts, *, name):
    n = len(parts)

    def body(*refs):
        ins, outs = refs[:n], refs[n:2 * n]
        send_sems, recv_sems, local_sems = refs[2 * n:]
        x, y, c = _position()
        me = (x, y, c)
        copies = []
        for a in range(n):
            own = pltpu.make_async_copy(ins[a].at[_index_of(me)], outs[a].at[_index_of(me)], local_sems.at[a])
            own.start()
            copies.append(own)
            for r in range(1, N_DEV):
                fx, fy, fc = (r >> 2) & 1, (r >> 1) & 1, r & 1
                peer = (x ^ fx, y ^ fy, c ^ fc)
                cp = pltpu.make_async_remote_copy(
                    src_ref=ins[a].at[_index_of(peer)], dst_ref=outs[a].at[_index_of(me)],
                    send_sem=send_sems.at[7 * a + r - 1], recv_sem=recv_sems.at[7 * a + r - 1],
                    device_id=peer, device_id_type=MESH_ID)
                cp.start()
                copies.append(cp)
        for cp in copies:
            cp.wait()

    spec = pl.BlockSpec(memory_space=pltpu.HBM)
    return pl.pallas_call(
        body, name=name,
        in_specs=[spec] * n, out_specs=[spec] * n,
        out_shape=[jax.ShapeDtypeStruct(p.shape, p.dtype) for p in parts],
        scratch_shapes=[pltpu.SemaphoreType.DMA((7 * n,)), pltpu.SemaphoreType.DMA((7 * n,)),
                        pltpu.SemaphoreType.DMA((n,))],
        compiler_params=pltpu.CompilerParams(vmem_limit_bytes=VMEM_LIMIT),
    )(*parts)


def adamw(parts, w, m, v, *, name, tr=64):
    p, r, c = parts.shape
    tr = min(tr, r)
    assert r % tr == 0
    c1 = 1.0 / (1.0 - ADAM_B1 ** ADAM_STEP)
    c2 = 1.0 / (1.0 - ADAM_B2 ** ADAM_STEP)

    def body(p_ref, w_ref, m_ref, v_ref, g_ref, d_ref, nm_ref, nv_ref):
        g = p_ref[0].astype(F32)
        for i in range(1, p):
            g = g + p_ref[i].astype(F32)
        nm = ADAM_B1 * m_ref[...] + (1.0 - ADAM_B1) * g
        nv = ADAM_B2 * v_ref[...] + (1.0 - ADAM_B2) * jnp.square(g)
        g_ref[...] = g
        nm_ref[...] = nm
        nv_ref[...] = nv
        d_ref[...] = -ADAM_LR * ((nm * c1) / (jnp.sqrt(nv * c2) + ADAM_EPS) + ADAM_WD * w_ref[...])

    blk = pl.BlockSpec((tr, c), lambda i: (i, 0))
    out = jax.ShapeDtypeStruct((r, c), F32)
    return pl.pallas_call(
        body, name=name, grid=(r // tr,),
        in_specs=[pl.BlockSpec((p, tr, c), lambda i: (0, i, 0)), blk, blk, blk],
        out_specs=[blk] * 4, out_shape=[out] * 4,
        compiler_params=_cparams(("parallel",)),
    )(parts, w, m, v)


def _pack(d, vals):
    heads = d // LANES
    vecs = jnp.zeros((8, LANES), F32)
    vecs = vecs.at[0:1].set(vals["a_norm_w"]).at[1:2, :heads].set(vals["b_a_log"])
    vecs = vecs.at[2:3, :heads].set(vals["b_dt_bias"]).at[3:4].set(vals["b_norm_w"])
    rows = [vals["a_lower_bounds"].reshape(-1, LANES), vals["ln_g"].reshape(-1, LANES),
            vals["ln_b"].reshape(-1, LANES), vecs]
    return jnp.concatenate(rows, axis=0)


def _unpack(d, packed):
    heads = d // LANES
    n3, n2 = 3 * heads, 2 * heads
    o = 0
    out = {}
    out["a_lower_bounds"] = packed[o:o + n3].reshape(3, d); o += n3
    out["ln_g"] = packed[o:o + n2].reshape(2, d); o += n2
    out["ln_b"] = packed[o:o + n2].reshape(2, d); o += n2
    out["a_norm_w"] = packed[o:o + 1]
    out["b_a_log"] = packed[o + 1:o + 2, :heads]
    out["b_dt_bias"] = packed[o + 2:o + 3, :heads]
    out["b_norm_w"] = packed[o + 3:o + 4]
    return out


ORDER = ("a_w_in", "a_lower_bounds", "a_norm_w", "a_w_out", "b_w_in", "b_conv_w", "b_a_log", "b_dt_bias", "b_norm_w",
         "b_w_out", "ln_g", "ln_b")


def kernel(x, a_w_in, a_lower_bounds, a_norm_w, a_w_out, b_w_in, b_conv_w, b_a_log, b_dt_bias, b_norm_w, b_w_out, ln_g, ln_b, loss_target, m_a_w_in, m_a_lower_bounds, m_a_norm_w, m_a_w_out, m_b_w_in, m_b_conv_w, m_b_a_log, m_b_dt_bias, m_b_norm_w, m_b_w_out, m_ln_g, m_ln_b, v_a_w_in, v_a_lower_bounds, v_a_norm_w, v_a_w_out, v_b_w_in, v_b_conv_w, v_b_a_log, v_b_dt_bias, v_b_norm_w, v_b_w_out, v_ln_g, v_ln_b):
    w = dict(a_w_in=a_w_in, a_lower_bounds=a_lower_bounds, a_norm_w=a_norm_w, a_w_out=a_w_out, b_w_in=b_w_in,
             b_conv_w=b_conv_w, b_a_log=b_a_log, b_dt_bias=b_dt_bias, b_norm_w=b_norm_w, b_w_out=b_w_out, ln_g=ln_g, ln_b=ln_b)
    m = dict(a_w_in=m_a_w_in, a_lower_bounds=m_a_lower_bounds, a_norm_w=m_a_norm_w, a_w_out=m_a_w_out, b_w_in=m_b_w_in,
             b_conv_w=m_b_conv_w, b_a_log=m_b_a_log, b_dt_bias=m_b_dt_bias, b_norm_w=m_b_norm_w, b_w_out=m_b_w_out,
             ln_g=m_ln_g, ln_b=m_ln_b)
    v = dict(a_w_in=v_a_w_in, a_lower_bounds=v_a_lower_bounds, a_norm_w=v_a_norm_w, a_w_out=v_a_w_out, b_w_in=v_b_w_in,
             b_conv_w=v_b_conv_w, b_a_log=v_b_a_log, b_dt_bias=v_b_dt_bias, b_norm_w=v_b_norm_w, b_w_out=v_b_w_out,
             ln_g=v_ln_g, ln_b=v_ln_b)
    t, d = x.shape[1], x.shape[2]
    heads = d // LANES
    n_tail = 2 * heads
    me = _index_of(_position())

    ga_in, ga_out, gb_in, gb_out, g_conv = all_gather(
        [a_w_in[0].astype(BF16), a_w_out[0].astype(BF16), b_w_in[0].astype(BF16), b_w_out[0].astype(BF16), b_conv_w[0]],
        name="gather_weights", space=pltpu.HBM)
    wa_out = ga_out.reshape(d, d)
    wb_out = gb_out.reshape(d, d)
    wb_full = jnp.transpose(gb_in, (1, 0, 2)).reshape(d, 4 * d + n_tail)
    wb_main = wb_full[:, :4 * d]
    wb_tail = jnp.concatenate([wb_full[:, 4 * d:], jnp.zeros((d, LANES - n_tail), BF16)], axis=1)
    conv_w = jnp.transpose(g_conv, (1, 0, 2)).reshape(CONV_K, 3 * d)

    loss, grad_x, big, small = local_step(
        x[0], loss_target[0], ga_in, a_lower_bounds, a_norm_w, wa_out, wb_main, wb_tail, conv_w, b_a_log, b_dt_bias,
        b_norm_w, wb_out, ln_g, ln_b)
    loss = lax.psum(loss[0, 0], ("x", "y", "c"))

    conv_rows = small["b_conv_w"].reshape(-1, LANES)
    n_conv = conv_rows.shape[0]
    sent = jnp.concatenate([conv_rows, _pack(d, small)], axis=0)
    (got,) = all_gather([sent], name="gather_small", space=pltpu.VMEM)
    res = {}
    packed = adamw(got[:, n_conv:], _pack(d, w), _pack(d, m), _pack(d, v), name="adamw_small", tr=4096)
    for k, vals in zip(("grad", "delta", "new_m", "new_v"), packed):
        res[k] = _unpack(d, vals)
    shard_ch = 3 * d // N_DEV
    conv_parts = lax.dynamic_slice_in_dim(got[:, :n_conv].reshape(N_DEV, CONV_K, 3 * d), me * shard_ch, shard_ch, axis=2)
    conv_out = adamw(conv_parts, b_conv_w[0], m_b_conv_w[0], v_b_conv_w[0], name="adamw_conv")
    for k, vals in zip(("grad", "delta", "new_m", "new_v"), conv_out):
        res[k]["b_conv_w"] = vals[None]

    d_b_in = jnp.concatenate([big["b_w_main"], big["b_w_tail"][:, :n_tail]], axis=1)
    d_b_in = jnp.transpose(d_b_in.reshape(d, N_DEV, -1), (1, 0, 2))
    parts = [big["a_w_in"], big["a_w_out"].reshape(N_DEV, d // N_DEV, d), d_b_in, big["b_w_out"].reshape(N_DEV, d // N_DEV, d)]
    recv = exchange(parts, name="exchange_grads")
    for name, got_parts in zip(("a_w_in", "a_w_out", "b_w_in", "b_w_out"), recv):
        outs = adamw(got_parts, w[name][0], m[name][0], v[name][0], name="adamw_" + name)
        for k, vals in zip(("grad", "delta", "new_m", "new_v"), outs):
            res[k][name] = vals[None]

    return (loss, grad_x[None], *[res["grad"][k] for k in ORDER], *[res["delta"][k] for k in ORDER],
            *[res["new_m"][k] for k in ORDER], *[res["new_v"][k] for k in ORDER])
```

```python
import functools
import math

import jax
import jax.numpy as jnp
from jax import lax
from jax.experimental import pallas as pl
from jax.experimental.pallas import tpu as pltpu

F32 = jnp.float32
BF16 = jnp.bfloat16

N_DEV = 8
LANES = 128
CHUNK = 64
SUB = 16
HALO = 8
CONV_K = 4
DEPTH = 2
DEEPNORM_ALPHA = (2.0 * DEPTH) ** 0.25
LN_EPS = 1e-5
RMS_EPS = 1e-6
L2_EPS = 1e-6
EXP_CLAMP = 60.0
ADAM_LR = 0.001
ADAM_B1 = 0.9
ADAM_B2 = 0.999
ADAM_EPS = 1e-08
ADAM_WD = 0.01
ADAM_STEP = 10
VMEM_LIMIT = 56 * 1024 * 1024
HEADS_PER_STEP = 4

NN = ((1,), (0,))
NT = ((1,), (1,))
TN = ((0,), (0,))


def _dg(a, b, dims, precision=None):
    return lax.dot_general(a, b, (dims, ((), ())), precision=precision, preferred_element_type=F32)


def _bdot(a, b, dims):
    return _dg(a.astype(BF16), b.astype(BF16), dims)


def _split2(t):
    hi = t.astype(BF16)
    return hi, (t - hi.astype(F32)).astype(BF16)


def _split3(t):
    hi = t.astype(BF16)
    r = t - hi.astype(F32)
    mid = r.astype(BF16)
    return hi, mid, (r - mid.astype(F32)).astype(BF16)


@jax.custom_vjp
def _sdot(sel, t):
    sel = sel.astype(BF16)
    hi, mid, lo = _split3(t)
    return (_dg(sel, lo, NN) + _dg(sel, mid, NN)) + _dg(sel, hi, NN)


def _sdot_fwd(sel, t):
    return _sdot(sel, t), sel


def _sdot_bwd(sel, ct):
    selb = sel.astype(BF16)
    hi, mid, lo = _split3(ct)
    return jnp.zeros_like(sel), (_dg(selb, lo, TN) + _dg(selb, mid, TN)) + _dg(selb, hi, TN)


_sdot.defvjp(_sdot_fwd, _sdot_bwd)


def _dot3_raw(a, b, dims):
    ah, al = _split2(a)
    bh, bl = _split2(b)
    return (_dg(al, bh, dims) + _dg(ah, bl, dims)) + _dg(ah, bh, dims)


@jax.custom_vjp
def _dot3(a, b):
    return _dot3_raw(a, b, NN)


def _dot3_fwd(a, b):
    return _dot3_raw(a, b, NN), (a, b)


def _dot3_bwd(res, ct):
    a, b = res
    return _bdot(ct, b, NT), _bdot(a, ct, TN)


_dot3.defvjp(_dot3_fwd, _dot3_bwd)


@jax.custom_vjp
def _tri_inv(m):
    n = m.shape[0]
    row = lax.broadcasted_iota(jnp.int32, (n, n), 0)
    col = lax.broadcasted_iota(jnp.int32, (n, n), 1)
    inv = (col == row).astype(F32) + m
    mp = m
    for _ in range(int(math.log2(CHUNK)) - 1):
        mp = _dot3_raw(mp, mp, NN)
        inv = inv + _dot3_raw(inv, mp, NN)
    return inv


def _tri_inv_fwd(m):
    inv = _tri_inv(m)
    return inv, inv


def _tri_inv_bwd(inv, ct):
    return (_dot3_raw(_dot3_raw(inv, ct, TN), inv, NT),)


_tri_inv.defvjp(_tri_inv_fwd, _tri_inv_bwd)


@functools.partial(jax.custom_vjp, nondiff_argnums=(2,))
def _shift_rows(xm, xh, back):
    r = pltpu.roll(xm, back, 0)
    row = lax.broadcasted_iota(jnp.int32, xh.shape, 0)
    top = jnp.where(row < back, pltpu.roll(xh, back, 0), r[0:HALO])
    return jnp.concatenate([top, r[HALO:]], axis=0)


def _shift_rows_fwd(xm, xh, back):
    return _shift_rows(xm, xh, back), None


def _shift_rows_bwd(back, _, ct):
    row = lax.broadcasted_iota(jnp.int32, ct.shape, 0)
    dxm = jnp.where(row < CHUNK - back, pltpu.roll(ct, CHUNK - back, 0), 0.0)
    rowh = lax.broadcasted_iota(jnp.int32, (HALO, ct.shape[1]), 0)
    dxh = jnp.where(rowh >= HALO - back, pltpu.roll(ct[0:HALO], HALO - back, 0), 0.0)
    return dxm, dxh


_shift_rows.defvjp(_shift_rows_fwd, _shift_rows_bwd)


def _silu(t):
    return t * jax.nn.sigmoid(t)


def _softplus(t):
    return jnp.where(t > 20.0, t, jnp.log1p(jnp.exp(jnp.minimum(t, 20.0))))


def _cparams(sem=None):
    kw = dict(vmem_limit_bytes=VMEM_LIMIT)
    if sem is not None:
        kw["dimension_semantics"] = sem
    return pltpu.CompilerParams(**kw)


def matmul(a, b, *, name, nt=False, addend=None, alpha=1.0, out_dtype=F32, out_split=1, after=None,
           tm=1024, tn=1024, tk=512):
    m, k = a.shape
    b_split = b.shape[0] if b.ndim == 3 else 1
    b_rows, b_cols = b.shape[-2], b.shape[-1] * b_split
    n = b_rows if nt else b_cols
    tm, tn, tk = min(tm, m), min(tn, n), min(tk, k)
    if b_split > 1:
        part = b_cols // b_split
        tn, tk = (tn, min(tk, part)) if nt else (min(tn, part), tk)
    if out_split > 1:
        tn = min(tn, n // out_split)
    assert m % tm == 0 and n % tn == 0 and k % tk == 0, (a.shape, b.shape, nt)
    nk = k // tk
    dims = NT if nt else NN

    def body(*refs):
        a_ref, b_ref = refs[:2]
        add_ref = None if addend is None else refs[2]
        o_ref, acc_ref = refs[-2:]
        kk = pl.program_id(2)

        @pl.when(kk == 0)
        def _():
            acc_ref[...] = jnp.zeros_like(acc_ref)

        acc_ref[...] += _dg(a_ref[...], b_ref[...], dims)

        @pl.when(kk == nk - 1)
        def _():
            r = acc_ref[...]
            if add_ref is not None:
                r = r + alpha * add_ref[...].astype(F32)
            o_ref[...] = r.astype(o_ref.dtype)

    if b_split == 1:
        b_spec = (pl.BlockSpec((tn, tk), lambda i, j, kk: (j, kk)) if nt
                  else pl.BlockSpec((tk, tn), lambda i, j, kk: (kk, j)))
    elif nt:
        per = (b_cols // b_split) // tk
        b_spec = pl.BlockSpec((None, tn, tk), lambda i, j, kk: (kk // per, j, kk % per))
    else:
        per = (b_cols // b_split) // tn
        b_spec = pl.BlockSpec((None, tk, tn), lambda i, j, kk: (j // per, kk, j % per))
    in_specs = [pl.BlockSpec((tm, tk), lambda i, j, kk: (i, kk)), b_spec]
    args = [a, b]
    if addend is not None:
        in_specs.append(pl.BlockSpec((tm, tn), lambda i, j, kk: (i, j)))
        args.append(addend)
    if after is not None:
        in_specs.append(pl.BlockSpec(memory_space=pl.ANY))
        args.append(after)
    if out_split == 1:
        out_spec = pl.BlockSpec((tm, tn), lambda i, j, kk: (i, j))
        out_shape = jax.ShapeDtypeStruct((m, n), out_dtype)
    else:
        per_o = (n // out_split) // tn
        out_spec = pl.BlockSpec((None, tm, tn), lambda i, j, kk: (j // per_o, i, j % per_o))
        out_shape = jax.ShapeDtypeStruct((out_split, m, n // out_split), out_dtype)
    return pl.pallas_call(
        body, name=name, grid=(m // tm, n // tn, nk),
        in_specs=in_specs,
        out_specs=out_spec,
        out_shape=out_shape,
        scratch_shapes=[pltpu.VMEM((tm, tn), F32)],
        compiler_params=_cparams(("parallel", "parallel", "arbitrary")),
    )(*args)


def _gated_rmsnorm(o, z, nw):
    r = lax.rsqrt(jnp.mean(o * o, axis=1, keepdims=True) + RMS_EPS)
    return o * r * nw * _silu(z)


def _chunk_consts():
    row = lax.broadcasted_iota(jnp.int32, (CHUNK, CHUNK), 0)
    col = lax.broadcasted_iota(jnp.int32, (CHUNK, CHUNK), 1)
    return row, col


def _head_cols(tb, groups, hp, j, rev_nb=None):
    def bb(b):
        return b if rev_nb is None else rev_nb - 1 - b
    return pl.BlockSpec((tb, hp * LANES), lambda b, h: (bb(b), j * groups + h))


def _stack(pieces):
    return jnp.concatenate(pieces, axis=0) if len(pieces) > 1 else pieces[0]


def _place(t, g):
    if g == 1:
        return t
    head = lax.broadcasted_iota(jnp.int32, t.shape, 0) // CHUNK
    return jnp.concatenate([jnp.where(head == h, t, 0.0) for h in range(g)], axis=1)


def _hgrn2_chunk(qr, fr, iv, z, a0, a1, a2, nw, st):
    g = len(qr)
    n = g * CHUNK
    row = lax.broadcasted_iota(jnp.int32, (n, n), 0)
    col = lax.broadcasted_iota(jnp.int32, (n, n), 1)
    same = (row // CHUNK) == (col // CHUNK)
    qs, ks, gl = [], [], []
    for h in range(g):
        amax = jnp.maximum(jnp.maximum(a0[h], a1[h]), a2[h])
        e0, e1, e2 = jnp.exp(a0[h] - amax), jnp.exp(a1[h] - amax), jnp.exp(a2[h] - amax)
        lb = e0 / (e0 + e1 + e2)
        forget = lb + (1.0 - lb) * jax.nn.sigmoid(fr[h])
        qs.append(_silu(qr[h]))
        ks.append(1.0 - forget)
        gl.append(jnp.log(forget))
    q, k, glog, v, zs = _stack(qs), _stack(ks), _stack(gl), _stack(list(iv)), _stack(list(z))
    cum = _sdot((same & (col <= row)).astype(F32), glog)
    sub = (row % CHUNK) // SUB
    rowl = lax.broadcasted_iota(jnp.int32, (n, LANES), 0)
    headl, subl = rowl // CHUNK, (rowl % CHUNK) // SUB

    def row_of(r):
        picks = [jnp.sum(jnp.where(rowl == h * CHUNK + r, cum, 0.0), axis=0, keepdims=True) for h in range(g)]
        out = jnp.zeros((n, LANES), F32)
        for h in range(g):
            out = out + jnp.where(headl == h, picks[h], 0.0)
        return out, picks

    refs = [jnp.zeros((n, LANES), F32)] + [row_of(i * SUB - 1)[0] for i in range(1, CHUNK // SUB)]
    own = jnp.zeros((n, LANES), F32)
    for i, ref in enumerate(refs):
        own = own + jnp.where(subl == i, ref, 0.0)
    qt = q * jnp.exp(cum - own)
    att = jnp.zeros((n, n), F32)
    for i, ref in enumerate(refs):
        kt = k * jnp.exp(jnp.minimum(ref - cum, EXP_CLAMP))
        att = att + jnp.where(same & (sub == i) & (col <= row), _bdot(qt, kt, NT), 0.0)
    cl_rows, lasts = row_of(CHUNK - 1)
    o = _bdot(att, v, NN) + _bdot(_place(q * jnp.exp(cum), g), st, NT)
    kd = k * jnp.exp(cl_rows - cum)
    cl_wide = jnp.concatenate(lasts, axis=1) if g > 1 else lasts[0]
    st_new = st * jnp.exp(cl_wide) + _bdot(v, _place(kd, g), TN)
    return _gated_rmsnorm(o, zs, nw), st_new


def hgrn2_fwd(hproj, alb, nw, *, tb, hp):
    t, d4 = hproj.shape
    d = d4 // 4
    heads = d // LANES
    groups = heads // hp
    nb, nc = t // tb, tb // CHUNK

    def body(q_ref, f_ref, i_ref, z_ref, alb_ref, nw_ref, y_ref, states_ref, st_ref):
        b, hg = pl.program_id(0), pl.program_id(1)

        @pl.when(b == 0)
        def _():
            st_ref[hg] = jnp.zeros((LANES, hp * LANES), F32)

        nwv = nw_ref[...]
        lns = [slice(p * LANES, (p + 1) * LANES) for p in range(hp)]
        albs = []
        for r in range(3):
            albs.append(tuple(alb_ref[r:r + 1, pl.ds(pl.multiple_of((hg * hp + p) * LANES, LANES), LANES)]
                              for p in range(hp)))

        def step(c, carry):
            rows = pl.ds(pl.multiple_of(c * CHUNK, CHUNK), CHUNK)
            st = st_ref[hg]
            states_ref[c, 0] = st
            ins = [tuple(ref[rows, ln] for ln in lns) for ref in (q_ref, f_ref, i_ref, z_ref)]
            y, st_new = _hgrn2_chunk(*ins, *albs, nwv, st)
            for p in range(hp):
                y_ref[rows, lns[p]] = y[p * CHUNK:(p + 1) * CHUNK].astype(y_ref.dtype)
            st_ref[hg] = st_new
            return carry

        lax.fori_loop(0, nc, step, 0)

    return pl.pallas_call(
        body, name="hgrn2_fwd", grid=(nb, groups),
        in_specs=[_head_cols(tb, groups, hp, j) for j in range(4)] + [
            pl.BlockSpec((3, d), lambda b, h: (0, 0)),
            pl.BlockSpec((1, LANES), lambda b, h: (0, 0))],
        out_specs=[pl.BlockSpec((tb, hp * LANES), lambda b, h: (b, h)),
                   pl.BlockSpec((nc, 1, LANES, hp * LANES), lambda b, h: (b, h, 0, 0))],
        out_shape=[jax.ShapeDtypeStruct((t, d), BF16),
                   jax.ShapeDtypeStruct((t // CHUNK, groups, LANES, hp * LANES), F32)],
        scratch_shapes=[pltpu.VMEM((groups, LANES, hp * LANES), F32)],
        compiler_params=_cparams(("arbitrary", "arbitrary")),
    )(hproj, hproj, hproj, hproj, alb, nw)


def hgrn2_bwd(hproj, alb, nw, states, dy, *, tb, hp):
    t, d4 = hproj.shape
    d = d4 // 4
    heads = d // LANES
    groups = heads // hp
    nb, nc = t // tb, tb // CHUNK

    def body(q_ref, f_ref, i_ref, z_ref, alb_ref, nw_ref, states_ref, dy_ref,
             dh_ref, dalb_ref, dnw_ref, dst_ref):
        b, hg = pl.program_id(0), pl.program_id(1)

        @pl.when(b == 0)
        def _():
            dst_ref[hg] = jnp.zeros((LANES, hp * LANES), F32)

        @pl.when((b == 0) & (hg == 0))
        def _():
            dalb_ref[...] = jnp.zeros_like(dalb_ref)
            dnw_ref[...] = jnp.zeros_like(dnw_ref)

        nwv = nw_ref[...]
        lns = [slice(p * LANES, (p + 1) * LANES) for p in range(hp)]
        lanes_of = [pl.ds(pl.multiple_of((hg * hp + p) * LANES, LANES), LANES) for p in range(hp)]
        albs = [tuple(alb_ref[r:r + 1, lanes_of[p]] for p in range(hp)) for r in range(3)]

        def step(i, carry):
            c = nc - 1 - i
            rows = pl.ds(pl.multiple_of(c * CHUNK, CHUNK), CHUNK)
            ins = [tuple(ref[rows, ln] for ln in lns) for ref in (q_ref, f_ref, i_ref, z_ref)]
            dy = _stack([dy_ref[rows, ln].astype(F32) for ln in lns])
            _, vjp = jax.vjp(_hgrn2_chunk, *ins, *albs, nwv, states_ref[c, 0])
            dq, df, di, dz, da0, da1, da2, dnw, dst = vjp((dy, dst_ref[hg]))
            for p in range(hp):
                h = hg * hp + p
                for j, val in enumerate((dq, df, di, dz)):
                    dh_ref[rows, pl.ds(pl.multiple_of(j * d + h * LANES, LANES), LANES)] = val[p].astype(dh_ref.dtype)
                dalb_ref[0:1, lanes_of[p]] += da0[p]
                dalb_ref[1:2, lanes_of[p]] += da1[p]
                dalb_ref[2:3, lanes_of[p]] += da2[p]
            dnw_ref[0:1, :] += dnw
            dst_ref[hg] = dst
            return carry

        lax.fori_loop(0, nc, step, 0)

    return pl.pallas_call(
        body, name="hgrn2_bwd", grid=(nb, groups),
        in_specs=[_head_cols(tb, groups, hp, j, rev_nb=nb) for j in range(4)] + [
            pl.BlockSpec((3, d), lambda b, h: (0, 0)),
            pl.BlockSpec((1, LANES), lambda b, h: (0, 0)),
            pl.BlockSpec((nc, 1, LANES, hp * LANES), lambda b, h: (nb - 1 - b, h, 0, 0)),
            pl.BlockSpec((tb, hp * LANES), lambda b, h: (nb - 1 - b, h))],
        out_specs=[pl.BlockSpec((tb, d4), lambda b, h: (nb - 1 - b, 0)),
                   pl.BlockSpec((3, d), lambda b, h: (0, 0)),
                   pl.BlockSpec((8, LANES), lambda b, h: (0, 0))],
        out_shape=[jax.ShapeDtypeStruct((t, d4), BF16),
                   jax.ShapeDtypeStruct((3, d), F32),
                   jax.ShapeDtypeStruct((8, LANES), F32)],
        scratch_shapes=[pltpu.VMEM((groups, LANES, hp * LANES), F32)],
        compiler_params=_cparams(("arbitrary", "arbitrary")),
    )(hproj, hproj, hproj, hproj, alb, nw, states, dy)


def _conv_silu(xm, xh, w):
    acc = w[CONV_K - 1] * xm
    for j in range(CONV_K - 1):
        acc = acc + w[j] * _shift_rows(xm, xh, CONV_K - 1 - j)
    return _silu(acc)


def _l2norm(t):
    return t * lax.rsqrt(jnp.sum(t * t, axis=1, keepdims=True) + L2_EPS)


def _gdn_chunk(head0, hh, qm, qh, km, kh, vm, vh, z, tail, wq, wk, wv, alog_row, dtb_row, nw, s):
    g = len(qm)
    n = g * CHUNK
    row = lax.broadcasted_iota(jnp.int32, (n, n), 0)
    col = lax.broadcasted_iota(jnp.int32, (n, n), 1)
    same = (row // CHUNK) == (col // CHUNK)
    lane = lax.broadcasted_iota(jnp.int32, (CHUNK, LANES), 1)
    heads_lane = lax.broadcasted_iota(jnp.int32, (1, LANES), 1)
    q = _l2norm(_stack([_conv_silu(qm[h], qh[h], wq[h]) for h in range(g)])) * (LANES ** -0.5)
    k = _l2norm(_stack([_conv_silu(km[h], kh[h], wk[h]) for h in range(g)]))
    v = _stack([_conv_silu(vm[h], vh[h], wv[h]) for h in range(g)])
    zs = _stack(list(z))
    betas, gs = [], []
    for h in range(g):
        head = head0 + h
        betas.append(jax.nn.sigmoid(jnp.sum(jnp.where(lane == head, tail, 0.0), axis=1, keepdims=True)))
        a_t = jnp.sum(jnp.where(lane == hh + head, tail, 0.0), axis=1, keepdims=True)
        alog = jnp.sum(jnp.where(heads_lane == head, alog_row, 0.0), axis=1, keepdims=True)
        dtb = jnp.sum(jnp.where(heads_lane == head, dtb_row, 0.0), axis=1, keepdims=True)
        gs.append(-jnp.exp(alog) * _softplus(a_t + dtb))
    beta, gcol = _stack(betas), _stack(gs)
    tril = (same & (col <= row)).astype(F32)
    eye = (col == row).astype(F32)
    cum = _sdot(tril, gcol + jnp.zeros((n, n), F32))
    cum_r = _sdot(jnp.ones((n, n), F32), cum * eye)
    diff = cum - cum_r
    strict = same & (col < row)
    incl = same & (col <= row)
    dec_strict = jnp.where(strict, jnp.exp(jnp.where(strict, diff, 0.0)), 0.0)
    dec_incl = jnp.where(incl, jnp.exp(jnp.where(incl, diff, 0.0)), 0.0)
    cum_c = jnp.sum(cum * eye, axis=1, keepdims=True)
    last = same & (col % CHUNK == CHUNK - 1)
    cl_rows = jnp.sum(jnp.where(last, cum_r, 0.0), axis=1, keepdims=True)
    ecum = jnp.exp(cum_c)
    inv = _tri_inv(-(beta * _bdot(k, k, NT) * dec_strict))
    u0 = _dot3(inv, beta * v)
    w = _dot3(inv, (beta * ecum) * k)
    qk = _bdot(q, k, NT) * dec_incl
    u = u0 - _bdot(_place(w, g), s, NN)
    o = _bdot(_place(q * ecum, g), s, NN) + _bdot(qk, u, NN)
    kd = k * jnp.exp(cl_rows - cum_c)
    row1 = lax.broadcasted_iota(jnp.int32, (n, 1), 0)
    decay = []
    for h in range(g):
        cl_h = jnp.sum(jnp.where(row1 == h * CHUNK + CHUNK - 1, cum_c, 0.0), axis=0, keepdims=True)
        decay.append(jnp.exp(cl_h) + jnp.zeros((LANES, 1), F32))
    s_new = _stack(decay) * s + _bdot(_place(kd, g), u, TN)
    return _gated_rmsnorm(o, zs, nw), s_new


def _gdn_in_specs(tb, heads, hp, rev_nb=None):
    groups = heads // hp

    def bb(b):
        return b if rev_nb is None else rev_nb - 1 - b

    def halo(j):
        return pl.BlockSpec((HALO, hp * LANES),
                            lambda b, h, j=j: (jnp.maximum(bb(b) * (tb // HALO) - 1, 0), j * groups + h))

    def main(j):
        return _head_cols(tb, groups, hp, j, rev_nb=rev_nb)

    return [main(0), halo(0), main(1), halo(1), main(2), halo(2), main(3),
            pl.BlockSpec((tb, LANES), lambda b, h: (bb(b), 0)),
            pl.BlockSpec((CONV_K, 3 * heads * LANES), lambda b, h: (0, 0)),
            pl.BlockSpec((8, LANES), lambda b, h: (0, 0)),
            pl.BlockSpec((1, LANES), lambda b, h: (0, 0))]


def _gdn_chunk_args(c, head0, hp, blk, heads, q_ref, qh_ref, k_ref, kh_ref, v_ref, vh_ref, z_ref, tail_ref, cw_ref, prm_ref, nw_ref):
    d = heads * LANES
    lns = [slice(p * LANES, (p + 1) * LANES) for p in range(hp)]
    rows = pl.ds(pl.multiple_of(c * CHUNK, CHUNK), CHUNK)
    prev = pl.ds(pl.multiple_of(jnp.maximum(c * CHUNK - HALO, 0), HALO), HALO)
    first = c == 0
    live = jnp.where(first & (blk == 0), 0.0, 1.0)

    def main_of(ref):
        return tuple(ref[rows, ln] for ln in lns)

    def halo_of(ref, href):
        return tuple(jnp.where(first, href[:, ln], ref[prev, ln]) * live for ln in lns)

    def cw(j):
        out = []
        for p in range(hp):
            lanes = pl.ds(pl.multiple_of(j * d + (head0 + p) * LANES, LANES), LANES)
            out.append(tuple(cw_ref[r:r + 1, lanes] for r in range(CONV_K)))
        return tuple(out)

    return (main_of(q_ref), halo_of(q_ref, qh_ref), main_of(k_ref), halo_of(k_ref, kh_ref),
            main_of(v_ref), halo_of(v_ref, vh_ref), main_of(z_ref), tail_ref[rows, :],
            cw(0), cw(1), cw(2), prm_ref[0:1, :], prm_ref[1:2, :], nw_ref[...])


def gdn_fwd(hmain, tail, conv_w, prm, nw, *, tb, hp):
    t, d4 = hmain.shape
    d = d4 // 4
    heads = d // LANES
    groups = heads // hp
    nb, nc = t // tb, tb // CHUNK

    def body(q_ref, qh_ref, k_ref, kh_ref, v_ref, vh_ref, z_ref, tail_ref, cw_ref, prm_ref, nw_ref,
             y_ref, states_ref, s_ref):
        b, hg = pl.program_id(0), pl.program_id(1)

        @pl.when(b == 0)
        def _():
            s_ref[hg] = jnp.zeros((hp * LANES, LANES), F32)

        def step(c, carry):
            rows = pl.ds(pl.multiple_of(c * CHUNK, CHUNK), CHUNK)
            s = s_ref[hg]
            states_ref[c, 0] = s
            args = _gdn_chunk_args(c, hg * hp, hp, b, heads, q_ref, qh_ref, k_ref, kh_ref, v_ref, vh_ref, z_ref,
                                   tail_ref, cw_ref, prm_ref, nw_ref)
            y, s_new = _gdn_chunk(hg * hp, heads, *args, s)
            for p in range(hp):
                y_ref[rows, p * LANES:(p + 1) * LANES] = y[p * CHUNK:(p + 1) * CHUNK].astype(y_ref.dtype)
            s_ref[hg] = s_new
            return carry

        lax.fori_loop(0, nc, step, 0)

    return pl.pallas_call(
        body, name="gdn_fwd", grid=(nb, groups),
        in_specs=_gdn_in_specs(tb, heads, hp),
        out_specs=[pl.BlockSpec((tb, hp * LANES), lambda b, h: (b, h)),
                   pl.BlockSpec((nc, 1, hp * LANES, LANES), lambda b, h: (b, h, 0, 0))],
        out_shape=[jax.ShapeDtypeStruct((t, d), BF16),
                   jax.ShapeDtypeStruct((t // CHUNK, groups, hp * LANES, LANES), F32)],
        scratch_shapes=[pltpu.VMEM((groups, hp * LANES, LANES), F32)],
        compiler_params=_cparams(("arbitrary", "arbitrary")),
    )(hmain, hmain, hmain, hmain, hmain, hmain, hmain, tail, conv_w, prm, nw)


def gdn_bwd(hmain, tail, conv_w, prm, nw, states, dy, *, tb, hp):
    t, d4 = hmain.shape
    d = d4 // 4
    heads = d // LANES
    groups = heads // hp
    nb, nc = t // tb, tb // CHUNK

    def body(q_ref, qh_ref, k_ref, kh_ref, v_ref, vh_ref, z_ref, tail_ref, cw_ref, prm_ref, nw_ref,
             states_ref, dy_ref, dh_ref, dtail_ref, dcw_ref, dprm_ref, ds_ref, pend_ref):
        b, hg = pl.program_id(0), pl.program_id(1)
        blk = nb - 1 - b

        @pl.when(b == 0)
        def _():
            ds_ref[hg] = jnp.zeros((hp * LANES, LANES), F32)
            for p in range(hp):
                pend_ref[hg * hp + p] = jnp.zeros((3, HALO, LANES), F32)

        @pl.when((b == 0) & (hg == 0))
        def _():
            dcw_ref[...] = jnp.zeros_like(dcw_ref)
            dprm_ref[...] = jnp.zeros_like(dprm_ref)

        @pl.when(hg == 0)
        def _():
            dtail_ref[...] = jnp.zeros_like(dtail_ref)

        def step(i, carry):
            c = nc - 1 - i
            rows = pl.ds(pl.multiple_of(c * CHUNK, CHUNK), CHUNK)
            zpad = jnp.zeros((CHUNK - HALO, LANES), F32)
            args = _gdn_chunk_args(c, hg * hp, hp, blk, heads, q_ref, qh_ref, k_ref, kh_ref, v_ref, vh_ref, z_ref,
                                   tail_ref, cw_ref, prm_ref, nw_ref)
            dy = _stack([dy_ref[rows, p * LANES:(p + 1) * LANES].astype(F32) for p in range(hp)])
            pends = [pend_ref[hg * hp + p] for p in range(hp)]
            _, vjp = jax.vjp(functools.partial(_gdn_chunk, hg * hp, heads), *args, states_ref[c, 0])
            (dqm, dqh, dkm, dkh, dvm, dvh, dz, dtl, dwq, dwk, dwv, dalog, ddtb, dnw, ds) = vjp((dy, ds_ref[hg]))
            for p in range(hp):
                h = hg * hp + p
                for j, (dm, dhalo) in enumerate(((dqm, dqh), (dkm, dkh), (dvm, dvh))):
                    full = dm[p] + jnp.concatenate([zpad, pends[p][j]], axis=0)
                    dh_ref[rows, pl.ds(pl.multiple_of(j * d + h * LANES, LANES), LANES)] = full.astype(dh_ref.dtype)
                    pend_ref[h, j] = dhalo[p]
                dh_ref[rows, pl.ds(pl.multiple_of(3 * d + h * LANES, LANES), LANES)] = dz[p].astype(dh_ref.dtype)
                for j, dw in enumerate((dwq, dwk, dwv)):
                    lanes = pl.ds(pl.multiple_of(j * d + h * LANES, LANES), LANES)
                    for r in range(CONV_K):
                        dcw_ref[r:r + 1, lanes] += dw[p][r]
            dtail_ref[rows, :] += dtl
            dprm_ref[0:1, :] += dalog
            dprm_ref[1:2, :] += ddtb
            dprm_ref[2:3, :] += dnw
            ds_ref[hg] = ds
            return carry

        lax.fori_loop(0, nc, step, 0)

    return pl.pallas_call(
        body, name="gdn_bwd", grid=(nb, groups),
        in_specs=_gdn_in_specs(tb, heads, hp, rev_nb=nb) + [
            pl.BlockSpec((nc, 1, hp * LANES, LANES), lambda b, h: (nb - 1 - b, h, 0, 0)),
            pl.BlockSpec((tb, hp * LANES), lambda b, h: (nb - 1 - b, h))],
        out_specs=[pl.BlockSpec((tb, d4), lambda b, h: (nb - 1 - b, 0)),
                   pl.BlockSpec((tb, LANES), lambda b, h: (nb - 1 - b, 0)),
                   pl.BlockSpec((CONV_K, 3 * d), lambda b, h: (0, 0)),
                   pl.BlockSpec((8, LANES), lambda b, h: (0, 0))],
        out_shape=[jax.ShapeDtypeStruct((t, d4), BF16),
                   jax.ShapeDtypeStruct((t, LANES), F32),
                   jax.ShapeDtypeStruct((CONV_K, 3 * d), F32),
                   jax.ShapeDtypeStruct((8, LANES), F32)],
        scratch_shapes=[pltpu.VMEM((groups, hp * LANES, LANES), F32),
                        pltpu.VMEM((heads, 3, HALO, LANES), F32)],
        compiler_params=_cparams(("arbitrary", "arbitrary")),
    )(hmain, hmain, hmain, hmain, hmain, hmain, hmain, tail, conv_w, prm, nw, states, dy)


def _layer_norm(u, g, b):
    mu = jnp.mean(u, axis=1, keepdims=True)
    var = jnp.mean(jnp.square(u - mu), axis=1, keepdims=True)
    return (u - mu) * lax.rsqrt(var + LN_EPS) * g + b


def ln_fwd(u, g, b, *, tr):
    t, d = u.shape

    def body(u_ref, g_ref, b_ref, x_ref, xb_ref):
        x = _layer_norm(u_ref[...], g_ref[...], b_ref[...])
        x_ref[...] = x
        xb_ref[...] = x.astype(BF16)

    row = pl.BlockSpec((tr, d), lambda i: (i, 0))
    vec = pl.BlockSpec((1, d), lambda i: (0, 0))
    return pl.pallas_call(
        body, name="ln_fwd", grid=(t // tr,), in_specs=[row, vec, vec], out_specs=[row, row],
        out_shape=[jax.ShapeDtypeStruct((t, d), F32), jax.ShapeDtypeStruct((t, d), BF16)],
        compiler_params=_cparams(("parallel",)),
    )(u, g, b)


def ln_bwd(u, g, b, dout, *, tr):
    t, d = u.shape

    def body(u_ref, g_ref, b_ref, dout_ref, du_ref, dub_ref, dg_ref, db_ref):
        @pl.when(pl.program_id(0) == 0)
        def _():
            dg_ref[...] = jnp.zeros_like(dg_ref)
            db_ref[...] = jnp.zeros_like(db_ref)

        _, vjp = jax.vjp(_layer_norm, u_ref[...], g_ref[...], b_ref[...])
        du, dg, db = vjp(dout_ref[...])
        du_ref[...] = du
        dub_ref[...] = du.astype(BF16)
        dg_ref[0:1, :] += dg
        db_ref[0:1, :] += db

    row = pl.BlockSpec((tr, d), lambda i: (i, 0))
    vec = pl.BlockSpec((1, d), lambda i: (0, 0))
    acc = pl.BlockSpec((8, d), lambda i: (0, 0))
    return pl.pallas_call(
        body, name="ln_bwd", grid=(t // tr,), in_specs=[row, vec, vec, row], out_specs=[row, row, acc, acc],
        out_shape=[jax.ShapeDtypeStruct((t, d), F32), jax.ShapeDtypeStruct((t, d), BF16),
                   jax.ShapeDtypeStruct((8, d), F32), jax.ShapeDtypeStruct((8, d), F32)],
        compiler_params=_cparams(("arbitrary",)),
    )(u, g, b, dout)


def ln_loss_bwd(u, g, b, target, *, tr):
    t, d = u.shape

    def loss_of(uu, gg, bb, tgt):
        err = jnp.square(_layer_norm(uu, gg, bb) - tgt)
        return 0.5 * jnp.sum(jnp.mean(err, axis=1, keepdims=True), axis=0, keepdims=True)

    def body(u_ref, g_ref, b_ref, t_ref, loss_ref, du_ref, dub_ref, dg_ref, db_ref):
        @pl.when(pl.program_id(0) == 0)
        def _():
            loss_ref[...] = jnp.zeros_like(loss_ref)
            dg_ref[...] = jnp.zeros_like(dg_ref)
            db_ref[...] = jnp.zeros_like(db_ref)

        tgt = t_ref[...]
        val, vjp = jax.vjp(lambda uu, gg, bb: loss_of(uu, gg, bb, tgt), u_ref[...], g_ref[...], b_ref[...])
        du, dg, db = vjp(jnp.ones((1, 1), F32))
        loss_ref[...] += val
        du_ref[...] = du
        dub_ref[...] = du.astype(BF16)
        dg_ref[0:1, :] += dg
        db_ref[0:1, :] += db

    row = pl.BlockSpec((tr, d), lambda i: (i, 0))
    vec = pl.BlockSpec((1, d), lambda i: (0, 0))
    acc = pl.BlockSpec((8, d), lambda i: (0, 0))
    return pl.pallas_call(
        body, name="ln_loss_bwd", grid=(t // tr,), in_specs=[row, vec, vec, row],
        out_specs=[pl.BlockSpec((8, LANES), lambda i: (0, 0)), row, row, acc, acc],
        out_shape=[jax.ShapeDtypeStruct((8, LANES), F32),
                   jax.ShapeDtypeStruct((t, d), F32), jax.ShapeDtypeStruct((t, d), BF16),
                   jax.ShapeDtypeStruct((8, d), F32), jax.ShapeDtypeStruct((8, d), F32)],
        compiler_params=_cparams(("arbitrary",)),
    )(u, g, b, target)


def local_step(x, target, wa_in, alb, a_nw, wa_out, wb_main, wb_tail, conv_w, a_log, dt_bias, b_nw, wb_out,
               ln_g, ln_b, *, tb=256, tr=256, hp=HEADS_PER_STEP, first_after=None, b_weights=None, on_b_grads=None,
               on_a_grads=None):
    t, d = x.shape
    heads = d // LANES
    tb, tr, hp = min(tb, t), min(tr, t), min(hp, heads)
    xb = x.astype(BF16)
    prm = jnp.zeros((8, LANES), F32).at[0, :heads].set(a_log[0]).at[1, :heads].set(dt_bias[0])

    ha = matmul(xb, wa_in, name="mm_a_in", after=first_after)
    ya, st_a = hgrn2_fwd(ha, alb, a_nw, tb=tb, hp=hp)
    u1 = matmul(ya, wa_out, name="mm_a_out", addend=x, alpha=DEEPNORM_ALPHA)
    x1, x1b = ln_fwd(u1, ln_g[0:1], ln_b[0:1], tr=tr)
    if b_weights is not None:
        wb_main, wb_tail, wb_out, conv_w = b_weights(x1b)
    hb = matmul(x1b, wb_main, name="mm_b_in")
    tl = matmul(x1b, wb_tail, name="mm_b_tail")
    yb, st_b = gdn_fwd(hb, tl, conv_w, prm, b_nw, tb=tb, hp=hp)
    u2 = matmul(yb, wb_out, name="mm_b_out", addend=x1, alpha=DEEPNORM_ALPHA)

    loss, du2, du2b, dg2, db2 = ln_loss_bwd(u2, ln_g[1:2], ln_b[1:2], target, tr=tr)
    d_wb_out = matmul(yb.T, du2b, name="mm_dwb_out", out_dtype=BF16)
    dyb = matmul(du2b, wb_out, name="mm_dyb", nt=True)
    dhb, dtl, d_conv, dprm = gdn_bwd(hb, tl, conv_w, prm, b_nw, st_b, dyb, tb=tb, hp=hp)
    dtlb = dtl.astype(BF16)
    x1t = x1b.T
    d_wb_main = matmul(x1t, dhb, name="mm_dwb_main", out_dtype=BF16)
    d_wb_tail = matmul(x1t, dtlb, name="mm_dwb_tail", out_dtype=BF16)
    sent_b = on_b_grads(d_wb_main, d_wb_tail, d_wb_out) if on_b_grads is not None else None
    dx1_tail = matmul(dtlb, wb_tail, name="mm_dx1_tail", nt=True, addend=du2, alpha=DEEPNORM_ALPHA, after=sent_b)
    dx1 = matmul(dhb, wb_main, name="mm_dx1", nt=True, addend=dx1_tail, alpha=1.0)
    du1, du1b, dg1, db1 = ln_bwd(u1, ln_g[0:1], ln_b[0:1], dx1, tr=tr)
    d_wa_out = matmul(ya.T, du1b, name="mm_dwa_out", out_dtype=BF16)
    dya = matmul(du1b, wa_out, name="mm_dya", nt=True)
    dha, d_alb, d_anw = hgrn2_bwd(ha, alb, a_nw, st_a, dya, tb=tb, hp=hp)
    d_wa_in = matmul(xb.T, dha, name="mm_dwa_in", out_dtype=BF16, out_split=wa_in.shape[0] if wa_in.ndim == 3 else 1)
    sent_a = on_a_grads(d_wa_in, d_wa_out) if on_a_grads is not None else None
    grad_x = matmul(dha, wa_in, name="mm_dx", nt=True, addend=du1, alpha=DEEPNORM_ALPHA, after=sent_a)

    small = dict(
        a_lower_bounds=d_alb, a_norm_w=d_anw[0:1], b_conv_w=d_conv,
        b_a_log=dprm[0:1, :heads], b_dt_bias=dprm[1:2, :heads], b_norm_w=dprm[2:3],
        ln_g=jnp.concatenate([dg1[0:1], dg2[0:1]], axis=0), ln_b=jnp.concatenate([db1[0:1], db2[0:1]], axis=0))
    big = dict(a_w_in=d_wa_in, a_w_out=d_wa_out, b_w_main=d_wb_main, b_w_tail=d_wb_tail, b_w_out=d_wb_out)
    return loss, grad_x, big, small


MESH_ID = pl.DeviceIdType.MESH


def _position():
    return lax.axis_index("x"), lax.axis_index("y"), lax.axis_index("c")


def _index_of(p):
    return 4 * p[0] + 2 * p[1] + p[2]


def all_gather(shards, *, name, space):
    n = len(shards)

    def body(*refs):
        ins, outs = refs[:n], refs[n:2 * n]
        send_sems, recv_sems, local_sems = refs[2 * n:]
        x, y, c = _position()
        me, sibling = (x, y, c), (x, y, 1 - c)
        chips = [(1 - x, y), (x, 1 - y), (1 - x, 1 - y)]

        def copy(a, k, block, to, own=False):
            dst = outs[a].at[_index_of(block)]
            return pltpu.make_async_remote_copy(
                src_ref=ins[a] if own else dst, dst_ref=dst,
                send_sem=send_sems.at[7 * a + k], recv_sem=recv_sems.at[7 * a + k],
                device_id=to, device_id_type=MESH_ID)

        mine = [pltpu.make_async_copy(ins[a], outs[a].at[_index_of(me)], local_sems.at[a]) for a in range(n)]
        for cp in mine:
            cp.start()
        first = []
        for a in range(n):
            first.append(copy(a, 0, me, sibling, own=True))
            first += [copy(a, 1 + j, me, (*chip, c), own=True) for j, chip in enumerate(chips)]
        for cp in first:
            cp.start()
        passed = []
        for j, chip in enumerate(chips):
            for a in range(n):
                copy(a, 1 + j, (*chip, c), me).wait_recv()
                fwd = copy(a, 4 + j, (*chip, c), sibling)
                fwd.start()
                passed.append(fwd)
        for a in range(n):
            copy(a, 0, sibling, me).wait_recv()
            for j, chip in enumerate(chips):
                copy(a, 4 + j, (*chip, 1 - c), me).wait_recv()
        for cp in first + passed:
            cp.wait_send()
        for cp in mine:
            cp.wait()

    spec = pl.BlockSpec(memory_space=space)
    return pl.pallas_call(
        body, name=name,
        in_specs=[spec] * n, out_specs=[spec] * n,
        out_shape=[jax.ShapeDtypeStruct((N_DEV, *s.shape), s.dtype) for s in shards],
        scratch_shapes=[pltpu.SemaphoreType.DMA((7 * n,)), pltpu.SemaphoreType.DMA((7 * n,)),
                        pltpu.SemaphoreType.DMA((n,))],
        compiler_params=pltpu.CompilerParams(vmem_limit_bytes=VMEM_LIMIT),
    )(*shards)


HBM_SPEC = pl.BlockSpec(memory_space=pltpu.HBM)
SEM_SPEC = pl.BlockSpec(memory_space=pltpu.SEMAPHORE)


def _peer_copies(srcs, lands, send_sems, recv_sems, by_dest):
    x, y, c = _position()
    me = _index_of((x, y, c))
    copies = []
    for a, (src, land) in enumerate(zip(srcs, lands)):
        for r in range(1, N_DEV):
            peer = (x ^ ((r >> 2) & 1), y ^ ((r >> 1) & 1), c ^ (r & 1))
            copies.append(pltpu.make_async_remote_copy(
                src_ref=src.at[_index_of(peer)] if by_dest else src, dst_ref=land.at[me],
                send_sem=send_sems.at[7 * a + r - 1], recv_sem=recv_sems.at[7 * a + r - 1],
                device_id=peer, device_id_type=MESH_ID))
    return copies


def copies_start(srcs, lands, *, name, by_dest):
    n = len(srcs)

    def body(*refs):
        src_refs, land_refs = refs[:n], refs[n:2 * n]
        send_sems, recv_sems = refs[2 * n], refs[2 * n + 1]
        token = refs[-1]
        for cp in _peer_copies(src_refs, land_refs, send_sems, recv_sems, by_dest):
            cp.start()
        token[...] = jnp.zeros_like(token)

    arrays = [pltpu.with_memory_space_constraint(t, pltpu.HBM) for t in (*srcs, *lands)]
    outs = pl.pallas_call(
        body, name=name,
        out_shape=(pltpu.SemaphoreType.DMA((7 * n,)), pltpu.SemaphoreType.DMA((7 * n,)),
                   *[pltpu.HBM(t.shape, t.dtype) for t in arrays], jax.ShapeDtypeStruct((8, LANES), F32)),
        in_specs=[HBM_SPEC] * (2 * n),
        out_specs=(SEM_SPEC, SEM_SPEC, *[HBM_SPEC] * (2 * n), pl.BlockSpec(memory_space=pltpu.VMEM)),
        input_output_aliases={i: 2 + i for i in range(2 * n)},
        compiler_params=pltpu.CompilerParams(has_side_effects=pltpu.SideEffectType.DATAFLOW_SIDE_EFFECTING),
    )(*arrays)
    return (outs[0], outs[1]), list(outs[2:2 + n]), list(outs[2 + n:2 + 2 * n]), outs[-1]


def copies_wait(sems, srcs, lands, after, *, name, by_dest):
    n = len(srcs)

    def body(*refs):
        src_refs, land_refs = refs[:n], refs[n:2 * n]
        send_sems, recv_sems = refs[2 * n], refs[2 * n + 1]
        for cp in _peer_copies(src_refs, land_refs, send_sems, recv_sems, by_dest):
            cp.wait_send()
            cp.wait_recv()

    outs = pl.pallas_call(
        body, name=name,
        out_shape=tuple(pltpu.HBM(t.shape, t.dtype) for t in (*srcs, *lands)),
        in_specs=[HBM_SPEC] * (2 * n) + [SEM_SPEC, SEM_SPEC, pl.BlockSpec(memory_space=pl.ANY)],
        out_specs=tuple([HBM_SPEC] * (2 * n)),
        input_output_aliases={i: i for i in range(2 * n)},
        compiler_params=pltpu.CompilerParams(has_side_effects=pltpu.SideEffectType.DATAFLOW_SIDE_EFFECTING),
    )(*srcs, *lands, sems[0], sems[1], after)
    return list(outs[n:])


def _landing(own, me):
    return lax.dynamic_update_slice_in_dim(lax.empty((N_DEV, *own.shape), own.dtype), own[None], me, 0)


def adamw(parts, w, m, v, *, name, tr=64):
    p, r, c = parts.shape
    tr = min(tr, r)
    assert r % tr == 0
    c1 = 1.0 / (1.0 - ADAM_B1 ** ADAM_STEP)
    c2 = 1.0 / (1.0 - ADAM_B2 ** ADAM_STEP)

    def body(p_ref, w_ref, m_ref, v_ref, g_ref, d_ref, nm_ref, nv_ref):
        g = p_ref[0].astype(F32)
        for i in range(1, p):
            g = g + p_ref[i].astype(F32)
        nm = ADAM_B1 * m_ref[...] + (1.0 - ADAM_B1) * g
        nv = ADAM_B2 * v_ref[...] + (1.0 - ADAM_B2) * jnp.square(g)
        g_ref[...] = g
        nm_ref[...] = nm
        nv_ref[...] = nv
        d_ref[...] = -ADAM_LR * ((nm * c1) / (jnp.sqrt(nv * c2) + ADAM_EPS) + ADAM_WD * w_ref[...])

    blk = pl.BlockSpec((tr, c), lambda i: (i, 0))
    out = jax.ShapeDtypeStruct((r, c), F32)
    return pl.pallas_call(
        body, name=name, grid=(r // tr,),
        in_specs=[pl.BlockSpec((p, tr, c), lambda i: (0, i, 0)), blk, blk, blk],
        out_specs=[blk] * 4, out_shape=[out] * 4,
        compiler_params=_cparams(("parallel",)),
    )(parts, w, m, v)


def _pack(d, vals):
    heads = d // LANES
    vecs = jnp.zeros((8, LANES), F32)
    vecs = vecs.at[0:1].set(vals["a_norm_w"]).at[1:2, :heads].set(vals["b_a_log"])
    vecs = vecs.at[2:3, :heads].set(vals["b_dt_bias"]).at[3:4].set(vals["b_norm_w"])
    rows = [vals["a_lower_bounds"].reshape(-1, LANES), vals["ln_g"].reshape(-1, LANES),
            vals["ln_b"].reshape(-1, LANES), vecs]
    return jnp.concatenate(rows, axis=0)


def _unpack(d, packed):
    heads = d // LANES
    n3, n2 = 3 * heads, 2 * heads
    o = 0
    out = {}
    out["a_lower_bounds"] = packed[o:o + n3].reshape(3, d); o += n3
    out["ln_g"] = packed[o:o + n2].reshape(2, d); o += n2
    out["ln_b"] = packed[o:o + n2].reshape(2, d); o += n2
    out["a_norm_w"] = packed[o:o + 1]
    out["b_a_log"] = packed[o + 1:o + 2, :heads]
    out["b_dt_bias"] = packed[o + 2:o + 3, :heads]
    out["b_norm_w"] = packed[o + 3:o + 4]
    return out


ORDER = ("a_w_in", "a_lower_bounds", "a_norm_w", "a_w_out", "b_w_in", "b_conv_w", "b_a_log", "b_dt_bias", "b_norm_w",
         "b_w_out", "ln_g", "ln_b")


def kernel(x, a_w_in, a_lower_bounds, a_norm_w, a_w_out, b_w_in, b_conv_w, b_a_log, b_dt_bias, b_norm_w, b_w_out, ln_g, ln_b, loss_target, m_a_w_in, m_a_lower_bounds, m_a_norm_w, m_a_w_out, m_b_w_in, m_b_conv_w, m_b_a_log, m_b_dt_bias, m_b_norm_w, m_b_w_out, m_ln_g, m_ln_b, v_a_w_in, v_a_lower_bounds, v_a_norm_w, v_a_w_out, v_b_w_in, v_b_conv_w, v_b_a_log, v_b_dt_bias, v_b_norm_w, v_b_w_out, v_ln_g, v_ln_b):
    w = dict(a_w_in=a_w_in, a_lower_bounds=a_lower_bounds, a_norm_w=a_norm_w, a_w_out=a_w_out, b_w_in=b_w_in,
             b_conv_w=b_conv_w, b_a_log=b_a_log, b_dt_bias=b_dt_bias, b_norm_w=b_norm_w, b_w_out=b_w_out, ln_g=ln_g, ln_b=ln_b)
    m = dict(a_w_in=m_a_w_in, a_lower_bounds=m_a_lower_bounds, a_norm_w=m_a_norm_w, a_w_out=m_a_w_out, b_w_in=m_b_w_in,
             b_conv_w=m_b_conv_w, b_a_log=m_b_a_log, b_dt_bias=m_b_dt_bias, b_norm_w=m_b_norm_w, b_w_out=m_b_w_out,
             ln_g=m_ln_g, ln_b=m_ln_b)
    v = dict(a_w_in=v_a_w_in, a_lower_bounds=v_a_lower_bounds, a_norm_w=v_a_norm_w, a_w_out=v_a_w_out, b_w_in=v_b_w_in,
             b_conv_w=v_b_conv_w, b_a_log=v_b_a_log, b_dt_bias=v_b_dt_bias, b_norm_w=v_b_norm_w, b_w_out=v_b_w_out,
             ln_g=v_ln_g, ln_b=v_ln_b)
    t, d = x.shape[1], x.shape[2]
    heads = d // LANES
    n_tail = 2 * heads
    me = _index_of(_position())

    ga_in, ga_out, g_conv = all_gather([a_w_in[0].astype(BF16), a_w_out[0].astype(BF16), b_conv_w[0]],
                                       name="gather_weights_a", space=pltpu.HBM)
    b_shards = [b_w_in[0].astype(BF16), b_w_out[0].astype(BF16)]
    sems_w, b_shards, lands_w, token_w = copies_start(b_shards, [_landing(s, me) for s in b_shards],
                                                      name="gather_weights_b_start", by_dest=False)
    wa_out = ga_out.reshape(d, d)
    conv_w = jnp.transpose(g_conv, (1, 0, 2)).reshape(CONV_K, 3 * d)

    def b_weights(after):
        gb_in, gb_out = copies_wait(sems_w, b_shards, lands_w, after, name="gather_weights_b_wait", by_dest=False)
        wb_full = jnp.transpose(gb_in, (1, 0, 2)).reshape(d, 4 * d + n_tail)
        wb_tail = jnp.concatenate([wb_full[:, 4 * d:], jnp.zeros((d, LANES - n_tail), BF16)], axis=1)
        return wb_full[:, :4 * d], wb_tail, gb_out.reshape(d, d), conv_w

    sent = {}

    def send_grads(key, parts):
        own = [lax.dynamic_index_in_dim(p, me, 0, keepdims=False) for p in parts]
        sems, parts, lands, token = copies_start(parts, [_landing(o, me) for o in own],
                                                 name="exchange_" + key + "_start", by_dest=True)
        sent[key] = (sems, parts, lands)
        return token

    def on_b_grads(d_main, d_tail, d_out):
        d_b_in = jnp.concatenate([d_main, d_tail[:, :n_tail]], axis=1)
        d_b_in = jnp.transpose(d_b_in.reshape(d, N_DEV, -1), (1, 0, 2))
        return send_grads("b", [d_b_in, d_out.reshape(N_DEV, d // N_DEV, d)])

    def on_a_grads(d_in, d_out):
        return send_grads("a", [d_in, d_out.reshape(N_DEV, d // N_DEV, d)])

    loss, grad_x, big, small = local_step(
        x[0], loss_target[0], ga_in, a_lower_bounds, a_norm_w, wa_out, None, None, None, b_a_log, b_dt_bias,
        b_norm_w, None, ln_g, ln_b, first_after=token_w, b_weights=b_weights, on_b_grads=on_b_grads, on_a_grads=on_a_grads)
    loss = lax.psum(loss[0, 0], ("x", "y", "c"))

    conv_rows = small["b_conv_w"].reshape(-1, LANES)
    n_conv = conv_rows.shape[0]
    packed_grads = jnp.concatenate([conv_rows, _pack(d, small)], axis=0)
    (got,) = all_gather([packed_grads], name="gather_small", space=pltpu.VMEM)
    res = {}
    packed = adamw(got[:, n_conv:], _pack(d, w), _pack(d, m), _pack(d, v), name="adamw_small", tr=4096)
    for k, vals in zip(("grad", "delta", "new_m", "new_v"), packed):
        res[k] = _unpack(d, vals)
    shard_ch = 3 * d // N_DEV
    conv_parts = lax.dynamic_slice_in_dim(got[:, :n_conv].reshape(N_DEV, CONV_K, 3 * d), me * shard_ch, shard_ch, axis=2)
    conv_out = adamw(conv_parts, b_conv_w[0], m_b_conv_w[0], v_b_conv_w[0], name="adamw_conv")
    for k, vals in zip(("grad", "delta", "new_m", "new_v"), conv_out):
        res[k]["b_conv_w"] = vals[None]

    after = conv_out[0]
    for key, names in (("b", ("b_w_in", "b_w_out")), ("a", ("a_w_in", "a_w_out"))):
        sems, parts, lands = sent[key]
        recv = copies_wait(sems, parts, lands, after, name="exchange_" + key + "_wait", by_dest=True)
        for name, got_parts in zip(names, recv):
            outs = adamw(got_parts, w[name][0], m[name][0], v[name][0], name="adamw_" + name)
            for k, vals in zip(("grad", "delta", "new_m", "new_v"), outs):
                res[k][name] = vals[None]
            after = outs[0]

    return (loss, grad_x[None], *[res["grad"][k] for k in ORDER], *[res["delta"][k] for k in ORDER],
            *[res["new_m"][k] for k in ORDER], *[res["new_v"][k] for k in ORDER])
```

```python
import functools
import math

import jax
import jax.numpy as jnp
from jax import lax
from jax.experimental import pallas as pl
from jax.experimental.pallas import tpu as pltpu

F32 = jnp.float32
BF16 = jnp.bfloat16

N_DEV = 8
LANES = 128
CHUNK = 64
SUB = 16
HALO = 8
CONV_K = 4
DEPTH = 2
DEEPNORM_ALPHA = (2.0 * DEPTH) ** 0.25
LN_EPS = 1e-5
RMS_EPS = 1e-6
L2_EPS = 1e-6
EXP_CLAMP = 60.0
ADAM_LR = 0.001
ADAM_B1 = 0.9
ADAM_B2 = 0.999
ADAM_EPS = 1e-08
ADAM_WD = 0.01
ADAM_STEP = 10
VMEM_LIMIT = 56 * 1024 * 1024
HEADS_PER_STEP = 4

NN = ((1,), (0,))
NT = ((1,), (1,))
TN = ((0,), (0,))


def _dg(a, b, dims, precision=None):
    return lax.dot_general(a, b, (dims, ((), ())), precision=precision, preferred_element_type=F32)


def _bdot(a, b, dims):
    return _dg(a.astype(BF16), b.astype(BF16), dims)


def _split3(t):
    hi = t.astype(BF16)
    r = t - hi.astype(F32)
    mid = r.astype(BF16)
    return hi, mid, (r - mid.astype(F32)).astype(BF16)


@jax.custom_vjp
def _sdot(sel, t):
    sel = sel.astype(BF16)
    hi, mid, lo = _split3(t)
    return (_dg(sel, lo, NN) + _dg(sel, mid, NN)) + _dg(sel, hi, NN)


def _sdot_fwd(sel, t):
    return _sdot(sel, t), sel


def _sdot_bwd(sel, ct):
    selb = sel.astype(BF16)
    hi, mid, lo = _split3(ct)
    return jnp.zeros_like(sel), (_dg(selb, lo, TN) + _dg(selb, mid, TN)) + _dg(selb, hi, TN)


_sdot.defvjp(_sdot_fwd, _sdot_bwd)


@jax.custom_vjp
def _tri_inv(m):
    n = m.shape[0]
    row = lax.broadcasted_iota(jnp.int32, (n, n), 0)
    col = lax.broadcasted_iota(jnp.int32, (n, n), 1)
    inv = (col == row).astype(F32) + m
    mp = m
    for _ in range(int(math.log2(CHUNK)) - 1):
        mp = _bdot(mp, mp, NN)
        inv = inv + _bdot(inv, mp, NN)
    return inv


def _tri_inv_fwd(m):
    inv = _tri_inv(m)
    return inv, inv


def _tri_inv_bwd(inv, ct):
    return (_bdot(_bdot(inv, ct, TN), inv, NT),)


_tri_inv.defvjp(_tri_inv_fwd, _tri_inv_bwd)


@functools.partial(jax.custom_vjp, nondiff_argnums=(2,))
def _shift_rows(xm, xh, back):
    r = pltpu.roll(xm, back, 0)
    row = lax.broadcasted_iota(jnp.int32, xh.shape, 0)
    top = jnp.where(row < back, pltpu.roll(xh, back, 0), r[0:HALO])
    return jnp.concatenate([top, r[HALO:]], axis=0)


def _shift_rows_fwd(xm, xh, back):
    return _shift_rows(xm, xh, back), None


def _shift_rows_bwd(back, _, ct):
    row = lax.broadcasted_iota(jnp.int32, ct.shape, 0)
    dxm = jnp.where(row < CHUNK - back, pltpu.roll(ct, CHUNK - back, 0), 0.0)
    rowh = lax.broadcasted_iota(jnp.int32, (HALO, ct.shape[1]), 0)
    dxh = jnp.where(rowh >= HALO - back, pltpu.roll(ct[0:HALO], HALO - back, 0), 0.0)
    return dxm, dxh


_shift_rows.defvjp(_shift_rows_fwd, _shift_rows_bwd)


def _silu(t):
    return t * jax.nn.sigmoid(t)


def _softplus(t):
    return jnp.where(t > 20.0, t, jnp.log1p(jnp.exp(jnp.minimum(t, 20.0))))


def _cparams(sem=None):
    kw = dict(vmem_limit_bytes=VMEM_LIMIT)
    if sem is not None:
        kw["dimension_semantics"] = sem
    return pltpu.CompilerParams(**kw)


def matmul(a, b, *, name, nt=False, addend=None, alpha=1.0, out_dtype=F32, out_split=1, after=None,
           tm=1024, tn=1024, tk=4096):
    m, k = a.shape
    b_split = b.shape[0] if b.ndim == 3 else 1
    b_rows, b_cols = b.shape[-2], b.shape[-1] * b_split
    n = b_rows if nt else b_cols
    tm, tn, tk = min(tm, m), min(tn, n), min(tk, k)
    if b_split > 1:
        part = b_cols // b_split
        tn, tk = (tn, min(tk, part)) if nt else (min(tn, part), tk)
    if out_split > 1:
        tn = min(tn, n // out_split)
    assert m % tm == 0 and n % tn == 0 and k % tk == 0, (a.shape, b.shape, nt)
    nk = k // tk
    dims = NT if nt else NN

    def body(*refs):
        a_ref, b_ref = refs[:2]
        add_ref = None if addend is None else refs[2]

        def finish(r, o_ref):
            if add_ref is not None:
                r = r + alpha * add_ref[...].astype(F32)
            o_ref[...] = r.astype(o_ref.dtype)

        if nk == 1:
            finish(_dg(a_ref[...], b_ref[...], dims), refs[-1])
            return
        o_ref, acc_ref = refs[-2:]
        kk = pl.program_id(2)

        @pl.when(kk == 0)
        def _():
            acc_ref[...] = _dg(a_ref[...], b_ref[...], dims)

        @pl.when(kk > 0)
        def _():
            acc_ref[...] += _dg(a_ref[...], b_ref[...], dims)

        @pl.when(kk == nk - 1)
        def _():
            finish(acc_ref[...], o_ref)

    if b_split == 1:
        b_spec = (pl.BlockSpec((tn, tk), lambda i, j, kk: (j, kk)) if nt
                  else pl.BlockSpec((tk, tn), lambda i, j, kk: (kk, j)))
    elif nt:
        per = (b_cols // b_split) // tk
        b_spec = pl.BlockSpec((None, tn, tk), lambda i, j, kk: (kk // per, j, kk % per))
    else:
        per = (b_cols // b_split) // tn
        b_spec = pl.BlockSpec((None, tk, tn), lambda i, j, kk: (j // per, kk, j % per))
    in_specs = [pl.BlockSpec((tm, tk), lambda i, j, kk: (i, kk)), b_spec]
    args = [a, b]
    if addend is not None:
        in_specs.append(pl.BlockSpec((tm, tn), lambda i, j, kk: (i, j)))
        args.append(addend)
    if after is not None:
        in_specs.append(pl.BlockSpec(memory_space=pl.ANY))
        args.append(after)
    if out_split == 1:
        out_spec = pl.BlockSpec((tm, tn), lambda i, j, kk: (i, j))
        out_shape = jax.ShapeDtypeStruct((m, n), out_dtype)
    else:
        per_o = (n // out_split) // tn
        out_spec = pl.BlockSpec((None, tm, tn), lambda i, j, kk: (j // per_o, i, j % per_o))
        out_shape = jax.ShapeDtypeStruct((out_split, m, n // out_split), out_dtype)
    return pl.pallas_call(
        body, name=name, grid=(m // tm, n // tn, nk),
        in_specs=in_specs,
        out_specs=out_spec,
        out_shape=out_shape,
        scratch_shapes=[] if nk == 1 else [pltpu.VMEM((tm, tn), F32)],
        compiler_params=_cparams(("parallel", "parallel", "arbitrary")),
    )(*args)


def _gated_rmsnorm(o, z, nw):
    r = lax.rsqrt(jnp.mean(o * o, axis=1, keepdims=True) + RMS_EPS)
    return o * r * nw * _silu(z)


def _chunk_consts():
    row = lax.broadcasted_iota(jnp.int32, (CHUNK, CHUNK), 0)
    col = lax.broadcasted_iota(jnp.int32, (CHUNK, CHUNK), 1)
    return row, col


def _head_cols(tb, groups, hp, j, rev_nb=None):
    def bb(b):
        return b if rev_nb is None else rev_nb - 1 - b
    return pl.BlockSpec((tb, hp * LANES), lambda b, h: (bb(b), j * groups + h))


def _stack(pieces):
    return jnp.concatenate(pieces, axis=0) if len(pieces) > 1 else pieces[0]


def _place(t, g):
    if g == 1:
        return t
    head = lax.broadcasted_iota(jnp.int32, t.shape, 0) // CHUNK
    return jnp.concatenate([jnp.where(head == h, t, 0.0) for h in range(g)], axis=1)


def _hgrn2_chunk(qr, fr, iv, z, a0, a1, a2, nw, st):
    g = len(qr)
    n = g * CHUNK
    row = lax.broadcasted_iota(jnp.int32, (n, n), 0)
    col = lax.broadcasted_iota(jnp.int32, (n, n), 1)
    same = (row // CHUNK) == (col // CHUNK)
    qs, ks, gl = [], [], []
    for h in range(g):
        amax = jnp.maximum(jnp.maximum(a0[h], a1[h]), a2[h])
        e0, e1, e2 = jnp.exp(a0[h] - amax), jnp.exp(a1[h] - amax), jnp.exp(a2[h] - amax)
        lb = e0 / (e0 + e1 + e2)
        forget = lb + (1.0 - lb) * jax.nn.sigmoid(fr[h])
        qs.append(_silu(qr[h]))
        ks.append(1.0 - forget)
        gl.append(jnp.log(forget))
    q, k, glog, v, zs = _stack(qs), _stack(ks), _stack(gl), _stack(list(iv)), _stack(list(z))
    cum = _sdot((same & (col <= row)).astype(F32), glog)
    sub = (row % CHUNK) // SUB
    rowl = lax.broadcasted_iota(jnp.int32, (n, LANES), 0)
    headl, subl = rowl // CHUNK, (rowl % CHUNK) // SUB

    def row_of(r):
        picks = [jnp.sum(jnp.where(rowl == h * CHUNK + r, cum, 0.0), axis=0, keepdims=True) for h in range(g)]
        out = jnp.zeros((n, LANES), F32)
        for h in range(g):
            out = out + jnp.where(headl == h, picks[h], 0.0)
        return out, picks

    refs = [jnp.zeros((n, LANES), F32)] + [row_of(i * SUB - 1)[0] for i in range(1, CHUNK // SUB)]
    own = jnp.zeros((n, LANES), F32)
    for i, ref in enumerate(refs):
        own = own + jnp.where(subl == i, ref, 0.0)
    qt = q * jnp.exp(cum - own)
    att = jnp.zeros((n, n), F32)
    for i, ref in enumerate(refs):
        kt = k * jnp.exp(jnp.minimum(ref - cum, EXP_CLAMP))
        att = att + jnp.where(same & (sub == i) & (col <= row), _bdot(qt, kt, NT), 0.0)
    cl_rows, lasts = row_of(CHUNK - 1)
    o = _bdot(att, v, NN) + _bdot(_place(q * jnp.exp(cum), g), st, NT)
    kd = k * jnp.exp(cl_rows - cum)
    cl_wide = jnp.concatenate(lasts, axis=1) if g > 1 else lasts[0]
    st_new = st * jnp.exp(cl_wide) + _bdot(v, _place(kd, g), TN)
    return _gated_rmsnorm(o, zs, nw), st_new


def hgrn2_fwd(hproj, alb, nw, *, tb, hp):
    t, d4 = hproj.shape
    d = d4 // 4
    heads = d // LANES
    groups = heads // hp
    nb, nc = t // tb, tb // CHUNK

    def body(q_ref, f_ref, i_ref, z_ref, alb_ref, nw_ref, y_ref, states_ref, st_ref):
        b, hg = pl.program_id(0), pl.program_id(1)

        @pl.when(b == 0)
        def _():
            st_ref[hg] = jnp.zeros((LANES, hp * LANES), F32)

        nwv = nw_ref[...]
        lns = [slice(p * LANES, (p + 1) * LANES) for p in range(hp)]
        albs = []
        for r in range(3):
            albs.append(tuple(alb_ref[r:r + 1, pl.ds(pl.multiple_of((hg * hp + p) * LANES, LANES), LANES)]
                              for p in range(hp)))

        def step(c, carry):
            rows = pl.ds(pl.multiple_of(c * CHUNK, CHUNK), CHUNK)
            st = st_ref[hg]
            states_ref[c, 0] = st
            ins = [tuple(ref[rows, ln] for ln in lns) for ref in (q_ref, f_ref, i_ref, z_ref)]
            y, st_new = _hgrn2_chunk(*ins, *albs, nwv, st)
            for p in range(hp):
                y_ref[rows, lns[p]] = y[p * CHUNK:(p + 1) * CHUNK].astype(y_ref.dtype)
            st_ref[hg] = st_new
            return carry

        lax.fori_loop(0, nc, step, 0)

    return pl.pallas_call(
        body, name="hgrn2_fwd", grid=(nb, groups),
        in_specs=[_head_cols(tb, groups, hp, j) for j in range(4)] + [
            pl.BlockSpec((3, d), lambda b, h: (0, 0)),
            pl.BlockSpec((1, LANES), lambda b, h: (0, 0))],
        out_specs=[pl.BlockSpec((tb, hp * LANES), lambda b, h: (b, h)),
                   pl.BlockSpec((nc, 1, LANES, hp * LANES), lambda b, h: (b, h, 0, 0))],
        out_shape=[jax.ShapeDtypeStruct((t, d), BF16),
                   jax.ShapeDtypeStruct((t // CHUNK, groups, LANES, hp * LANES), F32)],
        scratch_shapes=[pltpu.VMEM((groups, LANES, hp * LANES), F32)],
        compiler_params=_cparams(("arbitrary", "arbitrary")),
    )(hproj, hproj, hproj, hproj, alb, nw)


def hgrn2_bwd(hproj, alb, nw, states, dy, *, tb, hp):
    t, d4 = hproj.shape
    d = d4 // 4
    heads = d // LANES
    groups = heads // hp
    nb, nc = t // tb, tb // CHUNK

    def body(q_ref, f_ref, i_ref, z_ref, alb_ref, nw_ref, states_ref, dy_ref,
             dh_ref, dalb_ref, dnw_ref, dst_ref):
        b, hg = pl.program_id(0), pl.program_id(1)

        @pl.when(b == 0)
        def _():
            dst_ref[hg] = jnp.zeros((LANES, hp * LANES), F32)

        @pl.when((b == 0) & (hg == 0))
        def _():
            dalb_ref[...] = jnp.zeros_like(dalb_ref)
            dnw_ref[...] = jnp.zeros_like(dnw_ref)

        nwv = nw_ref[...]
        lns = [slice(p * LANES, (p + 1) * LANES) for p in range(hp)]
        lanes_of = [pl.ds(pl.multiple_of((hg * hp + p) * LANES, LANES), LANES) for p in range(hp)]
        albs = [tuple(alb_ref[r:r + 1, lanes_of[p]] for p in range(hp)) for r in range(3)]

        def step(i, carry):
            c = nc - 1 - i
            rows = pl.ds(pl.multiple_of(c * CHUNK, CHUNK), CHUNK)
            ins = [tuple(ref[rows, ln] for ln in lns) for ref in (q_ref, f_ref, i_ref, z_ref)]
            dy = _stack([dy_ref[rows, ln].astype(F32) for ln in lns])
            _, vjp = jax.vjp(_hgrn2_chunk, *ins, *albs, nwv, states_ref[c, 0])
            dq, df, di, dz, da0, da1, da2, dnw, dst = vjp((dy, dst_ref[hg]))
            for p in range(hp):
                h = hg * hp + p
                for j, val in enumerate((dq, df, di, dz)):
                    dh_ref[rows, pl.ds(pl.multiple_of(j * d + h * LANES, LANES), LANES)] = val[p].astype(dh_ref.dtype)
                dalb_ref[0:1, lanes_of[p]] += da0[p]
                dalb_ref[1:2, lanes_of[p]] += da1[p]
                dalb_ref[2:3, lanes_of[p]] += da2[p]
            dnw_ref[0:1, :] += dnw
            dst_ref[hg] = dst
            return carry

        lax.fori_loop(0, nc, step, 0)

    return pl.pallas_call(
        body, name="hgrn2_bwd", grid=(nb, groups),
        in_specs=[_head_cols(tb, groups, hp, j, rev_nb=nb) for j in range(4)] + [
            pl.BlockSpec((3, d), lambda b, h: (0, 0)),
            pl.BlockSpec((1, LANES), lambda b, h: (0, 0)),
            pl.BlockSpec((nc, 1, LANES, hp * LANES), lambda b, h: (nb - 1 - b, h, 0, 0)),
            pl.BlockSpec((tb, hp * LANES), lambda b, h: (nb - 1 - b, h))],
        out_specs=[pl.BlockSpec((tb, d4), lambda b, h: (nb - 1 - b, 0)),
                   pl.BlockSpec((3, d), lambda b, h: (0, 0)),
                   pl.BlockSpec((8, LANES), lambda b, h: (0, 0))],
        out_shape=[jax.ShapeDtypeStruct((t, d4), BF16),
                   jax.ShapeDtypeStruct((3, d), F32),
                   jax.ShapeDtypeStruct((8, LANES), F32)],
        scratch_shapes=[pltpu.VMEM((groups, LANES, hp * LANES), F32)],
        compiler_params=_cparams(("arbitrary", "arbitrary")),
    )(hproj, hproj, hproj, hproj, alb, nw, states, dy)


def _conv_silu(xm, xh, w):
    acc = w[CONV_K - 1] * xm
    for j in range(CONV_K - 1):
        acc = acc + w[j] * _shift_rows(xm, xh, CONV_K - 1 - j)
    return _silu(acc)


def _l2norm(t):
    return t * lax.rsqrt(jnp.sum(t * t, axis=1, keepdims=True) + L2_EPS)


def _gdn_chunk(head0, hh, qm, qh, km, kh, vm, vh, z, tail, wq, wk, wv, alog_row, dtb_row, nw, s):
    g = len(qm)
    n = g * CHUNK
    row = lax.broadcasted_iota(jnp.int32, (n, n), 0)
    col = lax.broadcasted_iota(jnp.int32, (n, n), 1)
    same = (row // CHUNK) == (col // CHUNK)
    lane = lax.broadcasted_iota(jnp.int32, (CHUNK, LANES), 1)
    heads_lane = lax.broadcasted_iota(jnp.int32, (1, LANES), 1)
    q = _l2norm(_stack([_conv_silu(qm[h], qh[h], wq[h]) for h in range(g)])) * (LANES ** -0.5)
    k = _l2norm(_stack([_conv_silu(km[h], kh[h], wk[h]) for h in range(g)]))
    v = _stack([_conv_silu(vm[h], vh[h], wv[h]) for h in range(g)])
    zs = _stack(list(z))
    betas, gs = [], []
    for h in range(g):
        head = head0 + h
        betas.append(jax.nn.sigmoid(jnp.sum(jnp.where(lane == head, tail, 0.0), axis=1, keepdims=True)))
        a_t = jnp.sum(jnp.where(lane == hh + head, tail, 0.0), axis=1, keepdims=True)
        alog = jnp.sum(jnp.where(heads_lane == head, alog_row, 0.0), axis=1, keepdims=True)
        dtb = jnp.sum(jnp.where(heads_lane == head, dtb_row, 0.0), axis=1, keepdims=True)
        gs.append(-jnp.exp(alog) * _softplus(a_t + dtb))
    beta, gcol = _stack(betas), _stack(gs)
    tril = (same & (col <= row)).astype(F32)
    eye = (col == row).astype(F32)
    cum = _sdot(tril, gcol + jnp.zeros((n, n), F32))
    cum_r = cum.T
    diff = cum - cum_r
    strict = same & (col < row)
    incl = same & (col <= row)
    dec_strict = jnp.where(strict, jnp.exp(jnp.where(strict, diff, 0.0)), 0.0)
    dec_incl = jnp.where(incl, jnp.exp(jnp.where(incl, diff, 0.0)), 0.0)
    cum_c = jnp.sum(cum * eye, axis=1, keepdims=True)
    last = same & (col % CHUNK == CHUNK - 1)
    cl_rows = jnp.sum(jnp.where(last, cum_r, 0.0), axis=1, keepdims=True)
    ecum = jnp.exp(cum_c)
    inv = _tri_inv(-(beta * _bdot(k, k, NT) * dec_strict))
    u0 = _bdot(inv, beta * v, NN)
    w = _bdot(inv, (beta * ecum) * k, NN)
    qk = _bdot(q, k, NT) * dec_incl
    u = u0 - _bdot(_place(w, g), s, NN)
    o = _bdot(_place(q * ecum, g), s, NN) + _bdot(qk, u, NN)
    kd = k * jnp.exp(cl_rows - cum_c)
    row1 = lax.broadcasted_iota(jnp.int32, (n, 1), 0)
    decay = []
    for h in range(g):
        cl_h = jnp.sum(jnp.where(row1 == h * CHUNK + CHUNK - 1, cum_c, 0.0), axis=0, keepdims=True)
        decay.append(jnp.exp(cl_h) + jnp.zeros((LANES, 1), F32))
    s_new = _stack(decay) * s + _bdot(_place(kd, g), u, TN)
    return _gated_rmsnorm(o, zs, nw), s_new


def _gdn_in_specs(tb, heads, hp, rev_nb=None):
    groups = heads // hp

    def bb(b):
        return b if rev_nb is None else rev_nb - 1 - b

    def halo(j):
        return pl.BlockSpec((HALO, hp * LANES),
                            lambda b, h, j=j: (jnp.maximum(bb(b) * (tb // HALO) - 1, 0), j * groups + h))

    def main(j):
        return _head_cols(tb, groups, hp, j, rev_nb=rev_nb)

    return [main(0), halo(0), main(1), halo(1), main(2), halo(2), main(3),
            pl.BlockSpec((tb, LANES), lambda b, h: (bb(b), 0)),
            pl.BlockSpec((CONV_K, 3 * heads * LANES), lambda b, h: (0, 0)),
            pl.BlockSpec((8, LANES), lambda b, h: (0, 0)),
            pl.BlockSpec((1, LANES), lambda b, h: (0, 0))]


def _gdn_chunk_args(c, head0, hp, blk, heads, q_ref, qh_ref, k_ref, kh_ref, v_ref, vh_ref, z_ref, tail_ref, cw_ref, prm_ref, nw_ref):
    d = heads * LANES
    lns = [slice(p * LANES, (p + 1) * LANES) for p in range(hp)]
    rows = pl.ds(pl.multiple_of(c * CHUNK, CHUNK), CHUNK)
    prev = pl.ds(pl.multiple_of(jnp.maximum(c * CHUNK - HALO, 0), HALO), HALO)
    first = c == 0
    live = jnp.where(first & (blk == 0), 0.0, 1.0)

    def main_of(ref):
        return tuple(ref[rows, ln] for ln in lns)

    def halo_of(ref, href):
        return tuple(jnp.where(first, href[:, ln], ref[prev, ln]) * live for ln in lns)

    def cw(j):
        out = []
        for p in range(hp):
            lanes = pl.ds(pl.multiple_of(j * d + (head0 + p) * LANES, LANES), LANES)
            out.append(tuple(cw_ref[r:r + 1, lanes] for r in range(CONV_K)))
        return tuple(out)

    return (main_of(q_ref), halo_of(q_ref, qh_ref), main_of(k_ref), halo_of(k_ref, kh_ref),
            main_of(v_ref), halo_of(v_ref, vh_ref), main_of(z_ref), tail_ref[rows, :],
            cw(0), cw(1), cw(2), prm_ref[0:1, :], prm_ref[1:2, :], nw_ref[...])


def gdn_fwd(hmain, tail, conv_w, prm, nw, *, tb, hp):
    t, d4 = hmain.shape
    d = d4 // 4
    heads = d // LANES
    groups = heads // hp
    nb, nc = t // tb, tb // CHUNK

    def body(q_ref, qh_ref, k_ref, kh_ref, v_ref, vh_ref, z_ref, tail_ref, cw_ref, prm_ref, nw_ref,
             y_ref, states_ref, s_ref):
        b, hg = pl.program_id(0), pl.program_id(1)

        @pl.when(b == 0)
        def _():
            s_ref[hg] = jnp.zeros((hp * LANES, LANES), F32)

        def step(c, carry):
            rows = pl.ds(pl.multiple_of(c * CHUNK, CHUNK), CHUNK)
            s = s_ref[hg]
            states_ref[c, 0] = s
            args = _gdn_chunk_args(c, hg * hp, hp, b, heads, q_ref, qh_ref, k_ref, kh_ref, v_ref, vh_ref, z_ref,
                                   tail_ref, cw_ref, prm_ref, nw_ref)
            y, s_new = _gdn_chunk(hg * hp, heads, *args, s)
            for p in range(hp):
                y_ref[rows, p * LANES:(p + 1) * LANES] = y[p * CHUNK:(p + 1) * CHUNK].astype(y_ref.dtype)
            s_ref[hg] = s_new
            return carry

        lax.fori_loop(0, nc, step, 0)

    return pl.pallas_call(
        body, name="gdn_fwd", grid=(nb, groups),
        in_specs=_gdn_in_specs(tb, heads, hp),
        out_specs=[pl.BlockSpec((tb, hp * LANES), lambda b, h: (b, h)),
                   pl.BlockSpec((nc, 1, hp * LANES, LANES), lambda b, h: (b, h, 0, 0))],
        out_shape=[jax.ShapeDtypeStruct((t, d), BF16),
                   jax.ShapeDtypeStruct((t // CHUNK, groups, hp * LANES, LANES), F32)],
        scratch_shapes=[pltpu.VMEM((groups, hp * LANES, LANES), F32)],
        compiler_params=_cparams(("arbitrary", "arbitrary")),
    )(hmain, hmain, hmain, hmain, hmain, hmain, hmain, tail, conv_w, prm, nw)


def gdn_bwd(hmain, tail, conv_w, prm, nw, states, dy, *, tb, hp):
    t, d4 = hmain.shape
    d = d4 // 4
    heads = d // LANES
    groups = heads // hp
    nb, nc = t // tb, tb // CHUNK

    def body(q_ref, qh_ref, k_ref, kh_ref, v_ref, vh_ref, z_ref, tail_ref, cw_ref, prm_ref, nw_ref,
             states_ref, dy_ref, dh_ref, dtail_ref, dcw_ref, dprm_ref, ds_ref, pend_ref):
        b, hg = pl.program_id(0), pl.program_id(1)
        blk = nb - 1 - b

        @pl.when(b == 0)
        def _():
            ds_ref[hg] = jnp.zeros((hp * LANES, LANES), F32)
            for p in range(hp):
                pend_ref[hg * hp + p] = jnp.zeros((3, HALO, LANES), F32)

        @pl.when((b == 0) & (hg == 0))
        def _():
            dcw_ref[...] = jnp.zeros_like(dcw_ref)
            dprm_ref[...] = jnp.zeros_like(dprm_ref)

        @pl.when(hg == 0)
        def _():
            dtail_ref[...] = jnp.zeros_like(dtail_ref)

        def step(i, carry):
            c = nc - 1 - i
            rows = pl.ds(pl.multiple_of(c * CHUNK, CHUNK), CHUNK)
            zpad = jnp.zeros((CHUNK - HALO, LANES), F32)
            args = _gdn_chunk_args(c, hg * hp, hp, blk, heads, q_ref, qh_ref, k_ref, kh_ref, v_ref, vh_ref, z_ref,
                                   tail_ref, cw_ref, prm_ref, nw_ref)
            dy = _stack([dy_ref[rows, p * LANES:(p + 1) * LANES].astype(F32) for p in range(hp)])
            pends = [pend_ref[hg * hp + p] for p in range(hp)]
            _, vjp = jax.vjp(functools.partial(_gdn_chunk, hg * hp, heads), *args, states_ref[c, 0])
            (dqm, dqh, dkm, dkh, dvm, dvh, dz, dtl, dwq, dwk, dwv, dalog, ddtb, dnw, ds) = vjp((dy, ds_ref[hg]))
            for p in range(hp):
                h = hg * hp + p
                for j, (dm, dhalo) in enumerate(((dqm, dqh), (dkm, dkh), (dvm, dvh))):
                    full = dm[p] + jnp.concatenate([zpad, pends[p][j]], axis=0)
                    dh_ref[rows, pl.ds(pl.multiple_of(j * d + h * LANES, LANES), LANES)] = full.astype(dh_ref.dtype)
                    pend_ref[h, j] = dhalo[p]
                dh_ref[rows, pl.ds(pl.multiple_of(3 * d + h * LANES, LANES), LANES)] = dz[p].astype(dh_ref.dtype)
                for j, dw in enumerate((dwq, dwk, dwv)):
                    lanes = pl.ds(pl.multiple_of(j * d + h * LANES, LANES), LANES)
                    for r in range(CONV_K):
                        dcw_ref[r:r + 1, lanes] += dw[p][r]
            dtail_ref[rows, :] += dtl
            dprm_ref[0:1, :] += dalog
            dprm_ref[1:2, :] += ddtb
            dprm_ref[2:3, :] += dnw
            ds_ref[hg] = ds
            return carry

        lax.fori_loop(0, nc, step, 0)

    return pl.pallas_call(
        body, name="gdn_bwd", grid=(nb, groups),
        in_specs=_gdn_in_specs(tb, heads, hp, rev_nb=nb) + [
            pl.BlockSpec((nc, 1, hp * LANES, LANES), lambda b, h: (nb - 1 - b, h, 0, 0)),
            pl.BlockSpec((tb, hp * LANES), lambda b, h: (nb - 1 - b, h))],
        out_specs=[pl.BlockSpec((tb, d4), lambda b, h: (nb - 1 - b, 0)),
                   pl.BlockSpec((tb, LANES), lambda b, h: (nb - 1 - b, 0)),
                   pl.BlockSpec((CONV_K, 3 * d), lambda b, h: (0, 0)),
                   pl.BlockSpec((8, LANES), lambda b, h: (0, 0))],
        out_shape=[jax.ShapeDtypeStruct((t, d4), BF16),
                   jax.ShapeDtypeStruct((t, LANES), F32),
                   jax.ShapeDtypeStruct((CONV_K, 3 * d), F32),
                   jax.ShapeDtypeStruct((8, LANES), F32)],
        scratch_shapes=[pltpu.VMEM((groups, hp * LANES, LANES), F32),
                        pltpu.VMEM((heads, 3, HALO, LANES), F32)],
        compiler_params=_cparams(("arbitrary", "arbitrary")),
    )(hmain, hmain, hmain, hmain, hmain, hmain, hmain, tail, conv_w, prm, nw, states, dy)


def _layer_norm(u, g, b):
    mu = jnp.mean(u, axis=1, keepdims=True)
    var = jnp.mean(jnp.square(u - mu), axis=1, keepdims=True)
    return (u - mu) * lax.rsqrt(var + LN_EPS) * g + b


def ln_fwd(u, g, b, *, tr):
    t, d = u.shape

    def body(u_ref, g_ref, b_ref, x_ref, xb_ref):
        x = _layer_norm(u_ref[...], g_ref[...], b_ref[...])
        x_ref[...] = x
        xb_ref[...] = x.astype(BF16)

    row = pl.BlockSpec((tr, d), lambda i: (i, 0))
    vec = pl.BlockSpec((1, d), lambda i: (0, 0))
    return pl.pallas_call(
        body, name="ln_fwd", grid=(t // tr,), in_specs=[row, vec, vec], out_specs=[row, row],
        out_shape=[jax.ShapeDtypeStruct((t, d), F32), jax.ShapeDtypeStruct((t, d), BF16)],
        compiler_params=_cparams(("parallel",)),
    )(u, g, b)


def ln_bwd(u, g, b, dout, *, tr):
    t, d = u.shape

    def body(u_ref, g_ref, b_ref, dout_ref, du_ref, dub_ref, dg_ref, db_ref):
        @pl.when(pl.program_id(0) == 0)
        def _():
            dg_ref[...] = jnp.zeros_like(dg_ref)
            db_ref[...] = jnp.zeros_like(db_ref)

        _, vjp = jax.vjp(_layer_norm, u_ref[...], g_ref[...], b_ref[...])
        du, dg, db = vjp(dout_ref[...])
        du_ref[...] = du
        dub_ref[...] = du.astype(BF16)
        dg_ref[0:1, :] += dg
        db_ref[0:1, :] += db

    row = pl.BlockSpec((tr, d), lambda i: (i, 0))
    vec = pl.BlockSpec((1, d), lambda i: (0, 0))
    acc = pl.BlockSpec((8, d), lambda i: (0, 0))
    return pl.pallas_call(
        body, name="ln_bwd", grid=(t // tr,), in_specs=[row, vec, vec, row], out_specs=[row, row, acc, acc],
        out_shape=[jax.ShapeDtypeStruct((t, d), F32), jax.ShapeDtypeStruct((t, d), BF16),
                   jax.ShapeDtypeStruct((8, d), F32), jax.ShapeDtypeStruct((8, d), F32)],
        compiler_params=_cparams(("arbitrary",)),
    )(u, g, b, dout)


def ln_loss_bwd(u, g, b, target, *, tr):
    t, d = u.shape

    def loss_of(uu, gg, bb, tgt):
        err = jnp.square(_layer_norm(uu, gg, bb) - tgt)
        return 0.5 * jnp.sum(jnp.mean(err, axis=1, keepdims=True), axis=0, keepdims=True)

    def body(u_ref, g_ref, b_ref, t_ref, loss_ref, du_ref, dub_ref, dg_ref, db_ref):
        @pl.when(pl.program_id(0) == 0)
        def _():
            loss_ref[...] = jnp.zeros_like(loss_ref)
            dg_ref[...] = jnp.zeros_like(dg_ref)
            db_ref[...] = jnp.zeros_like(db_ref)

        tgt = t_ref[...]
        val, vjp = jax.vjp(lambda uu, gg, bb: loss_of(uu, gg, bb, tgt), u_ref[...], g_ref[...], b_ref[...])
        du, dg, db = vjp(jnp.ones((1, 1), F32))
        loss_ref[...] += val
        du_ref[...] = du
        dub_ref[...] = du.astype(BF16)
        dg_ref[0:1, :] += dg
        db_ref[0:1, :] += db

    row = pl.BlockSpec((tr, d), lambda i: (i, 0))
    vec = pl.BlockSpec((1, d), lambda i: (0, 0))
    acc = pl.BlockSpec((8, d), lambda i: (0, 0))
    return pl.pallas_call(
        body, name="ln_loss_bwd", grid=(t // tr,), in_specs=[row, vec, vec, row],
        out_specs=[pl.BlockSpec((8, LANES), lambda i: (0, 0)), row, row, acc, acc],
        out_shape=[jax.ShapeDtypeStruct((8, LANES), F32),
                   jax.ShapeDtypeStruct((t, d), F32), jax.ShapeDtypeStruct((t, d), BF16),
                   jax.ShapeDtypeStruct((8, d), F32), jax.ShapeDtypeStruct((8, d), F32)],
        compiler_params=_cparams(("arbitrary",)),
    )(u, g, b, target)


def local_step(x, target, wa_in, alb, a_nw, wa_out, wb_main, wb_tail, conv_w, a_log, dt_bias, b_nw, wb_out,
               ln_g, ln_b, *, tb=256, tr=256, hp=HEADS_PER_STEP, first_after=None, b_weights=None, on_b_grads=None,
               on_a_grads=None):
    t, d = x.shape
    heads = d // LANES
    tb, tr, hp = min(tb, t), min(tr, t), min(hp, heads)
    xb = x.astype(BF16)
    prm = jnp.zeros((8, LANES), F32).at[0, :heads].set(a_log[0]).at[1, :heads].set(dt_bias[0])

    ha = matmul(xb, wa_in, name="mm_a_in", after=first_after)
    ya, st_a = hgrn2_fwd(ha, alb, a_nw, tb=tb, hp=hp)
    u1 = matmul(ya, wa_out, name="mm_a_out", addend=x, alpha=DEEPNORM_ALPHA)
    x1, x1b = ln_fwd(u1, ln_g[0:1], ln_b[0:1], tr=tr)
    if b_weights is not None:
        wb_main, wb_tail, wb_out, conv_w = b_weights(x1b)
    hb = matmul(x1b, wb_main, name="mm_b_in")
    tl = matmul(x1b, wb_tail, name="mm_b_tail")
    yb, st_b = gdn_fwd(hb, tl, conv_w, prm, b_nw, tb=tb, hp=hp)
    u2 = matmul(yb, wb_out, name="mm_b_out", addend=x1, alpha=DEEPNORM_ALPHA)

    loss, du2, du2b, dg2, db2 = ln_loss_bwd(u2, ln_g[1:2], ln_b[1:2], target, tr=tr)
    d_wb_out = matmul(yb.T, du2b, name="mm_dwb_out", out_dtype=BF16)
    dyb = matmul(du2b, wb_out, name="mm_dyb", nt=True)
    dhb, dtl, d_conv, dprm = gdn_bwd(hb, tl, conv_w, prm, b_nw, st_b, dyb, tb=tb, hp=hp)
    dtlb = dtl.astype(BF16)
    x1t = x1b.T
    d_wb_main = matmul(x1t, dhb, name="mm_dwb_main", out_dtype=BF16)
    d_wb_tail = matmul(x1t, dtlb, name="mm_dwb_tail", out_dtype=BF16)
    sent_b = on_b_grads(d_wb_main, d_wb_tail, d_wb_out) if on_b_grads is not None else None
    dx1_tail = matmul(dtlb, wb_tail, name="mm_dx1_tail", nt=True, addend=du2, alpha=DEEPNORM_ALPHA, after=sent_b)
    dx1 = matmul(dhb, wb_main, name="mm_dx1", nt=True, addend=dx1_tail, alpha=1.0, tm=512)
    du1, du1b, dg1, db1 = ln_bwd(u1, ln_g[0:1], ln_b[0:1], dx1, tr=tr)
    d_wa_out = matmul(ya.T, du1b, name="mm_dwa_out", out_dtype=BF16)
    dya = matmul(du1b, wa_out, name="mm_dya", nt=True)
    dha, d_alb, d_anw = hgrn2_bwd(ha, alb, a_nw, st_a, dya, tb=tb, hp=hp)
    d_wa_in = matmul(xb.T, dha, name="mm_dwa_in", out_dtype=BF16, out_split=wa_in.shape[0] if wa_in.ndim == 3 else 1)
    sent_a = on_a_grads(d_wa_in, d_wa_out) if on_a_grads is not None else None
    grad_x = matmul(dha, wa_in, name="mm_dx", nt=True, addend=du1, alpha=DEEPNORM_ALPHA, after=sent_a, tm=512)

    small = dict(
        a_lower_bounds=d_alb, a_norm_w=d_anw[0:1], b_conv_w=d_conv,
        b_a_log=dprm[0:1, :heads], b_dt_bias=dprm[1:2, :heads], b_norm_w=dprm[2:3],
        ln_g=jnp.concatenate([dg1[0:1], dg2[0:1]], axis=0), ln_b=jnp.concatenate([db1[0:1], db2[0:1]], axis=0))
    big = dict(a_w_in=d_wa_in, a_w_out=d_wa_out, b_w_main=d_wb_main, b_w_tail=d_wb_tail, b_w_out=d_wb_out)
    return loss, grad_x, big, small


MESH_ID = pl.DeviceIdType.MESH


def _position():
    return lax.axis_index("x"), lax.axis_index("y"), lax.axis_index("c")


def _index_of(p):
    return 4 * p[0] + 2 * p[1] + p[2]


def all_gather(shards, *, name, space):
    n = len(shards)

    def body(*refs):
        ins, outs = refs[:n], refs[n:2 * n]
        send_sems, recv_sems, local_sems = refs[2 * n:]
        x, y, c = _position()
        me, sibling = (x, y, c), (x, y, 1 - c)
        chips = [(1 - x, y), (x, 1 - y), (1 - x, 1 - y)]

        def copy(a, k, block, to, own=False):
            dst = outs[a].at[_index_of(block)]
            return pltpu.make_async_remote_copy(
                src_ref=ins[a] if own else dst, dst_ref=dst,
                send_sem=send_sems.at[7 * a + k], recv_sem=recv_sems.at[7 * a + k],
                device_id=to, device_id_type=MESH_ID)

        mine = [pltpu.make_async_copy(ins[a], outs[a].at[_index_of(me)], local_sems.at[a]) for a in range(n)]
        for cp in mine:
            cp.start()
        first = []
        for a in range(n):
            first.append(copy(a, 0, me, sibling, own=True))
            first += [copy(a, 1 + j, me, (*chip, c), own=True) for j, chip in enumerate(chips)]
        for cp in first:
            cp.start()
        passed = []
        for j, chip in enumerate(chips):
            for a in range(n):
                copy(a, 1 + j, (*chip, c), me).wait_recv()
                fwd = copy(a, 4 + j, (*chip, c), sibling)
                fwd.start()
                passed.append(fwd)
        for a in range(n):
            copy(a, 0, sibling, me).wait_recv()
            for j, chip in enumerate(chips):
                copy(a, 4 + j, (*chip, 1 - c), me).wait_recv()
        for cp in first + passed:
            cp.wait_send()
        for cp in mine:
            cp.wait()

    spec = pl.BlockSpec(memory_space=space)
    return pl.pallas_call(
        body, name=name,
        in_specs=[spec] * n, out_specs=[spec] * n,
        out_shape=[jax.ShapeDtypeStruct((N_DEV, *s.shape), s.dtype) for s in shards],
        scratch_shapes=[pltpu.SemaphoreType.DMA((7 * n,)), pltpu.SemaphoreType.DMA((7 * n,)),
                        pltpu.SemaphoreType.DMA((n,))],
        compiler_params=pltpu.CompilerParams(vmem_limit_bytes=VMEM_LIMIT),
    )(*shards)


HBM_SPEC = pl.BlockSpec(memory_space=pltpu.HBM)
SEM_SPEC = pl.BlockSpec(memory_space=pltpu.SEMAPHORE)


def _peer_copies(srcs, lands, send_sems, recv_sems, by_dest):
    x, y, c = _position()
    me = _index_of((x, y, c))
    copies = []
    for a, (src, land) in enumerate(zip(srcs, lands)):
        for r in range(1, N_DEV):
            peer = (x ^ ((r >> 2) & 1), y ^ ((r >> 1) & 1), c ^ (r & 1))
            copies.append(pltpu.make_async_remote_copy(
                src_ref=src.at[_index_of(peer)] if by_dest else src, dst_ref=land.at[me],
                send_sem=send_sems.at[7 * a + r - 1], recv_sem=recv_sems.at[7 * a + r - 1],
                device_id=peer, device_id_type=MESH_ID))
    return copies


def copies_start(srcs, lands, *, name, by_dest):
    n = len(srcs)

    def body(*refs):
        src_refs, land_refs = refs[:n], refs[n:2 * n]
        send_sems, recv_sems = refs[2 * n], refs[2 * n + 1]
        token = refs[-1]
        for cp in _peer_copies(src_refs, land_refs, send_sems, recv_sems, by_dest):
            cp.start()
        token[...] = jnp.zeros_like(token)

    arrays = [pltpu.with_memory_space_constraint(t, pltpu.HBM) for t in (*srcs, *lands)]
    outs = pl.pallas_call(
        body, name=name,
        out_shape=(pltpu.SemaphoreType.DMA((7 * n,)), pltpu.SemaphoreType.DMA((7 * n,)),
                   *[pltpu.HBM(t.shape, t.dtype) for t in arrays], jax.ShapeDtypeStruct((8, LANES), F32)),
        in_specs=[HBM_SPEC] * (2 * n),
        out_specs=(SEM_SPEC, SEM_SPEC, *[HBM_SPEC] * (2 * n), pl.BlockSpec(memory_space=pltpu.VMEM)),
        input_output_aliases={i: 2 + i for i in range(2 * n)},
        compiler_params=pltpu.CompilerParams(has_side_effects=pltpu.SideEffectType.DATAFLOW_SIDE_EFFECTING),
    )(*arrays)
    return (outs[0], outs[1]), list(outs[2:2 + n]), list(outs[2 + n:2 + 2 * n]), outs[-1]


def copies_wait(sems, srcs, lands, after, *, name, by_dest):
    n = len(srcs)

    def body(*refs):
        src_refs, land_refs = refs[:n], refs[n:2 * n]
        send_sems, recv_sems = refs[2 * n], refs[2 * n + 1]
        for cp in _peer_copies(src_refs, land_refs, send_sems, recv_sems, by_dest):
            cp.wait_send()
            cp.wait_recv()

    outs = pl.pallas_call(
        body, name=name,
        out_shape=tuple(pltpu.HBM(t.shape, t.dtype) for t in (*srcs, *lands)),
        in_specs=[HBM_SPEC] * (2 * n) + [SEM_SPEC, SEM_SPEC, pl.BlockSpec(memory_space=pl.ANY)],
        out_specs=tuple([HBM_SPEC] * (2 * n)),
        input_output_aliases={i: i for i in range(2 * n)},
        compiler_params=pltpu.CompilerParams(has_side_effects=pltpu.SideEffectType.DATAFLOW_SIDE_EFFECTING),
    )(*srcs, *lands, sems[0], sems[1], after)
    return list(outs[n:])


def _landing(own, me):
    return lax.dynamic_update_slice_in_dim(lax.empty((N_DEV, *own.shape), own.dtype), own[None], me, 0)


def adamw(parts, w, m, v, *, name, tr=64):
    p, r, c = parts.shape
    tr = min(tr, r)
    assert r % tr == 0
    c1 = 1.0 / (1.0 - ADAM_B1 ** ADAM_STEP)
    c2 = 1.0 / (1.0 - ADAM_B2 ** ADAM_STEP)

    def body(p_ref, w_ref, m_ref, v_ref, g_ref, d_ref, nm_ref, nv_ref):
        g = p_ref[0].astype(F32)
        for i in range(1, p):
            g = g + p_ref[i].astype(F32)
        nm = ADAM_B1 * m_ref[...] + (1.0 - ADAM_B1) * g
        nv = ADAM_B2 * v_ref[...] + (1.0 - ADAM_B2) * jnp.square(g)
        g_ref[...] = g
        nm_ref[...] = nm
        nv_ref[...] = nv
        d_ref[...] = -ADAM_LR * ((nm * c1) / (jnp.sqrt(nv * c2) + ADAM_EPS) + ADAM_WD * w_ref[...])

    blk = pl.BlockSpec((tr, c), lambda i: (i, 0))
    out = jax.ShapeDtypeStruct((r, c), F32)
    return pl.pallas_call(
        body, name=name, grid=(r // tr,),
        in_specs=[pl.BlockSpec((p, tr, c), lambda i: (0, i, 0)), blk, blk, blk],
        out_specs=[blk] * 4, out_shape=[out] * 4,
        compiler_params=_cparams(("parallel",)),
    )(parts, w, m, v)


def _pack(d, vals):
    heads = d // LANES
    vecs = jnp.zeros((8, LANES), F32)
    vecs = vecs.at[0:1].set(vals["a_norm_w"]).at[1:2, :heads].set(vals["b_a_log"])
    vecs = vecs.at[2:3, :heads].set(vals["b_dt_bias"]).at[3:4].set(vals["b_norm_w"])
    rows = [vals["a_lower_bounds"].reshape(-1, LANES), vals["ln_g"].reshape(-1, LANES),
            vals["ln_b"].reshape(-1, LANES), vecs]
    return jnp.concatenate(rows, axis=0)


def _unpack(d, packed):
    heads = d // LANES
    n3, n2 = 3 * heads, 2 * heads
    o = 0
    out = {}
    out["a_lower_bounds"] = packed[o:o + n3].reshape(3, d); o += n3
    out["ln_g"] = packed[o:o + n2].reshape(2, d); o += n2
    out["ln_b"] = packed[o:o + n2].reshape(2, d); o += n2
    out["a_norm_w"] = packed[o:o + 1]
    out["b_a_log"] = packed[o + 1:o + 2, :heads]
    out["b_dt_bias"] = packed[o + 2:o + 3, :heads]
    out["b_norm_w"] = packed[o + 3:o + 4]
    return out


ORDER = ("a_w_in", "a_lower_bounds", "a_norm_w", "a_w_out", "b_w_in", "b_conv_w", "b_a_log", "b_dt_bias", "b_norm_w",
         "b_w_out", "ln_g", "ln_b")


def kernel(x, a_w_in, a_lower_bounds, a_norm_w, a_w_out, b_w_in, b_conv_w, b_a_log, b_dt_bias, b_norm_w, b_w_out, ln_g, ln_b, loss_target, m_a_w_in, m_a_lower_bounds, m_a_norm_w, m_a_w_out, m_b_w_in, m_b_conv_w, m_b_a_log, m_b_dt_bias, m_b_norm_w, m_b_w_out, m_ln_g, m_ln_b, v_a_w_in, v_a_lower_bounds, v_a_norm_w, v_a_w_out, v_b_w_in, v_b_conv_w, v_b_a_log, v_b_dt_bias, v_b_norm_w, v_b_w_out, v_ln_g, v_ln_b):
    w = dict(a_w_in=a_w_in, a_lower_bounds=a_lower_bounds, a_norm_w=a_norm_w, a_w_out=a_w_out, b_w_in=b_w_in,
             b_conv_w=b_conv_w, b_a_log=b_a_log, b_dt_bias=b_dt_bias, b_norm_w=b_norm_w, b_w_out=b_w_out, ln_g=ln_g, ln_b=ln_b)
    m = dict(a_w_in=m_a_w_in, a_lower_bounds=m_a_lower_bounds, a_norm_w=m_a_norm_w, a_w_out=m_a_w_out, b_w_in=m_b_w_in,
             b_conv_w=m_b_conv_w, b_a_log=m_b_a_log, b_dt_bias=m_b_dt_bias, b_norm_w=m_b_norm_w, b_w_out=m_b_w_out,
             ln_g=m_ln_g, ln_b=m_ln_b)
    v = dict(a_w_in=v_a_w_in, a_lower_bounds=v_a_lower_bounds, a_norm_w=v_a_norm_w, a_w_out=v_a_w_out, b_w_in=v_b_w_in,
             b_conv_w=v_b_conv_w, b_a_log=v_b_a_log, b_dt_bias=v_b_dt_bias, b_norm_w=v_b_norm_w, b_w_out=v_b_w_out,
             ln_g=v_ln_g, ln_b=v_ln_b)
    t, d = x.shape[1], x.shape[2]
    heads = d // LANES
    n_tail = 2 * heads
    me = _index_of(_position())

    ga_in, ga_out, g_conv = all_gather([a_w_in[0].astype(BF16), a_w_out[0].astype(BF16), b_conv_w[0]],
                                       name="gather_weights_a", space=pltpu.HBM)
    b_shards = [b_w_in[0].astype(BF16), b_w_out[0].astype(BF16)]
    sems_w, b_shards, lands_w, token_w = copies_start(b_shards, [_landing(s, me) for s in b_shards],
                                                      name="gather_weights_b_start", by_dest=False)
    wa_out = ga_out.reshape(d, d)
    conv_w = jnp.transpose(g_conv, (1, 0, 2)).reshape(CONV_K, 3 * d)

    def b_weights(after):
        gb_in, gb_out = copies_wait(sems_w, b_shards, lands_w, after, name="gather_weights_b_wait", by_dest=False)
        wb_full = jnp.transpose(gb_in, (1, 0, 2)).reshape(d, 4 * d + n_tail)
        wb_tail = jnp.concatenate([wb_full[:, 4 * d:], jnp.zeros((d, LANES - n_tail), BF16)], axis=1)
        return wb_full[:, :4 * d], wb_tail, gb_out.reshape(d, d), conv_w

    sent = {}

    def send_grads(key, parts):
        own = [lax.dynamic_index_in_dim(p, me, 0, keepdims=False) for p in parts]
        sems, parts, lands, token = copies_start(parts, [_landing(o, me) for o in own],
                                                 name="exchange_" + key + "_start", by_dest=True)
        sent[key] = (sems, parts, lands)
        return token

    def on_b_grads(d_main, d_tail, d_out):
        d_b_in = jnp.concatenate([d_main, d_tail[:, :n_tail]], axis=1)
        d_b_in = jnp.transpose(d_b_in.reshape(d, N_DEV, -1), (1, 0, 2))
        return send_grads("b", [d_b_in, d_out.reshape(N_DEV, d // N_DEV, d)])

    def on_a_grads(d_in, d_out):
        return send_grads("a", [d_in, d_out.reshape(N_DEV, d // N_DEV, d)])

    loss, grad_x, big, small = local_step(
        x[0], loss_target[0], ga_in, a_lower_bounds, a_norm_w, wa_out, None, None, None, b_a_log, b_dt_bias,
        b_norm_w, None, ln_g, ln_b, first_after=token_w, b_weights=b_weights, on_b_grads=on_b_grads, on_a_grads=on_a_grads)
    loss = lax.psum(loss[0, 0], ("x", "y", "c"))

    conv_rows = small["b_conv_w"].reshape(-1, LANES)
    n_conv = conv_rows.shape[0]
    packed_grads = jnp.concatenate([conv_rows, _pack(d, small)], axis=0)
    (got,) = all_gather([packed_grads], name="gather_small", space=pltpu.VMEM)
    res = {}
    packed = adamw(got[:, n_conv:], _pack(d, w), _pack(d, m), _pack(d, v), name="adamw_small", tr=4096)
    for k, vals in zip(("grad", "delta", "new_m", "new_v"), packed):
        res[k] = _unpack(d, vals)
    shard_ch = 3 * d // N_DEV
    conv_parts = lax.dynamic_slice_in_dim(got[:, :n_conv].reshape(N_DEV, CONV_K, 3 * d), me * shard_ch, shard_ch, axis=2)
    conv_out = adamw(conv_parts, b_conv_w[0], m_b_conv_w[0], v_b_conv_w[0], name="adamw_conv")
    for k, vals in zip(("grad", "delta", "new_m", "new_v"), conv_out):
        res[k]["b_conv_w"] = vals[None]

    after = conv_out[0]
    for key, names in (("b", ("b_w_in", "b_w_out")), ("a", ("a_w_in", "a_w_out"))):
        sems, parts, lands = sent[key]
        recv = copies_wait(sems, parts, lands, after, name="exchange_" + key + "_wait", by_dest=True)
        for name, got_parts in zip(names, recv):
            outs = adamw(got_parts, w[name][0], m[name][0], v[name][0], name="adamw_" + name)
            for k, vals in zip(("grad", "delta", "new_m", "new_v"), outs):
                res[k][name] = vals[None]
            after = outs[0]

    return (loss, grad_x[None], *[res["grad"][k] for k in ORDER], *[res["delta"][k] for k in ORDER],
            *[res["new_m"][k] for k in ORDER], *[res["new_v"][k] for k in ORDER])
```

```python
import functools
import math

import jax
import jax.numpy as jnp
from jax import lax
from jax.experimental import pallas as pl
from jax.experimental.pallas import tpu as pltpu

F32 = jnp.float32
BF16 = jnp.bfloat16

N_DEV = 8
LANES = 128
CHUNK = 64
SUB = 16
HALO = 8
CONV_K = 4
DEPTH = 2
DEEPNORM_ALPHA = (2.0 * DEPTH) ** 0.25
LN_EPS = 1e-5
RMS_EPS = 1e-6
L2_EPS = 1e-6
EXP_CLAMP = 60.0
ADAM_LR = 0.001
ADAM_B1 = 0.9
ADAM_B2 = 0.999
ADAM_EPS = 1e-08
ADAM_WD = 0.01
ADAM_STEP = 10
VMEM_LIMIT = 56 * 1024 * 1024
HEADS_PER_STEP = 4

NN = ((1,), (0,))
NT = ((1,), (1,))
TN = ((0,), (0,))


def _dg(a, b, dims, precision=None):
    return lax.dot_general(a, b, (dims, ((), ())), precision=precision, preferred_element_type=F32)


def _bdot(a, b, dims):
    return _dg(a.astype(BF16), b.astype(BF16), dims)


def _split3(t):
    hi = t.astype(BF16)
    r = t - hi.astype(F32)
    mid = r.astype(BF16)
    return hi, mid, (r - mid.astype(F32)).astype(BF16)


@jax.custom_vjp
def _sdot(sel, t):
    sel = sel.astype(BF16)
    hi, mid, lo = _split3(t)
    return (_dg(sel, lo, NN) + _dg(sel, mid, NN)) + _dg(sel, hi, NN)


def _sdot_fwd(sel, t):
    return _sdot(sel, t), sel


def _sdot_bwd(sel, ct):
    selb = sel.astype(BF16)
    hi, mid, lo = _split3(ct)
    return jnp.zeros_like(sel), (_dg(selb, lo, TN) + _dg(selb, mid, TN)) + _dg(selb, hi, TN)


_sdot.defvjp(_sdot_fwd, _sdot_bwd)


@jax.custom_vjp
def _tri_inv(m):
    n = m.shape[0]
    row = lax.broadcasted_iota(jnp.int32, (n, n), 0)
    col = lax.broadcasted_iota(jnp.int32, (n, n), 1)
    inv = (col == row).astype(F32) + m
    mp = m
    for _ in range(int(math.log2(CHUNK)) - 1):
        mp = _bdot(mp, mp, NN)
        inv = inv + _bdot(inv, mp, NN)
    return inv


def _tri_inv_fwd(m):
    inv = _tri_inv(m)
    return inv, inv


def _tri_inv_bwd(inv, ct):
    return (_bdot(_bdot(inv, ct, TN), inv, NT),)


_tri_inv.defvjp(_tri_inv_fwd, _tri_inv_bwd)


@functools.partial(jax.custom_vjp, nondiff_argnums=(2,))
def _shift_rows(xm, xh, back):
    r = pltpu.roll(xm, back, 0)
    row = lax.broadcasted_iota(jnp.int32, xh.shape, 0)
    top = jnp.where(row < back, pltpu.roll(xh, back, 0), r[0:HALO])
    return jnp.concatenate([top, r[HALO:]], axis=0)


def _shift_rows_fwd(xm, xh, back):
    return _shift_rows(xm, xh, back), None


def _shift_rows_bwd(back, _, ct):
    row = lax.broadcasted_iota(jnp.int32, ct.shape, 0)
    dxm = jnp.where(row < CHUNK - back, pltpu.roll(ct, CHUNK - back, 0), 0.0)
    rowh = lax.broadcasted_iota(jnp.int32, (HALO, ct.shape[1]), 0)
    dxh = jnp.where(rowh >= HALO - back, pltpu.roll(ct[0:HALO], HALO - back, 0), 0.0)
    return dxm, dxh


_shift_rows.defvjp(_shift_rows_fwd, _shift_rows_bwd)


def _silu(t):
    return t * jax.nn.sigmoid(t)


def _softplus(t):
    return jnp.where(t > 20.0, t, jnp.log1p(jnp.exp(jnp.minimum(t, 20.0))))


def _cparams(sem=None):
    kw = dict(vmem_limit_bytes=VMEM_LIMIT)
    if sem is not None:
        kw["dimension_semantics"] = sem
    return pltpu.CompilerParams(**kw)


def matmul(a, b, *, name, nt=False, lhs_t=False, n=None, addend=None, alpha=1.0, out_dtype=F32, out_split=1,
           out_rows=None, into=None, after=None, tm=1024, tn=1024, tk=4096):
    k, m = a.shape[::-1] if not lhs_t else a.shape
    b_split = b.shape[0] if b.ndim == 3 else 1
    b_rows, b_cols = b.shape[-2], b.shape[-1] * b_split
    n = (b_rows if nt else b_cols) if n is None else n
    tm, tn, tk = min(tm, m), min(tn, n), min(tk, k)
    if b_split > 1:
        part = b_cols // b_split
        tn, tk = (tn, min(tk, part)) if nt else (min(tn, part), tk)
    if out_split > 1:
        tn = min(tn, n // out_split)
    assert m % tm == 0 and n % tn == 0 and k % tk == 0, (a.shape, b.shape, nt)
    assert not (lhs_t and nt)
    nk = k // tk
    dims = TN if lhs_t else (NT if nt else NN)

    def body(*refs):
        a_ref, b_ref = refs[:2]
        add_ref = None if addend is None else refs[2]

        def finish(r, o_ref):
            if add_ref is not None:
                r = r + alpha * add_ref[...].astype(F32)
            o_ref[...] = r.astype(o_ref.dtype)

        if nk == 1:
            finish(_dg(a_ref[...], b_ref[...], dims), refs[-1])
            return
        o_ref, acc_ref = refs[-2:]
        kk = pl.program_id(2)

        @pl.when(kk == 0)
        def _():
            acc_ref[...] = _dg(a_ref[...], b_ref[...], dims)

        @pl.when(kk > 0)
        def _():
            acc_ref[...] += _dg(a_ref[...], b_ref[...], dims)

        @pl.when(kk == nk - 1)
        def _():
            finish(acc_ref[...], o_ref)

    if b_split == 1:
        b_spec = (pl.BlockSpec((tn, tk), lambda i, j, kk: (j, kk)) if nt
                  else pl.BlockSpec((tk, tn), lambda i, j, kk: (kk, j)))
    elif nt:
        per = (b_cols // b_split) // tk
        b_spec = pl.BlockSpec((None, tn, tk), lambda i, j, kk: (kk // per, j, kk % per))
    else:
        per = (b_cols // b_split) // tn
        b_spec = pl.BlockSpec((None, tk, tn), lambda i, j, kk: (j // per, kk, j % per))
    a_spec = (pl.BlockSpec((tk, tm), lambda i, j, kk: (kk, i)) if lhs_t
              else pl.BlockSpec((tm, tk), lambda i, j, kk: (i, kk)))
    in_specs = [a_spec, b_spec]
    args = [a, b]
    if addend is not None:
        in_specs.append(pl.BlockSpec((tm, tn), lambda i, j, kk: (i, j)))
        args.append(addend)
    for extra in (after, None if into is None else into[0]):
        if extra is not None:
            in_specs.append(pl.BlockSpec(memory_space=pl.ANY))
            args.append(extra)
    aliases = {}
    if into is not None:
        assert out_split == 1 and into[1] % tm == 0 and into[0].dtype == out_dtype
        row0 = into[1] // tm
        out_spec = pl.BlockSpec((tm, tn), lambda i, j, kk: (row0 + i, j))
        out_shape = jax.ShapeDtypeStruct(into[0].shape, out_dtype)
        aliases = {len(args) - 1: 0}
    elif out_split == 1:
        out_spec = pl.BlockSpec((tm, tn), lambda i, j, kk: (i, j))
        out_shape = jax.ShapeDtypeStruct((m if out_rows is None else out_rows, n), out_dtype)
    else:
        per_o = (n // out_split) // tn
        out_spec = pl.BlockSpec((None, tm, tn), lambda i, j, kk: (j // per_o, i, j % per_o))
        out_shape = jax.ShapeDtypeStruct((out_split, m, n // out_split), out_dtype)
    return pl.pallas_call(
        body, name=name, grid=(m // tm, n // tn, nk),
        in_specs=in_specs,
        out_specs=out_spec,
        out_shape=out_shape,
        input_output_aliases=aliases,
        scratch_shapes=[] if nk == 1 else [pltpu.VMEM((tm, tn), F32)],
        compiler_params=_cparams(("parallel", "parallel", "arbitrary")),
    )(*args)


def _gated_rmsnorm(o, z, nw):
    r = lax.rsqrt(jnp.mean(o * o, axis=1, keepdims=True) + RMS_EPS)
    return o * r * nw * _silu(z)


def _chunk_consts():
    row = lax.broadcasted_iota(jnp.int32, (CHUNK, CHUNK), 0)
    col = lax.broadcasted_iota(jnp.int32, (CHUNK, CHUNK), 1)
    return row, col


def _head_cols(tb, groups, hp, j, rev_nb=None):
    def bb(b):
        return b if rev_nb is None else rev_nb - 1 - b
    return pl.BlockSpec((tb, hp * LANES), lambda b, h: (bb(b), j * groups + h))


def _stack(pieces):
    return jnp.concatenate(pieces, axis=0) if len(pieces) > 1 else pieces[0]


def _place(t, g):
    if g == 1:
        return t
    head = lax.broadcasted_iota(jnp.int32, t.shape, 0) // CHUNK
    return jnp.concatenate([jnp.where(head == h, t, 0.0) for h in range(g)], axis=1)


def _hgrn2_chunk(qr, fr, iv, z, a0, a1, a2, nw, st):
    g = len(qr)
    n = g * CHUNK
    row = lax.broadcasted_iota(jnp.int32, (n, n), 0)
    col = lax.broadcasted_iota(jnp.int32, (n, n), 1)
    same = (row // CHUNK) == (col // CHUNK)
    qs, ks, gl = [], [], []
    for h in range(g):
        amax = jnp.maximum(jnp.maximum(a0[h], a1[h]), a2[h])
        e0, e1, e2 = jnp.exp(a0[h] - amax), jnp.exp(a1[h] - amax), jnp.exp(a2[h] - amax)
        lb = e0 / (e0 + e1 + e2)
        forget = lb + (1.0 - lb) * jax.nn.sigmoid(fr[h])
        qs.append(_silu(qr[h]))
        ks.append(1.0 - forget)
        gl.append(jnp.log(forget))
    q, k, glog, v, zs = _stack(qs), _stack(ks), _stack(gl), _stack(list(iv)), _stack(list(z))
    cum = _sdot((same & (col <= row)).astype(F32), glog)
    sub = (row % CHUNK) // SUB
    rowl = lax.broadcasted_iota(jnp.int32, (n, LANES), 0)
    headl, subl = rowl // CHUNK, (rowl % CHUNK) // SUB

    def row_of(r):
        picks = [jnp.sum(jnp.where(rowl == h * CHUNK + r, cum, 0.0), axis=0, keepdims=True) for h in range(g)]
        out = jnp.zeros((n, LANES), F32)
        for h in range(g):
            out = out + jnp.where(headl == h, picks[h], 0.0)
        return out, picks

    refs = [jnp.zeros((n, LANES), F32)] + [row_of(i * SUB - 1)[0] for i in range(1, CHUNK // SUB)]
    own = jnp.zeros((n, LANES), F32)
    for i, ref in enumerate(refs):
        own = own + jnp.where(subl == i, ref, 0.0)
    qt = q * jnp.exp(cum - own)
    att = jnp.zeros((n, n), F32)
    for i, ref in enumerate(refs):
        kt = k * jnp.exp(jnp.minimum(ref - cum, EXP_CLAMP))
        att = att + jnp.where(same & (sub == i) & (col <= row), _bdot(qt, kt, NT), 0.0)
    cl_rows, lasts = row_of(CHUNK - 1)
    o = _bdot(att, v, NN) + _bdot(_place(q * jnp.exp(cum), g), st, NT)
    kd = k * jnp.exp(cl_rows - cum)
    cl_wide = jnp.concatenate(lasts, axis=1) if g > 1 else lasts[0]
    st_new = st * jnp.exp(cl_wide) + _bdot(v, _place(kd, g), TN)
    return _gated_rmsnorm(o, zs, nw), st_new


def hgrn2_fwd(hproj, alb, nw, *, tb, hp):
    t, d4 = hproj.shape
    d = d4 // 4
    heads = d // LANES
    groups = heads // hp
    nb, nc = t // tb, tb // CHUNK

    def body(q_ref, f_ref, i_ref, z_ref, alb_ref, nw_ref, y_ref, states_ref, st_ref):
        b, hg = pl.program_id(0), pl.program_id(1)

        @pl.when(b == 0)
        def _():
            st_ref[hg] = jnp.zeros((LANES, hp * LANES), F32)

        nwv = nw_ref[...]
        lns = [slice(p * LANES, (p + 1) * LANES) for p in range(hp)]
        albs = []
        for r in range(3):
            albs.append(tuple(alb_ref[r:r + 1, pl.ds(pl.multiple_of((hg * hp + p) * LANES, LANES), LANES)]
                              for p in range(hp)))

        def step(c, carry):
            rows = pl.ds(pl.multiple_of(c * CHUNK, CHUNK), CHUNK)
            st = st_ref[hg]
            states_ref[c, 0] = st
            ins = [tuple(ref[rows, ln] for ln in lns) for ref in (q_ref, f_ref, i_ref, z_ref)]
            y, st_new = _hgrn2_chunk(*ins, *albs, nwv, st)
            for p in range(hp):
                y_ref[rows, lns[p]] = y[p * CHUNK:(p + 1) * CHUNK].astype(y_ref.dtype)
            st_ref[hg] = st_new
            return carry

        lax.fori_loop(0, nc, step, 0)

    return pl.pallas_call(
        body, name="hgrn2_fwd", grid=(nb, groups),
        in_specs=[_head_cols(tb, groups, hp, j) for j in range(4)] + [
            pl.BlockSpec((3, d), lambda b, h: (0, 0)),
            pl.BlockSpec((1, LANES), lambda b, h: (0, 0))],
        out_specs=[pl.BlockSpec((tb, hp * LANES), lambda b, h: (b, h)),
                   pl.BlockSpec((nc, 1, LANES, hp * LANES), lambda b, h: (b, h, 0, 0))],
        out_shape=[jax.ShapeDtypeStruct((t, d), BF16),
                   jax.ShapeDtypeStruct((t // CHUNK, groups, LANES, hp * LANES), F32)],
        scratch_shapes=[pltpu.VMEM((groups, LANES, hp * LANES), F32)],
        compiler_params=_cparams(("arbitrary", "arbitrary")),
    )(hproj, hproj, hproj, hproj, alb, nw)


def hgrn2_bwd(hproj, alb, nw, states, dy, *, tb, hp):
    t, d4 = hproj.shape
    d = d4 // 4
    heads = d // LANES
    groups = heads // hp
    nb, nc = t // tb, tb // CHUNK

    def body(q_ref, f_ref, i_ref, z_ref, alb_ref, nw_ref, states_ref, dy_ref,
             dh_ref, dalb_ref, dnw_ref, dst_ref):
        b, hg = pl.program_id(0), pl.program_id(1)

        @pl.when(b == 0)
        def _():
            dst_ref[hg] = jnp.zeros((LANES, hp * LANES), F32)

        @pl.when((b == 0) & (hg == 0))
        def _():
            dalb_ref[...] = jnp.zeros_like(dalb_ref)
            dnw_ref[...] = jnp.zeros_like(dnw_ref)

        nwv = nw_ref[...]
        lns = [slice(p * LANES, (p + 1) * LANES) for p in range(hp)]
        lanes_of = [pl.ds(pl.multiple_of((hg * hp + p) * LANES, LANES), LANES) for p in range(hp)]
        albs = [tuple(alb_ref[r:r + 1, lanes_of[p]] for p in range(hp)) for r in range(3)]

        def step(i, carry):
            c = nc - 1 - i
            rows = pl.ds(pl.multiple_of(c * CHUNK, CHUNK), CHUNK)
            ins = [tuple(ref[rows, ln] for ln in lns) for ref in (q_ref, f_ref, i_ref, z_ref)]
            dy = _stack([dy_ref[rows, ln].astype(F32) for ln in lns])
            _, vjp = jax.vjp(_hgrn2_chunk, *ins, *albs, nwv, states_ref[c, 0])
            dq, df, di, dz, da0, da1, da2, dnw, dst = vjp((dy, dst_ref[hg]))
            for p in range(hp):
                h = hg * hp + p
                for j, val in enumerate((dq, df, di, dz)):
                    dh_ref[rows, pl.ds(pl.multiple_of(j * d + h * LANES, LANES), LANES)] = val[p].astype(dh_ref.dtype)
                dalb_ref[0:1, lanes_of[p]] += da0[p]
                dalb_ref[1:2, lanes_of[p]] += da1[p]
                dalb_ref[2:3, lanes_of[p]] += da2[p]
            dnw_ref[0:1, :] += dnw
            dst_ref[hg] = dst
            return carry

        lax.fori_loop(0, nc, step, 0)

    return pl.pallas_call(
        body, name="hgrn2_bwd", grid=(nb, groups),
        in_specs=[_head_cols(tb, groups, hp, j, rev_nb=nb) for j in range(4)] + [
            pl.BlockSpec((3, d), lambda b, h: (0, 0)),
            pl.BlockSpec((1, LANES), lambda b, h: (0, 0)),
            pl.BlockSpec((nc, 1, LANES, hp * LANES), lambda b, h: (nb - 1 - b, h, 0, 0)),
            pl.BlockSpec((tb, hp * LANES), lambda b, h: (nb - 1 - b, h))],
        out_specs=[pl.BlockSpec((tb, d4), lambda b, h: (nb - 1 - b, 0)),
                   pl.BlockSpec((3, d), lambda b, h: (0, 0)),
                   pl.BlockSpec((8, LANES), lambda b, h: (0, 0))],
        out_shape=[jax.ShapeDtypeStruct((t, d4), BF16),
                   jax.ShapeDtypeStruct((3, d), F32),
                   jax.ShapeDtypeStruct((8, LANES), F32)],
        scratch_shapes=[pltpu.VMEM((groups, LANES, hp * LANES), F32)],
        compiler_params=_cparams(("arbitrary", "arbitrary")),
    )(hproj, hproj, hproj, hproj, alb, nw, states, dy)


def _conv_silu(xm, xh, w):
    acc = w[CONV_K - 1] * xm
    for j in range(CONV_K - 1):
        acc = acc + w[j] * _shift_rows(xm, xh, CONV_K - 1 - j)
    return _silu(acc)


def _l2norm(t):
    return t * lax.rsqrt(jnp.sum(t * t, axis=1, keepdims=True) + L2_EPS)


def _gdn_chunk(head0, hh, qm, qh, km, kh, vm, vh, z, tail, wq, wk, wv, alog_row, dtb_row, nw, s):
    g = len(qm)
    n = g * CHUNK
    row = lax.broadcasted_iota(jnp.int32, (n, n), 0)
    col = lax.broadcasted_iota(jnp.int32, (n, n), 1)
    same = (row // CHUNK) == (col // CHUNK)
    lane = lax.broadcasted_iota(jnp.int32, (CHUNK, LANES), 1)
    heads_lane = lax.broadcasted_iota(jnp.int32, (1, LANES), 1)
    q = _l2norm(_stack([_conv_silu(qm[h], qh[h], wq[h]) for h in range(g)])) * (LANES ** -0.5)
    k = _l2norm(_stack([_conv_silu(km[h], kh[h], wk[h]) for h in range(g)]))
    v = _stack([_conv_silu(vm[h], vh[h], wv[h]) for h in range(g)])
    zs = _stack(list(z))
    betas, gs = [], []
    for h in range(g):
        head = head0 + h
        betas.append(jax.nn.sigmoid(jnp.sum(jnp.where(lane == head, tail, 0.0), axis=1, keepdims=True)))
        a_t = jnp.sum(jnp.where(lane == hh + head, tail, 0.0), axis=1, keepdims=True)
        alog = jnp.sum(jnp.where(heads_lane == head, alog_row, 0.0), axis=1, keepdims=True)
        dtb = jnp.sum(jnp.where(heads_lane == head, dtb_row, 0.0), axis=1, keepdims=True)
        gs.append(-jnp.exp(alog) * _softplus(a_t + dtb))
    beta, gcol = _stack(betas), _stack(gs)
    tril = (same & (col <= row)).astype(F32)
    eye = (col == row).astype(F32)
    cum = _sdot(tril, gcol + jnp.zeros((n, n), F32))
    cum_r = cum.T
    diff = cum - cum_r
    strict = same & (col < row)
    incl = same & (col <= row)
    dec_strict = jnp.where(strict, jnp.exp(jnp.where(strict, diff, 0.0)), 0.0)
    dec_incl = jnp.where(incl, jnp.exp(jnp.where(incl, diff, 0.0)), 0.0)
    cum_c = jnp.sum(cum * eye, axis=1, keepdims=True)
    last = same & (col % CHUNK == CHUNK - 1)
    cl_rows = jnp.sum(jnp.where(last, cum_r, 0.0), axis=1, keepdims=True)
    ecum = jnp.exp(cum_c)
    inv = _tri_inv(-(beta * _bdot(k, k, NT) * dec_strict))
    u0 = _bdot(inv, beta * v, NN)
    w = _bdot(inv, (beta * ecum) * k, NN)
    qk = _bdot(q, k, NT) * dec_incl
    u = u0 - _bdot(_place(w, g), s, NN)
    o = _bdot(_place(q * ecum, g), s, NN) + _bdot(qk, u, NN)
    kd = k * jnp.exp(cl_rows - cum_c)
    row1 = lax.broadcasted_iota(jnp.int32, (n, 1), 0)
    decay = []
    for h in range(g):
        cl_h = jnp.sum(jnp.where(row1 == h * CHUNK + CHUNK - 1, cum_c, 0.0), axis=0, keepdims=True)
        decay.append(jnp.exp(cl_h) + jnp.zeros((LANES, 1), F32))
    s_new = _stack(decay) * s + _bdot(_place(kd, g), u, TN)
    return _gated_rmsnorm(o, zs, nw), s_new


def _gdn_in_specs(tb, heads, hp, rev_nb=None):
    groups = heads // hp

    def bb(b):
        return b if rev_nb is None else rev_nb - 1 - b

    def halo(j):
        return pl.BlockSpec((HALO, hp * LANES),
                            lambda b, h, j=j: (jnp.maximum(bb(b) * (tb // HALO) - 1, 0), j * groups + h))

    def main(j):
        return _head_cols(tb, groups, hp, j, rev_nb=rev_nb)

    return [main(0), halo(0), main(1), halo(1), main(2), halo(2), main(3),
            pl.BlockSpec((tb, LANES), lambda b, h: (bb(b), 0)),
            pl.BlockSpec((CONV_K, 3 * heads * LANES), lambda b, h: (0, 0)),
            pl.BlockSpec((8, LANES), lambda b, h: (0, 0)),
            pl.BlockSpec((1, LANES), lambda b, h: (0, 0))]


def _gdn_chunk_args(c, head0, hp, blk, heads, q_ref, qh_ref, k_ref, kh_ref, v_ref, vh_ref, z_ref, tail_ref, cw_ref, prm_ref, nw_ref):
    d = heads * LANES
    lns = [slice(p * LANES, (p + 1) * LANES) for p in range(hp)]
    rows = pl.ds(pl.multiple_of(c * CHUNK, CHUNK), CHUNK)
    prev = pl.ds(pl.multiple_of(jnp.maximum(c * CHUNK - HALO, 0), HALO), HALO)
    first = c == 0
    live = jnp.where(first & (blk == 0), 0.0, 1.0)

    def main_of(ref):
        return tuple(ref[rows, ln] for ln in lns)

    def halo_of(ref, href):
        return tuple(jnp.where(first, href[:, ln], ref[prev, ln]) * live for ln in lns)

    def cw(j):
        out = []
        for p in range(hp):
            lanes = pl.ds(pl.multiple_of(j * d + (head0 + p) * LANES, LANES), LANES)
            out.append(tuple(cw_ref[r:r + 1, lanes] for r in range(CONV_K)))
        return tuple(out)

    return (main_of(q_ref), halo_of(q_ref, qh_ref), main_of(k_ref), halo_of(k_ref, kh_ref),
            main_of(v_ref), halo_of(v_ref, vh_ref), main_of(z_ref), tail_ref[rows, :],
            cw(0), cw(1), cw(2), prm_ref[0:1, :], prm_ref[1:2, :], nw_ref[...])


def gdn_fwd(hmain, tail, conv_w, prm, nw, *, tb, hp):
    t, d4 = hmain.shape
    d = d4 // 4
    heads = d // LANES
    groups = heads // hp
    nb, nc = t // tb, tb // CHUNK

    def body(q_ref, qh_ref, k_ref, kh_ref, v_ref, vh_ref, z_ref, tail_ref, cw_ref, prm_ref, nw_ref,
             y_ref, states_ref, s_ref):
        b, hg = pl.program_id(0), pl.program_id(1)

        @pl.when(b == 0)
        def _():
            s_ref[hg] = jnp.zeros((hp * LANES, LANES), F32)

        def step(c, carry):
            rows = pl.ds(pl.multiple_of(c * CHUNK, CHUNK), CHUNK)
            s = s_ref[hg]
            states_ref[c, 0] = s
            args = _gdn_chunk_args(c, hg * hp, hp, b, heads, q_ref, qh_ref, k_ref, kh_ref, v_ref, vh_ref, z_ref,
                                   tail_ref, cw_ref, prm_ref, nw_ref)
            y, s_new = _gdn_chunk(hg * hp, heads, *args, s)
            for p in range(hp):
                y_ref[rows, p * LANES:(p + 1) * LANES] = y[p * CHUNK:(p + 1) * CHUNK].astype(y_ref.dtype)
            s_ref[hg] = s_new
            return carry

        lax.fori_loop(0, nc, step, 0)

    return pl.pallas_call(
        body, name="gdn_fwd", grid=(nb, groups),
        in_specs=_gdn_in_specs(tb, heads, hp),
        out_specs=[pl.BlockSpec((tb, hp * LANES), lambda b, h: (b, h)),
                   pl.BlockSpec((nc, 1, hp * LANES, LANES), lambda b, h: (b, h, 0, 0))],
        out_shape=[jax.ShapeDtypeStruct((t, d), BF16),
                   jax.ShapeDtypeStruct((t // CHUNK, groups, hp * LANES, LANES), F32)],
        scratch_shapes=[pltpu.VMEM((groups, hp * LANES, LANES), F32)],
        compiler_params=_cparams(("arbitrary", "arbitrary")),
    )(hmain, hmain, hmain, hmain, hmain, hmain, hmain, tail, conv_w, prm, nw)


def gdn_bwd(hmain, tail, conv_w, prm, nw, states, dy, *, tb, hp):
    t, d4 = hmain.shape
    d = d4 // 4
    heads = d // LANES
    groups = heads // hp
    nb, nc = t // tb, tb // CHUNK

    def body(q_ref, qh_ref, k_ref, kh_ref, v_ref, vh_ref, z_ref, tail_ref, cw_ref, prm_ref, nw_ref,
             states_ref, dy_ref, dh_ref, dtail_ref, dcw_ref, dprm_ref, ds_ref, pend_ref):
        b, hg = pl.program_id(0), pl.program_id(1)
        blk = nb - 1 - b

        @pl.when(b == 0)
        def _():
            ds_ref[hg] = jnp.zeros((hp * LANES, LANES), F32)
            for p in range(hp):
                pend_ref[hg * hp + p] = jnp.zeros((3, HALO, LANES), F32)

        @pl.when((b == 0) & (hg == 0))
        def _():
            dcw_ref[...] = jnp.zeros_like(dcw_ref)
            dprm_ref[...] = jnp.zeros_like(dprm_ref)

        @pl.when(hg == 0)
        def _():
            dtail_ref[...] = jnp.zeros_like(dtail_ref)

        def step(i, carry):
            c = nc - 1 - i
            rows = pl.ds(pl.multiple_of(c * CHUNK, CHUNK), CHUNK)
            zpad = jnp.zeros((CHUNK - HALO, LANES), F32)
            args = _gdn_chunk_args(c, hg * hp, hp, blk, heads, q_ref, qh_ref, k_ref, kh_ref, v_ref, vh_ref, z_ref,
                                   tail_ref, cw_ref, prm_ref, nw_ref)
            dy = _stack([dy_ref[rows, p * LANES:(p + 1) * LANES].astype(F32) for p in range(hp)])
            pends = [pend_ref[hg * hp + p] for p in range(hp)]
            _, vjp = jax.vjp(functools.partial(_gdn_chunk, hg * hp, heads), *args, states_ref[c, 0])
            (dqm, dqh, dkm, dkh, dvm, dvh, dz, dtl, dwq, dwk, dwv, dalog, ddtb, dnw, ds) = vjp((dy, ds_ref[hg]))
            for p in range(hp):
                h = hg * hp + p
                for j, (dm, dhalo) in enumerate(((dqm, dqh), (dkm, dkh), (dvm, dvh))):
                    full = dm[p] + jnp.concatenate([zpad, pends[p][j]], axis=0)
                    dh_ref[rows, pl.ds(pl.multiple_of(j * d + h * LANES, LANES), LANES)] = full.astype(dh_ref.dtype)
                    pend_ref[h, j] = dhalo[p]
                dh_ref[rows, pl.ds(pl.multiple_of(3 * d + h * LANES, LANES), LANES)] = dz[p].astype(dh_ref.dtype)
                for j, dw in enumerate((dwq, dwk, dwv)):
                    lanes = pl.ds(pl.multiple_of(j * d + h * LANES, LANES), LANES)
                    for r in range(CONV_K):
                        dcw_ref[r:r + 1, lanes] += dw[p][r]
            dtail_ref[rows, :] += dtl
            dprm_ref[0:1, :] += dalog
            dprm_ref[1:2, :] += ddtb
            dprm_ref[2:3, :] += dnw
            ds_ref[hg] = ds
            return carry

        lax.fori_loop(0, nc, step, 0)

    return pl.pallas_call(
        body, name="gdn_bwd", grid=(nb, groups),
        in_specs=_gdn_in_specs(tb, heads, hp, rev_nb=nb) + [
            pl.BlockSpec((nc, 1, hp * LANES, LANES), lambda b, h: (nb - 1 - b, h, 0, 0)),
            pl.BlockSpec((tb, hp * LANES), lambda b, h: (nb - 1 - b, h))],
        out_specs=[pl.BlockSpec((tb, d4), lambda b, h: (nb - 1 - b, 0)),
                   pl.BlockSpec((tb, LANES), lambda b, h: (nb - 1 - b, 0)),
                   pl.BlockSpec((CONV_K, 3 * d), lambda b, h: (0, 0)),
                   pl.BlockSpec((8, LANES), lambda b, h: (0, 0))],
        out_shape=[jax.ShapeDtypeStruct((t, d4), BF16),
                   jax.ShapeDtypeStruct((t, LANES), F32),
                   jax.ShapeDtypeStruct((CONV_K, 3 * d), F32),
                   jax.ShapeDtypeStruct((8, LANES), F32)],
        scratch_shapes=[pltpu.VMEM((groups, hp * LANES, LANES), F32),
                        pltpu.VMEM((heads, 3, HALO, LANES), F32)],
        compiler_params=_cparams(("arbitrary", "arbitrary")),
    )(hmain, hmain, hmain, hmain, hmain, hmain, hmain, tail, conv_w, prm, nw, states, dy)


def _layer_norm(u, g, b):
    mu = jnp.mean(u, axis=1, keepdims=True)
    var = jnp.mean(jnp.square(u - mu), axis=1, keepdims=True)
    return (u - mu) * lax.rsqrt(var + LN_EPS) * g + b


def ln_fwd(u, g, b, *, tr):
    t, d = u.shape

    def body(u_ref, g_ref, b_ref, x_ref, xb_ref):
        x = _layer_norm(u_ref[...], g_ref[...], b_ref[...])
        x_ref[...] = x
        xb_ref[...] = x.astype(BF16)

    row = pl.BlockSpec((tr, d), lambda i: (i, 0))
    vec = pl.BlockSpec((1, d), lambda i: (0, 0))
    return pl.pallas_call(
        body, name="ln_fwd", grid=(t // tr,), in_specs=[row, vec, vec], out_specs=[row, row],
        out_shape=[jax.ShapeDtypeStruct((t, d), F32), jax.ShapeDtypeStruct((t, d), BF16)],
        compiler_params=_cparams(("parallel",)),
    )(u, g, b)


def ln_bwd(u, g, b, dout, *, tr):
    t, d = u.shape

    def body(u_ref, g_ref, b_ref, dout_ref, du_ref, dub_ref, dg_ref, db_ref):
        @pl.when(pl.program_id(0) == 0)
        def _():
            dg_ref[...] = jnp.zeros_like(dg_ref)
            db_ref[...] = jnp.zeros_like(db_ref)

        _, vjp = jax.vjp(_layer_norm, u_ref[...], g_ref[...], b_ref[...])
        du, dg, db = vjp(dout_ref[...])
        du_ref[...] = du
        dub_ref[...] = du.astype(BF16)
        dg_ref[0:1, :] += dg
        db_ref[0:1, :] += db

    row = pl.BlockSpec((tr, d), lambda i: (i, 0))
    vec = pl.BlockSpec((1, d), lambda i: (0, 0))
    acc = pl.BlockSpec((8, d), lambda i: (0, 0))
    return pl.pallas_call(
        body, name="ln_bwd", grid=(t // tr,), in_specs=[row, vec, vec, row], out_specs=[row, row, acc, acc],
        out_shape=[jax.ShapeDtypeStruct((t, d), F32), jax.ShapeDtypeStruct((t, d), BF16),
                   jax.ShapeDtypeStruct((8, d), F32), jax.ShapeDtypeStruct((8, d), F32)],
        compiler_params=_cparams(("arbitrary",)),
    )(u, g, b, dout)


def ln_loss_bwd(u, g, b, target, *, tr):
    t, d = u.shape

    def loss_of(uu, gg, bb, tgt):
        err = jnp.square(_layer_norm(uu, gg, bb) - tgt)
        return 0.5 * jnp.sum(jnp.mean(err, axis=1, keepdims=True), axis=0, keepdims=True)

    def body(u_ref, g_ref, b_ref, t_ref, loss_ref, du_ref, dub_ref, dg_ref, db_ref):
        @pl.when(pl.program_id(0) == 0)
        def _():
            loss_ref[...] = jnp.zeros_like(loss_ref)
            dg_ref[...] = jnp.zeros_like(dg_ref)
            db_ref[...] = jnp.zeros_like(db_ref)

        tgt = t_ref[...]
        val, vjp = jax.vjp(lambda uu, gg, bb: loss_of(uu, gg, bb, tgt), u_ref[...], g_ref[...], b_ref[...])
        du, dg, db = vjp(jnp.ones((1, 1), F32))
        loss_ref[...] += val
        du_ref[...] = du
        dub_ref[...] = du.astype(BF16)
        dg_ref[0:1, :] += dg
        db_ref[0:1, :] += db

    row = pl.BlockSpec((tr, d), lambda i: (i, 0))
    vec = pl.BlockSpec((1, d), lambda i: (0, 0))
    acc = pl.BlockSpec((8, d), lambda i: (0, 0))
    return pl.pallas_call(
        body, name="ln_loss_bwd", grid=(t // tr,), in_specs=[row, vec, vec, row],
        out_specs=[pl.BlockSpec((8, LANES), lambda i: (0, 0)), row, row, acc, acc],
        out_shape=[jax.ShapeDtypeStruct((8, LANES), F32),
                   jax.ShapeDtypeStruct((t, d), F32), jax.ShapeDtypeStruct((t, d), BF16),
                   jax.ShapeDtypeStruct((8, d), F32), jax.ShapeDtypeStruct((8, d), F32)],
        compiler_params=_cparams(("arbitrary",)),
    )(u, g, b, target)


def local_step(x, target, wa_in, alb, a_nw, wa_out, wb_t, wb_tail_t, conv_w, a_log, dt_bias, b_nw, wb_out,
               ln_g, ln_b, *, tb=256, tr=256, hp=HEADS_PER_STEP, first_after=None, b_weights=None, on_b_grads=None,
               on_a_out_grad=None, on_a_grads=None):
    t, d = x.shape
    heads = d // LANES
    tb, tr, hp = min(tb, t), min(tr, t), min(hp, heads)
    xb = x.astype(BF16)
    prm = jnp.zeros((8, LANES), F32).at[0, :heads].set(a_log[0]).at[1, :heads].set(dt_bias[0])

    ha = matmul(xb, wa_in, name="mm_a_in", after=first_after)
    ya, st_a = hgrn2_fwd(ha, alb, a_nw, tb=tb, hp=hp)
    u1 = matmul(ya, wa_out, name="mm_a_out", addend=x, alpha=DEEPNORM_ALPHA)
    x1, x1b = ln_fwd(u1, ln_g[0:1], ln_b[0:1], tr=tr)
    if b_weights is not None:
        wb_t, wb_tail_t, wb_out, conv_w = b_weights(x1b)
    n_tail = 2 * heads
    hb = matmul(x1b, wb_t, name="mm_b_in", nt=True, n=4 * d)
    tl = matmul(x1b, wb_tail_t, name="mm_b_tail", nt=True)
    yb, st_b = gdn_fwd(hb, tl, conv_w, prm, b_nw, tb=tb, hp=hp)
    u2 = matmul(yb, wb_out, name="mm_b_out", addend=x1, alpha=DEEPNORM_ALPHA)

    loss, du2, du2b, dg2, db2 = ln_loss_bwd(u2, ln_g[1:2], ln_b[1:2], target, tr=tr)
    d_wb_out = matmul(yb, du2b, name="mm_dwb_out", lhs_t=True, out_dtype=BF16)
    dyb = matmul(du2b, wb_out, name="mm_dyb", nt=True)
    dhb, dtl, d_conv, dprm = gdn_bwd(hb, tl, conv_w, prm, b_nw, st_b, dyb, tb=tb, hp=hp)
    dtlb = dtl.astype(BF16)
    d_wb_t = matmul(dhb, x1b, name="mm_dwb_main", lhs_t=True, out_dtype=BF16, out_rows=4 * d + n_tail)
    d_wb_t = matmul(dtlb[:, :n_tail], x1b, name="mm_dwb_tail", lhs_t=True, out_dtype=BF16, into=(d_wb_t, 4 * d))
    sent_b = on_b_grads(d_wb_t, d_wb_out) if on_b_grads is not None else None
    dx1_tail = matmul(dtlb, wb_tail_t, name="mm_dx1_tail", addend=du2, alpha=DEEPNORM_ALPHA, after=sent_b)
    dx1 = matmul(dhb, wb_t, name="mm_dx1", addend=dx1_tail, alpha=1.0, tk=2048)
    du1, du1b, dg1, db1 = ln_bwd(u1, ln_g[0:1], ln_b[0:1], dx1, tr=tr)
    d_wa_out = matmul(ya, du1b, name="mm_dwa_out", lhs_t=True, out_dtype=BF16)
    sent_c = on_a_out_grad(d_wa_out) if on_a_out_grad is not None else None
    dya = matmul(du1b, wa_out, name="mm_dya", nt=True, after=sent_c)
    dha, d_alb, d_anw = hgrn2_bwd(ha, alb, a_nw, st_a, dya, tb=tb, hp=hp)
    d_wa_in = matmul(xb, dha, name="mm_dwa_in", lhs_t=True, out_dtype=BF16,
                     out_split=wa_in.shape[0] if wa_in.ndim == 3 else 1)
    sent_a = on_a_grads(d_wa_in) if on_a_grads is not None else None
    grad_x = matmul(dha, wa_in, name="mm_dx", nt=True, addend=du1, alpha=DEEPNORM_ALPHA, after=sent_a, tk=2048)

    small = dict(
        a_lower_bounds=d_alb, a_norm_w=d_anw[0:1], b_conv_w=d_conv,
        b_a_log=dprm[0:1, :heads], b_dt_bias=dprm[1:2, :heads], b_norm_w=dprm[2:3],
        ln_g=jnp.concatenate([dg1[0:1], dg2[0:1]], axis=0), ln_b=jnp.concatenate([db1[0:1], db2[0:1]], axis=0))
    big = dict(a_w_in=d_wa_in, a_w_out=d_wa_out, b_w_t=d_wb_t, b_w_out=d_wb_out)
    return loss, grad_x, big, small


MESH_ID = pl.DeviceIdType.MESH


def _position():
    return lax.axis_index("x"), lax.axis_index("y"), lax.axis_index("c")


def _index_of(p):
    return 4 * p[0] + 2 * p[1] + p[2]


def all_gather(shards, *, name, space):
    n = len(shards)

    def body(*refs):
        ins, outs = refs[:n], refs[n:2 * n]
        send_sems, recv_sems, local_sems = refs[2 * n:]
        x, y, c = _position()
        me, sibling = (x, y, c), (x, y, 1 - c)
        chips = [(1 - x, y), (x, 1 - y), (1 - x, 1 - y)]

        def copy(a, k, block, to, own=False):
            dst = outs[a].at[_index_of(block)]
            return pltpu.make_async_remote_copy(
                src_ref=ins[a] if own else dst, dst_ref=dst,
                send_sem=send_sems.at[7 * a + k], recv_sem=recv_sems.at[7 * a + k],
                device_id=to, device_id_type=MESH_ID)

        mine = [pltpu.make_async_copy(ins[a], outs[a].at[_index_of(me)], local_sems.at[a]) for a in range(n)]
        for cp in mine:
            cp.start()
        first = []
        for a in range(n):
            first.append(copy(a, 0, me, sibling, own=True))
            first += [copy(a, 1 + j, me, (*chip, c), own=True) for j, chip in enumerate(chips)]
        for cp in first:
            cp.start()
        passed = []
        for j, chip in enumerate(chips):
            for a in range(n):
                copy(a, 1 + j, (*chip, c), me).wait_recv()
                fwd = copy(a, 4 + j, (*chip, c), sibling)
                fwd.start()
                passed.append(fwd)
        for a in range(n):
            copy(a, 0, sibling, me).wait_recv()
            for j, chip in enumerate(chips):
                copy(a, 4 + j, (*chip, 1 - c), me).wait_recv()
        for cp in first + passed:
            cp.wait_send()
        for cp in mine:
            cp.wait()

    spec = pl.BlockSpec(memory_space=space)
    return pl.pallas_call(
        body, name=name,
        in_specs=[spec] * n, out_specs=[spec] * n,
        out_shape=[jax.ShapeDtypeStruct((N_DEV, *s.shape), s.dtype) for s in shards],
        scratch_shapes=[pltpu.SemaphoreType.DMA((7 * n,)), pltpu.SemaphoreType.DMA((7 * n,)),
                        pltpu.SemaphoreType.DMA((n,))],
        compiler_params=pltpu.CompilerParams(vmem_limit_bytes=VMEM_LIMIT),
    )(*shards)


HBM_SPEC = pl.BlockSpec(memory_space=pltpu.HBM)
SEM_SPEC = pl.BlockSpec(memory_space=pltpu.SEMAPHORE)


def _peer_copies(srcs, lands, send_sems, recv_sems, by_dest):
    x, y, c = _position()
    me = _index_of((x, y, c))
    copies = []
    for a, (src, land) in enumerate(zip(srcs, lands)):
        for r in range(1, N_DEV):
            peer = (x ^ ((r >> 2) & 1), y ^ ((r >> 1) & 1), c ^ (r & 1))
            copies.append(pltpu.make_async_remote_copy(
                src_ref=src.at[_index_of(peer)] if by_dest else src, dst_ref=land.at[me],
                send_sem=send_sems.at[7 * a + r - 1], recv_sem=recv_sems.at[7 * a + r - 1],
                device_id=peer, device_id_type=MESH_ID))
    return copies


def copies_start(srcs, lands, *, name, by_dest):
    n = len(srcs)

    def body(*refs):
        src_refs, land_refs = refs[:n], refs[n:2 * n]
        send_sems, recv_sems = refs[2 * n], refs[2 * n + 1]
        token = refs[-1]
        for cp in _peer_copies(src_refs, land_refs, send_sems, recv_sems, by_dest):
            cp.start()
        token[...] = jnp.zeros_like(token)

    arrays = [pltpu.with_memory_space_constraint(t, pltpu.HBM) for t in (*srcs, *lands)]
    outs = pl.pallas_call(
        body, name=name,
        out_shape=(pltpu.SemaphoreType.DMA((7 * n,)), pltpu.SemaphoreType.DMA((7 * n,)),
                   *[pltpu.HBM(t.shape, t.dtype) for t in arrays], jax.ShapeDtypeStruct((8, LANES), F32)),
        in_specs=[HBM_SPEC] * (2 * n),
        out_specs=(SEM_SPEC, SEM_SPEC, *[HBM_SPEC] * (2 * n), pl.BlockSpec(memory_space=pltpu.VMEM)),
        input_output_aliases={i: 2 + i for i in range(2 * n)},
        compiler_params=pltpu.CompilerParams(has_side_effects=pltpu.SideEffectType.DATAFLOW_SIDE_EFFECTING),
    )(*arrays)
    return (outs[0], outs[1]), list(outs[2:2 + n]), list(outs[2 + n:2 + 2 * n]), outs[-1]


def copies_wait(sems, srcs, lands, after, *, name, by_dest):
    n = len(srcs)

    def body(*refs):
        src_refs, land_refs = refs[:n], refs[n:2 * n]
        send_sems, recv_sems = refs[2 * n], refs[2 * n + 1]
        for cp in _peer_copies(src_refs, land_refs, send_sems, recv_sems, by_dest):
            cp.wait_send()
            cp.wait_recv()

    outs = pl.pallas_call(
        body, name=name,
        out_shape=tuple(pltpu.HBM(t.shape, t.dtype) for t in (*srcs, *lands)),
        in_specs=[HBM_SPEC] * (2 * n) + [SEM_SPEC, SEM_SPEC, pl.BlockSpec(memory_space=pl.ANY)],
        out_specs=tuple([HBM_SPEC] * (2 * n)),
        input_output_aliases={i: i for i in range(2 * n)},
        compiler_params=pltpu.CompilerParams(has_side_effects=pltpu.SideEffectType.DATAFLOW_SIDE_EFFECTING),
    )(*srcs, *lands, sems[0], sems[1], after)
    return list(outs[n:])


def _landing(own, me):
    return lax.dynamic_update_slice_in_dim(lax.empty((N_DEV, *own.shape), own.dtype), own[None], me, 0)


def adamw(parts, w, m, v, *, name, tr=64, tc=None):
    p, r, c = parts.shape
    tr = r if tc is not None else min(tr, r)
    tc = c if tc is None else tc
    assert r % tr == 0 and c % tc == 0
    c1 = 1.0 / (1.0 - ADAM_B1 ** ADAM_STEP)
    c2 = 1.0 / (1.0 - ADAM_B2 ** ADAM_STEP)

    def body(p_ref, w_ref, m_ref, v_ref, g_ref, d_ref, nm_ref, nv_ref):
        g = p_ref[0].astype(F32)
        for i in range(1, p):
            g = g + p_ref[i].astype(F32)
        nm = ADAM_B1 * m_ref[...] + (1.0 - ADAM_B1) * g
        nv = ADAM_B2 * v_ref[...] + (1.0 - ADAM_B2) * jnp.square(g)
        g_ref[...] = g
        nm_ref[...] = nm
        nv_ref[...] = nv
        d_ref[...] = -ADAM_LR * ((nm * c1) / (jnp.sqrt(nv * c2) + ADAM_EPS) + ADAM_WD * w_ref[...])

    blk = pl.BlockSpec((tr, tc), lambda i, j: (i, j))
    out = jax.ShapeDtypeStruct((r, c), F32)
    return pl.pallas_call(
        body, name=name, grid=(r // tr, c // tc),
        in_specs=[pl.BlockSpec((p, tr, tc), lambda i, j: (0, i, j)), blk, blk, blk],
        out_specs=[blk] * 4, out_shape=[out] * 4,
        compiler_params=_cparams(("parallel", "parallel")),
    )(parts, w, m, v)


def _pack(d, vals):
    heads = d // LANES
    vecs = jnp.zeros((8, LANES), F32)
    vecs = vecs.at[0:1].set(vals["a_norm_w"]).at[1:2, :heads].set(vals["b_a_log"])
    vecs = vecs.at[2:3, :heads].set(vals["b_dt_bias"]).at[3:4].set(vals["b_norm_w"])
    rows = [vals["a_lower_bounds"].reshape(-1, LANES), vals["ln_g"].reshape(-1, LANES),
            vals["ln_b"].reshape(-1, LANES), vecs]
    return jnp.concatenate(rows, axis=0)


def _unpack(d, packed):
    heads = d // LANES
    n3, n2 = 3 * heads, 2 * heads
    o = 0
    out = {}
    out["a_lower_bounds"] = packed[o:o + n3].reshape(3, d); o += n3
    out["ln_g"] = packed[o:o + n2].reshape(2, d); o += n2
    out["ln_b"] = packed[o:o + n2].reshape(2, d); o += n2
    out["a_norm_w"] = packed[o:o + 1]
    out["b_a_log"] = packed[o + 1:o + 2, :heads]
    out["b_dt_bias"] = packed[o + 2:o + 3, :heads]
    out["b_norm_w"] = packed[o + 3:o + 4]
    return out


ORDER = ("a_w_in", "a_lower_bounds", "a_norm_w", "a_w_out", "b_w_in", "b_conv_w", "b_a_log", "b_dt_bias", "b_norm_w",
         "b_w_out", "ln_g", "ln_b")


def kernel(x, a_w_in, a_lower_bounds, a_norm_w, a_w_out, b_w_in, b_conv_w, b_a_log, b_dt_bias, b_norm_w, b_w_out, ln_g, ln_b, loss_target, m_a_w_in, m_a_lower_bounds, m_a_norm_w, m_a_w_out, m_b_w_in, m_b_conv_w, m_b_a_log, m_b_dt_bias, m_b_norm_w, m_b_w_out, m_ln_g, m_ln_b, v_a_w_in, v_a_lower_bounds, v_a_norm_w, v_a_w_out, v_b_w_in, v_b_conv_w, v_b_a_log, v_b_dt_bias, v_b_norm_w, v_b_w_out, v_ln_g, v_ln_b):
    w = dict(a_w_in=a_w_in, a_lower_bounds=a_lower_bounds, a_norm_w=a_norm_w, a_w_out=a_w_out, b_w_in=b_w_in,
             b_conv_w=b_conv_w, b_a_log=b_a_log, b_dt_bias=b_dt_bias, b_norm_w=b_norm_w, b_w_out=b_w_out, ln_g=ln_g, ln_b=ln_b)
    m = dict(a_w_in=m_a_w_in, a_lower_bounds=m_a_lower_bounds, a_norm_w=m_a_norm_w, a_w_out=m_a_w_out, b_w_in=m_b_w_in,
             b_conv_w=m_b_conv_w, b_a_log=m_b_a_log, b_dt_bias=m_b_dt_bias, b_norm_w=m_b_norm_w, b_w_out=m_b_w_out,
             ln_g=m_ln_g, ln_b=m_ln_b)
    v = dict(a_w_in=v_a_w_in, a_lower_bounds=v_a_lower_bounds, a_norm_w=v_a_norm_w, a_w_out=v_a_w_out, b_w_in=v_b_w_in,
             b_conv_w=v_b_conv_w, b_a_log=v_b_a_log, b_dt_bias=v_b_dt_bias, b_norm_w=v_b_norm_w, b_w_out=v_b_w_out,
             ln_g=v_ln_g, ln_b=v_ln_b)
    t, d = x.shape[1], x.shape[2]
    heads = d // LANES
    n_tail = 2 * heads
    me = _index_of(_position())

    ga_in, ga_out, g_conv = all_gather([a_w_in[0].astype(BF16), a_w_out[0].astype(BF16), b_conv_w[0]],
                                       name="gather_weights_a", space=pltpu.HBM)
    b_shards = [jnp.transpose(b_w_in[0]).astype(BF16), b_w_out[0].astype(BF16)]
    sems_w, b_shards, lands_w, token_w = copies_start(b_shards, [_landing(s, me) for s in b_shards],
                                                      name="gather_weights_b_start", by_dest=False)
    wa_out = ga_out.reshape(d, d)
    conv_w = jnp.transpose(g_conv, (1, 0, 2)).reshape(CONV_K, 3 * d)

    def b_weights(after):
        gb_in, gb_out = copies_wait(sems_w, b_shards, lands_w, after, name="gather_weights_b_wait", by_dest=False)
        wb_t = gb_in.reshape(4 * d + n_tail, d)
        wb_tail_t = jnp.concatenate([wb_t[4 * d:], jnp.zeros((LANES - n_tail, d), BF16)], axis=0)
        return wb_t, wb_tail_t, gb_out.reshape(d, d), conv_w

    sent = {}

    def send_grads(key, parts):
        own = [lax.dynamic_index_in_dim(p, me, 0, keepdims=False) for p in parts]
        sems, parts, lands, token = copies_start(parts, [_landing(o, me) for o in own],
                                                 name="exchange_" + key + "_start", by_dest=True)
        sent[key] = (sems, parts, lands)
        return token

    def on_b_grads(d_wb_t, d_out):
        return send_grads("b", [d_wb_t.reshape(N_DEV, -1, d), d_out.reshape(N_DEV, d // N_DEV, d)])

    loss, grad_x, big, small = local_step(
        x[0], loss_target[0], ga_in, a_lower_bounds, a_norm_w, wa_out, None, None, None, b_a_log, b_dt_bias,
        b_norm_w, None, ln_g, ln_b, first_after=token_w, b_weights=b_weights, on_b_grads=on_b_grads,
        on_a_out_grad=lambda g: send_grads("c", [g.reshape(N_DEV, d // N_DEV, d)]),
        on_a_grads=lambda g: send_grads("a", [g]))
    loss = lax.psum(loss[0, 0], ("x", "y", "c"))

    conv_rows = small["b_conv_w"].reshape(-1, LANES)
    n_conv = conv_rows.shape[0]
    packed_grads = jnp.concatenate([conv_rows, _pack(d, small)], axis=0)
    (got,) = all_gather([packed_grads], name="gather_small", space=pltpu.VMEM)
    res = {}
    packed = adamw(got[:, n_conv:], _pack(d, w), _pack(d, m), _pack(d, v), name="adamw_small", tr=4096)
    for k, vals in zip(("grad", "delta", "new_m", "new_v"), packed):
        res[k] = _unpack(d, vals)
    shard_ch = 3 * d // N_DEV
    conv_parts = lax.dynamic_slice_in_dim(got[:, :n_conv].reshape(N_DEV, CONV_K, 3 * d), me * shard_ch, shard_ch, axis=2)
    conv_out = adamw(conv_parts, b_conv_w[0], m_b_conv_w[0], v_b_conv_w[0], name="adamw_conv")
    for k, vals in zip(("grad", "delta", "new_m", "new_v"), conv_out):
        res[k]["b_conv_w"] = vals[None]

    after = conv_out[0]
    for key, names in (("b", ("b_w_in", "b_w_out")), ("c", ("a_w_out",)), ("a", ("a_w_in",))):
        sems, parts, lands = sent[key]
        recv = copies_wait(sems, parts, lands, after, name="exchange_" + key + "_wait", by_dest=True)
        for name, got_parts in zip(names, recv):
            if name == "b_w_in":
                raw = adamw(got_parts, *[jnp.transpose(t[name][0]) for t in (w, m, v)], name="adamw_" + name, tc=256)
                outs = [jnp.transpose(o) for o in raw]
            else:
                raw = outs = adamw(got_parts, w[name][0], m[name][0], v[name][0], name="adamw_" + name)
            for k, vals in zip(("grad", "delta", "new_m", "new_v"), outs):
                res[k][name] = vals[None]
            after = raw[0]

    return (loss, grad_x[None], *[res["grad"][k] for k in ORDER], *[res["delta"][k] for k in ORDER],
            *[res["new_m"][k] for k in ORDER], *[res["new_v"][k] for k in ORDER])
```

```python
import functools
import math

import jax
import jax.numpy as jnp
from jax import lax
from jax.experimental import pallas as pl
from jax.experimental.pallas import tpu as pltpu

F32 = jnp.float32
BF16 = jnp.bfloat16

N_DEV = 8
LANES = 128
CHUNK = 64
SUB = 16
HALO = 8
CONV_K = 4
DEPTH = 2
DEEPNORM_ALPHA = (2.0 * DEPTH) ** 0.25
LN_EPS = 1e-5
RMS_EPS = 1e-6
L2_EPS = 1e-6
EXP_CLAMP = 60.0
ADAM_LR = 0.001
ADAM_B1 = 0.9
ADAM_B2 = 0.999
ADAM_EPS = 1e-08
ADAM_WD = 0.01
ADAM_STEP = 10
VMEM_LIMIT = 56 * 1024 * 1024
HEADS_PER_STEP = 4

NN = ((1,), (0,))
NT = ((1,), (1,))
TN = ((0,), (0,))


def _dg(a, b, dims, precision=None):
    return lax.dot_general(a, b, (dims, ((), ())), precision=precision, preferred_element_type=F32)


def _bdot(a, b, dims):
    return _dg(a.astype(BF16), b.astype(BF16), dims)


def _split3(t):
    hi = t.astype(BF16)
    r = t - hi.astype(F32)
    mid = r.astype(BF16)
    return hi, mid, (r - mid.astype(F32)).astype(BF16)


@jax.custom_vjp
def _sdot(sel, t):
    sel = sel.astype(BF16)
    hi, mid, lo = _split3(t)
    return (_dg(sel, lo, NN) + _dg(sel, mid, NN)) + _dg(sel, hi, NN)


def _sdot_fwd(sel, t):
    return _sdot(sel, t), sel


def _sdot_bwd(sel, ct):
    selb = sel.astype(BF16)
    hi, mid, lo = _split3(ct)
    return jnp.zeros_like(sel), (_dg(selb, lo, TN) + _dg(selb, mid, TN)) + _dg(selb, hi, TN)


_sdot.defvjp(_sdot_fwd, _sdot_bwd)


@jax.custom_vjp
def _tri_inv(m):
    n = m.shape[0]
    row = lax.broadcasted_iota(jnp.int32, (n, n), 0)
    col = lax.broadcasted_iota(jnp.int32, (n, n), 1)
    inv = (col == row).astype(F32) + m
    mp = m
    for _ in range(int(math.log2(CHUNK)) - 1):
        mp = _bdot(mp, mp, NN)
        inv = inv + _bdot(inv, mp, NN)
    return inv


def _tri_inv_fwd(m):
    inv = _tri_inv(m)
    return inv, inv


def _tri_inv_bwd(inv, ct):
    return (_bdot(_bdot(inv, ct, TN), inv, NT),)


_tri_inv.defvjp(_tri_inv_fwd, _tri_inv_bwd)


@functools.partial(jax.custom_vjp, nondiff_argnums=(2,))
def _shift_rows(xm, xh, back):
    r = pltpu.roll(xm, back, 0)
    row = lax.broadcasted_iota(jnp.int32, xh.shape, 0)
    top = jnp.where(row < back, pltpu.roll(xh, back, 0), r[0:HALO])
    return jnp.concatenate([top, r[HALO:]], axis=0)


def _shift_rows_fwd(xm, xh, back):
    return _shift_rows(xm, xh, back), None


def _shift_rows_bwd(back, _, ct):
    row = lax.broadcasted_iota(jnp.int32, ct.shape, 0)
    dxm = jnp.where(row < CHUNK - back, pltpu.roll(ct, CHUNK - back, 0), 0.0)
    rowh = lax.broadcasted_iota(jnp.int32, (HALO, ct.shape[1]), 0)
    dxh = jnp.where(rowh >= HALO - back, pltpu.roll(ct[0:HALO], HALO - back, 0), 0.0)
    return dxm, dxh


_shift_rows.defvjp(_shift_rows_fwd, _shift_rows_bwd)


def _silu(t):
    return t * jax.nn.sigmoid(t)


def _softplus(t):
    return jnp.where(t > 20.0, t, jnp.log1p(jnp.exp(jnp.minimum(t, 20.0))))


def _cparams(sem=None):
    kw = dict(vmem_limit_bytes=VMEM_LIMIT)
    if sem is not None:
        kw["dimension_semantics"] = sem
    return pltpu.CompilerParams(**kw)


def matmul(a, b, *, name, nt=False, lhs_t=False, n=None, addend=None, alpha=1.0, out_dtype=F32, out_split=1,
           out_rows=None, into=None, after=None, tm=1024, tn=1024, tk=4096):
    k, m = a.shape[::-1] if not lhs_t else a.shape
    b_split = b.shape[0] if b.ndim == 3 else 1
    b_rows, b_cols = b.shape[-2], b.shape[-1] * b_split
    n = (b_rows if nt else b_cols) if n is None else n
    tm, tn, tk = min(tm, m), min(tn, n), min(tk, k)
    if b_split > 1:
        part = b_cols // b_split
        tn, tk = (tn, min(tk, part)) if nt else (min(tn, part), tk)
    if out_split > 1:
        tn = min(tn, n // out_split)
    assert m % tm == 0 and n % tn == 0 and k % tk == 0, (a.shape, b.shape, nt)
    assert not (lhs_t and nt)
    nk = k // tk
    dims = TN if lhs_t else (NT if nt else NN)

    def body(*refs):
        a_ref, b_ref = refs[:2]
        add_ref = None if addend is None else refs[2]

        def finish(r, o_ref):
            if add_ref is not None:
                r = r + alpha * add_ref[...].astype(F32)
            o_ref[...] = r.astype(o_ref.dtype)

        if nk == 1:
            finish(_dg(a_ref[...], b_ref[...], dims), refs[-1])
            return
        o_ref, acc_ref = refs[-2:]
        kk = pl.program_id(2)

        @pl.when(kk == 0)
        def _():
            acc_ref[...] = _dg(a_ref[...], b_ref[...], dims)

        @pl.when(kk > 0)
        def _():
            acc_ref[...] += _dg(a_ref[...], b_ref[...], dims)

        @pl.when(kk == nk - 1)
        def _():
            finish(acc_ref[...], o_ref)

    if b_split == 1:
        b_spec = (pl.BlockSpec((tn, tk), lambda i, j, kk: (j, kk)) if nt
                  else pl.BlockSpec((tk, tn), lambda i, j, kk: (kk, j)))
    elif nt:
        per = (b_cols // b_split) // tk
        b_spec = pl.BlockSpec((None, tn, tk), lambda i, j, kk: (kk // per, j, kk % per))
    else:
        per = (b_cols // b_split) // tn
        b_spec = pl.BlockSpec((None, tk, tn), lambda i, j, kk: (j // per, kk, j % per))
    a_spec = (pl.BlockSpec((tk, tm), lambda i, j, kk: (kk, i)) if lhs_t
              else pl.BlockSpec((tm, tk), lambda i, j, kk: (i, kk)))
    in_specs = [a_spec, b_spec]
    args = [a, b]
    if addend is not None:
        in_specs.append(pl.BlockSpec((tm, tn), lambda i, j, kk: (i, j)))
        args.append(addend)
    for extra in (after, None if into is None else into[0]):
        if extra is not None:
            in_specs.append(pl.BlockSpec(memory_space=pl.ANY))
            args.append(extra)
    aliases = {}
    if into is not None:
        assert out_split == 1 and into[1] % tm == 0 and into[0].dtype == out_dtype
        row0 = into[1] // tm
        out_spec = pl.BlockSpec((tm, tn), lambda i, j, kk: (row0 + i, j))
        out_shape = jax.ShapeDtypeStruct(into[0].shape, out_dtype)
        aliases = {len(args) - 1: 0}
    elif out_split == 1:
        out_spec = pl.BlockSpec((tm, tn), lambda i, j, kk: (i, j))
        out_shape = jax.ShapeDtypeStruct((m if out_rows is None else out_rows, n), out_dtype)
    else:
        per_o = (n // out_split) // tn
        out_spec = pl.BlockSpec((None, tm, tn), lambda i, j, kk: (j // per_o, i, j % per_o))
        out_shape = jax.ShapeDtypeStruct((out_split, m, n // out_split), out_dtype)
    return pl.pallas_call(
        body, name=name, grid=(m // tm, n // tn, nk),
        in_specs=in_specs,
        out_specs=out_spec,
        out_shape=out_shape,
        input_output_aliases=aliases,
        scratch_shapes=[] if nk == 1 else [pltpu.VMEM((tm, tn), F32)],
        compiler_params=_cparams(("parallel", "parallel", "arbitrary")),
    )(*args)


def _gated_rmsnorm(o, z, nw):
    r = lax.rsqrt(jnp.mean(o * o, axis=1, keepdims=True) + RMS_EPS)
    return o * r * nw * _silu(z)


def _chunk_consts():
    row = lax.broadcasted_iota(jnp.int32, (CHUNK, CHUNK), 0)
    col = lax.broadcasted_iota(jnp.int32, (CHUNK, CHUNK), 1)
    return row, col


def _head_cols(tb, groups, hp, j, rev_nb=None):
    def bb(b):
        return b if rev_nb is None else rev_nb - 1 - b
    return pl.BlockSpec((tb, hp * LANES), lambda b, h: (bb(b), j * groups + h))


def _stack(pieces):
    return jnp.concatenate(pieces, axis=0) if len(pieces) > 1 else pieces[0]


def _place(t, g):
    if g == 1:
        return t
    head = lax.broadcasted_iota(jnp.int32, t.shape, 0) // CHUNK
    return jnp.concatenate([jnp.where(head == h, t, 0.0) for h in range(g)], axis=1)


def _hgrn2_chunk(qr, fr, iv, z, a0, a1, a2, nw, st):
    g = len(qr)
    n = g * CHUNK
    row = lax.broadcasted_iota(jnp.int32, (n, n), 0)
    col = lax.broadcasted_iota(jnp.int32, (n, n), 1)
    same = (row // CHUNK) == (col // CHUNK)
    qs, ks, gl = [], [], []
    for h in range(g):
        amax = jnp.maximum(jnp.maximum(a0[h], a1[h]), a2[h])
        e0, e1, e2 = jnp.exp(a0[h] - amax), jnp.exp(a1[h] - amax), jnp.exp(a2[h] - amax)
        lb = e0 / (e0 + e1 + e2)
        forget = lb + (1.0 - lb) * jax.nn.sigmoid(fr[h])
        qs.append(_silu(qr[h]))
        ks.append(1.0 - forget)
        gl.append(jnp.log(forget))
    q, k, glog, v, zs = _stack(qs), _stack(ks), _stack(gl), _stack(list(iv)), _stack(list(z))
    cum = _sdot((same & (col <= row)).astype(F32), glog)
    sub = (row % CHUNK) // SUB
    rowl = lax.broadcasted_iota(jnp.int32, (n, LANES), 0)
    headl, subl = rowl // CHUNK, (rowl % CHUNK) // SUB

    def row_of(r):
        picks = [jnp.sum(jnp.where(rowl == h * CHUNK + r, cum, 0.0), axis=0, keepdims=True) for h in range(g)]
        out = jnp.zeros((n, LANES), F32)
        for h in range(g):
            out = out + jnp.where(headl == h, picks[h], 0.0)
        return out, picks

    refs = [jnp.zeros((n, LANES), F32)] + [row_of(i * SUB - 1)[0] for i in range(1, CHUNK // SUB)]
    own = jnp.zeros((n, LANES), F32)
    for i, ref in enumerate(refs):
        own = own + jnp.where(subl == i, ref, 0.0)
    qt = q * jnp.exp(cum - own)
    att = jnp.zeros((n, n), F32)
    for i, ref in enumerate(refs):
        kt = k * jnp.exp(jnp.minimum(ref - cum, EXP_CLAMP))
        att = att + jnp.where(same & (sub == i) & (col <= row), _bdot(qt, kt, NT), 0.0)
    cl_rows, lasts = row_of(CHUNK - 1)
    o = _bdot(att, v, NN) + _bdot(_place(q * jnp.exp(cum), g), st, NT)
    kd = k * jnp.exp(cl_rows - cum)
    cl_wide = jnp.concatenate(lasts, axis=1) if g > 1 else lasts[0]
    st_new = st * jnp.exp(cl_wide) + _bdot(v, _place(kd, g), TN)
    return _gated_rmsnorm(o, zs, nw), st_new


def hgrn2_fwd(hproj, alb, nw, *, tb, hp):
    t, d4 = hproj.shape
    d = d4 // 4
    heads = d // LANES
    groups = heads // hp
    nb, nc = t // tb, tb // CHUNK

    def body(q_ref, f_ref, i_ref, z_ref, alb_ref, nw_ref, y_ref, states_ref, st_ref):
        b, hg = pl.program_id(0), pl.program_id(1)

        @pl.when(b == 0)
        def _():
            st_ref[hg] = jnp.zeros((LANES, hp * LANES), F32)

        nwv = nw_ref[...]
        lns = [slice(p * LANES, (p + 1) * LANES) for p in range(hp)]
        albs = []
        for r in range(3):
            albs.append(tuple(alb_ref[r:r + 1, pl.ds(pl.multiple_of((hg * hp + p) * LANES, LANES), LANES)]
                              for p in range(hp)))

        def step(c, carry):
            rows = pl.ds(pl.multiple_of(c * CHUNK, CHUNK), CHUNK)
            st = st_ref[hg]
            states_ref[c, 0] = st
            ins = [tuple(ref[rows, ln] for ln in lns) for ref in (q_ref, f_ref, i_ref, z_ref)]
            y, st_new = _hgrn2_chunk(*ins, *albs, nwv, st)
            for p in range(hp):
                y_ref[rows, lns[p]] = y[p * CHUNK:(p + 1) * CHUNK].astype(y_ref.dtype)
            st_ref[hg] = st_new
            return carry

        lax.fori_loop(0, nc, step, 0)

    return pl.pallas_call(
        body, name="hgrn2_fwd", grid=(nb, groups),
        in_specs=[_head_cols(tb, groups, hp, j) for j in range(4)] + [
            pl.BlockSpec((3, d), lambda b, h: (0, 0)),
            pl.BlockSpec((1, LANES), lambda b, h: (0, 0))],
        out_specs=[pl.BlockSpec((tb, hp * LANES), lambda b, h: (b, h)),
                   pl.BlockSpec((nc, 1, LANES, hp * LANES), lambda b, h: (b, h, 0, 0))],
        out_shape=[jax.ShapeDtypeStruct((t, d), BF16),
                   jax.ShapeDtypeStruct((t // CHUNK, groups, LANES, hp * LANES), F32)],
        scratch_shapes=[pltpu.VMEM((groups, LANES, hp * LANES), F32)],
        compiler_params=_cparams(("arbitrary", "arbitrary")),
    )(hproj, hproj, hproj, hproj, alb, nw)


def hgrn2_bwd(hproj, alb, nw, states, dy, *, tb, hp):
    t, d4 = hproj.shape
    d = d4 // 4
    heads = d // LANES
    groups = heads // hp
    nb, nc = t // tb, tb // CHUNK

    def body(q_ref, f_ref, i_ref, z_ref, alb_ref, nw_ref, states_ref, dy_ref,
             dh_ref, dalb_ref, dnw_ref, dst_ref):
        b, hg = pl.program_id(0), pl.program_id(1)

        @pl.when(b == 0)
        def _():
            dst_ref[hg] = jnp.zeros((LANES, hp * LANES), F32)

        @pl.when((b == 0) & (hg == 0))
        def _():
            dalb_ref[...] = jnp.zeros_like(dalb_ref)
            dnw_ref[...] = jnp.zeros_like(dnw_ref)

        nwv = nw_ref[...]
        lns = [slice(p * LANES, (p + 1) * LANES) for p in range(hp)]
        lanes_of = [pl.ds(pl.multiple_of((hg * hp + p) * LANES, LANES), LANES) for p in range(hp)]
        albs = [tuple(alb_ref[r:r + 1, lanes_of[p]] for p in range(hp)) for r in range(3)]

        def step(i, carry):
            c = nc - 1 - i
            rows = pl.ds(pl.multiple_of(c * CHUNK, CHUNK), CHUNK)
            ins = [tuple(ref[rows, ln] for ln in lns) for ref in (q_ref, f_ref, i_ref, z_ref)]
            dy = _stack([dy_ref[rows, ln].astype(F32) for ln in lns])
            _, vjp = jax.vjp(_hgrn2_chunk, *ins, *albs, nwv, states_ref[c, 0])
            dq, df, di, dz, da0, da1, da2, dnw, dst = vjp((dy, dst_ref[hg]))
            for p in range(hp):
                h = hg * hp + p
                for j, val in enumerate((dq, df, di, dz)):
                    dh_ref[rows, pl.ds(pl.multiple_of(j * d + h * LANES, LANES), LANES)] = val[p].astype(dh_ref.dtype)
                dalb_ref[0:1, lanes_of[p]] += da0[p]
                dalb_ref[1:2, lanes_of[p]] += da1[p]
                dalb_ref[2:3, lanes_of[p]] += da2[p]
            dnw_ref[0:1, :] += dnw
            dst_ref[hg] = dst
            return carry

        lax.fori_loop(0, nc, step, 0)

    return pl.pallas_call(
        body, name="hgrn2_bwd", grid=(nb, groups),
        in_specs=[_head_cols(tb, groups, hp, j, rev_nb=nb) for j in range(4)] + [
            pl.BlockSpec((3, d), lambda b, h: (0, 0)),
            pl.BlockSpec((1, LANES), lambda b, h: (0, 0)),
            pl.BlockSpec((nc, 1, LANES, hp * LANES), lambda b, h: (nb - 1 - b, h, 0, 0)),
            pl.BlockSpec((tb, hp * LANES), lambda b, h: (nb - 1 - b, h))],
        out_specs=[pl.BlockSpec((tb, d4), lambda b, h: (nb - 1 - b, 0)),
                   pl.BlockSpec((3, d), lambda b, h: (0, 0)),
                   pl.BlockSpec((8, LANES), lambda b, h: (0, 0))],
        out_shape=[jax.ShapeDtypeStruct((t, d4), BF16),
                   jax.ShapeDtypeStruct((3, d), F32),
                   jax.ShapeDtypeStruct((8, LANES), F32)],
        scratch_shapes=[pltpu.VMEM((groups, LANES, hp * LANES), F32)],
        compiler_params=_cparams(("arbitrary", "arbitrary")),
    )(hproj, hproj, hproj, hproj, alb, nw, states, dy)


def _conv_silu(xm, xh, w):
    acc = w[CONV_K - 1] * xm
    for j in range(CONV_K - 1):
        acc = acc + w[j] * _shift_rows(xm, xh, CONV_K - 1 - j)
    return _silu(acc)


def _l2norm(t):
    return t * lax.rsqrt(jnp.sum(t * t, axis=1, keepdims=True) + L2_EPS)


def _gdn_chunk(head0, hh, qm, qh, km, kh, vm, vh, z, tail, wq, wk, wv, alog_row, dtb_row, nw, s):
    g = len(qm)
    n = g * CHUNK
    row = lax.broadcasted_iota(jnp.int32, (n, n), 0)
    col = lax.broadcasted_iota(jnp.int32, (n, n), 1)
    same = (row // CHUNK) == (col // CHUNK)
    lane = lax.broadcasted_iota(jnp.int32, (CHUNK, LANES), 1)
    heads_lane = lax.broadcasted_iota(jnp.int32, (1, LANES), 1)
    q = _l2norm(_stack([_conv_silu(qm[h], qh[h], wq[h]) for h in range(g)])) * (LANES ** -0.5)
    k = _l2norm(_stack([_conv_silu(km[h], kh[h], wk[h]) for h in range(g)]))
    v = _stack([_conv_silu(vm[h], vh[h], wv[h]) for h in range(g)])
    zs = _stack(list(z))
    betas, gs = [], []
    for h in range(g):
        head = head0 + h
        betas.append(jax.nn.sigmoid(jnp.sum(jnp.where(lane == head, tail, 0.0), axis=1, keepdims=True)))
        a_t = jnp.sum(jnp.where(lane == hh + head, tail, 0.0), axis=1, keepdims=True)
        alog = jnp.sum(jnp.where(heads_lane == head, alog_row, 0.0), axis=1, keepdims=True)
        dtb = jnp.sum(jnp.where(heads_lane == head, dtb_row, 0.0), axis=1, keepdims=True)
        gs.append(-jnp.exp(alog) * _softplus(a_t + dtb))
    beta, gcol = _stack(betas), _stack(gs)
    tril = (same & (col <= row)).astype(F32)
    eye = (col == row).astype(F32)
    cum = _sdot(tril, gcol + jnp.zeros((n, n), F32))
    cum_r = cum.T
    diff = cum - cum_r
    strict = same & (col < row)
    incl = same & (col <= row)
    dec_strict = jnp.where(strict, jnp.exp(jnp.where(strict, diff, 0.0)), 0.0)
    dec_incl = jnp.where(incl, jnp.exp(jnp.where(incl, diff, 0.0)), 0.0)
    cum_c = jnp.sum(cum * eye, axis=1, keepdims=True)
    last = same & (col % CHUNK == CHUNK - 1)
    cl_rows = jnp.sum(jnp.where(last, cum_r, 0.0), axis=1, keepdims=True)
    ecum = jnp.exp(cum_c)
    inv = _tri_inv(-(beta * _bdot(k, k, NT) * dec_strict))
    u0 = _bdot(inv, beta * v, NN)
    w = _bdot(inv, (beta * ecum) * k, NN)
    qk = _bdot(q, k, NT) * dec_incl
    u = u0 - _bdot(_place(w, g), s, NN)
    o = _bdot(_place(q * ecum, g), s, NN) + _bdot(qk, u, NN)
    kd = k * jnp.exp(cl_rows - cum_c)
    row1 = lax.broadcasted_iota(jnp.int32, (n, 1), 0)
    decay = []
    for h in range(g):
        cl_h = jnp.sum(jnp.where(row1 == h * CHUNK + CHUNK - 1, cum_c, 0.0), axis=0, keepdims=True)
        decay.append(jnp.exp(cl_h) + jnp.zeros((LANES, 1), F32))
    s_new = _stack(decay) * s + _bdot(_place(kd, g), u, TN)
    return _gated_rmsnorm(o, zs, nw), s_new


def _gdn_in_specs(tb, heads, hp, rev_nb=None):
    groups = heads // hp

    def bb(b):
        return b if rev_nb is None else rev_nb - 1 - b

    def halo(j):
        return pl.BlockSpec((HALO, hp * LANES),
                            lambda b, h, j=j: (jnp.maximum(bb(b) * (tb // HALO) - 1, 0), j * groups + h))

    def main(j):
        return _head_cols(tb, groups, hp, j, rev_nb=rev_nb)

    return [main(0), halo(0), main(1), halo(1), main(2), halo(2), main(3),
            pl.BlockSpec((tb, LANES), lambda b, h: (bb(b), 0)),
            pl.BlockSpec((CONV_K, 3 * heads * LANES), lambda b, h: (0, 0)),
            pl.BlockSpec((8, LANES), lambda b, h: (0, 0)),
            pl.BlockSpec((1, LANES), lambda b, h: (0, 0))]


def _gdn_chunk_args(c, head0, hp, blk, heads, q_ref, qh_ref, k_ref, kh_ref, v_ref, vh_ref, z_ref, tail_ref, cw_ref, prm_ref, nw_ref):
    d = heads * LANES
    lns = [slice(p * LANES, (p + 1) * LANES) for p in range(hp)]
    rows = pl.ds(pl.multiple_of(c * CHUNK, CHUNK), CHUNK)
    prev = pl.ds(pl.multiple_of(jnp.maximum(c * CHUNK - HALO, 0), HALO), HALO)
    first = c == 0
    live = jnp.where(first & (blk == 0), 0.0, 1.0)

    def main_of(ref):
        return tuple(ref[rows, ln] for ln in lns)

    def halo_of(ref, href):
        return tuple(jnp.where(first, href[:, ln], ref[prev, ln]) * live for ln in lns)

    def cw(j):
        out = []
        for p in range(hp):
            lanes = pl.ds(pl.multiple_of(j * d + (head0 + p) * LANES, LANES), LANES)
            out.append(tuple(cw_ref[r:r + 1, lanes] for r in range(CONV_K)))
        return tuple(out)

    return (main_of(q_ref), halo_of(q_ref, qh_ref), main_of(k_ref), halo_of(k_ref, kh_ref),
            main_of(v_ref), halo_of(v_ref, vh_ref), main_of(z_ref), tail_ref[rows, :],
            cw(0), cw(1), cw(2), prm_ref[0:1, :], prm_ref[1:2, :], nw_ref[...])


def gdn_fwd(hmain, tail, conv_w, prm, nw, *, tb, hp):
    t, d4 = hmain.shape
    d = d4 // 4
    heads = d // LANES
    groups = heads // hp
    nb, nc = t // tb, tb // CHUNK

    def body(q_ref, qh_ref, k_ref, kh_ref, v_ref, vh_ref, z_ref, tail_ref, cw_ref, prm_ref, nw_ref,
             y_ref, states_ref, s_ref):
        b, hg = pl.program_id(0), pl.program_id(1)

        @pl.when(b == 0)
        def _():
            s_ref[hg] = jnp.zeros((hp * LANES, LANES), F32)

        def step(c, carry):
            rows = pl.ds(pl.multiple_of(c * CHUNK, CHUNK), CHUNK)
            s = s_ref[hg]
            states_ref[c, 0] = s
            args = _gdn_chunk_args(c, hg * hp, hp, b, heads, q_ref, qh_ref, k_ref, kh_ref, v_ref, vh_ref, z_ref,
                                   tail_ref, cw_ref, prm_ref, nw_ref)
            y, s_new = _gdn_chunk(hg * hp, heads, *args, s)
            for p in range(hp):
                y_ref[rows, p * LANES:(p + 1) * LANES] = y[p * CHUNK:(p + 1) * CHUNK].astype(y_ref.dtype)
            s_ref[hg] = s_new
            return carry

        lax.fori_loop(0, nc, step, 0)

    return pl.pallas_call(
        body, name="gdn_fwd", grid=(nb, groups),
        in_specs=_gdn_in_specs(tb, heads, hp),
        out_specs=[pl.BlockSpec((tb, hp * LANES), lambda b, h: (b, h)),
                   pl.BlockSpec((nc, 1, hp * LANES, LANES), lambda b, h: (b, h, 0, 0))],
        out_shape=[jax.ShapeDtypeStruct((t, d), BF16),
                   jax.ShapeDtypeStruct((t // CHUNK, groups, hp * LANES, LANES), F32)],
        scratch_shapes=[pltpu.VMEM((groups, hp * LANES, LANES), F32)],
        compiler_params=_cparams(("arbitrary", "arbitrary")),
    )(hmain, hmain, hmain, hmain, hmain, hmain, hmain, tail, conv_w, prm, nw)


def gdn_bwd(hmain, tail, conv_w, prm, nw, states, dy, *, tb, hp):
    t, d4 = hmain.shape
    d = d4 // 4
    heads = d // LANES
    groups = heads // hp
    nb, nc = t // tb, tb // CHUNK

    def body(q_ref, qh_ref, k_ref, kh_ref, v_ref, vh_ref, z_ref, tail_ref, cw_ref, prm_ref, nw_ref,
             states_ref, dy_ref, dh_ref, dtail_ref, dcw_ref, dprm_ref, ds_ref, pend_ref):
        b, hg = pl.program_id(0), pl.program_id(1)
        blk = nb - 1 - b

        @pl.when(b == 0)
        def _():
            ds_ref[hg] = jnp.zeros((hp * LANES, LANES), F32)
            for p in range(hp):
                pend_ref[hg * hp + p] = jnp.zeros((3, HALO, LANES), F32)

        @pl.when((b == 0) & (hg == 0))
        def _():
            dcw_ref[...] = jnp.zeros_like(dcw_ref)
            dprm_ref[...] = jnp.zeros_like(dprm_ref)

        @pl.when(hg == 0)
        def _():
            dtail_ref[...] = jnp.zeros_like(dtail_ref)

        def step(i, carry):
            c = nc - 1 - i
            rows = pl.ds(pl.multiple_of(c * CHUNK, CHUNK), CHUNK)
            zpad = jnp.zeros((CHUNK - HALO, LANES), F32)
            args = _gdn_chunk_args(c, hg * hp, hp, blk, heads, q_ref, qh_ref, k_ref, kh_ref, v_ref, vh_ref, z_ref,
                                   tail_ref, cw_ref, prm_ref, nw_ref)
            dy = _stack([dy_ref[rows, p * LANES:(p + 1) * LANES].astype(F32) for p in range(hp)])
            pends = [pend_ref[hg * hp + p] for p in range(hp)]
            _, vjp = jax.vjp(functools.partial(_gdn_chunk, hg * hp, heads), *args, states_ref[c, 0])
            (dqm, dqh, dkm, dkh, dvm, dvh, dz, dtl, dwq, dwk, dwv, dalog, ddtb, dnw, ds) = vjp((dy, ds_ref[hg]))
            for p in range(hp):
                h = hg * hp + p
                for j, (dm, dhalo) in enumerate(((dqm, dqh), (dkm, dkh), (dvm, dvh))):
                    full = dm[p] + jnp.concatenate([zpad, pends[p][j]], axis=0)
                    dh_ref[rows, pl.ds(pl.multiple_of(j * d + h * LANES, LANES), LANES)] = full.astype(dh_ref.dtype)
                    pend_ref[h, j] = dhalo[p]
                dh_ref[rows, pl.ds(pl.multiple_of(3 * d + h * LANES, LANES), LANES)] = dz[p].astype(dh_ref.dtype)
                for j, dw in enumerate((dwq, dwk, dwv)):
                    lanes = pl.ds(pl.multiple_of(j * d + h * LANES, LANES), LANES)
                    for r in range(CONV_K):
                        dcw_ref[r:r + 1, lanes] += dw[p][r]
            dtail_ref[rows, :] += dtl
            dprm_ref[0:1, :] += dalog
            dprm_ref[1:2, :] += ddtb
            dprm_ref[2:3, :] += dnw
            ds_ref[hg] = ds
            return carry

        lax.fori_loop(0, nc, step, 0)

    return pl.pallas_call(
        body, name="gdn_bwd", grid=(nb, groups),
        in_specs=_gdn_in_specs(tb, heads, hp, rev_nb=nb) + [
            pl.BlockSpec((nc, 1, hp * LANES, LANES), lambda b, h: (nb - 1 - b, h, 0, 0)),
            pl.BlockSpec((tb, hp * LANES), lambda b, h: (nb - 1 - b, h))],
        out_specs=[pl.BlockSpec((tb, d4), lambda b, h: (nb - 1 - b, 0)),
                   pl.BlockSpec((tb, LANES), lambda b, h: (nb - 1 - b, 0)),
                   pl.BlockSpec((CONV_K, 3 * d), lambda b, h: (0, 0)),
                   pl.BlockSpec((8, LANES), lambda b, h: (0, 0))],
        out_shape=[jax.ShapeDtypeStruct((t, d4), BF16),
                   jax.ShapeDtypeStruct((t, LANES), F32),
                   jax.ShapeDtypeStruct((CONV_K, 3 * d), F32),
                   jax.ShapeDtypeStruct((8, LANES), F32)],
        scratch_shapes=[pltpu.VMEM((groups, hp * LANES, LANES), F32),
                        pltpu.VMEM((heads, 3, HALO, LANES), F32)],
        compiler_params=_cparams(("arbitrary", "arbitrary")),
    )(hmain, hmain, hmain, hmain, hmain, hmain, hmain, tail, conv_w, prm, nw, states, dy)


def _layer_norm(u, g, b):
    mu = jnp.mean(u, axis=1, keepdims=True)
    var = jnp.mean(jnp.square(u - mu), axis=1, keepdims=True)
    return (u - mu) * lax.rsqrt(var + LN_EPS) * g + b


def ln_fwd(u, g, b, *, tr):
    t, d = u.shape

    def body(u_ref, g_ref, b_ref, x_ref, xb_ref):
        x = _layer_norm(u_ref[...], g_ref[...], b_ref[...])
        x_ref[...] = x
        xb_ref[...] = x.astype(BF16)

    row = pl.BlockSpec((tr, d), lambda i: (i, 0))
    vec = pl.BlockSpec((1, d), lambda i: (0, 0))
    return pl.pallas_call(
        body, name="ln_fwd", grid=(t // tr,), in_specs=[row, vec, vec], out_specs=[row, row],
        out_shape=[jax.ShapeDtypeStruct((t, d), F32), jax.ShapeDtypeStruct((t, d), BF16)],
        compiler_params=_cparams(("parallel",)),
    )(u, g, b)


def ln_bwd(u, g, b, dout, *, tr):
    t, d = u.shape

    def body(u_ref, g_ref, b_ref, dout_ref, du_ref, dub_ref, dg_ref, db_ref):
        @pl.when(pl.program_id(0) == 0)
        def _():
            dg_ref[...] = jnp.zeros_like(dg_ref)
            db_ref[...] = jnp.zeros_like(db_ref)

        _, vjp = jax.vjp(_layer_norm, u_ref[...], g_ref[...], b_ref[...])
        du, dg, db = vjp(dout_ref[...])
        du_ref[...] = du
        dub_ref[...] = du.astype(BF16)
        dg_ref[0:1, :] += dg
        db_ref[0:1, :] += db

    row = pl.BlockSpec((tr, d), lambda i: (i, 0))
    vec = pl.BlockSpec((1, d), lambda i: (0, 0))
    acc = pl.BlockSpec((8, d), lambda i: (0, 0))
    return pl.pallas_call(
        body, name="ln_bwd", grid=(t // tr,), in_specs=[row, vec, vec, row], out_specs=[row, row, acc, acc],
        out_shape=[jax.ShapeDtypeStruct((t, d), F32), jax.ShapeDtypeStruct((t, d), BF16),
                   jax.ShapeDtypeStruct((8, d), F32), jax.ShapeDtypeStruct((8, d), F32)],
        compiler_params=_cparams(("arbitrary",)),
    )(u, g, b, dout)


def ln_loss_bwd(u, g, b, target, *, tr):
    t, d = u.shape

    def loss_of(uu, gg, bb, tgt):
        err = jnp.square(_layer_norm(uu, gg, bb) - tgt)
        return 0.5 * jnp.sum(jnp.mean(err, axis=1, keepdims=True), axis=0, keepdims=True)

    def body(u_ref, g_ref, b_ref, t_ref, loss_ref, du_ref, dub_ref, dg_ref, db_ref):
        @pl.when(pl.program_id(0) == 0)
        def _():
            loss_ref[...] = jnp.zeros_like(loss_ref)
            dg_ref[...] = jnp.zeros_like(dg_ref)
            db_ref[...] = jnp.zeros_like(db_ref)

        tgt = t_ref[...]
        val, vjp = jax.vjp(lambda uu, gg, bb: loss_of(uu, gg, bb, tgt), u_ref[...], g_ref[...], b_ref[...])
        du, dg, db = vjp(jnp.ones((1, 1), F32))
        loss_ref[...] += val
        du_ref[...] = du
        dub_ref[...] = du.astype(BF16)
        dg_ref[0:1, :] += dg
        db_ref[0:1, :] += db

    row = pl.BlockSpec((tr, d), lambda i: (i, 0))
    vec = pl.BlockSpec((1, d), lambda i: (0, 0))
    acc = pl.BlockSpec((8, d), lambda i: (0, 0))
    return pl.pallas_call(
        body, name="ln_loss_bwd", grid=(t // tr,), in_specs=[row, vec, vec, row],
        out_specs=[pl.BlockSpec((8, LANES), lambda i: (0, 0)), row, row, acc, acc],
        out_shape=[jax.ShapeDtypeStruct((8, LANES), F32),
                   jax.ShapeDtypeStruct((t, d), F32), jax.ShapeDtypeStruct((t, d), BF16),
                   jax.ShapeDtypeStruct((8, d), F32), jax.ShapeDtypeStruct((8, d), F32)],
        compiler_params=_cparams(("arbitrary",)),
    )(u, g, b, target)


def local_step(x, target, wa_in, alb, a_nw, wa_out, wb_t, wb_tail_t, conv_w, a_log, dt_bias, b_nw, wb_out,
               ln_g, ln_b, *, tb=256, tr=256, hp=HEADS_PER_STEP, first_after=None, b_weights=None, on_b_grads=None,
               on_a_out_grad=None, on_a_grads=None, on_mid=None, on_small_grads=None):
    t, d = x.shape
    heads = d // LANES
    tb, tr, hp = min(tb, t), min(tr, t), min(hp, heads)
    xb = x.astype(BF16)
    prm = jnp.zeros((8, LANES), F32).at[0, :heads].set(a_log[0]).at[1, :heads].set(dt_bias[0])

    ha = matmul(xb, wa_in, name="mm_a_in", after=first_after)
    ya, st_a = hgrn2_fwd(ha, alb, a_nw, tb=tb, hp=hp)
    u1 = matmul(ya, wa_out, name="mm_a_out", addend=x, alpha=DEEPNORM_ALPHA,
                after=None if on_mid is None else on_mid(ya))
    x1, x1b = ln_fwd(u1, ln_g[0:1], ln_b[0:1], tr=tr)
    if b_weights is not None:
        wb_t, wb_tail_t, wb_out, conv_w = b_weights(x1b)
    n_tail = 2 * heads
    hb = matmul(x1b, wb_t, name="mm_b_in", nt=True, n=4 * d)
    tl = matmul(x1b, wb_tail_t, name="mm_b_tail", nt=True)
    yb, st_b = gdn_fwd(hb, tl, conv_w, prm, b_nw, tb=tb, hp=hp)
    u2 = matmul(yb, wb_out, name="mm_b_out", addend=x1, alpha=DEEPNORM_ALPHA)

    loss, du2, du2b, dg2, db2 = ln_loss_bwd(u2, ln_g[1:2], ln_b[1:2], target, tr=tr)
    d_wb_out = matmul(yb, du2b, name="mm_dwb_out", lhs_t=True, out_dtype=BF16)
    dyb = matmul(du2b, wb_out, name="mm_dyb", nt=True)
    dhb, dtl, d_conv, dprm = gdn_bwd(hb, tl, conv_w, prm, b_nw, st_b, dyb, tb=tb, hp=hp)
    dtlb = dtl.astype(BF16)
    d_wb_t = matmul(dhb, x1b, name="mm_dwb_main", lhs_t=True, out_dtype=BF16, out_rows=4 * d + n_tail)
    d_wb_t = matmul(dtlb[:, :n_tail], x1b, name="mm_dwb_tail", lhs_t=True, out_dtype=BF16, into=(d_wb_t, 4 * d))
    sent_b = on_b_grads(d_wb_t, d_wb_out) if on_b_grads is not None else None
    dx1_tail = matmul(dtlb, wb_tail_t, name="mm_dx1_tail", addend=du2, alpha=DEEPNORM_ALPHA, after=sent_b)
    dx1 = matmul(dhb, wb_t, name="mm_dx1", addend=dx1_tail, alpha=1.0, tk=2048)
    du1, du1b, dg1, db1 = ln_bwd(u1, ln_g[0:1], ln_b[0:1], dx1, tr=tr)
    d_wa_out = matmul(ya, du1b, name="mm_dwa_out", lhs_t=True, out_dtype=BF16)
    sent_c = on_a_out_grad(d_wa_out) if on_a_out_grad is not None else None
    dya = matmul(du1b, wa_out, name="mm_dya", nt=True, after=sent_c)
    dha, d_alb, d_anw = hgrn2_bwd(ha, alb, a_nw, st_a, dya, tb=tb, hp=hp)
    small = dict(
        a_lower_bounds=d_alb, a_norm_w=d_anw[0:1], b_conv_w=d_conv,
        b_a_log=dprm[0:1, :heads], b_dt_bias=dprm[1:2, :heads], b_norm_w=dprm[2:3],
        ln_g=jnp.concatenate([dg1[0:1], dg2[0:1]], axis=0), ln_b=jnp.concatenate([db1[0:1], db2[0:1]], axis=0))
    d_wa_in = matmul(xb, dha, name="mm_dwa_in", lhs_t=True, out_dtype=BF16,
                     out_split=wa_in.shape[0] if wa_in.ndim == 3 else 1,
                     after=None if on_small_grads is None else on_small_grads(small))
    sent_a = on_a_grads(d_wa_in) if on_a_grads is not None else None
    grad_x = matmul(dha, wa_in, name="mm_dx", nt=True, addend=du1, alpha=DEEPNORM_ALPHA, after=sent_a, tk=2048)
    big = dict(a_w_in=d_wa_in, a_w_out=d_wa_out, b_w_t=d_wb_t, b_w_out=d_wb_out)
    return loss, grad_x, big, small


MESH_ID = pl.DeviceIdType.MESH


def _position():
    return lax.axis_index("x"), lax.axis_index("y"), lax.axis_index("c")


def _index_of(p):
    return 4 * p[0] + 2 * p[1] + p[2]


def all_gather(shards, *, name, space):
    n = len(shards)

    def body(*refs):
        ins, outs = refs[:n], refs[n:2 * n]
        send_sems, recv_sems, local_sems = refs[2 * n:]
        x, y, c = _position()
        me, sibling = (x, y, c), (x, y, 1 - c)
        chips = [(1 - x, y), (x, 1 - y), (1 - x, 1 - y)]

        def copy(a, k, block, to, own=False):
            dst = outs[a].at[_index_of(block)]
            return pltpu.make_async_remote_copy(
                src_ref=ins[a] if own else dst, dst_ref=dst,
                send_sem=send_sems.at[7 * a + k], recv_sem=recv_sems.at[7 * a + k],
                device_id=to, device_id_type=MESH_ID)

        mine = [pltpu.make_async_copy(ins[a], outs[a].at[_index_of(me)], local_sems.at[a]) for a in range(n)]
        for cp in mine:
            cp.start()
        first = []
        for a in range(n):
            first.append(copy(a, 0, me, sibling, own=True))
            first += [copy(a, 1 + j, me, (*chip, c), own=True) for j, chip in enumerate(chips)]
        for cp in first:
            cp.start()
        passed = []
        for j, chip in enumerate(chips):
            for a in range(n):
                copy(a, 1 + j, (*chip, c), me).wait_recv()
                fwd = copy(a, 4 + j, (*chip, c), sibling)
                fwd.start()
                passed.append(fwd)
        for a in range(n):
            copy(a, 0, sibling, me).wait_recv()
            for j, chip in enumerate(chips):
                copy(a, 4 + j, (*chip, 1 - c), me).wait_recv()
        for cp in first + passed:
            cp.wait_send()
        for cp in mine:
            cp.wait()

    spec = pl.BlockSpec(memory_space=space)
    return pl.pallas_call(
        body, name=name,
        in_specs=[spec] * n, out_specs=[spec] * n,
        out_shape=[jax.ShapeDtypeStruct((N_DEV, *s.shape), s.dtype) for s in shards],
        scratch_shapes=[pltpu.SemaphoreType.DMA((7 * n,)), pltpu.SemaphoreType.DMA((7 * n,)),
                        pltpu.SemaphoreType.DMA((n,))],
        compiler_params=pltpu.CompilerParams(vmem_limit_bytes=VMEM_LIMIT),
    )(*shards)


HBM_SPEC = pl.BlockSpec(memory_space=pltpu.HBM)
SEM_SPEC = pl.BlockSpec(memory_space=pltpu.SEMAPHORE)


CHIP_PEERS = (4, 2, 6)
COPIES_PER_ARRAY = dict(gather=7, exchange=7, gather_near=4, gather_pass=3, reduce_pair=4, reduce_chip=3)


def _planned_copies(kind, srcs, lands, send_sems, recv_sems):
    x, y, c = _position()
    me, my_chip = _index_of((x, y, c)), 2 * x + y
    per = COPIES_PER_ARRAY[kind]
    copies = []

    def peer_of(r):
        return (x ^ ((r >> 2) & 1), y ^ ((r >> 1) & 1), c ^ (r & 1))

    def add(a, j, src, dst, peer):
        copies.append(pltpu.make_async_remote_copy(
            src_ref=src, dst_ref=dst, send_sem=send_sems.at[per * a + j], recv_sem=recv_sems.at[per * a + j],
            device_id=peer, device_id_type=MESH_ID))

    for a, land in enumerate(lands):
        src = srcs[a] if srcs else land
        if kind == "gather":
            for r in range(1, N_DEV):
                add(a, r - 1, src, land.at[me], peer_of(r))
        elif kind == "exchange":
            for r in range(1, N_DEV):
                add(a, r - 1, src.at[_index_of(peer_of(r))], land.at[me], peer_of(r))
        elif kind == "gather_near":
            for j, r in enumerate((1,) + CHIP_PEERS):
                add(a, j, src, land.at[me], peer_of(r))
        elif kind == "gather_pass":
            for j, r in enumerate(CHIP_PEERS):
                slot = _index_of(peer_of(r))
                add(a, j, land.at[slot], land.at[slot], peer_of(1))
        elif kind == "reduce_pair":
            for q in range(4):
                add(a, q, src.at[2 * q + (1 - c)], land.at[q], peer_of(1))
        elif kind == "reduce_chip":
            for j, r in enumerate(CHIP_PEERS):
                p = peer_of(r)
                add(a, j, src.at[2 * p[0] + p[1]], land.at[my_chip], p)
    return copies


def copies_start(kind, srcs, lands, *, name):
    ns, n = len(srcs), len(srcs) + len(lands)
    n_sem = COPIES_PER_ARRAY[kind] * len(lands)

    def body(*refs):
        for cp in _planned_copies(kind, refs[:ns], refs[ns:n], refs[n], refs[n + 1]):
            cp.start()
        refs[-1][...] = jnp.zeros_like(refs[-1])

    arrays = [pltpu.with_memory_space_constraint(t, pltpu.HBM) for t in (*srcs, *lands)]
    outs = pl.pallas_call(
        body, name=name,
        out_shape=(pltpu.SemaphoreType.DMA((n_sem,)), pltpu.SemaphoreType.DMA((n_sem,)),
                   *[pltpu.HBM(t.shape, t.dtype) for t in arrays], jax.ShapeDtypeStruct((8, LANES), F32)),
        in_specs=[HBM_SPEC] * n,
        out_specs=(SEM_SPEC, SEM_SPEC, *[HBM_SPEC] * n, pl.BlockSpec(memory_space=pltpu.VMEM)),
        input_output_aliases={i: 2 + i for i in range(n)},
        compiler_params=pltpu.CompilerParams(has_side_effects=pltpu.SideEffectType.DATAFLOW_SIDE_EFFECTING),
    )(*arrays)
    return (outs[0], outs[1]), list(outs[2:2 + ns]), list(outs[2 + ns:2 + n]), outs[-1]


def copies_wait(kind, sems, srcs, lands, after, *, name):
    ns, n = len(srcs), len(srcs) + len(lands)

    def body(*refs):
        for cp in _planned_copies(kind, refs[:ns], refs[ns:n], refs[n], refs[n + 1]):
            cp.wait_send()
            cp.wait_recv()

    outs = pl.pallas_call(
        body, name=name,
        out_shape=tuple(pltpu.HBM(t.shape, t.dtype) for t in (*srcs, *lands)),
        in_specs=[HBM_SPEC] * n + [SEM_SPEC, SEM_SPEC, pl.BlockSpec(memory_space=pl.ANY)],
        out_specs=tuple([HBM_SPEC] * n),
        input_output_aliases={i: i for i in range(n)},
        compiler_params=pltpu.CompilerParams(has_side_effects=pltpu.SideEffectType.DATAFLOW_SIDE_EFFECTING),
    )(*srcs, *lands, sems[0], sems[1], after)
    return list(outs[ns:])


def _landing(own, slot, slots=N_DEV):
    return lax.dynamic_update_slice_in_dim(lax.empty((slots, *own.shape), own.dtype), own[None], slot, 0)


def pair_sum(a, b, *, name, tr=256):
    p, r, c = a.shape
    tr = min(tr, r)

    def body(a_ref, b_ref, o_ref):
        o_ref[...] = (a_ref[...].astype(F32) + b_ref[...].astype(F32)).astype(o_ref.dtype)

    blk = pl.BlockSpec((1, tr, c), lambda i, j: (i, j, 0))
    return pl.pallas_call(
        body, name=name, grid=(p, r // tr), in_specs=[blk, blk], out_specs=blk,
        out_shape=jax.ShapeDtypeStruct(a.shape, a.dtype), compiler_params=_cparams(("parallel", "parallel")),
    )(a, b)


def adamw(parts, w, m, v, *, name, tr=64, tc=None):
    p, r, c = parts.shape
    tr = r if tc is not None else min(tr, r)
    tc = c if tc is None else tc
    assert r % tr == 0 and c % tc == 0
    c1 = 1.0 / (1.0 - ADAM_B1 ** ADAM_STEP)
    c2 = 1.0 / (1.0 - ADAM_B2 ** ADAM_STEP)

    def body(p_ref, w_ref, m_ref, v_ref, g_ref, d_ref, nm_ref, nv_ref):
        g = p_ref[0].astype(F32)
        for i in range(1, p):
            g = g + p_ref[i].astype(F32)
        nm = ADAM_B1 * m_ref[...] + (1.0 - ADAM_B1) * g
        nv = ADAM_B2 * v_ref[...] + (1.0 - ADAM_B2) * jnp.square(g)
        g_ref[...] = g
        nm_ref[...] = nm
        nv_ref[...] = nv
        d_ref[...] = -ADAM_LR * ((nm * c1) / (jnp.sqrt(nv * c2) + ADAM_EPS) + ADAM_WD * w_ref[...])

    blk = pl.BlockSpec((tr, tc), lambda i, j: (i, j))
    out = jax.ShapeDtypeStruct((r, c), F32)
    return pl.pallas_call(
        body, name=name, grid=(r // tr, c // tc),
        in_specs=[pl.BlockSpec((p, tr, tc), lambda i, j: (0, i, j)), blk, blk, blk],
        out_specs=[blk] * 4, out_shape=[out] * 4,
        compiler_params=_cparams(("parallel", "parallel")),
    )(parts, w, m, v)


def _pack(d, vals):
    heads = d // LANES
    vecs = jnp.zeros((8, LANES), F32)
    vecs = vecs.at[0:1].set(vals["a_norm_w"]).at[1:2, :heads].set(vals["b_a_log"])
    vecs = vecs.at[2:3, :heads].set(vals["b_dt_bias"]).at[3:4].set(vals["b_norm_w"])
    rows = [vals["a_lower_bounds"].reshape(-1, LANES), vals["ln_g"].reshape(-1, LANES),
            vals["ln_b"].reshape(-1, LANES), vecs]
    return jnp.concatenate(rows, axis=0)


def _unpack(d, packed):
    heads = d // LANES
    n3, n2 = 3 * heads, 2 * heads
    o = 0
    out = {}
    out["a_lower_bounds"] = packed[o:o + n3].reshape(3, d); o += n3
    out["ln_g"] = packed[o:o + n2].reshape(2, d); o += n2
    out["ln_b"] = packed[o:o + n2].reshape(2, d); o += n2
    out["a_norm_w"] = packed[o:o + 1]
    out["b_a_log"] = packed[o + 1:o + 2, :heads]
    out["b_dt_bias"] = packed[o + 2:o + 3, :heads]
    out["b_norm_w"] = packed[o + 3:o + 4]
    return out


ORDER = ("a_w_in", "a_lower_bounds", "a_norm_w", "a_w_out", "b_w_in", "b_conv_w", "b_a_log", "b_dt_bias", "b_norm_w",
         "b_w_out", "ln_g", "ln_b")


def kernel(x, a_w_in, a_lower_bounds, a_norm_w, a_w_out, b_w_in, b_conv_w, b_a_log, b_dt_bias, b_norm_w, b_w_out, ln_g, ln_b, loss_target, m_a_w_in, m_a_lower_bounds, m_a_norm_w, m_a_w_out, m_b_w_in, m_b_conv_w, m_b_a_log, m_b_dt_bias, m_b_norm_w, m_b_w_out, m_ln_g, m_ln_b, v_a_w_in, v_a_lower_bounds, v_a_norm_w, v_a_w_out, v_b_w_in, v_b_conv_w, v_b_a_log, v_b_dt_bias, v_b_norm_w, v_b_w_out, v_ln_g, v_ln_b):
    w = dict(a_w_in=a_w_in, a_lower_bounds=a_lower_bounds, a_norm_w=a_norm_w, a_w_out=a_w_out, b_w_in=b_w_in,
             b_conv_w=b_conv_w, b_a_log=b_a_log, b_dt_bias=b_dt_bias, b_norm_w=b_norm_w, b_w_out=b_w_out, ln_g=ln_g, ln_b=ln_b)
    m = dict(a_w_in=m_a_w_in, a_lower_bounds=m_a_lower_bounds, a_norm_w=m_a_norm_w, a_w_out=m_a_w_out, b_w_in=m_b_w_in,
             b_conv_w=m_b_conv_w, b_a_log=m_b_a_log, b_dt_bias=m_b_dt_bias, b_norm_w=m_b_norm_w, b_w_out=m_b_w_out,
             ln_g=m_ln_g, ln_b=m_ln_b)
    v = dict(a_w_in=v_a_w_in, a_lower_bounds=v_a_lower_bounds, a_norm_w=v_a_norm_w, a_w_out=v_a_w_out, b_w_in=v_b_w_in,
             b_conv_w=v_b_conv_w, b_a_log=v_b_a_log, b_dt_bias=v_b_dt_bias, b_norm_w=v_b_norm_w, b_w_out=v_b_w_out,
             ln_g=v_ln_g, ln_b=v_ln_b)
    t, d = x.shape[1], x.shape[2]
    heads = d // LANES
    n_tail = 2 * heads
    me = _index_of(_position())

    ga_in, ga_out, g_conv = all_gather([a_w_in[0].astype(BF16), a_w_out[0].astype(BF16), b_conv_w[0]],
                                       name="gather_weights_a", space=pltpu.HBM)
    b_shards = [jnp.transpose(b_w_in[0]).astype(BF16), b_w_out[0].astype(BF16)]
    flight = {}
    sems_w, b_shards, lands_w, token_w = copies_start("gather_near", b_shards, [_landing(s, me) for s in b_shards],
                                                      name="gather_weights_b_start")
    wa_out = ga_out.reshape(d, d)
    conv_w = jnp.transpose(g_conv, (1, 0, 2)).reshape(CONV_K, 3 * d)

    def on_mid(after):
        lands = copies_wait("gather_near", sems_w, b_shards, lands_w, after, name="gather_weights_b_wait")
        sems, _, lands, token = copies_start("gather_pass", [], lands, name="pass_weights_b_start")
        flight["w"] = (sems, lands)
        return token

    def b_weights(after):
        sems, lands = flight["w"]
        gb_in, gb_out = copies_wait("gather_pass", sems, [], lands, after, name="pass_weights_b_wait")
        wb_t = gb_in.reshape(4 * d + n_tail, d)
        wb_tail_t = jnp.concatenate([wb_t[4 * d:], jnp.zeros((LANES - n_tail, d), BF16)], axis=0)
        return wb_t, wb_tail_t, gb_out.reshape(d, d), conv_w

    def send_grads(key, parts):
        own = [lax.dynamic_index_in_dim(p, me, 0, keepdims=False) for p in parts]
        sems, parts, lands, token = copies_start("exchange", parts, [_landing(o, me) for o in own],
                                                 name="exchange_" + key + "_start")
        flight[key] = (sems, parts, lands)
        return token

    def on_b_grads(d_wb_t, d_out):
        return send_grads("b", [d_wb_t.reshape(N_DEV, -1, d), d_out.reshape(N_DEV, d // N_DEV, d)])

    res = {}

    def on_small_grads(small):
        conv_rows = small["b_conv_w"].reshape(-1, LANES)
        n_conv = conv_rows.shape[0]
        packed_grads = jnp.concatenate([conv_rows, _pack(d, small)], axis=0)
        (got,) = all_gather([packed_grads], name="gather_small", space=pltpu.VMEM)
        packed = adamw(got[:, n_conv:], _pack(d, w), _pack(d, m), _pack(d, v), name="adamw_small", tr=4096)
        for k, vals in zip(("grad", "delta", "new_m", "new_v"), packed):
            res[k] = _unpack(d, vals)
        shard_ch = 3 * d // N_DEV
        conv_parts = lax.dynamic_slice_in_dim(got[:, :n_conv].reshape(N_DEV, CONV_K, 3 * d), me * shard_ch, shard_ch,
                                              axis=2)
        conv_out = adamw(conv_parts, b_conv_w[0], m_b_conv_w[0], v_b_conv_w[0], name="adamw_conv")
        for k, vals in zip(("grad", "delta", "new_m", "new_v"), conv_out):
            res[k]["b_conv_w"] = vals[None]
        return conv_out[0]

    my_chip, my_core = me // 2, me % 2

    def on_a_grads(d_in):
        mine = lax.dynamic_index_in_dim(d_in.reshape(N_DEV // 2, 2, *d_in.shape[1:]), my_core, 1, keepdims=False)
        land1 = lax.empty(mine.shape, mine.dtype)
        sems, (d_in,), (land1,), token = copies_start("reduce_pair", [d_in], [land1], name="reduce_a_pair_start")
        (land1,) = copies_wait("reduce_pair", sems, [d_in], [land1], token, name="reduce_a_pair_wait")
        summed = pair_sum(mine, land1, name="reduce_a_pair_sum")
        own = lax.dynamic_index_in_dim(summed, my_chip, 0, keepdims=False)
        sems, (summed,), (land2,), token = copies_start("reduce_chip", [summed], [_landing(own, my_chip, N_DEV // 2)],
                                                        name="reduce_a_chip_start")
        flight["a"] = (sems, [summed], [land2])
        return token

    loss, grad_x, big, small = local_step(
        x[0], loss_target[0], ga_in, a_lower_bounds, a_norm_w, wa_out, None, None, None, b_a_log, b_dt_bias,
        b_norm_w, None, ln_g, ln_b, first_after=token_w, on_mid=on_mid, b_weights=b_weights, on_b_grads=on_b_grads,
        on_a_out_grad=lambda g: send_grads("c", [g.reshape(N_DEV, d // N_DEV, d)]),
        on_small_grads=on_small_grads, on_a_grads=on_a_grads)
    loss = lax.psum(loss[0, 0], ("x", "y", "c"))

    after = grad_x
    for key, kind, names in (("b", "exchange", ("b_w_in", "b_w_out")), ("c", "exchange", ("a_w_out",)),
                             ("a", "reduce_chip", ("a_w_in",))):
        sems, parts, lands = flight[key]
        recv = copies_wait(kind, sems, parts, lands, after, name="exchange_" + key + "_wait")
        for name, got_parts in zip(names, recv):
            if name == "b_w_in":
                raw = adamw(got_parts, *[jnp.transpose(t[name][0]) for t in (w, m, v)], name="adamw_" + name, tc=256)
                outs = [jnp.transpose(o) for o in raw]
            else:
                raw = outs = adamw(got_parts, w[name][0], m[name][0], v[name][0], name="adamw_" + name)
            for k, vals in zip(("grad", "delta", "new_m", "new_v"), outs):
                res[k][name] = vals[None]
            after = raw[0]

    return (loss, grad_x[None], *[res["grad"][k] for k in ORDER], *[res["delta"][k] for k in ORDER],
            *[res["new_m"][k] for k in ORDER], *[res["new_v"][k] for k in ORDER])
```

```python
import functools
import math

import jax
import jax.numpy as jnp
from jax import lax
from jax.experimental import pallas as pl
from jax.experimental.pallas import tpu as pltpu

F32 = jnp.float32
BF16 = jnp.bfloat16

N_DEV = 8
LANES = 128
CHUNK = 64
SUB = 16
HALO = 8
CONV_K = 4
DEPTH = 2
DEEPNORM_ALPHA = (2.0 * DEPTH) ** 0.25
LN_EPS = 1e-5
RMS_EPS = 1e-6
L2_EPS = 1e-6
EXP_CLAMP = 60.0
ADAM_LR = 0.001
ADAM_B1 = 0.9
ADAM_B2 = 0.999
ADAM_EPS = 1e-08
ADAM_WD = 0.01
ADAM_STEP = 10
VMEM_LIMIT = 56 * 1024 * 1024
HEADS_PER_STEP = 4

NN = ((1,), (0,))
NT = ((1,), (1,))
TN = ((0,), (0,))


def _dg(a, b, dims, precision=None):
    return lax.dot_general(a, b, (dims, ((), ())), precision=precision, preferred_element_type=F32)


def _bdot(a, b, dims):
    return _dg(a.astype(BF16), b.astype(BF16), dims)


def _split3(t):
    hi = t.astype(BF16)
    r = t - hi.astype(F32)
    mid = r.astype(BF16)
    return hi, mid, (r - mid.astype(F32)).astype(BF16)


@jax.custom_vjp
def _sdot(sel, t):
    sel = sel.astype(BF16)
    hi, mid, lo = _split3(t)
    return (_dg(sel, lo, NN) + _dg(sel, mid, NN)) + _dg(sel, hi, NN)


def _sdot_fwd(sel, t):
    return _sdot(sel, t), sel


def _sdot_bwd(sel, ct):
    selb = sel.astype(BF16)
    hi, mid, lo = _split3(ct)
    return jnp.zeros_like(sel), (_dg(selb, lo, TN) + _dg(selb, mid, TN)) + _dg(selb, hi, TN)


_sdot.defvjp(_sdot_fwd, _sdot_bwd)


@jax.custom_vjp
def _tri_inv(m):
    n = m.shape[0]
    row = lax.broadcasted_iota(jnp.int32, (n, n), 0)
    col = lax.broadcasted_iota(jnp.int32, (n, n), 1)
    inv = (col == row).astype(F32) + m
    mp = m
    for _ in range(int(math.log2(CHUNK)) - 1):
        mp = _bdot(mp, mp, NN)
        inv = inv + _bdot(inv, mp, NN)
    return inv


def _tri_inv_fwd(m):
    inv = _tri_inv(m)
    return inv, inv


def _tri_inv_bwd(inv, ct):
    return (_bdot(_bdot(inv, ct, TN), inv, NT),)


_tri_inv.defvjp(_tri_inv_fwd, _tri_inv_bwd)


@jax.custom_vjp
def _tri_inv_known(m, inv):
    return inv


_tri_inv_known.defvjp(lambda m, inv: (inv, inv), lambda inv, ct: (*_tri_inv_bwd(inv, ct), jnp.zeros_like(inv)))


@jax.custom_vjp
def _bdot_cols2(a, b1, b2):
    return _bdot_cols2_fwd(a, b1, b2)[0]


def _bdot_cols2_fwd(a, b1, b2):
    both = jnp.concatenate([b1, b2], axis=1)
    r = _bdot(a, both, NN)
    return (r[:, :b1.shape[1]], r[:, b1.shape[1]:]), (a, both)


def _bdot_cols2_bwd(res, cts):
    a, both = res
    ct = jnp.concatenate(cts, axis=1)
    d_both = _bdot(a, ct, TN)
    half = both.shape[1] // 2
    return _bdot(ct, both, NT), d_both[:, :half], d_both[:, half:]


_bdot_cols2.defvjp(_bdot_cols2_fwd, _bdot_cols2_bwd)


@functools.partial(jax.custom_vjp, nondiff_argnums=(2,))
def _shift_rows(xm, xh, back):
    r = pltpu.roll(xm, back, 0)
    row = lax.broadcasted_iota(jnp.int32, xh.shape, 0)
    top = jnp.where(row < back, pltpu.roll(xh, back, 0), r[0:HALO])
    return jnp.concatenate([top, r[HALO:]], axis=0)


def _shift_rows_fwd(xm, xh, back):
    return _shift_rows(xm, xh, back), None


def _shift_rows_bwd(back, _, ct):
    row = lax.broadcasted_iota(jnp.int32, ct.shape, 0)
    dxm = jnp.where(row < CHUNK - back, pltpu.roll(ct, CHUNK - back, 0), 0.0)
    rowh = lax.broadcasted_iota(jnp.int32, (HALO, ct.shape[1]), 0)
    dxh = jnp.where(rowh >= HALO - back, pltpu.roll(ct[0:HALO], HALO - back, 0), 0.0)
    return dxm, dxh


_shift_rows.defvjp(_shift_rows_fwd, _shift_rows_bwd)


def _silu(t):
    return t * jax.nn.sigmoid(t)


def _softplus(t):
    return jnp.where(t > 20.0, t, jnp.log1p(jnp.exp(jnp.minimum(t, 20.0))))


def _cparams(sem=None):
    kw = dict(vmem_limit_bytes=VMEM_LIMIT)
    if sem is not None:
        kw["dimension_semantics"] = sem
    return pltpu.CompilerParams(**kw)


def matmul(a, b, *, name, nt=False, lhs_t=False, n=None, addend=None, alpha=1.0, out_dtype=F32, out_split=1,
           out_rows=None, into=None, after=None, tm=1024, tn=1024, tk=4096):
    k, m = a.shape[::-1] if not lhs_t else a.shape
    b_split = b.shape[0] if b.ndim == 3 else 1
    b_rows, b_cols = b.shape[-2], b.shape[-1] * b_split
    n = (b_rows if nt else b_cols) if n is None else n
    tm, tn, tk = min(tm, m), min(tn, n), min(tk, k)
    if b_split > 1:
        part = b_cols // b_split
        tn, tk = (tn, min(tk, part)) if nt else (min(tn, part), tk)
    if out_split > 1:
        tn = min(tn, n // out_split)
    assert m % tm == 0 and n % tn == 0 and k % tk == 0, (a.shape, b.shape, nt)
    assert not (lhs_t and nt)
    nk = k // tk
    dims = TN if lhs_t else (NT if nt else NN)

    def body(*refs):
        a_ref, b_ref = refs[:2]
        add_ref = None if addend is None else refs[2]

        def finish(r, o_ref):
            if add_ref is not None:
                r = r + alpha * add_ref[...].astype(F32)
            o_ref[...] = r.astype(o_ref.dtype)

        if nk == 1:
            finish(_dg(a_ref[...], b_ref[...], dims), refs[-1])
            return
        o_ref, acc_ref = refs[-2:]
        kk = pl.program_id(2)

        @pl.when(kk == 0)
        def _():
            acc_ref[...] = _dg(a_ref[...], b_ref[...], dims)

        @pl.when(kk > 0)
        def _():
            acc_ref[...] += _dg(a_ref[...], b_ref[...], dims)

        @pl.when(kk == nk - 1)
        def _():
            finish(acc_ref[...], o_ref)

    if b_split == 1:
        b_spec = (pl.BlockSpec((tn, tk), lambda i, j, kk: (j, kk)) if nt
                  else pl.BlockSpec((tk, tn), lambda i, j, kk: (kk, j)))
    elif nt:
        per = (b_cols // b_split) // tk
        b_spec = pl.BlockSpec((None, tn, tk), lambda i, j, kk: (kk // per, j, kk % per))
    else:
        per = (b_cols // b_split) // tn
        b_spec = pl.BlockSpec((None, tk, tn), lambda i, j, kk: (j // per, kk, j % per))
    a_spec = (pl.BlockSpec((tk, tm), lambda i, j, kk: (kk, i)) if lhs_t
              else pl.BlockSpec((tm, tk), lambda i, j, kk: (i, kk)))
    in_specs = [a_spec, b_spec]
    args = [a, b]
    if addend is not None:
        in_specs.append(pl.BlockSpec((tm, tn), lambda i, j, kk: (i, j)))
        args.append(addend)
    for extra in (after, None if into is None else into[0]):
        if extra is not None:
            in_specs.append(pl.BlockSpec(memory_space=pl.ANY))
            args.append(extra)
    aliases = {}
    if into is not None:
        assert out_split == 1 and into[1] % tm == 0 and into[0].dtype == out_dtype
        row0 = into[1] // tm
        out_spec = pl.BlockSpec((tm, tn), lambda i, j, kk: (row0 + i, j))
        out_shape = jax.ShapeDtypeStruct(into[0].shape, out_dtype)
        aliases = {len(args) - 1: 0}
    elif out_split == 1:
        out_spec = pl.BlockSpec((tm, tn), lambda i, j, kk: (i, j))
        out_shape = jax.ShapeDtypeStruct((m if out_rows is None else out_rows, n), out_dtype)
    else:
        per_o = (n // out_split) // tn
        out_spec = pl.BlockSpec((None, tm, tn), lambda i, j, kk: (j // per_o, i, j % per_o))
        out_shape = jax.ShapeDtypeStruct((out_split, m, n // out_split), out_dtype)
    return pl.pallas_call(
        body, name=name, grid=(m // tm, n // tn, nk),
        in_specs=in_specs,
        out_specs=out_spec,
        out_shape=out_shape,
        input_output_aliases=aliases,
        scratch_shapes=[] if nk == 1 else [pltpu.VMEM((tm, tn), F32)],
        compiler_params=_cparams(("parallel", "parallel", "arbitrary")),
    )(*args)


def _gated_rmsnorm(o, z, nw):
    r = lax.rsqrt(jnp.mean(o * o, axis=1, keepdims=True) + RMS_EPS)
    return o * r * nw * _silu(z)


def _chunk_consts():
    row = lax.broadcasted_iota(jnp.int32, (CHUNK, CHUNK), 0)
    col = lax.broadcasted_iota(jnp.int32, (CHUNK, CHUNK), 1)
    return row, col


def _head_cols(tb, groups, hp, j, rev_nb=None):
    def bb(b):
        return b if rev_nb is None else rev_nb - 1 - b
    return pl.BlockSpec((tb, hp * LANES), lambda b, h: (bb(b), j * groups + h))


def _stack(pieces):
    return jnp.concatenate(pieces, axis=0) if len(pieces) > 1 else pieces[0]


def _place(t, g):
    if g == 1:
        return t
    head = lax.broadcasted_iota(jnp.int32, t.shape, 0) // CHUNK
    return jnp.concatenate([jnp.where(head == h, t, 0.0) for h in range(g)], axis=1)


def _hgrn2_chunk(qr, fr, iv, z, a0, a1, a2, nw, st):
    g = len(qr)
    n = g * CHUNK
    row = lax.broadcasted_iota(jnp.int32, (n, n), 0)
    col = lax.broadcasted_iota(jnp.int32, (n, n), 1)
    same = (row // CHUNK) == (col // CHUNK)
    qs, ks, gl = [], [], []
    for h in range(g):
        amax = jnp.maximum(jnp.maximum(a0[h], a1[h]), a2[h])
        e0, e1, e2 = jnp.exp(a0[h] - amax), jnp.exp(a1[h] - amax), jnp.exp(a2[h] - amax)
        lb = e0 / (e0 + e1 + e2)
        forget = lb + (1.0 - lb) * jax.nn.sigmoid(fr[h])
        qs.append(_silu(qr[h]))
        ks.append(1.0 - forget)
        gl.append(jnp.log(forget))
    q, k, glog, v, zs = _stack(qs), _stack(ks), _stack(gl), _stack(list(iv)), _stack(list(z))
    cum = _sdot((same & (col <= row)).astype(F32), glog)
    sub = (row % CHUNK) // SUB
    rowl = lax.broadcasted_iota(jnp.int32, (n, LANES), 0)
    headl, subl = rowl // CHUNK, (rowl % CHUNK) // SUB

    def row_of(r):
        picks = [jnp.sum(jnp.where(rowl == h * CHUNK + r, cum, 0.0), axis=0, keepdims=True) for h in range(g)]
        out = jnp.zeros((n, LANES), F32)
        for h in range(g):
            out = out + jnp.where(headl == h, picks[h], 0.0)
        return out, picks

    refs = [jnp.zeros((n, LANES), F32)] + [row_of(i * SUB - 1)[0] for i in range(1, CHUNK // SUB)]
    own = jnp.zeros((n, LANES), F32)
    for i, ref in enumerate(refs):
        own = own + jnp.where(subl == i, ref, 0.0)
    qt = q * jnp.exp(cum - own)
    att = jnp.zeros((n, n), F32)
    for i, ref in enumerate(refs):
        kt = k * jnp.exp(jnp.minimum(ref - cum, EXP_CLAMP))
        att = att + jnp.where(same & (sub == i) & (col <= row), _bdot(qt, kt, NT), 0.0)
    cl_rows, lasts = row_of(CHUNK - 1)
    o = _bdot(att, v, NN) + _bdot(_place(q * jnp.exp(cum), g), st, NT)
    kd = k * jnp.exp(cl_rows - cum)
    cl_wide = jnp.concatenate(lasts, axis=1) if g > 1 else lasts[0]
    st_new = st * jnp.exp(cl_wide) + _bdot(v, _place(kd, g), TN)
    return _gated_rmsnorm(o, zs, nw), st_new


def hgrn2_fwd(hproj, alb, nw, *, tb, hp):
    t, d4 = hproj.shape
    d = d4 // 4
    heads = d // LANES
    groups = heads // hp
    nb, nc = t // tb, tb // CHUNK

    def body(q_ref, f_ref, i_ref, z_ref, alb_ref, nw_ref, y_ref, states_ref, st_ref):
        b, hg = pl.program_id(0), pl.program_id(1)

        @pl.when(b == 0)
        def _():
            st_ref[hg] = jnp.zeros((LANES, hp * LANES), F32)

        nwv = nw_ref[...]
        lns = [slice(p * LANES, (p + 1) * LANES) for p in range(hp)]
        albs = []
        for r in range(3):
            albs.append(tuple(alb_ref[r:r + 1, pl.ds(pl.multiple_of((hg * hp + p) * LANES, LANES), LANES)]
                              for p in range(hp)))

        def step(c, carry):
            rows = pl.ds(pl.multiple_of(c * CHUNK, CHUNK), CHUNK)
            st = st_ref[hg]
            states_ref[c, 0] = st
            ins = [tuple(ref[rows, ln] for ln in lns) for ref in (q_ref, f_ref, i_ref, z_ref)]
            y, st_new = _hgrn2_chunk(*ins, *albs, nwv, st)
            for p in range(hp):
                y_ref[rows, lns[p]] = y[p * CHUNK:(p + 1) * CHUNK].astype(y_ref.dtype)
            st_ref[hg] = st_new
            return carry

        lax.fori_loop(0, nc, step, 0)

    return pl.pallas_call(
        body, name="hgrn2_fwd", grid=(nb, groups),
        in_specs=[_head_cols(tb, groups, hp, j) for j in range(4)] + [
            pl.BlockSpec((3, d), lambda b, h: (0, 0)),
            pl.BlockSpec((1, LANES), lambda b, h: (0, 0))],
        out_specs=[pl.BlockSpec((tb, hp * LANES), lambda b, h: (b, h)),
                   pl.BlockSpec((nc, 1, LANES, hp * LANES), lambda b, h: (b, h, 0, 0))],
        out_shape=[jax.ShapeDtypeStruct((t, d), BF16),
                   jax.ShapeDtypeStruct((t // CHUNK, groups, LANES, hp * LANES), F32)],
        scratch_shapes=[pltpu.VMEM((groups, LANES, hp * LANES), F32)],
        compiler_params=_cparams(("arbitrary", "arbitrary")),
    )(hproj, hproj, hproj, hproj, alb, nw)


def hgrn2_bwd(hproj, alb, nw, states, dy, *, tb, hp):
    t, d4 = hproj.shape
    d = d4 // 4
    heads = d // LANES
    groups = heads // hp
    nb, nc = t // tb, tb // CHUNK

    def body(q_ref, f_ref, i_ref, z_ref, alb_ref, nw_ref, states_ref, dy_ref,
             dh_ref, dalb_ref, dnw_ref, dst_ref):
        b, hg = pl.program_id(0), pl.program_id(1)

        @pl.when(b == 0)
        def _():
            dst_ref[hg] = jnp.zeros((LANES, hp * LANES), F32)

        @pl.when((b == 0) & (hg == 0))
        def _():
            dalb_ref[...] = jnp.zeros_like(dalb_ref)
            dnw_ref[...] = jnp.zeros_like(dnw_ref)

        nwv = nw_ref[...]
        lns = [slice(p * LANES, (p + 1) * LANES) for p in range(hp)]
        lanes_of = [pl.ds(pl.multiple_of((hg * hp + p) * LANES, LANES), LANES) for p in range(hp)]
        albs = [tuple(alb_ref[r:r + 1, lanes_of[p]] for p in range(hp)) for r in range(3)]

        def step(i, carry):
            c = nc - 1 - i
            rows = pl.ds(pl.multiple_of(c * CHUNK, CHUNK), CHUNK)
            ins = [tuple(ref[rows, ln] for ln in lns) for ref in (q_ref, f_ref, i_ref, z_ref)]
            dy = _stack([dy_ref[rows, ln].astype(F32) for ln in lns])
            _, vjp = jax.vjp(_hgrn2_chunk, *ins, *albs, nwv, states_ref[c, 0])
            dq, df, di, dz, da0, da1, da2, dnw, dst = vjp((dy, dst_ref[hg]))
            for p in range(hp):
                h = hg * hp + p
                for j, val in enumerate((dq, df, di, dz)):
                    dh_ref[rows, pl.ds(pl.multiple_of(j * d + h * LANES, LANES), LANES)] = val[p].astype(dh_ref.dtype)
                dalb_ref[0:1, lanes_of[p]] += da0[p]
                dalb_ref[1:2, lanes_of[p]] += da1[p]
                dalb_ref[2:3, lanes_of[p]] += da2[p]
            dnw_ref[0:1, :] += dnw
            dst_ref[hg] = dst
            return carry

        lax.fori_loop(0, nc, step, 0)

    return pl.pallas_call(
        body, name="hgrn2_bwd", grid=(nb, groups),
        in_specs=[_head_cols(tb, groups, hp, j, rev_nb=nb) for j in range(4)] + [
            pl.BlockSpec((3, d), lambda b, h: (0, 0)),
            pl.BlockSpec((1, LANES), lambda b, h: (0, 0)),
            pl.BlockSpec((nc, 1, LANES, hp * LANES), lambda b, h: (nb - 1 - b, h, 0, 0)),
            pl.BlockSpec((tb, hp * LANES), lambda b, h: (nb - 1 - b, h))],
        out_specs=[pl.BlockSpec((tb, d4), lambda b, h: (nb - 1 - b, 0)),
                   pl.BlockSpec((3, d), lambda b, h: (0, 0)),
                   pl.BlockSpec((8, LANES), lambda b, h: (0, 0))],
        out_shape=[jax.ShapeDtypeStruct((t, d4), BF16),
                   jax.ShapeDtypeStruct((3, d), F32),
                   jax.ShapeDtypeStruct((8, LANES), F32)],
        scratch_shapes=[pltpu.VMEM((groups, LANES, hp * LANES), F32)],
        compiler_params=_cparams(("arbitrary", "arbitrary")),
    )(hproj, hproj, hproj, hproj, alb, nw, states, dy)


def _conv_silu(xm, xh, w):
    acc = w[CONV_K - 1] * xm
    for j in range(CONV_K - 1):
        acc = acc + w[j] * _shift_rows(xm, xh, CONV_K - 1 - j)
    return _silu(acc)


def _l2norm(t):
    return t * lax.rsqrt(jnp.sum(t * t, axis=1, keepdims=True) + L2_EPS)


def _gdn_chunk(head0, hh, known_inv, qm, qh, km, kh, vm, vh, z, tail, wq, wk, wv, alog_row, dtb_row, nw, s):
    g = len(qm)
    n = g * CHUNK
    row = lax.broadcasted_iota(jnp.int32, (n, n), 0)
    col = lax.broadcasted_iota(jnp.int32, (n, n), 1)
    same = (row // CHUNK) == (col // CHUNK)
    lane = lax.broadcasted_iota(jnp.int32, (CHUNK, LANES), 1)
    heads_lane = lax.broadcasted_iota(jnp.int32, (1, LANES), 1)
    q = _l2norm(_stack([_conv_silu(qm[h], qh[h], wq[h]) for h in range(g)])) * (LANES ** -0.5)
    k = _l2norm(_stack([_conv_silu(km[h], kh[h], wk[h]) for h in range(g)]))
    v = _stack([_conv_silu(vm[h], vh[h], wv[h]) for h in range(g)])
    zs = _stack(list(z))
    betas, gs = [], []
    for h in range(g):
        head = head0 + h
        betas.append(jax.nn.sigmoid(jnp.sum(jnp.where(lane == head, tail, 0.0), axis=1, keepdims=True)))
        a_t = jnp.sum(jnp.where(lane == hh + head, tail, 0.0), axis=1, keepdims=True)
        alog = jnp.sum(jnp.where(heads_lane == head, alog_row, 0.0), axis=1, keepdims=True)
        dtb = jnp.sum(jnp.where(heads_lane == head, dtb_row, 0.0), axis=1, keepdims=True)
        gs.append(-jnp.exp(alog) * _softplus(a_t + dtb))
    beta, gcol = _stack(betas), _stack(gs)
    tril = (same & (col <= row)).astype(F32)
    eye = (col == row).astype(F32)
    if n % LANES == 0:
        cum = jnp.concatenate([_sdot(tril, gcol + jnp.zeros((n, LANES), F32))] * (n // LANES), axis=1)
    else:
        cum = _sdot(tril, gcol + jnp.zeros((n, n), F32))
    cum_r = cum.T
    diff = cum - cum_r
    strict = same & (col < row)
    incl = same & (col <= row)
    dec_strict = jnp.where(strict, jnp.exp(jnp.where(strict, diff, 0.0)), 0.0)
    dec_incl = jnp.where(incl, jnp.exp(jnp.where(incl, diff, 0.0)), 0.0)
    cum_c = jnp.sum(cum * eye, axis=1, keepdims=True)
    last = same & (col % CHUNK == CHUNK - 1)
    cl_rows = jnp.sum(jnp.where(last, cum_r, 0.0), axis=1, keepdims=True)
    ecum = jnp.exp(cum_c)
    m = -(beta * _bdot(k, k, NT) * dec_strict)
    inv = _tri_inv(m) if known_inv is None else _tri_inv_known(m, known_inv)
    u0, w = _bdot_cols2(inv, beta * v, (beta * ecum) * k)
    qk = _bdot(q, k, NT) * dec_incl
    u = u0 - _bdot(_place(w, g), s, NN)
    o = _bdot(_place(q * ecum, g), s, NN) + _bdot(qk, u, NN)
    kd = k * jnp.exp(cl_rows - cum_c)
    row1 = lax.broadcasted_iota(jnp.int32, (n, 1), 0)
    decay = []
    for h in range(g):
        cl_h = jnp.sum(jnp.where(row1 == h * CHUNK + CHUNK - 1, cum_c, 0.0), axis=0, keepdims=True)
        decay.append(jnp.exp(cl_h) + jnp.zeros((LANES, 1), F32))
    s_new = _stack(decay) * s + _bdot(_place(kd, g), u, TN)
    y = _gated_rmsnorm(o, zs, nw)
    return (y, s_new, inv) if known_inv is None else (y, s_new)


def _gdn_in_specs(tb, heads, hp, rev_nb=None):
    groups = heads // hp

    def bb(b):
        return b if rev_nb is None else rev_nb - 1 - b

    def halo(j):
        return pl.BlockSpec((HALO, hp * LANES),
                            lambda b, h, j=j: (jnp.maximum(bb(b) * (tb // HALO) - 1, 0), j * groups + h))

    def main(j):
        return _head_cols(tb, groups, hp, j, rev_nb=rev_nb)

    return [main(0), halo(0), main(1), halo(1), main(2), halo(2), main(3),
            pl.BlockSpec((tb, LANES), lambda b, h: (bb(b), 0)),
            pl.BlockSpec((CONV_K, 3 * heads * LANES), lambda b, h: (0, 0)),
            pl.BlockSpec((8, LANES), lambda b, h: (0, 0)),
            pl.BlockSpec((1, LANES), lambda b, h: (0, 0))]


def _gdn_chunk_args(c, head0, hp, blk, heads, q_ref, qh_ref, k_ref, kh_ref, v_ref, vh_ref, z_ref, tail_ref, cw_ref, prm_ref, nw_ref):
    d = heads * LANES
    lns = [slice(p * LANES, (p + 1) * LANES) for p in range(hp)]
    rows = pl.ds(pl.multiple_of(c * CHUNK, CHUNK), CHUNK)
    prev = pl.ds(pl.multiple_of(jnp.maximum(c * CHUNK - HALO, 0), HALO), HALO)
    first = c == 0
    live = jnp.where(first & (blk == 0), 0.0, 1.0)

    def main_of(ref):
        return tuple(ref[rows, ln] for ln in lns)

    def halo_of(ref, href):
        return tuple(jnp.where(first, href[:, ln], ref[prev, ln]) * live for ln in lns)

    def cw(j):
        out = []
        for p in range(hp):
            lanes = pl.ds(pl.multiple_of(j * d + (head0 + p) * LANES, LANES), LANES)
            out.append(tuple(cw_ref[r:r + 1, lanes] for r in range(CONV_K)))
        return tuple(out)

    return (main_of(q_ref), halo_of(q_ref, qh_ref), main_of(k_ref), halo_of(k_ref, kh_ref),
            main_of(v_ref), halo_of(v_ref, vh_ref), main_of(z_ref), tail_ref[rows, :],
            cw(0), cw(1), cw(2), prm_ref[0:1, :], prm_ref[1:2, :], nw_ref[...])


def gdn_fwd(hmain, tail, conv_w, prm, nw, *, tb, hp):
    t, d4 = hmain.shape
    d = d4 // 4
    heads = d // LANES
    groups = heads // hp
    nb, nc = t // tb, tb // CHUNK
    n = hp * CHUNK

    def body(q_ref, qh_ref, k_ref, kh_ref, v_ref, vh_ref, z_ref, tail_ref, cw_ref, prm_ref, nw_ref,
             y_ref, states_ref, inv_ref, s_ref):
        b, hg = pl.program_id(0), pl.program_id(1)

        @pl.when(b == 0)
        def _():
            s_ref[hg] = jnp.zeros((hp * LANES, LANES), F32)

        def step(c, carry):
            rows = pl.ds(pl.multiple_of(c * CHUNK, CHUNK), CHUNK)
            s = s_ref[hg]
            states_ref[c, 0] = s
            args = _gdn_chunk_args(c, hg * hp, hp, b, heads, q_ref, qh_ref, k_ref, kh_ref, v_ref, vh_ref, z_ref,
                                   tail_ref, cw_ref, prm_ref, nw_ref)
            y, s_new, inv = _gdn_chunk(hg * hp, heads, None, *args, s)
            for p in range(hp):
                y_ref[rows, p * LANES:(p + 1) * LANES] = y[p * CHUNK:(p + 1) * CHUNK].astype(y_ref.dtype)
            inv_ref[c, 0] = inv.astype(inv_ref.dtype)
            s_ref[hg] = s_new
            return carry

        lax.fori_loop(0, nc, step, 0)

    return pl.pallas_call(
        body, name="gdn_fwd", grid=(nb, groups),
        in_specs=_gdn_in_specs(tb, heads, hp),
        out_specs=[pl.BlockSpec((tb, hp * LANES), lambda b, h: (b, h)),
                   pl.BlockSpec((nc, 1, hp * LANES, LANES), lambda b, h: (b, h, 0, 0)),
                   pl.BlockSpec((nc, 1, n, n), lambda b, h: (b, h, 0, 0))],
        out_shape=[jax.ShapeDtypeStruct((t, d), BF16),
                   jax.ShapeDtypeStruct((t // CHUNK, groups, hp * LANES, LANES), F32),
                   jax.ShapeDtypeStruct((t // CHUNK, groups, n, n), BF16)],
        scratch_shapes=[pltpu.VMEM((groups, hp * LANES, LANES), F32)],
        compiler_params=_cparams(("arbitrary", "arbitrary")),
    )(hmain, hmain, hmain, hmain, hmain, hmain, hmain, tail, conv_w, prm, nw)


def gdn_bwd(hmain, tail, conv_w, prm, nw, states, invs, dy, *, tb, hp):
    t, d4 = hmain.shape
    d = d4 // 4
    heads = d // LANES
    groups = heads // hp
    nb, nc = t // tb, tb // CHUNK

    def body(q_ref, qh_ref, k_ref, kh_ref, v_ref, vh_ref, z_ref, tail_ref, cw_ref, prm_ref, nw_ref,
             states_ref, inv_ref, dy_ref, dh_ref, dtail_ref, dcw_ref, dprm_ref, ds_ref, pend_ref):
        b, hg = pl.program_id(0), pl.program_id(1)
        blk = nb - 1 - b

        @pl.when(b == 0)
        def _():
            ds_ref[hg] = jnp.zeros((hp * LANES, LANES), F32)
            for p in range(hp):
                pend_ref[hg * hp + p] = jnp.zeros((3, HALO, LANES), F32)

        @pl.when((b == 0) & (hg == 0))
        def _():
            dcw_ref[...] = jnp.zeros_like(dcw_ref)
            dprm_ref[...] = jnp.zeros_like(dprm_ref)

        @pl.when(hg == 0)
        def _():
            dtail_ref[...] = jnp.zeros_like(dtail_ref)

        def step(i, carry):
            c = nc - 1 - i
            rows = pl.ds(pl.multiple_of(c * CHUNK, CHUNK), CHUNK)
            zpad = jnp.zeros((CHUNK - HALO, LANES), F32)
            args = _gdn_chunk_args(c, hg * hp, hp, blk, heads, q_ref, qh_ref, k_ref, kh_ref, v_ref, vh_ref, z_ref,
                                   tail_ref, cw_ref, prm_ref, nw_ref)
            dy = _stack([dy_ref[rows, p * LANES:(p + 1) * LANES].astype(F32) for p in range(hp)])
            pends = [pend_ref[hg * hp + p] for p in range(hp)]
            known_inv = inv_ref[c, 0].astype(F32)
            _, vjp = jax.vjp(functools.partial(_gdn_chunk, hg * hp, heads, known_inv), *args, states_ref[c, 0])
            (dqm, dqh, dkm, dkh, dvm, dvh, dz, dtl, dwq, dwk, dwv, dalog, ddtb, dnw, ds) = vjp((dy, ds_ref[hg]))
            for p in range(hp):
                h = hg * hp + p
                for j, (dm, dhalo) in enumerate(((dqm, dqh), (dkm, dkh), (dvm, dvh))):
                    full = dm[p] + jnp.concatenate([zpad, pends[p][j]], axis=0)
                    dh_ref[rows, pl.ds(pl.multiple_of(j * d + h * LANES, LANES), LANES)] = full.astype(dh_ref.dtype)
                    pend_ref[h, j] = dhalo[p]
                dh_ref[rows, pl.ds(pl.multiple_of(3 * d + h * LANES, LANES), LANES)] = dz[p].astype(dh_ref.dtype)
                for j, dw in enumerate((dwq, dwk, dwv)):
                    lanes = pl.ds(pl.multiple_of(j * d + h * LANES, LANES), LANES)
                    for r in range(CONV_K):
                        dcw_ref[r:r + 1, lanes] += dw[p][r]
            dtail_ref[rows, :] += dtl
            dprm_ref[0:1, :] += dalog
            dprm_ref[1:2, :] += ddtb
            dprm_ref[2:3, :] += dnw
            ds_ref[hg] = ds
            return carry

        lax.fori_loop(0, nc, step, 0)

    return pl.pallas_call(
        body, name="gdn_bwd", grid=(nb, groups),
        in_specs=_gdn_in_specs(tb, heads, hp, rev_nb=nb) + [
            pl.BlockSpec((nc, 1, hp * LANES, LANES), lambda b, h: (nb - 1 - b, h, 0, 0)),
            pl.BlockSpec((nc, 1, hp * CHUNK, hp * CHUNK), lambda b, h: (nb - 1 - b, h, 0, 0)),
            pl.BlockSpec((tb, hp * LANES), lambda b, h: (nb - 1 - b, h))],
        out_specs=[pl.BlockSpec((tb, d4), lambda b, h: (nb - 1 - b, 0)),
                   pl.BlockSpec((tb, LANES), lambda b, h: (nb - 1 - b, 0)),
                   pl.BlockSpec((CONV_K, 3 * d), lambda b, h: (0, 0)),
                   pl.BlockSpec((8, LANES), lambda b, h: (0, 0))],
        out_shape=[jax.ShapeDtypeStruct((t, d4), BF16),
                   jax.ShapeDtypeStruct((t, LANES), F32),
                   jax.ShapeDtypeStruct((CONV_K, 3 * d), F32),
                   jax.ShapeDtypeStruct((8, LANES), F32)],
        scratch_shapes=[pltpu.VMEM((groups, hp * LANES, LANES), F32),
                        pltpu.VMEM((heads, 3, HALO, LANES), F32)],
        compiler_params=_cparams(("arbitrary", "arbitrary")),
    )(hmain, hmain, hmain, hmain, hmain, hmain, hmain, tail, conv_w, prm, nw, states, invs, dy)


def _layer_norm(u, g, b):
    mu = jnp.mean(u, axis=1, keepdims=True)
    var = jnp.mean(jnp.square(u - mu), axis=1, keepdims=True)
    return (u - mu) * lax.rsqrt(var + LN_EPS) * g + b


def ln_fwd(u, g, b, *, tr):
    t, d = u.shape

    def body(u_ref, g_ref, b_ref, x_ref, xb_ref):
        x = _layer_norm(u_ref[...], g_ref[...], b_ref[...])
        x_ref[...] = x
        xb_ref[...] = x.astype(BF16)

    row = pl.BlockSpec((tr, d), lambda i: (i, 0))
    vec = pl.BlockSpec((1, d), lambda i: (0, 0))
    return pl.pallas_call(
        body, name="ln_fwd", grid=(t // tr,), in_specs=[row, vec, vec], out_specs=[row, row],
        out_shape=[jax.ShapeDtypeStruct((t, d), F32), jax.ShapeDtypeStruct((t, d), BF16)],
        compiler_params=_cparams(("parallel",)),
    )(u, g, b)


def ln_bwd(u, g, b, dout, *, tr):
    t, d = u.shape

    def body(u_ref, g_ref, b_ref, dout_ref, du_ref, dub_ref, dg_ref, db_ref):
        @pl.when(pl.program_id(0) == 0)
        def _():
            dg_ref[...] = jnp.zeros_like(dg_ref)
            db_ref[...] = jnp.zeros_like(db_ref)

        _, vjp = jax.vjp(_layer_norm, u_ref[...], g_ref[...], b_ref[...])
        du, dg, db = vjp(dout_ref[...])
        du_ref[...] = du
        dub_ref[...] = du.astype(BF16)
        dg_ref[0:1, :] += dg
        db_ref[0:1, :] += db

    row = pl.BlockSpec((tr, d), lambda i: (i, 0))
    vec = pl.BlockSpec((1, d), lambda i: (0, 0))
    acc = pl.BlockSpec((8, d), lambda i: (0, 0))
    return pl.pallas_call(
        body, name="ln_bwd", grid=(t // tr,), in_specs=[row, vec, vec, row], out_specs=[row, row, acc, acc],
        out_shape=[jax.ShapeDtypeStruct((t, d), F32), jax.ShapeDtypeStruct((t, d), BF16),
                   jax.ShapeDtypeStruct((8, d), F32), jax.ShapeDtypeStruct((8, d), F32)],
        compiler_params=_cparams(("arbitrary",)),
    )(u, g, b, dout)


def ln_loss_bwd(u, g, b, target, *, tr):
    t, d = u.shape

    def loss_of(uu, gg, bb, tgt):
        err = jnp.square(_layer_norm(uu, gg, bb) - tgt)
        return 0.5 * jnp.sum(jnp.mean(err, axis=1, keepdims=True), axis=0, keepdims=True)

    def body(u_ref, g_ref, b_ref, t_ref, loss_ref, du_ref, dub_ref, dg_ref, db_ref):
        @pl.when(pl.program_id(0) == 0)
        def _():
            loss_ref[...] = jnp.zeros_like(loss_ref)
            dg_ref[...] = jnp.zeros_like(dg_ref)
            db_ref[...] = jnp.zeros_like(db_ref)

        tgt = t_ref[...]
        val, vjp = jax.vjp(lambda uu, gg, bb: loss_of(uu, gg, bb, tgt), u_ref[...], g_ref[...], b_ref[...])
        du, dg, db = vjp(jnp.ones((1, 1), F32))
        loss_ref[...] += val
        du_ref[...] = du
        dub_ref[...] = du.astype(BF16)
        dg_ref[0:1, :] += dg
        db_ref[0:1, :] += db

    row = pl.BlockSpec((tr, d), lambda i: (i, 0))
    vec = pl.BlockSpec((1, d), lambda i: (0, 0))
    acc = pl.BlockSpec((8, d), lambda i: (0, 0))
    return pl.pallas_call(
        body, name="ln_loss_bwd", grid=(t // tr,), in_specs=[row, vec, vec, row],
        out_specs=[pl.BlockSpec((8, LANES), lambda i: (0, 0)), row, row, acc, acc],
        out_shape=[jax.ShapeDtypeStruct((8, LANES), F32),
                   jax.ShapeDtypeStruct((t, d), F32), jax.ShapeDtypeStruct((t, d), BF16),
                   jax.ShapeDtypeStruct((8, d), F32), jax.ShapeDtypeStruct((8, d), F32)],
        compiler_params=_cparams(("arbitrary",)),
    )(u, g, b, target)


def local_step(x, target, wa_in, alb, a_nw, wa_out, wb_t, wb_tail_t, conv_w, a_log, dt_bias, b_nw, wb_out,
               ln_g, ln_b, *, tb=256, tr=256, hp=HEADS_PER_STEP, first_after=None, b_weights=None, on_b_grads=None,
               on_a_out_grad=None, on_a_grads=None, on_mid=None, on_small_grads=None):
    t, d = x.shape
    heads = d // LANES
    tb, tr, hp = min(tb, t), min(tr, t), min(hp, heads)
    xb = x.astype(BF16)
    prm = jnp.zeros((8, LANES), F32).at[0, :heads].set(a_log[0]).at[1, :heads].set(dt_bias[0])

    ha = matmul(xb, wa_in, name="mm_a_in", after=first_after)
    ya, st_a = hgrn2_fwd(ha, alb, a_nw, tb=tb, hp=hp)
    u1 = matmul(ya, wa_out, name="mm_a_out", addend=x, alpha=DEEPNORM_ALPHA,
                after=None if on_mid is None else on_mid(ya))
    x1, x1b = ln_fwd(u1, ln_g[0:1], ln_b[0:1], tr=tr)
    if b_weights is not None:
        wb_t, wb_tail_t, wb_out, conv_w = b_weights(x1b)
    n_tail = 2 * heads
    hb = matmul(x1b, wb_t, name="mm_b_in", nt=True, n=4 * d)
    tl = matmul(x1b, wb_tail_t, name="mm_b_tail", nt=True)
    yb, st_b, inv_b = gdn_fwd(hb, tl, conv_w, prm, b_nw, tb=tb, hp=hp)
    u2 = matmul(yb, wb_out, name="mm_b_out", addend=x1, alpha=DEEPNORM_ALPHA)

    loss, du2, du2b, dg2, db2 = ln_loss_bwd(u2, ln_g[1:2], ln_b[1:2], target, tr=tr)
    d_wb_out = matmul(yb, du2b, name="mm_dwb_out", lhs_t=True, out_dtype=BF16)
    dyb = matmul(du2b, wb_out, name="mm_dyb", nt=True)
    dhb, dtl, d_conv, dprm = gdn_bwd(hb, tl, conv_w, prm, b_nw, st_b, inv_b, dyb, tb=tb, hp=hp)
    dtlb = dtl.astype(BF16)
    d_wb_t = matmul(dhb, x1b, name="mm_dwb_main", lhs_t=True, out_dtype=BF16, out_rows=4 * d + n_tail)
    d_wb_t = matmul(dtlb[:, :n_tail], x1b, name="mm_dwb_tail", lhs_t=True, out_dtype=BF16, into=(d_wb_t, 4 * d))
    sent_b = on_b_grads(d_wb_t, d_wb_out) if on_b_grads is not None else None
    dx1_tail = matmul(dtlb, wb_tail_t, name="mm_dx1_tail", addend=du2, alpha=DEEPNORM_ALPHA, after=sent_b)
    dx1 = matmul(dhb, wb_t, name="mm_dx1", addend=dx1_tail, alpha=1.0, tk=2048)
    du1, du1b, dg1, db1 = ln_bwd(u1, ln_g[0:1], ln_b[0:1], dx1, tr=tr)
    d_wa_out = matmul(ya, du1b, name="mm_dwa_out", lhs_t=True, out_dtype=BF16)
    sent_c = on_a_out_grad(d_wa_out) if on_a_out_grad is not None else None
    dya = matmul(du1b, wa_out, name="mm_dya", nt=True, after=sent_c)
    dha, d_alb, d_anw = hgrn2_bwd(ha, alb, a_nw, st_a, dya, tb=tb, hp=hp)
    small = dict(
        a_lower_bounds=d_alb, a_norm_w=d_anw[0:1], b_conv_w=d_conv,
        b_a_log=dprm[0:1, :heads], b_dt_bias=dprm[1:2, :heads], b_norm_w=dprm[2:3],
        ln_g=jnp.concatenate([dg1[0:1], dg2[0:1]], axis=0), ln_b=jnp.concatenate([db1[0:1], db2[0:1]], axis=0))
    d_wa_in = matmul(xb, dha, name="mm_dwa_in", lhs_t=True, out_dtype=BF16,
                     out_split=wa_in.shape[0] if wa_in.ndim == 3 else 1,
                     after=None if on_small_grads is None else on_small_grads(small))
    sent_a = on_a_grads(d_wa_in) if on_a_grads is not None else None
    grad_x = matmul(dha, wa_in, name="mm_dx", nt=True, addend=du1, alpha=DEEPNORM_ALPHA, after=sent_a, tk=2048)
    big = dict(a_w_in=d_wa_in, a_w_out=d_wa_out, b_w_t=d_wb_t, b_w_out=d_wb_out)
    return loss, grad_x, big, small


MESH_ID = pl.DeviceIdType.MESH


def _position():
    return lax.axis_index("x"), lax.axis_index("y"), lax.axis_index("c")


def _index_of(p):
    return 4 * p[0] + 2 * p[1] + p[2]


def all_gather(shards, *, name, space):
    n = len(shards)

    def body(*refs):
        ins, outs = refs[:n], refs[n:2 * n]
        send_sems, recv_sems, local_sems = refs[2 * n:]
        x, y, c = _position()
        me, sibling = (x, y, c), (x, y, 1 - c)
        chips = [(1 - x, y), (x, 1 - y), (1 - x, 1 - y)]

        def copy(a, k, block, to, own=False):
            dst = outs[a].at[_index_of(block)]
            return pltpu.make_async_remote_copy(
                src_ref=ins[a] if own else dst, dst_ref=dst,
                send_sem=send_sems.at[7 * a + k], recv_sem=recv_sems.at[7 * a + k],
                device_id=to, device_id_type=MESH_ID)

        mine = [pltpu.make_async_copy(ins[a], outs[a].at[_index_of(me)], local_sems.at[a]) for a in range(n)]
        for cp in mine:
            cp.start()
        first = []
        for a in range(n):
            first.append(copy(a, 0, me, sibling, own=True))
            first += [copy(a, 1 + j, me, (*chip, c), own=True) for j, chip in enumerate(chips)]
        for cp in first:
            cp.start()
        passed = []
        for j, chip in enumerate(chips):
            for a in range(n):
                copy(a, 1 + j, (*chip, c), me).wait_recv()
                fwd = copy(a, 4 + j, (*chip, c), sibling)
                fwd.start()
                passed.append(fwd)
        for a in range(n):
            copy(a, 0, sibling, me).wait_recv()
            for j, chip in enumerate(chips):
                copy(a, 4 + j, (*chip, 1 - c), me).wait_recv()
        for cp in first + passed:
            cp.wait_send()
        for cp in mine:
            cp.wait()

    spec = pl.BlockSpec(memory_space=space)
    return pl.pallas_call(
        body, name=name,
        in_specs=[spec] * n, out_specs=[spec] * n,
        out_shape=[jax.ShapeDtypeStruct((N_DEV, *s.shape), s.dtype) for s in shards],
        scratch_shapes=[pltpu.SemaphoreType.DMA((7 * n,)), pltpu.SemaphoreType.DMA((7 * n,)),
                        pltpu.SemaphoreType.DMA((n,))],
        compiler_params=pltpu.CompilerParams(vmem_limit_bytes=VMEM_LIMIT),
    )(*shards)


HBM_SPEC = pl.BlockSpec(memory_space=pltpu.HBM)
SEM_SPEC = pl.BlockSpec(memory_space=pltpu.SEMAPHORE)


CHIP_PEERS = (4, 2, 6)
COPIES_PER_ARRAY = dict(gather=7, exchange=7, gather_near=4, gather_pass=3, reduce_pair=4, reduce_chip=3)


def _planned_copies(kind, srcs, lands, send_sems, recv_sems):
    x, y, c = _position()
    me, my_chip = _index_of((x, y, c)), 2 * x + y
    per = COPIES_PER_ARRAY[kind]
    copies = []

    def peer_of(r):
        return (x ^ ((r >> 2) & 1), y ^ ((r >> 1) & 1), c ^ (r & 1))

    def add(a, j, src, dst, peer):
        copies.append(pltpu.make_async_remote_copy(
            src_ref=src, dst_ref=dst, send_sem=send_sems.at[per * a + j], recv_sem=recv_sems.at[per * a + j],
            device_id=peer, device_id_type=MESH_ID))

    for a, land in enumerate(lands):
        src = srcs[a] if srcs else land
        if kind == "gather":
            for r in range(1, N_DEV):
                add(a, r - 1, src, land.at[me], peer_of(r))
        elif kind == "exchange":
            for r in range(1, N_DEV):
                add(a, r - 1, src.at[_index_of(peer_of(r))], land.at[me], peer_of(r))
        elif kind == "gather_near":
            for j, r in enumerate((1,) + CHIP_PEERS):
                add(a, j, src, land.at[me], peer_of(r))
        elif kind == "gather_pass":
            for j, r in enumerate(CHIP_PEERS):
                slot = _index_of(peer_of(r))
                add(a, j, land.at[slot], land.at[slot], peer_of(1))
        elif kind == "reduce_pair":
            for q in range(4):
                add(a, q, src.at[2 * q + (1 - c)], land.at[q], peer_of(1))
        elif kind == "reduce_chip":
            for j, r in enumerate(CHIP_PEERS):
                p = peer_of(r)
                add(a, j, src.at[2 * p[0] + p[1]], land.at[my_chip], p)
    return copies


def copies_start(kind, srcs, lands, *, name):
    ns, n = len(srcs), len(srcs) + len(lands)
    n_sem = COPIES_PER_ARRAY[kind] * len(lands)

    def body(*refs):
        for cp in _planned_copies(kind, refs[:ns], refs[ns:n], refs[n], refs[n + 1]):
            cp.start()
        refs[-1][...] = jnp.zeros_like(refs[-1])

    arrays = [pltpu.with_memory_space_constraint(t, pltpu.HBM) for t in (*srcs, *lands)]
    outs = pl.pallas_call(
        body, name=name,
        out_shape=(pltpu.SemaphoreType.DMA((n_sem,)), pltpu.SemaphoreType.DMA((n_sem,)),
                   *[pltpu.HBM(t.shape, t.dtype) for t in arrays], jax.ShapeDtypeStruct((8, LANES), F32)),
        in_specs=[HBM_SPEC] * n,
        out_specs=(SEM_SPEC, SEM_SPEC, *[HBM_SPEC] * n, pl.BlockSpec(memory_space=pltpu.VMEM)),
        input_output_aliases={i: 2 + i for i in range(n)},
        compiler_params=pltpu.CompilerParams(has_side_effects=pltpu.SideEffectType.DATAFLOW_SIDE_EFFECTING),
    )(*arrays)
    return (outs[0], outs[1]), list(outs[2:2 + ns]), list(outs[2 + ns:2 + n]), outs[-1]


def copies_wait(kind, sems, srcs, lands, after, *, name):
    ns, n = len(srcs), len(srcs) + len(lands)

    def body(*refs):
        for cp in _planned_copies(kind, refs[:ns], refs[ns:n], refs[n], refs[n + 1]):
            cp.wait_send()
            cp.wait_recv()

    outs = pl.pallas_call(
        body, name=name,
        out_shape=tuple(pltpu.HBM(t.shape, t.dtype) for t in (*srcs, *lands)),
        in_specs=[HBM_SPEC] * n + [SEM_SPEC, SEM_SPEC, pl.BlockSpec(memory_space=pl.ANY)],
        out_specs=tuple([HBM_SPEC] * n),
        input_output_aliases={i: i for i in range(n)},
        compiler_params=pltpu.CompilerParams(has_side_effects=pltpu.SideEffectType.DATAFLOW_SIDE_EFFECTING),
    )(*srcs, *lands, sems[0], sems[1], after)
    return list(outs[ns:])


def _landing(own, slot, slots=N_DEV):
    return lax.dynamic_update_slice_in_dim(lax.empty((slots, *own.shape), own.dtype), own[None], slot, 0)


def pair_sum(a, b, *, name, tr=256):
    p, r, c = a.shape
    tr = min(tr, r)

    def body(a_ref, b_ref, o_ref):
        o_ref[...] = (a_ref[...].astype(F32) + b_ref[...].astype(F32)).astype(o_ref.dtype)

    blk = pl.BlockSpec((1, tr, c), lambda i, j: (i, j, 0))
    return pl.pallas_call(
        body, name=name, grid=(p, r // tr), in_specs=[blk, blk], out_specs=blk,
        out_shape=jax.ShapeDtypeStruct(a.shape, a.dtype), compiler_params=_cparams(("parallel", "parallel")),
    )(a, b)


def adamw(parts, w, m, v, *, name, tr=64, tc=None):
    p, r, c = parts.shape
    tr = r if tc is not None else min(tr, r)
    tc = c if tc is None else tc
    assert r % tr == 0 and c % tc == 0
    c1 = 1.0 / (1.0 - ADAM_B1 ** ADAM_STEP)
    c2 = 1.0 / (1.0 - ADAM_B2 ** ADAM_STEP)

    def body(p_ref, w_ref, m_ref, v_ref, g_ref, d_ref, nm_ref, nv_ref):
        g = p_ref[0].astype(F32)
        for i in range(1, p):
            g = g + p_ref[i].astype(F32)
        nm = ADAM_B1 * m_ref[...] + (1.0 - ADAM_B1) * g
        nv = ADAM_B2 * v_ref[...] + (1.0 - ADAM_B2) * jnp.square(g)
        g_ref[...] = g
        nm_ref[...] = nm
        nv_ref[...] = nv
        d_ref[...] = -ADAM_LR * ((nm * c1) / (jnp.sqrt(nv * c2) + ADAM_EPS) + ADAM_WD * w_ref[...])

    blk = pl.BlockSpec((tr, tc), lambda i, j: (i, j))
    out = jax.ShapeDtypeStruct((r, c), F32)
    return pl.pallas_call(
        body, name=name, grid=(r // tr, c // tc),
        in_specs=[pl.BlockSpec((p, tr, tc), lambda i, j: (0, i, j)), blk, blk, blk],
        out_specs=[blk] * 4, out_shape=[out] * 4,
        compiler_params=_cparams(("parallel", "parallel")),
    )(parts, w, m, v)


def _pack(d, vals):
    heads = d // LANES
    vecs = jnp.zeros((8, LANES), F32)
    vecs = vecs.at[0:1].set(vals["a_norm_w"]).at[1:2, :heads].set(vals["b_a_log"])
    vecs = vecs.at[2:3, :heads].set(vals["b_dt_bias"]).at[3:4].set(vals["b_norm_w"])
    rows = [vals["a_lower_bounds"].reshape(-1, LANES), vals["ln_g"].reshape(-1, LANES),
            vals["ln_b"].reshape(-1, LANES), vecs]
    return jnp.concatenate(rows, axis=0)


def _unpack(d, packed):
    heads = d // LANES
    n3, n2 = 3 * heads, 2 * heads
    o = 0
    out = {}
    out["a_lower_bounds"] = packed[o:o + n3].reshape(3, d); o += n3
    out["ln_g"] = packed[o:o + n2].reshape(2, d); o += n2
    out["ln_b"] = packed[o:o + n2].reshape(2, d); o += n2
    out["a_norm_w"] = packed[o:o + 1]
    out["b_a_log"] = packed[o + 1:o + 2, :heads]
    out["b_dt_bias"] = packed[o + 2:o + 3, :heads]
    out["b_norm_w"] = packed[o + 3:o + 4]
    return out


ORDER = ("a_w_in", "a_lower_bounds", "a_norm_w", "a_w_out", "b_w_in", "b_conv_w", "b_a_log", "b_dt_bias", "b_norm_w",
         "b_w_out", "ln_g", "ln_b")


def kernel(x, a_w_in, a_lower_bounds, a_norm_w, a_w_out, b_w_in, b_conv_w, b_a_log, b_dt_bias, b_norm_w, b_w_out, ln_g, ln_b, loss_target, m_a_w_in, m_a_lower_bounds, m_a_norm_w, m_a_w_out, m_b_w_in, m_b_conv_w, m_b_a_log, m_b_dt_bias, m_b_norm_w, m_b_w_out, m_ln_g, m_ln_b, v_a_w_in, v_a_lower_bounds, v_a_norm_w, v_a_w_out, v_b_w_in, v_b_conv_w, v_b_a_log, v_b_dt_bias, v_b_norm_w, v_b_w_out, v_ln_g, v_ln_b):
    w = dict(a_w_in=a_w_in, a_lower_bounds=a_lower_bounds, a_norm_w=a_norm_w, a_w_out=a_w_out, b_w_in=b_w_in,
             b_conv_w=b_conv_w, b_a_log=b_a_log, b_dt_bias=b_dt_bias, b_norm_w=b_norm_w, b_w_out=b_w_out, ln_g=ln_g, ln_b=ln_b)
    m = dict(a_w_in=m_a_w_in, a_lower_bounds=m_a_lower_bounds, a_norm_w=m_a_norm_w, a_w_out=m_a_w_out, b_w_in=m_b_w_in,
             b_conv_w=m_b_conv_w, b_a_log=m_b_a_log, b_dt_bias=m_b_dt_bias, b_norm_w=m_b_norm_w, b_w_out=m_b_w_out,
             ln_g=m_ln_g, ln_b=m_ln_b)
    v = dict(a_w_in=v_a_w_in, a_lower_bounds=v_a_lower_bounds, a_norm_w=v_a_norm_w, a_w_out=v_a_w_out, b_w_in=v_b_w_in,
             b_conv_w=v_b_conv_w, b_a_log=v_b_a_log, b_dt_bias=v_b_dt_bias, b_norm_w=v_b_norm_w, b_w_out=v_b_w_out,
             ln_g=v_ln_g, ln_b=v_ln_b)
    t, d = x.shape[1], x.shape[2]
    heads = d // LANES
    n_tail = 2 * heads
    me = _index_of(_position())

    ga_in, ga_out, g_conv = all_gather([a_w_in[0].astype(BF16), a_w_out[0].astype(BF16), b_conv_w[0]],
                                       name="gather_weights_a", space=pltpu.HBM)
    b_shards = [jnp.transpose(b_w_in[0]).astype(BF16), b_w_out[0].astype(BF16)]
    flight = {}
    sems_w, b_shards, lands_w, token_w = copies_start("gather_near", b_shards, [_landing(s, me) for s in b_shards],
                                                      name="gather_weights_b_start")
    wa_out = ga_out.reshape(d, d)
    conv_w = jnp.transpose(g_conv, (1, 0, 2)).reshape(CONV_K, 3 * d)

    def on_mid(after):
        lands = copies_wait("gather_near", sems_w, b_shards, lands_w, after, name="gather_weights_b_wait")
        sems, _, lands, token = copies_start("gather_pass", [], lands, name="pass_weights_b_start")
        flight["w"] = (sems, lands)
        return token

    def b_weights(after):
        sems, lands = flight["w"]
        gb_in, gb_out = copies_wait("gather_pass", sems, [], lands, after, name="pass_weights_b_wait")
        wb_t = gb_in.reshape(4 * d + n_tail, d)
        wb_tail_t = jnp.concatenate([wb_t[4 * d:], jnp.zeros((LANES - n_tail, d), BF16)], axis=0)
        return wb_t, wb_tail_t, gb_out.reshape(d, d), conv_w

    def send_grads(key, parts):
        own = [lax.dynamic_index_in_dim(p, me, 0, keepdims=False) for p in parts]
        sems, parts, lands, token = copies_start("exchange", parts, [_landing(o, me) for o in own],
                                                 name="exchange_" + key + "_start")
        flight[key] = (sems, parts, lands)
        return token

    def on_b_grads(d_wb_t, d_out):
        return send_grads("b", [d_wb_t.reshape(N_DEV, -1, d), d_out.reshape(N_DEV, d // N_DEV, d)])

    res = {}

    def on_small_grads(small):
        conv_rows = small["b_conv_w"].reshape(-1, LANES)
        n_conv = conv_rows.shape[0]
        packed_grads = jnp.concatenate([conv_rows, _pack(d, small)], axis=0)
        (got,) = all_gather([packed_grads], name="gather_small", space=pltpu.VMEM)
        packed = adamw(got[:, n_conv:], _pack(d, w), _pack(d, m), _pack(d, v), name="adamw_small", tr=4096)
        for k, vals in zip(("grad", "delta", "new_m", "new_v"), packed):
            res[k] = _unpack(d, vals)
        shard_ch = 3 * d // N_DEV
        conv_parts = lax.dynamic_slice_in_dim(got[:, :n_conv].reshape(N_DEV, CONV_K, 3 * d), me * shard_ch, shard_ch,
                                              axis=2)
        conv_out = adamw(conv_parts, b_conv_w[0], m_b_conv_w[0], v_b_conv_w[0], name="adamw_conv")
        for k, vals in zip(("grad", "delta", "new_m", "new_v"), conv_out):
            res[k]["b_conv_w"] = vals[None]
        return conv_out[0]

    my_chip, my_core = me // 2, me % 2

    def on_a_grads(d_in):
        mine = lax.dynamic_index_in_dim(d_in.reshape(N_DEV // 2, 2, *d_in.shape[1:]), my_core, 1, keepdims=False)
        land1 = lax.empty(mine.shape, mine.dtype)
        sems, (d_in,), (land1,), token = copies_start("reduce_pair", [d_in], [land1], name="reduce_a_pair_start")
        (land1,) = copies_wait("reduce_pair", sems, [d_in], [land1], token, name="reduce_a_pair_wait")
        summed = pair_sum(mine, land1, name="reduce_a_pair_sum")
        own = lax.dynamic_index_in_dim(summed, my_chip, 0, keepdims=False)
        sems, (summed,), (land2,), token = copies_start("reduce_chip", [summed], [_landing(own, my_chip, N_DEV // 2)],
                                                        name="reduce_a_chip_start")
        flight["a"] = (sems, [summed], [land2])
        return token

    loss, grad_x, big, small = local_step(
        x[0], loss_target[0], ga_in, a_lower_bounds, a_norm_w, wa_out, None, None, None, b_a_log, b_dt_bias,
        b_norm_w, None, ln_g, ln_b, first_after=token_w, on_mid=on_mid, b_weights=b_weights, on_b_grads=on_b_grads,
        on_a_out_grad=lambda g: send_grads("c", [g.reshape(N_DEV, d // N_DEV, d)]),
        on_small_grads=on_small_grads, on_a_grads=on_a_grads)
    loss = lax.psum(loss[0, 0], ("x", "y", "c"))

    after = grad_x
    for key, kind, names in (("b", "exchange", ("b_w_in", "b_w_out")), ("c", "exchange", ("a_w_out",)),
                             ("a", "reduce_chip", ("a_w_in",))):
        sems, parts, lands = flight[key]
        recv = copies_wait(kind, sems, parts, lands, after, name="exchange_" + key + "_wait")
        for name, got_parts in zip(names, recv):
            if name == "b_w_in":
                raw = adamw(got_parts, *[jnp.transpose(t[name][0]) for t in (w, m, v)], name="adamw_" + name, tc=256)
                outs = [jnp.transpose(o) for o in raw]
            else:
                raw = outs = adamw(got_parts, w[name][0], m[name][0], v[name][0], name="adamw_" + name)
            for k, vals in zip(("grad", "delta", "new_m", "new_v"), outs):
                res[k][name] = vals[None]
            after = raw[0]

    return (loss, grad_x[None], *[res["grad"][k] for k in ORDER], *[res["delta"][k] for k in ORDER],
            *[res["new_m"][k] for k in ORDER], *[res["new_v"][k] for k in ORDER])
```

```python
import functools
import math

import jax
import jax.numpy as jnp
from jax import lax
from jax.experimental import pallas as pl
from jax.experimental.pallas import tpu as pltpu

F32 = jnp.float32
BF16 = jnp.bfloat16

N_DEV = 8
LANES = 128
CHUNK = 64
SUB = 16
HALO = 8
CONV_K = 4
DEPTH = 2
DEEPNORM_ALPHA = (2.0 * DEPTH) ** 0.25
LN_EPS = 1e-5
RMS_EPS = 1e-6
L2_EPS = 1e-6
EXP_CLAMP = 60.0
ADAM_LR = 0.001
ADAM_B1 = 0.9
ADAM_B2 = 0.999
ADAM_EPS = 1e-08
ADAM_WD = 0.01
ADAM_STEP = 10
VMEM_LIMIT = 56 * 1024 * 1024
HEADS_PER_STEP = 4

NN = ((1,), (0,))
NT = ((1,), (1,))
TN = ((0,), (0,))


def _dg(a, b, dims, precision=None):
    return lax.dot_general(a, b, (dims, ((), ())), precision=precision, preferred_element_type=F32)


def _bdot(a, b, dims):
    return _dg(a.astype(BF16), b.astype(BF16), dims)


def _split3(t):
    hi = t.astype(BF16)
    r = t - hi.astype(F32)
    mid = r.astype(BF16)
    return hi, mid, (r - mid.astype(F32)).astype(BF16)


@jax.custom_vjp
def _sdot(sel, t):
    sel = sel.astype(BF16)
    hi, mid, lo = _split3(t)
    return (_dg(sel, lo, NN) + _dg(sel, mid, NN)) + _dg(sel, hi, NN)


def _sdot_fwd(sel, t):
    return _sdot(sel, t), sel


def _sdot_bwd(sel, ct):
    selb = sel.astype(BF16)
    hi, mid, lo = _split3(ct)
    return jnp.zeros_like(sel), (_dg(selb, lo, TN) + _dg(selb, mid, TN)) + _dg(selb, hi, TN)


_sdot.defvjp(_sdot_fwd, _sdot_bwd)


@jax.custom_vjp
def _tri_inv(m):
    n = m.shape[0]
    row = lax.broadcasted_iota(jnp.int32, (n, n), 0)
    col = lax.broadcasted_iota(jnp.int32, (n, n), 1)
    inv = (col == row).astype(F32) + m
    mp = m
    for _ in range(int(math.log2(CHUNK)) - 1):
        mp = _bdot(mp, mp, NN)
        inv = inv + _bdot(inv, mp, NN)
    return inv


def _tri_inv_fwd(m):
    inv = _tri_inv(m)
    return inv, inv


def _tri_inv_bwd(inv, ct):
    return (_bdot(_bdot(inv, ct, TN), inv, NT),)


_tri_inv.defvjp(_tri_inv_fwd, _tri_inv_bwd)


@jax.custom_vjp
def _tri_inv_known(m, inv):
    return inv


_tri_inv_known.defvjp(lambda m, inv: (inv, inv), lambda inv, ct: (*_tri_inv_bwd(inv, ct), jnp.zeros_like(inv)))


@jax.custom_vjp
def _bdot_cols2(a, b1, b2):
    return _bdot_cols2_fwd(a, b1, b2)[0]


def _bdot_cols2_fwd(a, b1, b2):
    both = jnp.concatenate([b1, b2], axis=1)
    r = _bdot(a, both, NN)
    return (r[:, :b1.shape[1]], r[:, b1.shape[1]:]), (a, both)


def _bdot_cols2_bwd(res, cts):
    a, both = res
    ct = jnp.concatenate(cts, axis=1)
    d_both = _bdot(a, ct, TN)
    half = both.shape[1] // 2
    return _bdot(ct, both, NT), d_both[:, :half], d_both[:, half:]


_bdot_cols2.defvjp(_bdot_cols2_fwd, _bdot_cols2_bwd)


@functools.partial(jax.custom_vjp, nondiff_argnums=(2,))
def _shift_rows(xm, xh, back):
    r = pltpu.roll(xm, back, 0)
    row = lax.broadcasted_iota(jnp.int32, xh.shape, 0)
    top = jnp.where(row < back, pltpu.roll(xh, back, 0), r[0:HALO])
    return jnp.concatenate([top, r[HALO:]], axis=0)


def _shift_rows_fwd(xm, xh, back):
    return _shift_rows(xm, xh, back), None


def _shift_rows_bwd(back, _, ct):
    row = lax.broadcasted_iota(jnp.int32, ct.shape, 0)
    dxm = jnp.where(row < CHUNK - back, pltpu.roll(ct, CHUNK - back, 0), 0.0)
    rowh = lax.broadcasted_iota(jnp.int32, (HALO, ct.shape[1]), 0)
    dxh = jnp.where(rowh >= HALO - back, pltpu.roll(ct[0:HALO], HALO - back, 0), 0.0)
    return dxm, dxh


_shift_rows.defvjp(_shift_rows_fwd, _shift_rows_bwd)


def _silu(t):
    return t * jax.nn.sigmoid(t)


def _softplus(t):
    return jnp.where(t > 20.0, t, jnp.log1p(jnp.exp(jnp.minimum(t, 20.0))))


def _cparams(sem=None):
    kw = dict(vmem_limit_bytes=VMEM_LIMIT)
    if sem is not None:
        kw["dimension_semantics"] = sem
    return pltpu.CompilerParams(**kw)


def matmul(a, b, *, name, nt=False, lhs_t=False, n=None, addend=None, alpha=1.0, out_dtype=F32, out_split=1,
           out_rows=None, into=None, after=None, tm=1024, tn=1024, tk=4096):
    k, m = a.shape[::-1] if not lhs_t else a.shape
    b_split = b.shape[0] if b.ndim == 3 else 1
    b_rows, b_cols = b.shape[-2], b.shape[-1] * b_split
    n = (b_rows if nt else b_cols) if n is None else n
    tm, tn, tk = min(tm, m), min(tn, n), min(tk, k)
    if b_split > 1:
        part = b_cols // b_split
        tn, tk = (tn, min(tk, part)) if nt else (min(tn, part), tk)
    if out_split > 1:
        tn = min(tn, n // out_split)
    assert m % tm == 0 and n % tn == 0 and k % tk == 0, (a.shape, b.shape, nt)
    assert not (lhs_t and nt)
    nk = k // tk
    dims = TN if lhs_t else (NT if nt else NN)

    def body(*refs):
        a_ref, b_ref = refs[:2]
        add_ref = None if addend is None else refs[2]

        def finish(r, o_ref):
            if add_ref is not None:
                r = r + alpha * add_ref[...].astype(F32)
            o_ref[...] = r.astype(o_ref.dtype)

        if nk == 1:
            finish(_dg(a_ref[...], b_ref[...], dims), refs[-1])
            return
        o_ref, acc_ref = refs[-2:]
        kk = pl.program_id(2)

        @pl.when(kk == 0)
        def _():
            acc_ref[...] = _dg(a_ref[...], b_ref[...], dims)

        @pl.when(kk > 0)
        def _():
            acc_ref[...] += _dg(a_ref[...], b_ref[...], dims)

        @pl.when(kk == nk - 1)
        def _():
            finish(acc_ref[...], o_ref)

    if b_split == 1:
        b_spec = (pl.BlockSpec((tn, tk), lambda i, j, kk: (j, kk)) if nt
                  else pl.BlockSpec((tk, tn), lambda i, j, kk: (kk, j)))
    elif nt:
        per = (b_cols // b_split) // tk
        b_spec = pl.BlockSpec((None, tn, tk), lambda i, j, kk: (kk // per, j, kk % per))
    else:
        per = (b_cols // b_split) // tn
        b_spec = pl.BlockSpec((None, tk, tn), lambda i, j, kk: (j // per, kk, j % per))
    a_spec = (pl.BlockSpec((tk, tm), lambda i, j, kk: (kk, i)) if lhs_t
              else pl.BlockSpec((tm, tk), lambda i, j, kk: (i, kk)))
    in_specs = [a_spec, b_spec]
    args = [a, b]
    if addend is not None:
        in_specs.append(pl.BlockSpec((tm, tn), lambda i, j, kk: (i, j)))
        args.append(addend)
    for extra in (after, None if into is None else into[0]):
        if extra is not None:
            in_specs.append(pl.BlockSpec(memory_space=pl.ANY))
            args.append(extra)
    aliases = {}
    if into is not None:
        assert out_split == 1 and into[1] % tm == 0 and into[0].dtype == out_dtype
        row0 = into[1] // tm
        out_spec = pl.BlockSpec((tm, tn), lambda i, j, kk: (row0 + i, j))
        out_shape = jax.ShapeDtypeStruct(into[0].shape, out_dtype)
        aliases = {len(args) - 1: 0}
    elif out_split == 1:
        out_spec = pl.BlockSpec((tm, tn), lambda i, j, kk: (i, j))
        out_shape = jax.ShapeDtypeStruct((m if out_rows is None else out_rows, n), out_dtype)
    else:
        per_o = (n // out_split) // tn
        out_spec = pl.BlockSpec((None, tm, tn), lambda i, j, kk: (j // per_o, i, j % per_o))
        out_shape = jax.ShapeDtypeStruct((out_split, m, n // out_split), out_dtype)
    return pl.pallas_call(
        body, name=name, grid=(m // tm, n // tn, nk),
        in_specs=in_specs,
        out_specs=out_spec,
        out_shape=out_shape,
        input_output_aliases=aliases,
        scratch_shapes=[] if nk == 1 else [pltpu.VMEM((tm, tn), F32)],
        compiler_params=_cparams(("parallel", "parallel", "arbitrary")),
    )(*args)


def _gated_rmsnorm(o, z, nw):
    r = lax.rsqrt(jnp.mean(o * o, axis=1, keepdims=True) + RMS_EPS)
    return o * r * nw * _silu(z)


def _chunk_consts():
    row = lax.broadcasted_iota(jnp.int32, (CHUNK, CHUNK), 0)
    col = lax.broadcasted_iota(jnp.int32, (CHUNK, CHUNK), 1)
    return row, col


def _head_cols(tb, groups, hp, j, rev_nb=None):
    def bb(b):
        return b if rev_nb is None else rev_nb - 1 - b
    return pl.BlockSpec((tb, hp * LANES), lambda b, h: (bb(b), j * groups + h))


def _stack(pieces):
    return jnp.concatenate(pieces, axis=0) if len(pieces) > 1 else pieces[0]


def _place(t, g):
    if g == 1:
        return t
    head = lax.broadcasted_iota(jnp.int32, t.shape, 0) // CHUNK
    return jnp.concatenate([jnp.where(head == h, t, 0.0) for h in range(g)], axis=1)


def _hgrn2_chunk(qr, fr, iv, z, a0, a1, a2, nw, st):
    g = len(qr)
    n = g * CHUNK
    row = lax.broadcasted_iota(jnp.int32, (n, n), 0)
    col = lax.broadcasted_iota(jnp.int32, (n, n), 1)
    same = (row // CHUNK) == (col // CHUNK)
    qs, ks, gl = [], [], []
    for h in range(g):
        amax = jnp.maximum(jnp.maximum(a0[h], a1[h]), a2[h])
        e0, e1, e2 = jnp.exp(a0[h] - amax), jnp.exp(a1[h] - amax), jnp.exp(a2[h] - amax)
        lb = e0 / (e0 + e1 + e2)
        forget = lb + (1.0 - lb) * jax.nn.sigmoid(fr[h])
        qs.append(_silu(qr[h]))
        ks.append(1.0 - forget)
        gl.append(jnp.log(forget))
    q, k, glog, v, zs = _stack(qs), _stack(ks), _stack(gl), _stack(list(iv)), _stack(list(z))
    cum = _sdot((same & (col <= row)).astype(F32), glog)
    sub = (row % CHUNK) // SUB
    rowl = lax.broadcasted_iota(jnp.int32, (n, LANES), 0)
    headl, subl = rowl // CHUNK, (rowl % CHUNK) // SUB

    def row_of(r):
        picks = [jnp.sum(jnp.where(rowl == h * CHUNK + r, cum, 0.0), axis=0, keepdims=True) for h in range(g)]
        out = jnp.zeros((n, LANES), F32)
        for h in range(g):
            out = out + jnp.where(headl == h, picks[h], 0.0)
        return out, picks

    refs = [jnp.zeros((n, LANES), F32)] + [row_of(i * SUB - 1)[0] for i in range(1, CHUNK // SUB)]
    own = jnp.zeros((n, LANES), F32)
    for i, ref in enumerate(refs):
        own = own + jnp.where(subl == i, ref, 0.0)
    qt = q * jnp.exp(cum - own)
    att = jnp.zeros((n, n), F32)
    for i, ref in enumerate(refs):
        kt = k * jnp.exp(jnp.minimum(ref - cum, EXP_CLAMP))
        att = att + jnp.where(same & (sub == i) & (col <= row), _bdot(qt, kt, NT), 0.0)
    cl_rows, lasts = row_of(CHUNK - 1)
    o = _bdot(att, v, NN) + _bdot(_place(q * jnp.exp(cum), g), st, NT)
    kd = k * jnp.exp(cl_rows - cum)
    cl_wide = jnp.concatenate(lasts, axis=1) if g > 1 else lasts[0]
    st_new = st * jnp.exp(cl_wide) + _bdot(v, _place(kd, g), TN)
    return _gated_rmsnorm(o, zs, nw), st_new


def hgrn2_fwd(hproj, alb, nw, *, tb, hp):
    t, d4 = hproj.shape
    d = d4 // 4
    heads = d // LANES
    groups = heads // hp
    nb, nc = t // tb, tb // CHUNK

    def body(q_ref, f_ref, i_ref, z_ref, alb_ref, nw_ref, y_ref, states_ref, st_ref):
        b, hg = pl.program_id(0), pl.program_id(1)

        @pl.when(b == 0)
        def _():
            st_ref[hg] = jnp.zeros((LANES, hp * LANES), F32)

        nwv = nw_ref[...]
        lns = [slice(p * LANES, (p + 1) * LANES) for p in range(hp)]
        albs = []
        for r in range(3):
            albs.append(tuple(alb_ref[r:r + 1, pl.ds(pl.multiple_of((hg * hp + p) * LANES, LANES), LANES)]
                              for p in range(hp)))

        def step(c, carry):
            rows = pl.ds(pl.multiple_of(c * CHUNK, CHUNK), CHUNK)
            st = st_ref[hg]
            states_ref[c, 0] = st
            ins = [tuple(ref[rows, ln] for ln in lns) for ref in (q_ref, f_ref, i_ref, z_ref)]
            y, st_new = _hgrn2_chunk(*ins, *albs, nwv, st)
            for p in range(hp):
                y_ref[rows, lns[p]] = y[p * CHUNK:(p + 1) * CHUNK].astype(y_ref.dtype)
            st_ref[hg] = st_new
            return carry

        lax.fori_loop(0, nc, step, 0)

    return pl.pallas_call(
        body, name="hgrn2_fwd", grid=(nb, groups),
        in_specs=[_head_cols(tb, groups, hp, j) for j in range(4)] + [
            pl.BlockSpec((3, d), lambda b, h: (0, 0)),
            pl.BlockSpec((1, LANES), lambda b, h: (0, 0))],
        out_specs=[pl.BlockSpec((tb, hp * LANES), lambda b, h: (b, h)),
                   pl.BlockSpec((nc, 1, LANES, hp * LANES), lambda b, h: (b, h, 0, 0))],
        out_shape=[jax.ShapeDtypeStruct((t, d), BF16),
                   jax.ShapeDtypeStruct((t // CHUNK, groups, LANES, hp * LANES), F32)],
        scratch_shapes=[pltpu.VMEM((groups, LANES, hp * LANES), F32)],
        compiler_params=_cparams(("arbitrary", "arbitrary")),
    )(hproj, hproj, hproj, hproj, alb, nw)


def hgrn2_bwd(hproj, alb, nw, states, dy, *, tb, hp):
    t, d4 = hproj.shape
    d = d4 // 4
    heads = d // LANES
    groups = heads // hp
    nb, nc = t // tb, tb // CHUNK

    def body(q_ref, f_ref, i_ref, z_ref, alb_ref, nw_ref, states_ref, dy_ref,
             dh_ref, dalb_ref, dnw_ref, dst_ref):
        b, hg = pl.program_id(0), pl.program_id(1)

        @pl.when(b == 0)
        def _():
            dst_ref[hg] = jnp.zeros((LANES, hp * LANES), F32)

        @pl.when((b == 0) & (hg == 0))
        def _():
            dalb_ref[...] = jnp.zeros_like(dalb_ref)
            dnw_ref[...] = jnp.zeros_like(dnw_ref)

        nwv = nw_ref[...]
        lns = [slice(p * LANES, (p + 1) * LANES) for p in range(hp)]
        lanes_of = [pl.ds(pl.multiple_of((hg * hp + p) * LANES, LANES), LANES) for p in range(hp)]
        albs = [tuple(alb_ref[r:r + 1, lanes_of[p]] for p in range(hp)) for r in range(3)]

        def step(i, carry):
            c = nc - 1 - i
            rows = pl.ds(pl.multiple_of(c * CHUNK, CHUNK), CHUNK)
            ins = [tuple(ref[rows, ln] for ln in lns) for ref in (q_ref, f_ref, i_ref, z_ref)]
            dy = _stack([dy_ref[rows, ln].astype(F32) for ln in lns])
            _, vjp = jax.vjp(_hgrn2_chunk, *ins, *albs, nwv, states_ref[c, 0])
            dq, df, di, dz, da0, da1, da2, dnw, dst = vjp((dy, dst_ref[hg]))
            for p in range(hp):
                h = hg * hp + p
                for j, val in enumerate((dq, df, di, dz)):
                    dh_ref[rows, pl.ds(pl.multiple_of(j * d + h * LANES, LANES), LANES)] = val[p].astype(dh_ref.dtype)
                dalb_ref[0:1, lanes_of[p]] += da0[p]
                dalb_ref[1:2, lanes_of[p]] += da1[p]
                dalb_ref[2:3, lanes_of[p]] += da2[p]
            dnw_ref[0:1, :] += dnw
            dst_ref[hg] = dst
            return carry

        lax.fori_loop(0, nc, step, 0)

    return pl.pallas_call(
        body, name="hgrn2_bwd", grid=(nb, groups),
        in_specs=[_head_cols(tb, groups, hp, j, rev_nb=nb) for j in range(4)] + [
            pl.BlockSpec((3, d), lambda b, h: (0, 0)),
            pl.BlockSpec((1, LANES), lambda b, h: (0, 0)),
            pl.BlockSpec((nc, 1, LANES, hp * LANES), lambda b, h: (nb - 1 - b, h, 0, 0)),
            pl.BlockSpec((tb, hp * LANES), lambda b, h: (nb - 1 - b, h))],
        out_specs=[pl.BlockSpec((tb, d4), lambda b, h: (nb - 1 - b, 0)),
                   pl.BlockSpec((3, d), lambda b, h: (0, 0)),
                   pl.BlockSpec((8, LANES), lambda b, h: (0, 0))],
        out_shape=[jax.ShapeDtypeStruct((t, d4), BF16),
                   jax.ShapeDtypeStruct((3, d), F32),
                   jax.ShapeDtypeStruct((8, LANES), F32)],
        scratch_shapes=[pltpu.VMEM((groups, LANES, hp * LANES), F32)],
        compiler_params=_cparams(("arbitrary", "arbitrary")),
    )(hproj, hproj, hproj, hproj, alb, nw, states, dy)


def _conv_silu(xm, xh, w):
    acc = w[CONV_K - 1] * xm
    for j in range(CONV_K - 1):
        acc = acc + w[j] * _shift_rows(xm, xh, CONV_K - 1 - j)
    return _silu(acc)


def _l2norm(t):
    return t * lax.rsqrt(jnp.sum(t * t, axis=1, keepdims=True) + L2_EPS)


def _gdn_chunk(head0, hh, known_inv, qm, qh, km, kh, vm, vh, z, tail, wq, wk, wv, alog_row, dtb_row, nw, s):
    g = len(qm)
    n = g * CHUNK
    row = lax.broadcasted_iota(jnp.int32, (n, n), 0)
    col = lax.broadcasted_iota(jnp.int32, (n, n), 1)
    same = (row // CHUNK) == (col // CHUNK)
    lane = lax.broadcasted_iota(jnp.int32, (CHUNK, LANES), 1)
    heads_lane = lax.broadcasted_iota(jnp.int32, (1, LANES), 1)
    q = _l2norm(_stack([_conv_silu(qm[h], qh[h], wq[h]) for h in range(g)])) * (LANES ** -0.5)
    k = _l2norm(_stack([_conv_silu(km[h], kh[h], wk[h]) for h in range(g)]))
    v = _stack([_conv_silu(vm[h], vh[h], wv[h]) for h in range(g)])
    zs = _stack(list(z))
    betas, gs = [], []
    for h in range(g):
        head = head0 + h
        betas.append(jax.nn.sigmoid(jnp.sum(jnp.where(lane == head, tail, 0.0), axis=1, keepdims=True)))
        a_t = jnp.sum(jnp.where(lane == hh + head, tail, 0.0), axis=1, keepdims=True)
        alog = jnp.sum(jnp.where(heads_lane == head, alog_row, 0.0), axis=1, keepdims=True)
        dtb = jnp.sum(jnp.where(heads_lane == head, dtb_row, 0.0), axis=1, keepdims=True)
        gs.append(-jnp.exp(alog) * _softplus(a_t + dtb))
    beta, gcol = _stack(betas), _stack(gs)
    tril = (same & (col <= row)).astype(F32)
    eye = (col == row).astype(F32)
    if n % LANES == 0:
        cum1 = _sdot(tril, gcol + jnp.zeros((n, LANES), F32))
        cum = jnp.concatenate([cum1] * (n // LANES), axis=1)
        lane1 = lax.broadcasted_iota(jnp.int32, (n, LANES), 1)
        cum_c = jnp.sum(jnp.where(lane1 == 0, cum1, 0.0), axis=1, keepdims=True)
    else:
        cum = _sdot(tril, gcol + jnp.zeros((n, n), F32))
        cum_c = jnp.sum(cum * eye, axis=1, keepdims=True)
    cum_r = cum.T
    diff = cum - cum_r
    strict = same & (col < row)
    incl = same & (col <= row)
    dec_incl = jnp.exp(jnp.where(incl, diff, -1e30))
    dec_strict = jnp.where(strict, dec_incl, 0.0)
    last = same & (col % CHUNK == CHUNK - 1)
    cl_rows = jnp.sum(jnp.where(last, cum_r, 0.0), axis=1, keepdims=True)
    ecum = jnp.exp(cum_c)
    m = -(beta * _bdot(k, k, NT) * dec_strict)
    inv = _tri_inv(m) if known_inv is None else _tri_inv_known(m, known_inv)
    u0, w = _bdot_cols2(inv, beta * v, (beta * ecum) * k)
    qk = _bdot(q, k, NT) * dec_incl
    u = u0 - _bdot(_place(w, g), s, NN)
    o = _bdot(_place(q * ecum, g), s, NN) + _bdot(qk, u, NN)
    kd = k * jnp.exp(cl_rows - cum_c)
    row1 = lax.broadcasted_iota(jnp.int32, (n, 1), 0)
    decay = []
    for h in range(g):
        cl_h = jnp.sum(jnp.where(row1 == h * CHUNK + CHUNK - 1, cum_c, 0.0), axis=0, keepdims=True)
        decay.append(jnp.exp(cl_h) + jnp.zeros((LANES, 1), F32))
    s_new = _stack(decay) * s + _bdot(_place(kd, g), u, TN)
    y = _gated_rmsnorm(o, zs, nw)
    return (y, s_new, inv) if known_inv is None else (y, s_new)


def _gdn_in_specs(tb, heads, hp, rev_nb=None):
    groups = heads // hp

    def bb(b):
        return b if rev_nb is None else rev_nb - 1 - b

    def halo(j):
        return pl.BlockSpec((HALO, hp * LANES),
                            lambda b, h, j=j: (jnp.maximum(bb(b) * (tb // HALO) - 1, 0), j * groups + h))

    def main(j):
        return _head_cols(tb, groups, hp, j, rev_nb=rev_nb)

    return [main(0), halo(0), main(1), halo(1), main(2), halo(2), main(3),
            pl.BlockSpec((tb, LANES), lambda b, h: (bb(b), 0)),
            pl.BlockSpec((CONV_K, 3 * heads * LANES), lambda b, h: (0, 0)),
            pl.BlockSpec((8, LANES), lambda b, h: (0, 0)),
            pl.BlockSpec((1, LANES), lambda b, h: (0, 0))]


def _gdn_chunk_args(c, head0, hp, blk, heads, q_ref, qh_ref, k_ref, kh_ref, v_ref, vh_ref, z_ref, tail_ref, cw_ref, prm_ref, nw_ref):
    d = heads * LANES
    lns = [slice(p * LANES, (p + 1) * LANES) for p in range(hp)]
    rows = pl.ds(pl.multiple_of(c * CHUNK, CHUNK), CHUNK)
    prev = pl.ds(pl.multiple_of(jnp.maximum(c * CHUNK - HALO, 0), HALO), HALO)
    first = c == 0
    live = jnp.where(first & (blk == 0), 0.0, 1.0)

    def main_of(ref):
        return tuple(ref[rows, ln] for ln in lns)

    def halo_of(ref, href):
        return tuple(jnp.where(first, href[:, ln], ref[prev, ln]) * live for ln in lns)

    def cw(j):
        out = []
        for p in range(hp):
            lanes = pl.ds(pl.multiple_of(j * d + (head0 + p) * LANES, LANES), LANES)
            out.append(tuple(cw_ref[r:r + 1, lanes] for r in range(CONV_K)))
        return tuple(out)

    return (main_of(q_ref), halo_of(q_ref, qh_ref), main_of(k_ref), halo_of(k_ref, kh_ref),
            main_of(v_ref), halo_of(v_ref, vh_ref), main_of(z_ref), tail_ref[rows, :],
            cw(0), cw(1), cw(2), prm_ref[0:1, :], prm_ref[1:2, :], nw_ref[...])


def gdn_fwd(hmain, tail, conv_w, prm, nw, *, tb, hp):
    t, d4 = hmain.shape
    d = d4 // 4
    heads = d // LANES
    groups = heads // hp
    nb, nc = t // tb, tb // CHUNK
    n = hp * CHUNK

    def body(q_ref, qh_ref, k_ref, kh_ref, v_ref, vh_ref, z_ref, tail_ref, cw_ref, prm_ref, nw_ref,
             y_ref, states_ref, inv_ref, s_ref):
        b, hg = pl.program_id(0), pl.program_id(1)

        @pl.when(b == 0)
        def _():
            s_ref[hg] = jnp.zeros((hp * LANES, LANES), F32)

        def step(c, carry):
            rows = pl.ds(pl.multiple_of(c * CHUNK, CHUNK), CHUNK)
            s = s_ref[hg]
            states_ref[c, 0] = s
            args = _gdn_chunk_args(c, hg * hp, hp, b, heads, q_ref, qh_ref, k_ref, kh_ref, v_ref, vh_ref, z_ref,
                                   tail_ref, cw_ref, prm_ref, nw_ref)
            y, s_new, inv = _gdn_chunk(hg * hp, heads, None, *args, s)
            for p in range(hp):
                y_ref[rows, p * LANES:(p + 1) * LANES] = y[p * CHUNK:(p + 1) * CHUNK].astype(y_ref.dtype)
            inv_ref[c, 0] = inv.astype(inv_ref.dtype)
            s_ref[hg] = s_new
            return carry

        lax.fori_loop(0, nc, step, 0)

    return pl.pallas_call(
        body, name="gdn_fwd", grid=(nb, groups),
        in_specs=_gdn_in_specs(tb, heads, hp),
        out_specs=[pl.BlockSpec((tb, hp * LANES), lambda b, h: (b, h)),
                   pl.BlockSpec((nc, 1, hp * LANES, LANES), lambda b, h: (b, h, 0, 0)),
                   pl.BlockSpec((nc, 1, n, n), lambda b, h: (b, h, 0, 0))],
        out_shape=[jax.ShapeDtypeStruct((t, d), BF16),
                   jax.ShapeDtypeStruct((t // CHUNK, groups, hp * LANES, LANES), F32),
                   jax.ShapeDtypeStruct((t // CHUNK, groups, n, n), BF16)],
        scratch_shapes=[pltpu.VMEM((groups, hp * LANES, LANES), F32)],
        compiler_params=_cparams(("arbitrary", "arbitrary")),
    )(hmain, hmain, hmain, hmain, hmain, hmain, hmain, tail, conv_w, prm, nw)


def gdn_bwd(hmain, tail, conv_w, prm, nw, states, invs, dy, *, tb, hp):
    t, d4 = hmain.shape
    d = d4 // 4
    heads = d // LANES
    groups = heads // hp
    nb, nc = t // tb, tb // CHUNK

    def body(q_ref, qh_ref, k_ref, kh_ref, v_ref, vh_ref, z_ref, tail_ref, cw_ref, prm_ref, nw_ref,
             states_ref, inv_ref, dy_ref, dh_ref, dtail_ref, dcw_ref, dprm_ref, ds_ref, pend_ref):
        b, hg = pl.program_id(0), pl.program_id(1)
        blk = nb - 1 - b

        @pl.when(b == 0)
        def _():
            ds_ref[hg] = jnp.zeros((hp * LANES, LANES), F32)
            for p in range(hp):
                pend_ref[hg * hp + p] = jnp.zeros((3, HALO, LANES), F32)

        @pl.when((b == 0) & (hg == 0))
        def _():
            dcw_ref[...] = jnp.zeros_like(dcw_ref)
            dprm_ref[...] = jnp.zeros_like(dprm_ref)

        @pl.when(hg == 0)
        def _():
            dtail_ref[...] = jnp.zeros_like(dtail_ref)

        def step(i, carry):
            c = nc - 1 - i
            rows = pl.ds(pl.multiple_of(c * CHUNK, CHUNK), CHUNK)
            zpad = jnp.zeros((CHUNK - HALO, LANES), F32)
            args = _gdn_chunk_args(c, hg * hp, hp, blk, heads, q_ref, qh_ref, k_ref, kh_ref, v_ref, vh_ref, z_ref,
                                   tail_ref, cw_ref, prm_ref, nw_ref)
            dy = _stack([dy_ref[rows, p * LANES:(p + 1) * LANES].astype(F32) for p in range(hp)])
            pends = [pend_ref[hg * hp + p] for p in range(hp)]
            known_inv = inv_ref[c, 0].astype(F32)
            _, vjp = jax.vjp(functools.partial(_gdn_chunk, hg * hp, heads, known_inv), *args, states_ref[c, 0])
            (dqm, dqh, dkm, dkh, dvm, dvh, dz, dtl, dwq, dwk, dwv, dalog, ddtb, dnw, ds) = vjp((dy, ds_ref[hg]))
            for p in range(hp):
                h = hg * hp + p
                for j, (dm, dhalo) in enumerate(((dqm, dqh), (dkm, dkh), (dvm, dvh))):
                    full = dm[p] + jnp.concatenate([zpad, pends[p][j]], axis=0)
                    dh_ref[rows, pl.ds(pl.multiple_of(j * d + h * LANES, LANES), LANES)] = full.astype(dh_ref.dtype)
                    pend_ref[h, j] = dhalo[p]
                dh_ref[rows, pl.ds(pl.multiple_of(3 * d + h * LANES, LANES), LANES)] = dz[p].astype(dh_ref.dtype)
                for j, dw in enumerate((dwq, dwk, dwv)):
                    lanes = pl.ds(pl.multiple_of(j * d + h * LANES, LANES), LANES)
                    for r in range(CONV_K):
                        dcw_ref[r:r + 1, lanes] += dw[p][r]
            dtail_ref[rows, :] += dtl
            dprm_ref[0:1, :] += dalog
            dprm_ref[1:2, :] += ddtb
            dprm_ref[2:3, :] += dnw
            ds_ref[hg] = ds
            return carry

        lax.fori_loop(0, nc, step, 0)

    return pl.pallas_call(
        body, name="gdn_bwd", grid=(nb, groups),
        in_specs=_gdn_in_specs(tb, heads, hp, rev_nb=nb) + [
            pl.BlockSpec((nc, 1, hp * LANES, LANES), lambda b, h: (nb - 1 - b, h, 0, 0)),
            pl.BlockSpec((nc, 1, hp * CHUNK, hp * CHUNK), lambda b, h: (nb - 1 - b, h, 0, 0)),
            pl.BlockSpec((tb, hp * LANES), lambda b, h: (nb - 1 - b, h))],
        out_specs=[pl.BlockSpec((tb, d4), lambda b, h: (nb - 1 - b, 0)),
                   pl.BlockSpec((tb, LANES), lambda b, h: (nb - 1 - b, 0)),
                   pl.BlockSpec((CONV_K, 3 * d), lambda b, h: (0, 0)),
                   pl.BlockSpec((8, LANES), lambda b, h: (0, 0))],
        out_shape=[jax.ShapeDtypeStruct((t, d4), BF16),
                   jax.ShapeDtypeStruct((t, LANES), F32),
                   jax.ShapeDtypeStruct((CONV_K, 3 * d), F32),
                   jax.ShapeDtypeStruct((8, LANES), F32)],
        scratch_shapes=[pltpu.VMEM((groups, hp * LANES, LANES), F32),
                        pltpu.VMEM((heads, 3, HALO, LANES), F32)],
        compiler_params=_cparams(("arbitrary", "arbitrary")),
    )(hmain, hmain, hmain, hmain, hmain, hmain, hmain, tail, conv_w, prm, nw, states, invs, dy)


def _layer_norm(u, g, b):
    mu = jnp.mean(u, axis=1, keepdims=True)
    var = jnp.mean(jnp.square(u - mu), axis=1, keepdims=True)
    return (u - mu) * lax.rsqrt(var + LN_EPS) * g + b


def ln_fwd(u, g, b, *, tr):
    t, d = u.shape

    def body(u_ref, g_ref, b_ref, x_ref, xb_ref):
        x = _layer_norm(u_ref[...], g_ref[...], b_ref[...])
        x_ref[...] = x
        xb_ref[...] = x.astype(BF16)

    row = pl.BlockSpec((tr, d), lambda i: (i, 0))
    vec = pl.BlockSpec((1, d), lambda i: (0, 0))
    return pl.pallas_call(
        body, name="ln_fwd", grid=(t // tr,), in_specs=[row, vec, vec], out_specs=[row, row],
        out_shape=[jax.ShapeDtypeStruct((t, d), F32), jax.ShapeDtypeStruct((t, d), BF16)],
        compiler_params=_cparams(("parallel",)),
    )(u, g, b)


def ln_bwd(u, g, b, dout, *, tr):
    t, d = u.shape

    def body(u_ref, g_ref, b_ref, dout_ref, du_ref, dub_ref, dg_ref, db_ref):
        @pl.when(pl.program_id(0) == 0)
        def _():
            dg_ref[...] = jnp.zeros_like(dg_ref)
            db_ref[...] = jnp.zeros_like(db_ref)

        _, vjp = jax.vjp(_layer_norm, u_ref[...], g_ref[...], b_ref[...])
        du, dg, db = vjp(dout_ref[...])
        du_ref[...] = du
        dub_ref[...] = du.astype(BF16)
        dg_ref[0:1, :] += dg
        db_ref[0:1, :] += db

    row = pl.BlockSpec((tr, d), lambda i: (i, 0))
    vec = pl.BlockSpec((1, d), lambda i: (0, 0))
    acc = pl.BlockSpec((8, d), lambda i: (0, 0))
    return pl.pallas_call(
        body, name="ln_bwd", grid=(t // tr,), in_specs=[row, vec, vec, row], out_specs=[row, row, acc, acc],
        out_shape=[jax.ShapeDtypeStruct((t, d), F32), jax.ShapeDtypeStruct((t, d), BF16),
                   jax.ShapeDtypeStruct((8, d), F32), jax.ShapeDtypeStruct((8, d), F32)],
        compiler_params=_cparams(("arbitrary",)),
    )(u, g, b, dout)


def ln_loss_bwd(u, g, b, target, *, tr):
    t, d = u.shape

    def loss_of(uu, gg, bb, tgt):
        err = jnp.square(_layer_norm(uu, gg, bb) - tgt)
        return 0.5 * jnp.sum(jnp.mean(err, axis=1, keepdims=True), axis=0, keepdims=True)

    def body(u_ref, g_ref, b_ref, t_ref, loss_ref, du_ref, dub_ref, dg_ref, db_ref):
        @pl.when(pl.program_id(0) == 0)
        def _():
            loss_ref[...] = jnp.zeros_like(loss_ref)
            dg_ref[...] = jnp.zeros_like(dg_ref)
            db_ref[...] = jnp.zeros_like(db_ref)

        tgt = t_ref[...]
        val, vjp = jax.vjp(lambda uu, gg, bb: loss_of(uu, gg, bb, tgt), u_ref[...], g_ref[...], b_ref[...])
        du, dg, db = vjp(jnp.ones((1, 1), F32))
        loss_ref[...] += val
        du_ref[...] = du
        dub_ref[...] = du.astype(BF16)
        dg_ref[0:1, :] += dg
        db_ref[0:1, :] += db

    row = pl.BlockSpec((tr, d), lambda i: (i, 0))
    vec = pl.BlockSpec((1, d), lambda i: (0, 0))
    acc = pl.BlockSpec((8, d), lambda i: (0, 0))
    return pl.pallas_call(
        body, name="ln_loss_bwd", grid=(t // tr,), in_specs=[row, vec, vec, row],
        out_specs=[pl.BlockSpec((8, LANES), lambda i: (0, 0)), row, row, acc, acc],
        out_shape=[jax.ShapeDtypeStruct((8, LANES), F32),
                   jax.ShapeDtypeStruct((t, d), F32), jax.ShapeDtypeStruct((t, d), BF16),
                   jax.ShapeDtypeStruct((8, d), F32), jax.ShapeDtypeStruct((8, d), F32)],
        compiler_params=_cparams(("arbitrary",)),
    )(u, g, b, target)


def local_step(x, target, wa_in, alb, a_nw, wa_out, wb_t, wb_tail_t, conv_w, a_log, dt_bias, b_nw, wb_out,
               ln_g, ln_b, *, tb=256, tr=256, hp=HEADS_PER_STEP, first_after=None, b_weights=None, on_b_grads=None,
               on_a_out_grad=None, on_a_grads=None, on_mid=None, on_small_grads=None):
    t, d = x.shape
    heads = d // LANES
    tb, tr, hp = min(tb, t), min(tr, t), min(hp, heads)
    xb = x.astype(BF16)
    prm = jnp.zeros((8, LANES), F32).at[0, :heads].set(a_log[0]).at[1, :heads].set(dt_bias[0])

    ha = matmul(xb, wa_in, name="mm_a_in", after=first_after)
    ya, st_a = hgrn2_fwd(ha, alb, a_nw, tb=tb, hp=hp)
    u1 = matmul(ya, wa_out, name="mm_a_out", addend=x, alpha=DEEPNORM_ALPHA,
                after=None if on_mid is None else on_mid(ya))
    x1, x1b = ln_fwd(u1, ln_g[0:1], ln_b[0:1], tr=tr)
    if b_weights is not None:
        wb_t, wb_tail_t, wb_out, conv_w = b_weights(x1b)
    n_tail = 2 * heads
    hb = matmul(x1b, wb_t, name="mm_b_in", nt=True, n=4 * d)
    tl = matmul(x1b, wb_tail_t, name="mm_b_tail", nt=True)
    yb, st_b, inv_b = gdn_fwd(hb, tl, conv_w, prm, b_nw, tb=tb, hp=hp)
    u2 = matmul(yb, wb_out, name="mm_b_out", addend=x1, alpha=DEEPNORM_ALPHA)

    loss, du2, du2b, dg2, db2 = ln_loss_bwd(u2, ln_g[1:2], ln_b[1:2], target, tr=tr)
    d_wb_out = matmul(yb, du2b, name="mm_dwb_out", lhs_t=True, out_dtype=BF16)
    dyb = matmul(du2b, wb_out, name="mm_dyb", nt=True)
    dhb, dtl, d_conv, dprm = gdn_bwd(hb, tl, conv_w, prm, b_nw, st_b, inv_b, dyb, tb=tb, hp=hp)
    dtlb = dtl.astype(BF16)
    d_wb_t = matmul(dhb, x1b, name="mm_dwb_main", lhs_t=True, out_dtype=BF16, out_rows=4 * d + n_tail)
    d_wb_t = matmul(dtlb[:, :n_tail], x1b, name="mm_dwb_tail", lhs_t=True, out_dtype=BF16, into=(d_wb_t, 4 * d))
    sent_b = on_b_grads(d_wb_t, d_wb_out) if on_b_grads is not None else None
    dx1_tail = matmul(dtlb, wb_tail_t, name="mm_dx1_tail", addend=du2, alpha=DEEPNORM_ALPHA, after=sent_b)
    dx1 = matmul(dhb, wb_t, name="mm_dx1", addend=dx1_tail, alpha=1.0, tk=2048)
    du1, du1b, dg1, db1 = ln_bwd(u1, ln_g[0:1], ln_b[0:1], dx1, tr=tr)
    d_wa_out = matmul(ya, du1b, name="mm_dwa_out", lhs_t=True, out_dtype=BF16)
    sent_c = on_a_out_grad(d_wa_out) if on_a_out_grad is not None else None
    dya = matmul(du1b, wa_out, name="mm_dya", nt=True, after=sent_c)
    dha, d_alb, d_anw = hgrn2_bwd(ha, alb, a_nw, st_a, dya, tb=tb, hp=hp)
    small = dict(
        a_lower_bounds=d_alb, a_norm_w=d_anw[0:1], b_conv_w=d_conv,
        b_a_log=dprm[0:1, :heads], b_dt_bias=dprm[1:2, :heads], b_norm_w=dprm[2:3],
        ln_g=jnp.concatenate([dg1[0:1], dg2[0:1]], axis=0), ln_b=jnp.concatenate([db1[0:1], db2[0:1]], axis=0))
    d_wa_in = matmul(xb, dha, name="mm_dwa_in", lhs_t=True, out_dtype=BF16,
                     out_split=wa_in.shape[0] if wa_in.ndim == 3 else 1,
                     after=None if on_small_grads is None else on_small_grads(small))
    sent_a = on_a_grads(d_wa_in) if on_a_grads is not None else None
    grad_x = matmul(dha, wa_in, name="mm_dx", nt=True, addend=du1, alpha=DEEPNORM_ALPHA, after=sent_a, tk=2048)
    big = dict(a_w_in=d_wa_in, a_w_out=d_wa_out, b_w_t=d_wb_t, b_w_out=d_wb_out)
    return loss, grad_x, big, small


MESH_ID = pl.DeviceIdType.MESH


def _position():
    return lax.axis_index("x"), lax.axis_index("y"), lax.axis_index("c")


def _index_of(p):
    return 4 * p[0] + 2 * p[1] + p[2]


def all_gather(shards, *, name, space):
    n = len(shards)

    def body(*refs):
        ins, outs = refs[:n], refs[n:2 * n]
        send_sems, recv_sems, local_sems = refs[2 * n:]
        x, y, c = _position()
        me, sibling = (x, y, c), (x, y, 1 - c)
        chips = [(1 - x, y), (x, 1 - y), (1 - x, 1 - y)]

        def copy(a, k, block, to, own=False):
            dst = outs[a].at[_index_of(block)]
            return pltpu.make_async_remote_copy(
                src_ref=ins[a] if own else dst, dst_ref=dst,
                send_sem=send_sems.at[7 * a + k], recv_sem=recv_sems.at[7 * a + k],
                device_id=to, device_id_type=MESH_ID)

        mine = [pltpu.make_async_copy(ins[a], outs[a].at[_index_of(me)], local_sems.at[a]) for a in range(n)]
        for cp in mine:
            cp.start()
        first = []
        for a in range(n):
            first.append(copy(a, 0, me, sibling, own=True))
            first += [copy(a, 1 + j, me, (*chip, c), own=True) for j, chip in enumerate(chips)]
        for cp in first:
            cp.start()
        passed = []
        for j, chip in enumerate(chips):
            for a in range(n):
                copy(a, 1 + j, (*chip, c), me).wait_recv()
                fwd = copy(a, 4 + j, (*chip, c), sibling)
                fwd.start()
                passed.append(fwd)
        for a in range(n):
            copy(a, 0, sibling, me).wait_recv()
            for j, chip in enumerate(chips):
                copy(a, 4 + j, (*chip, 1 - c), me).wait_recv()
        for cp in first + passed:
            cp.wait_send()
        for cp in mine:
            cp.wait()

    spec = pl.BlockSpec(memory_space=space)
    return pl.pallas_call(
        body, name=name,
        in_specs=[spec] * n, out_specs=[spec] * n,
        out_shape=[jax.ShapeDtypeStruct((N_DEV, *s.shape), s.dtype) for s in shards],
        scratch_shapes=[pltpu.SemaphoreType.DMA((7 * n,)), pltpu.SemaphoreType.DMA((7 * n,)),
                        pltpu.SemaphoreType.DMA((n,))],
        compiler_params=pltpu.CompilerParams(vmem_limit_bytes=VMEM_LIMIT),
    )(*shards)


def all_gather_relay(shards, *, name):
    n, per = len(shards), 9

    def body(*refs):
        ins, outs = refs[:n], refs[n:2 * n]
        send_sems, recv_sems, local_sems = refs[2 * n:]
        x, y, c = _position()
        me, sibling = (x, y, c), (x, y, 1 - c)
        xn, yn, dg = (1 - x, y), (x, 1 - y), (1 - x, 1 - y)

        def parts(a):
            rows = shards[a].shape[0]
            return [(0, rows // 2), (rows // 2, rows // 2)] if rows % 32 == 0 else [(0, rows)]

        def copy(a, k, block, to, own=False, part=None):
            dst = outs[a].at[_index_of(block)]
            src = ins[a] if own else dst
            if part is not None:
                src, dst = src.at[pl.ds(*part)], dst.at[pl.ds(*part)]
            return pltpu.make_async_remote_copy(
                src_ref=src, dst_ref=dst, send_sem=send_sems.at[per * a + k], recv_sem=recv_sems.at[per * a + k],
                device_id=to, device_id_type=MESH_ID)

        mine = [pltpu.make_async_copy(ins[a], outs[a].at[_index_of(me)], local_sems.at[a]) for a in range(n)]
        for cp in mine:
            cp.start()
        started = []
        for a in range(n):
            started += [copy(a, 0, me, sibling, own=True), copy(a, 1, me, (*xn, c), own=True),
                        copy(a, 2, me, (*yn, c), own=True)]
        for cp in started:
            cp.start()

        def go(cp):
            cp.start()
            started.append(cp)

        for a in range(n):
            copy(a, 1, (*xn, c), me).wait_recv()
            go(copy(a, 3, (*xn, c), sibling))
            go(copy(a, 5, (*xn, c), (*yn, c), part=parts(a)[0]))
        for a in range(n):
            copy(a, 2, (*yn, c), me).wait_recv()
            go(copy(a, 4, (*yn, c), sibling))
            if len(parts(a)) == 2:
                go(copy(a, 6, (*yn, c), (*xn, c), part=parts(a)[1]))
        for a in range(n):
            for k, part in zip((5, 6), parts(a)):
                copy(a, k, (*dg, c), me, part=part).wait_recv()
                go(copy(a, k + 2, (*dg, c), sibling, part=part))
        for a in range(n):
            copy(a, 0, sibling, me).wait_recv()
            copy(a, 3, (*xn, 1 - c), me).wait_recv()
            copy(a, 4, (*yn, 1 - c), me).wait_recv()
            for k, part in zip((7, 8), parts(a)):
                copy(a, k, (*dg, 1 - c), me, part=part).wait_recv()
        for cp in started:
            cp.wait_send()
        for cp in mine:
            cp.wait()

    return pl.pallas_call(
        body, name=name,
        in_specs=[HBM_SPEC] * n, out_specs=[HBM_SPEC] * n,
        out_shape=[jax.ShapeDtypeStruct((N_DEV, *s.shape), s.dtype) for s in shards],
        scratch_shapes=[pltpu.SemaphoreType.DMA((per * n,)), pltpu.SemaphoreType.DMA((per * n,)),
                        pltpu.SemaphoreType.DMA((n,))],
        compiler_params=pltpu.CompilerParams(vmem_limit_bytes=VMEM_LIMIT),
    )(*shards)


HBM_SPEC = pl.BlockSpec(memory_space=pltpu.HBM)
SEM_SPEC = pl.BlockSpec(memory_space=pltpu.SEMAPHORE)


CHIP_PEERS = (4, 2, 6)
COPIES_PER_ARRAY = dict(gather=7, exchange=7, gather_near=4, gather_pass=3, reduce_pair=4, reduce_chip=3)


def _planned_copies(kind, srcs, lands, send_sems, recv_sems):
    x, y, c = _position()
    me, my_chip = _index_of((x, y, c)), 2 * x + y
    per = COPIES_PER_ARRAY[kind]
    copies = []

    def peer_of(r):
        return (x ^ ((r >> 2) & 1), y ^ ((r >> 1) & 1), c ^ (r & 1))

    def add(a, j, src, dst, peer):
        copies.append(pltpu.make_async_remote_copy(
            src_ref=src, dst_ref=dst, send_sem=send_sems.at[per * a + j], recv_sem=recv_sems.at[per * a + j],
            device_id=peer, device_id_type=MESH_ID))

    for a, land in enumerate(lands):
        src = srcs[a] if srcs else land
        if kind == "gather":
            for r in range(1, N_DEV):
                add(a, r - 1, src, land.at[me], peer_of(r))
        elif kind == "exchange":
            for r in range(1, N_DEV):
                add(a, r - 1, src.at[_index_of(peer_of(r))], land.at[me], peer_of(r))
        elif kind == "gather_near":
            for j, r in enumerate((1,) + CHIP_PEERS):
                add(a, j, src, land.at[me], peer_of(r))
        elif kind == "gather_pass":
            for j, r in enumerate(CHIP_PEERS):
                slot = _index_of(peer_of(r))
                add(a, j, land.at[slot], land.at[slot], peer_of(1))
        elif kind == "reduce_pair":
            for q in range(4):
                add(a, q, src.at[2 * q + (1 - c)], land.at[q], peer_of(1))
        elif kind == "reduce_chip":
            for j, r in enumerate(CHIP_PEERS):
                p = peer_of(r)
                add(a, j, src.at[2 * p[0] + p[1]], land.at[my_chip], p)
    return copies


def copies_start(kind, srcs, lands, *, name):
    ns, n = len(srcs), len(srcs) + len(lands)
    n_sem = COPIES_PER_ARRAY[kind] * len(lands)

    def body(*refs):
        for cp in _planned_copies(kind, refs[:ns], refs[ns:n], refs[n], refs[n + 1]):
            cp.start()
        refs[-1][...] = jnp.zeros_like(refs[-1])

    arrays = [pltpu.with_memory_space_constraint(t, pltpu.HBM) for t in (*srcs, *lands)]
    outs = pl.pallas_call(
        body, name=name,
        out_shape=(pltpu.SemaphoreType.DMA((n_sem,)), pltpu.SemaphoreType.DMA((n_sem,)),
                   *[pltpu.HBM(t.shape, t.dtype) for t in arrays], jax.ShapeDtypeStruct((8, LANES), F32)),
        in_specs=[HBM_SPEC] * n,
        out_specs=(SEM_SPEC, SEM_SPEC, *[HBM_SPEC] * n, pl.BlockSpec(memory_space=pltpu.VMEM)),
        input_output_aliases={i: 2 + i for i in range(n)},
        compiler_params=pltpu.CompilerParams(has_side_effects=pltpu.SideEffectType.DATAFLOW_SIDE_EFFECTING),
    )(*arrays)
    return (outs[0], outs[1]), list(outs[2:2 + ns]), list(outs[2 + ns:2 + n]), outs[-1]


def copies_wait(kind, sems, srcs, lands, after, *, name):
    ns, n = len(srcs), len(srcs) + len(lands)

    def body(*refs):
        for cp in _planned_copies(kind, refs[:ns], refs[ns:n], refs[n], refs[n + 1]):
            cp.wait_send()
            cp.wait_recv()

    outs = pl.pallas_call(
        body, name=name,
        out_shape=tuple(pltpu.HBM(t.shape, t.dtype) for t in (*srcs, *lands)),
        in_specs=[HBM_SPEC] * n + [SEM_SPEC, SEM_SPEC, pl.BlockSpec(memory_space=pl.ANY)],
        out_specs=tuple([HBM_SPEC] * n),
        input_output_aliases={i: i for i in range(n)},
        compiler_params=pltpu.CompilerParams(has_side_effects=pltpu.SideEffectType.DATAFLOW_SIDE_EFFECTING),
    )(*srcs, *lands, sems[0], sems[1], after)
    return list(outs[ns:])


def _landing(own, slot, slots=N_DEV):
    return lax.dynamic_update_slice_in_dim(lax.empty((slots, *own.shape), own.dtype), own[None], slot, 0)


def pair_sum(a, b, *, name, tr=256):
    p, r, c = a.shape
    tr = min(tr, r)

    def body(a_ref, b_ref, o_ref):
        o_ref[...] = (a_ref[...].astype(F32) + b_ref[...].astype(F32)).astype(o_ref.dtype)

    blk = pl.BlockSpec((1, tr, c), lambda i, j: (i, j, 0))
    return pl.pallas_call(
        body, name=name, grid=(p, r // tr), in_specs=[blk, blk], out_specs=blk,
        out_shape=jax.ShapeDtypeStruct(a.shape, a.dtype), compiler_params=_cparams(("parallel", "parallel")),
    )(a, b)


def adamw(parts, w, m, v, *, name, tr=64, tc=None):
    p, r, c = parts.shape
    tr = r if tc is not None else min(tr, r)
    tc = c if tc is None else tc
    assert r % tr == 0 and c % tc == 0
    c1 = 1.0 / (1.0 - ADAM_B1 ** ADAM_STEP)
    c2 = 1.0 / (1.0 - ADAM_B2 ** ADAM_STEP)

    def body(p_ref, w_ref, m_ref, v_ref, g_ref, d_ref, nm_ref, nv_ref):
        g = p_ref[0].astype(F32)
        for i in range(1, p):
            g = g + p_ref[i].astype(F32)
        nm = ADAM_B1 * m_ref[...] + (1.0 - ADAM_B1) * g
        nv = ADAM_B2 * v_ref[...] + (1.0 - ADAM_B2) * jnp.square(g)
        g_ref[...] = g
        nm_ref[...] = nm
        nv_ref[...] = nv
        d_ref[...] = -ADAM_LR * ((nm * c1) / (jnp.sqrt(nv * c2) + ADAM_EPS) + ADAM_WD * w_ref[...])

    blk = pl.BlockSpec((tr, tc), lambda i, j: (i, j))
    out = jax.ShapeDtypeStruct((r, c), F32)
    return pl.pallas_call(
        body, name=name, grid=(r // tr, c // tc),
        in_specs=[pl.BlockSpec((p, tr, tc), lambda i, j: (0, i, j)), blk, blk, blk],
        out_specs=[blk] * 4, out_shape=[out] * 4,
        compiler_params=_cparams(("parallel", "parallel")),
    )(parts, w, m, v)


def _pack(d, vals):
    heads = d // LANES
    vecs = jnp.zeros((8, LANES), F32)
    vecs = vecs.at[0:1].set(vals["a_norm_w"]).at[1:2, :heads].set(vals["b_a_log"])
    vecs = vecs.at[2:3, :heads].set(vals["b_dt_bias"]).at[3:4].set(vals["b_norm_w"])
    rows = [vals["a_lower_bounds"].reshape(-1, LANES), vals["ln_g"].reshape(-1, LANES),
            vals["ln_b"].reshape(-1, LANES), vecs]
    return jnp.concatenate(rows, axis=0)


def _unpack(d, packed):
    heads = d // LANES
    n3, n2 = 3 * heads, 2 * heads
    o = 0
    out = {}
    out["a_lower_bounds"] = packed[o:o + n3].reshape(3, d); o += n3
    out["ln_g"] = packed[o:o + n2].reshape(2, d); o += n2
    out["ln_b"] = packed[o:o + n2].reshape(2, d); o += n2
    out["a_norm_w"] = packed[o:o + 1]
    out["b_a_log"] = packed[o + 1:o + 2, :heads]
    out["b_dt_bias"] = packed[o + 2:o + 3, :heads]
    out["b_norm_w"] = packed[o + 3:o + 4]
    return out


ORDER = ("a_w_in", "a_lower_bounds", "a_norm_w", "a_w_out", "b_w_in", "b_conv_w", "b_a_log", "b_dt_bias", "b_norm_w",
         "b_w_out", "ln_g", "ln_b")


def kernel(x, a_w_in, a_lower_bounds, a_norm_w, a_w_out, b_w_in, b_conv_w, b_a_log, b_dt_bias, b_norm_w, b_w_out, ln_g, ln_b, loss_target, m_a_w_in, m_a_lower_bounds, m_a_norm_w, m_a_w_out, m_b_w_in, m_b_conv_w, m_b_a_log, m_b_dt_bias, m_b_norm_w, m_b_w_out, m_ln_g, m_ln_b, v_a_w_in, v_a_lower_bounds, v_a_norm_w, v_a_w_out, v_b_w_in, v_b_conv_w, v_b_a_log, v_b_dt_bias, v_b_norm_w, v_b_w_out, v_ln_g, v_ln_b):
    w = dict(a_w_in=a_w_in, a_lower_bounds=a_lower_bounds, a_norm_w=a_norm_w, a_w_out=a_w_out, b_w_in=b_w_in,
             b_conv_w=b_conv_w, b_a_log=b_a_log, b_dt_bias=b_dt_bias, b_norm_w=b_norm_w, b_w_out=b_w_out, ln_g=ln_g, ln_b=ln_b)
    m = dict(a_w_in=m_a_w_in, a_lower_bounds=m_a_lower_bounds, a_norm_w=m_a_norm_w, a_w_out=m_a_w_out, b_w_in=m_b_w_in,
             b_conv_w=m_b_conv_w, b_a_log=m_b_a_log, b_dt_bias=m_b_dt_bias, b_norm_w=m_b_norm_w, b_w_out=m_b_w_out,
             ln_g=m_ln_g, ln_b=m_ln_b)
    v = dict(a_w_in=v_a_w_in, a_lower_bounds=v_a_lower_bounds, a_norm_w=v_a_norm_w, a_w_out=v_a_w_out, b_w_in=v_b_w_in,
             b_conv_w=v_b_conv_w, b_a_log=v_b_a_log, b_dt_bias=v_b_dt_bias, b_norm_w=v_b_norm_w, b_w_out=v_b_w_out,
             ln_g=v_ln_g, ln_b=v_ln_b)
    t, d = x.shape[1], x.shape[2]
    heads = d // LANES
    n_tail = 2 * heads
    me = _index_of(_position())

    ga_in, ga_out, g_conv = all_gather_relay([a_w_in[0].astype(BF16), a_w_out[0].astype(BF16), b_conv_w[0]],
                                             name="gather_weights_a")
    b_shards = [jnp.transpose(b_w_in[0]).astype(BF16), b_w_out[0].astype(BF16)]
    flight = {}
    sems_w, b_shards, lands_w, token_w = copies_start("gather_near", b_shards, [_landing(s, me) for s in b_shards],
                                                      name="gather_weights_b_start")
    wa_out = ga_out.reshape(d, d)
    conv_w = jnp.transpose(g_conv, (1, 0, 2)).reshape(CONV_K, 3 * d)

    def on_mid(after):
        lands = copies_wait("gather_near", sems_w, b_shards, lands_w, after, name="gather_weights_b_wait")
        sems, _, lands, token = copies_start("gather_pass", [], lands, name="pass_weights_b_start")
        flight["w"] = (sems, lands)
        return token

    def b_weights(after):
        sems, lands = flight["w"]
        gb_in, gb_out = copies_wait("gather_pass", sems, [], lands, after, name="pass_weights_b_wait")
        wb_t = gb_in.reshape(4 * d + n_tail, d)
        wb_tail_t = jnp.concatenate([wb_t[4 * d:], jnp.zeros((LANES - n_tail, d), BF16)], axis=0)
        return wb_t, wb_tail_t, gb_out.reshape(d, d), conv_w

    def send_grads(key, parts):
        own = [lax.dynamic_index_in_dim(p, me, 0, keepdims=False) for p in parts]
        sems, parts, lands, token = copies_start("exchange", parts, [_landing(o, me) for o in own],
                                                 name="exchange_" + key + "_start")
        flight[key] = (sems, parts, lands)
        return token

    def on_b_grads(d_wb_t, d_out):
        return send_grads("b", [d_wb_t.reshape(N_DEV, -1, d), d_out.reshape(N_DEV, d // N_DEV, d)])

    res = {}

    def on_small_grads(small):
        conv_rows = small["b_conv_w"].reshape(-1, LANES)
        n_conv = conv_rows.shape[0]
        packed_grads = jnp.concatenate([conv_rows, _pack(d, small)], axis=0)
        (got,) = all_gather([packed_grads], name="gather_small", space=pltpu.VMEM)
        packed = adamw(got[:, n_conv:], _pack(d, w), _pack(d, m), _pack(d, v), name="adamw_small", tr=4096)
        for k, vals in zip(("grad", "delta", "new_m", "new_v"), packed):
            res[k] = _unpack(d, vals)
        shard_ch = 3 * d // N_DEV
        conv_parts = lax.dynamic_slice_in_dim(got[:, :n_conv].reshape(N_DEV, CONV_K, 3 * d), me * shard_ch, shard_ch,
                                              axis=2)
        conv_out = adamw(conv_parts, b_conv_w[0], m_b_conv_w[0], v_b_conv_w[0], name="adamw_conv")
        for k, vals in zip(("grad", "delta", "new_m", "new_v"), conv_out):
            res[k]["b_conv_w"] = vals[None]
        return conv_out[0]

    my_chip, my_core = me // 2, me % 2

    def on_a_grads(d_in):
        mine = lax.dynamic_index_in_dim(d_in.reshape(N_DEV // 2, 2, *d_in.shape[1:]), my_core, 1, keepdims=False)
        land1 = lax.empty(mine.shape, mine.dtype)
        sems, (d_in,), (land1,), token = copies_start("reduce_pair", [d_in], [land1], name="reduce_a_pair_start")
        (land1,) = copies_wait("reduce_pair", sems, [d_in], [land1], token, name="reduce_a_pair_wait")
        summed = pair_sum(mine, land1, name="reduce_a_pair_sum")
        own = lax.dynamic_index_in_dim(summed, my_chip, 0, keepdims=False)
        sems, (summed,), (land2,), token = copies_start("reduce_chip", [summed], [_landing(own, my_chip, N_DEV // 2)],
                                                        name="reduce_a_chip_start")
        flight["a"] = (sems, [summed], [land2])
        return token

    loss, grad_x, big, small = local_step(
        x[0], loss_target[0], ga_in, a_lower_bounds, a_norm_w, wa_out, None, None, None, b_a_log, b_dt_bias,
        b_norm_w, None, ln_g, ln_b, first_after=token_w, on_mid=on_mid, b_weights=b_weights, on_b_grads=on_b_grads,
        on_a_out_grad=lambda g: send_grads("c", [g.reshape(N_DEV, d // N_DEV, d)]),
        on_small_grads=on_small_grads, on_a_grads=on_a_grads)
    loss = lax.psum(loss[0, 0], ("x", "y", "c"))

    after = grad_x
    for key, kind, names in (("b", "exchange", ("b_w_in", "b_w_out")), ("c", "exchange", ("a_w_out",)),
                             ("a", "reduce_chip", ("a_w_in",))):
        sems, parts, lands = flight[key]
        recv = copies_wait(kind, sems, parts, lands, after, name="exchange_" + key + "_wait")
        for name, got_parts in zip(names, recv):
            if name == "b_w_in":
                raw = adamw(got_parts, *[jnp.transpose(t[name][0]) for t in (w, m, v)], name="adamw_" + name, tc=256)
                outs = [jnp.transpose(o) for o in raw]
            else:
                raw = outs = adamw(got_parts, w[name][0], m[name][0], v[name][0], name="adamw_" + name)
            for k, vals in zip(("grad", "delta", "new_m", "new_v"), outs):
                res[k][name] = vals[None]
            after = raw[0]

    return (loss, grad_x[None], *[res["grad"][k] for k in ORDER], *[res["delta"][k] for k in ORDER],
            *[res["new_m"][k] for k in ORDER], *[res["new_v"][k] for k in ORDER])
```

```python
import functools
import math

import jax
import jax.numpy as jnp
from jax import lax
from jax.experimental import pallas as pl
from jax.experimental.pallas import tpu as pltpu

F32 = jnp.float32
BF16 = jnp.bfloat16

N_DEV = 8
LANES = 128
CHUNK = 64
SUB = 16
HALO = 8
CONV_K = 4
DEPTH = 2
DEEPNORM_ALPHA = (2.0 * DEPTH) ** 0.25
LN_EPS = 1e-5
RMS_EPS = 1e-6
L2_EPS = 1e-6
EXP_CLAMP = 60.0
ADAM_LR = 0.001
ADAM_B1 = 0.9
ADAM_B2 = 0.999
ADAM_EPS = 1e-08
ADAM_WD = 0.01
ADAM_STEP = 10
VMEM_LIMIT = 56 * 1024 * 1024
HEADS_PER_STEP = 4

NN = ((1,), (0,))
NT = ((1,), (1,))
TN = ((0,), (0,))


def _dg(a, b, dims, precision=None):
    return lax.dot_general(a, b, (dims, ((), ())), precision=precision, preferred_element_type=F32)


def _bdot(a, b, dims):
    return _dg(a.astype(BF16), b.astype(BF16), dims)


def _split3(t):
    hi = t.astype(BF16)
    r = t - hi.astype(F32)
    mid = r.astype(BF16)
    return hi, mid, (r - mid.astype(F32)).astype(BF16)


@jax.custom_vjp
def _sdot(sel, t):
    sel = sel.astype(BF16)
    hi, mid, lo = _split3(t)
    return (_dg(sel, lo, NN) + _dg(sel, mid, NN)) + _dg(sel, hi, NN)


def _sdot_fwd(sel, t):
    return _sdot(sel, t), sel


def _sdot_bwd(sel, ct):
    selb = sel.astype(BF16)
    hi, mid, lo = _split3(ct)
    return jnp.zeros_like(sel), (_dg(selb, lo, TN) + _dg(selb, mid, TN)) + _dg(selb, hi, TN)


_sdot.defvjp(_sdot_fwd, _sdot_bwd)


@jax.custom_vjp
def _tri_inv(m):
    n = m.shape[0]
    row = lax.broadcasted_iota(jnp.int32, (n, n), 0)
    col = lax.broadcasted_iota(jnp.int32, (n, n), 1)
    inv = (col == row).astype(F32) + m
    mp = m
    for _ in range(int(math.log2(CHUNK)) - 1):
        mp = _bdot(mp, mp, NN)
        inv = inv + _bdot(inv, mp, NN)
    return inv


def _tri_inv_fwd(m):
    inv = _tri_inv(m)
    return inv, inv


def _tri_inv_bwd(inv, ct):
    return (_bdot(_bdot(inv, ct, TN), inv, NT),)


_tri_inv.defvjp(_tri_inv_fwd, _tri_inv_bwd)


@jax.custom_vjp
def _tri_inv_known(m, inv):
    return inv


_tri_inv_known.defvjp(lambda m, inv: (inv, inv), lambda inv, ct: (*_tri_inv_bwd(inv, ct), jnp.zeros_like(inv)))


@jax.custom_vjp
def _bdot_cols2(a, b1, b2):
    return _bdot_cols2_fwd(a, b1, b2)[0]


def _bdot_cols2_fwd(a, b1, b2):
    both = jnp.concatenate([b1, b2], axis=1)
    r = _bdot(a, both, NN)
    return (r[:, :b1.shape[1]], r[:, b1.shape[1]:]), (a, both)


def _bdot_cols2_bwd(res, cts):
    a, both = res
    ct = jnp.concatenate(cts, axis=1)
    d_both = _bdot(a, ct, TN)
    half = both.shape[1] // 2
    return _bdot(ct, both, NT), d_both[:, :half], d_both[:, half:]


_bdot_cols2.defvjp(_bdot_cols2_fwd, _bdot_cols2_bwd)


@functools.partial(jax.custom_vjp, nondiff_argnums=(2,))
def _shift_rows(xm, xh, back):
    r = pltpu.roll(xm, back, 0)
    row = lax.broadcasted_iota(jnp.int32, xh.shape, 0)
    top = jnp.where(row < back, pltpu.roll(xh, back, 0), r[0:HALO])
    return jnp.concatenate([top, r[HALO:]], axis=0)


def _shift_rows_fwd(xm, xh, back):
    return _shift_rows(xm, xh, back), None


def _shift_rows_bwd(back, _, ct):
    row = lax.broadcasted_iota(jnp.int32, ct.shape, 0)
    dxm = jnp.where(row < CHUNK - back, pltpu.roll(ct, CHUNK - back, 0), 0.0)
    rowh = lax.broadcasted_iota(jnp.int32, (HALO, ct.shape[1]), 0)
    dxh = jnp.where(rowh >= HALO - back, pltpu.roll(ct[0:HALO], HALO - back, 0), 0.0)
    return dxm, dxh


_shift_rows.defvjp(_shift_rows_fwd, _shift_rows_bwd)


def _silu(t):
    return t * jax.nn.sigmoid(t)


def _softplus(t):
    return jnp.where(t > 20.0, t, jnp.log1p(jnp.exp(jnp.minimum(t, 20.0))))


def _cparams(sem=None):
    kw = dict(vmem_limit_bytes=VMEM_LIMIT)
    if sem is not None:
        kw["dimension_semantics"] = sem
    return pltpu.CompilerParams(**kw)


def matmul(a, b, *, name, nt=False, lhs_t=False, n=None, addend=None, alpha=1.0, out_dtype=F32, out_split=1,
           out_rows=None, into=None, after=None, tm=1024, tn=1024, tk=4096):
    k, m = a.shape[::-1] if not lhs_t else a.shape
    b_split = b.shape[0] if b.ndim == 3 else 1
    b_rows, b_cols = b.shape[-2], b.shape[-1] * b_split
    n = (b_rows if nt else b_cols) if n is None else n
    tm, tn, tk = min(tm, m), min(tn, n), min(tk, k)
    if b_split > 1:
        part = b_cols // b_split
        tn, tk = (tn, min(tk, part)) if nt else (min(tn, part), tk)
    if out_split > 1:
        tn = min(tn, n // out_split)
    assert m % tm == 0 and n % tn == 0 and k % tk == 0, (a.shape, b.shape, nt)
    assert not (lhs_t and nt)
    nk = k // tk
    dims = TN if lhs_t else (NT if nt else NN)

    def body(*refs):
        a_ref, b_ref = refs[:2]
        add_ref = None if addend is None else refs[2]

        def finish(r, o_ref):
            if add_ref is not None:
                r = r + alpha * add_ref[...].astype(F32)
            o_ref[...] = r.astype(o_ref.dtype)

        if nk == 1:
            finish(_dg(a_ref[...], b_ref[...], dims), refs[-1])
            return
        o_ref, acc_ref = refs[-2:]
        kk = pl.program_id(2)

        @pl.when(kk == 0)
        def _():
            acc_ref[...] = _dg(a_ref[...], b_ref[...], dims)

        @pl.when(kk > 0)
        def _():
            acc_ref[...] += _dg(a_ref[...], b_ref[...], dims)

        @pl.when(kk == nk - 1)
        def _():
            finish(acc_ref[...], o_ref)

    if b_split == 1:
        b_spec = (pl.BlockSpec((tn, tk), lambda i, j, kk: (j, kk)) if nt
                  else pl.BlockSpec((tk, tn), lambda i, j, kk: (kk, j)))
    elif nt:
        per = (b_cols // b_split) // tk
        b_spec = pl.BlockSpec((None, tn, tk), lambda i, j, kk: (kk // per, j, kk % per))
    else:
        per = (b_cols // b_split) // tn
        b_spec = pl.BlockSpec((None, tk, tn), lambda i, j, kk: (j // per, kk, j % per))
    a_spec = (pl.BlockSpec((tk, tm), lambda i, j, kk: (kk, i)) if lhs_t
              else pl.BlockSpec((tm, tk), lambda i, j, kk: (i, kk)))
    in_specs = [a_spec, b_spec]
    args = [a, b]
    if addend is not None:
        in_specs.append(pl.BlockSpec((tm, tn), lambda i, j, kk: (i, j)))
        args.append(addend)
    for extra in (after, None if into is None else into[0]):
        if extra is not None:
            in_specs.append(pl.BlockSpec(memory_space=pl.ANY))
            args.append(extra)
    aliases = {}
    if into is not None:
        assert out_split == 1 and into[1] % tm == 0 and into[0].dtype == out_dtype
        row0 = into[1] // tm
        out_spec = pl.BlockSpec((tm, tn), lambda i, j, kk: (row0 + i, j))
        out_shape = jax.ShapeDtypeStruct(into[0].shape, out_dtype)
        aliases = {len(args) - 1: 0}
    elif out_split == 1:
        out_spec = pl.BlockSpec((tm, tn), lambda i, j, kk: (i, j))
        out_shape = jax.ShapeDtypeStruct((m if out_rows is None else out_rows, n), out_dtype)
    else:
        per_o = (n // out_split) // tn
        out_spec = pl.BlockSpec((None, tm, tn), lambda i, j, kk: (j // per_o, i, j % per_o))
        out_shape = jax.ShapeDtypeStruct((out_split, m, n // out_split), out_dtype)
    return pl.pallas_call(
        body, name=name, grid=(m // tm, n // tn, nk),
        in_specs=in_specs,
        out_specs=out_spec,
        out_shape=out_shape,
        input_output_aliases=aliases,
        scratch_shapes=[] if nk == 1 else [pltpu.VMEM((tm, tn), F32)],
        compiler_params=_cparams(("parallel", "parallel", "arbitrary")),
    )(*args)


def _gated_rmsnorm(o, z, nw):
    r = lax.rsqrt(jnp.mean(o * o, axis=1, keepdims=True) + RMS_EPS)
    return o * r * nw * _silu(z)


def _chunk_consts():
    row = lax.broadcasted_iota(jnp.int32, (CHUNK, CHUNK), 0)
    col = lax.broadcasted_iota(jnp.int32, (CHUNK, CHUNK), 1)
    return row, col


def _head_cols(tb, groups, hp, j, rev_nb=None):
    def bb(b):
        return b if rev_nb is None else rev_nb - 1 - b
    return pl.BlockSpec((tb, hp * LANES), lambda b, h: (bb(b), j * groups + h))


def _stack(pieces):
    return jnp.concatenate(pieces, axis=0) if len(pieces) > 1 else pieces[0]


def _place(t, g):
    if g == 1:
        return t
    head = lax.broadcasted_iota(jnp.int32, t.shape, 0) // CHUNK
    return jnp.concatenate([jnp.where(head == h, t, 0.0) for h in range(g)], axis=1)


def _hgrn2_chunk(qr, fr, iv, z, a0, a1, a2, nw, st):
    g = len(qr)
    n = g * CHUNK
    row = lax.broadcasted_iota(jnp.int32, (n, n), 0)
    col = lax.broadcasted_iota(jnp.int32, (n, n), 1)
    same = (row // CHUNK) == (col // CHUNK)
    qs, ks, gl = [], [], []
    for h in range(g):
        amax = jnp.maximum(jnp.maximum(a0[h], a1[h]), a2[h])
        e0, e1, e2 = jnp.exp(a0[h] - amax), jnp.exp(a1[h] - amax), jnp.exp(a2[h] - amax)
        lb = e0 / (e0 + e1 + e2)
        forget = lb + (1.0 - lb) * jax.nn.sigmoid(fr[h])
        qs.append(_silu(qr[h]))
        ks.append(1.0 - forget)
        gl.append(jnp.log(forget))
    q, k, glog, v, zs = _stack(qs), _stack(ks), _stack(gl), _stack(list(iv)), _stack(list(z))
    cum = _sdot((same & (col <= row)).astype(F32), glog)
    sub = (row % CHUNK) // SUB
    rowl = lax.broadcasted_iota(jnp.int32, (n, LANES), 0)
    headl, subl = rowl // CHUNK, (rowl % CHUNK) // SUB

    def row_of(r):
        picks = [jnp.sum(jnp.where(rowl == h * CHUNK + r, cum, 0.0), axis=0, keepdims=True) for h in range(g)]
        out = jnp.zeros((n, LANES), F32)
        for h in range(g):
            out = out + jnp.where(headl == h, picks[h], 0.0)
        return out, picks

    refs = [jnp.zeros((n, LANES), F32)] + [row_of(i * SUB - 1)[0] for i in range(1, CHUNK // SUB)]
    own = jnp.zeros((n, LANES), F32)
    for i, ref in enumerate(refs):
        own = own + jnp.where(subl == i, ref, 0.0)
    qt = q * jnp.exp(cum - own)
    att = jnp.zeros((n, n), F32)
    for i, ref in enumerate(refs):
        kt = k * jnp.exp(jnp.minimum(ref - cum, EXP_CLAMP))
        att = att + jnp.where(same & (sub == i) & (col <= row), _bdot(qt, kt, NT), 0.0)
    cl_rows, lasts = row_of(CHUNK - 1)
    o = _bdot(att, v, NN) + _bdot(_place(q * jnp.exp(cum), g), st, NT)
    kd = k * jnp.exp(cl_rows - cum)
    cl_wide = jnp.concatenate(lasts, axis=1) if g > 1 else lasts[0]
    st_new = st * jnp.exp(cl_wide) + _bdot(v, _place(kd, g), TN)
    return _gated_rmsnorm(o, zs, nw), st_new


def hgrn2_fwd(hproj, alb, nw, *, tb, hp):
    t, d4 = hproj.shape
    d = d4 // 4
    heads = d // LANES
    groups = heads // hp
    nb, nc = t // tb, tb // CHUNK

    def body(q_ref, f_ref, i_ref, z_ref, alb_ref, nw_ref, y_ref, states_ref, st_ref):
        b, hg = pl.program_id(0), pl.program_id(1)

        @pl.when(b == 0)
        def _():
            st_ref[hg] = jnp.zeros((LANES, hp * LANES), F32)

        nwv = nw_ref[...]
        lns = [slice(p * LANES, (p + 1) * LANES) for p in range(hp)]
        albs = []
        for r in range(3):
            albs.append(tuple(alb_ref[r:r + 1, pl.ds(pl.multiple_of((hg * hp + p) * LANES, LANES), LANES)]
                              for p in range(hp)))

        def step(c, carry):
            rows = pl.ds(pl.multiple_of(c * CHUNK, CHUNK), CHUNK)
            st = st_ref[hg]
            states_ref[c, 0] = st
            ins = [tuple(ref[rows, ln] for ln in lns) for ref in (q_ref, f_ref, i_ref, z_ref)]
            y, st_new = _hgrn2_chunk(*ins, *albs, nwv, st)
            for p in range(hp):
                y_ref[rows, lns[p]] = y[p * CHUNK:(p + 1) * CHUNK].astype(y_ref.dtype)
            st_ref[hg] = st_new
            return carry

        lax.fori_loop(0, nc, step, 0, unroll=True)

    return pl.pallas_call(
        body, name="hgrn2_fwd", grid=(nb, groups),
        in_specs=[_head_cols(tb, groups, hp, j) for j in range(4)] + [
            pl.BlockSpec((3, d), lambda b, h: (0, 0)),
            pl.BlockSpec((1, LANES), lambda b, h: (0, 0))],
        out_specs=[pl.BlockSpec((tb, hp * LANES), lambda b, h: (b, h)),
                   pl.BlockSpec((nc, 1, LANES, hp * LANES), lambda b, h: (b, h, 0, 0))],
        out_shape=[jax.ShapeDtypeStruct((t, d), BF16),
                   jax.ShapeDtypeStruct((t // CHUNK, groups, LANES, hp * LANES), F32)],
        scratch_shapes=[pltpu.VMEM((groups, LANES, hp * LANES), F32)],
        compiler_params=_cparams(("arbitrary", "arbitrary")),
    )(hproj, hproj, hproj, hproj, alb, nw)


def hgrn2_bwd(hproj, alb, nw, states, dy, *, tb, hp):
    t, d4 = hproj.shape
    d = d4 // 4
    heads = d // LANES
    groups = heads // hp
    nb, nc = t // tb, tb // CHUNK

    def body(q_ref, f_ref, i_ref, z_ref, alb_ref, nw_ref, states_ref, dy_ref,
             dh_ref, dalb_ref, dnw_ref, dst_ref):
        b, hg = pl.program_id(0), pl.program_id(1)

        @pl.when(b == 0)
        def _():
            dst_ref[hg] = jnp.zeros((LANES, hp * LANES), F32)

        @pl.when((b == 0) & (hg == 0))
        def _():
            dalb_ref[...] = jnp.zeros_like(dalb_ref)
            dnw_ref[...] = jnp.zeros_like(dnw_ref)

        nwv = nw_ref[...]
        lns = [slice(p * LANES, (p + 1) * LANES) for p in range(hp)]
        lanes_of = [pl.ds(pl.multiple_of((hg * hp + p) * LANES, LANES), LANES) for p in range(hp)]
        albs = [tuple(alb_ref[r:r + 1, lanes_of[p]] for p in range(hp)) for r in range(3)]

        def step(i, carry):
            c = nc - 1 - i
            rows = pl.ds(pl.multiple_of(c * CHUNK, CHUNK), CHUNK)
            ins = [tuple(ref[rows, ln] for ln in lns) for ref in (q_ref, f_ref, i_ref, z_ref)]
            dy = _stack([dy_ref[rows, ln].astype(F32) for ln in lns])
            _, vjp = jax.vjp(_hgrn2_chunk, *ins, *albs, nwv, states_ref[c, 0])
            dq, df, di, dz, da0, da1, da2, dnw, dst = vjp((dy, dst_ref[hg]))
            for p in range(hp):
                h = hg * hp + p
                for j, val in enumerate((dq, df, di, dz)):
                    dh_ref[rows, pl.ds(pl.multiple_of(j * d + h * LANES, LANES), LANES)] = val[p].astype(dh_ref.dtype)
                dalb_ref[0:1, lanes_of[p]] += da0[p]
                dalb_ref[1:2, lanes_of[p]] += da1[p]
                dalb_ref[2:3, lanes_of[p]] += da2[p]
            dnw_ref[0:1, :] += dnw
            dst_ref[hg] = dst
            return carry

        lax.fori_loop(0, nc, step, 0, unroll=True)

    return pl.pallas_call(
        body, name="hgrn2_bwd", grid=(nb, groups),
        in_specs=[_head_cols(tb, groups, hp, j, rev_nb=nb) for j in range(4)] + [
            pl.BlockSpec((3, d), lambda b, h: (0, 0)),
            pl.BlockSpec((1, LANES), lambda b, h: (0, 0)),
            pl.BlockSpec((nc, 1, LANES, hp * LANES), lambda b, h: (nb - 1 - b, h, 0, 0)),
            pl.BlockSpec((tb, hp * LANES), lambda b, h: (nb - 1 - b, h))],
        out_specs=[pl.BlockSpec((tb, d4), lambda b, h: (nb - 1 - b, 0)),
                   pl.BlockSpec((3, d), lambda b, h: (0, 0)),
                   pl.BlockSpec((8, LANES), lambda b, h: (0, 0))],
        out_shape=[jax.ShapeDtypeStruct((t, d4), BF16),
                   jax.ShapeDtypeStruct((3, d), F32),
                   jax.ShapeDtypeStruct((8, LANES), F32)],
        scratch_shapes=[pltpu.VMEM((groups, LANES, hp * LANES), F32)],
        compiler_params=_cparams(("arbitrary", "arbitrary")),
    )(hproj, hproj, hproj, hproj, alb, nw, states, dy)


def _conv_silu(xm, xh, w):
    acc = w[CONV_K - 1] * xm
    for j in range(CONV_K - 1):
        acc = acc + w[j] * _shift_rows(xm, xh, CONV_K - 1 - j)
    return _silu(acc)


def _l2norm(t):
    return t * lax.rsqrt(jnp.sum(t * t, axis=1, keepdims=True) + L2_EPS)


def _gdn_chunk(head0, hh, known_inv, qm, qh, km, kh, vm, vh, z, tail, wq, wk, wv, alog_row, dtb_row, nw, s):
    g = len(qm)
    n = g * CHUNK
    row = lax.broadcasted_iota(jnp.int32, (n, n), 0)
    col = lax.broadcasted_iota(jnp.int32, (n, n), 1)
    same = (row // CHUNK) == (col // CHUNK)
    lane = lax.broadcasted_iota(jnp.int32, (CHUNK, LANES), 1)
    heads_lane = lax.broadcasted_iota(jnp.int32, (1, LANES), 1)
    q = _l2norm(_stack([_conv_silu(qm[h], qh[h], wq[h]) for h in range(g)])) * (LANES ** -0.5)
    k = _l2norm(_stack([_conv_silu(km[h], kh[h], wk[h]) for h in range(g)]))
    v = _stack([_conv_silu(vm[h], vh[h], wv[h]) for h in range(g)])
    zs = _stack(list(z))
    betas, gs = [], []
    for h in range(g):
        head = head0 + h
        betas.append(jax.nn.sigmoid(jnp.sum(jnp.where(lane == head, tail, 0.0), axis=1, keepdims=True)))
        a_t = jnp.sum(jnp.where(lane == hh + head, tail, 0.0), axis=1, keepdims=True)
        alog = jnp.sum(jnp.where(heads_lane == head, alog_row, 0.0), axis=1, keepdims=True)
        dtb = jnp.sum(jnp.where(heads_lane == head, dtb_row, 0.0), axis=1, keepdims=True)
        gs.append(-jnp.exp(alog) * _softplus(a_t + dtb))
    beta, gcol = _stack(betas), _stack(gs)
    tril = (same & (col <= row)).astype(F32)
    eye = (col == row).astype(F32)
    if n % LANES == 0:
        cum1 = _sdot(tril, gcol + jnp.zeros((n, LANES), F32))
        cum = jnp.concatenate([cum1] * (n // LANES), axis=1)
        lane1 = lax.broadcasted_iota(jnp.int32, (n, LANES), 1)
        cum_c = jnp.sum(jnp.where(lane1 == 0, cum1, 0.0), axis=1, keepdims=True)
    else:
        cum = _sdot(tril, gcol + jnp.zeros((n, n), F32))
        cum_c = jnp.sum(cum * eye, axis=1, keepdims=True)
    cum_r = cum.T
    diff = cum - cum_r
    strict = same & (col < row)
    incl = same & (col <= row)
    dec_incl = jnp.exp(jnp.where(incl, diff, -1e30))
    dec_strict = jnp.where(strict, dec_incl, 0.0)
    last = same & (col % CHUNK == CHUNK - 1)
    cl_rows = jnp.sum(jnp.where(last, cum_r, 0.0), axis=1, keepdims=True)
    ecum = jnp.exp(cum_c)
    m = -(beta * _bdot(k, k, NT) * dec_strict)
    inv = _tri_inv(m) if known_inv is None else _tri_inv_known(m, known_inv)
    u0, w = _bdot_cols2(inv, beta * v, (beta * ecum) * k)
    qk = _bdot(q, k, NT) * dec_incl
    u = u0 - _bdot(_place(w, g), s, NN)
    o = _bdot(_place(q * ecum, g), s, NN) + _bdot(qk, u, NN)
    kd = k * jnp.exp(cl_rows - cum_c)
    row1 = lax.broadcasted_iota(jnp.int32, (n, 1), 0)
    decay = []
    for h in range(g):
        cl_h = jnp.sum(jnp.where(row1 == h * CHUNK + CHUNK - 1, cum_c, 0.0), axis=0, keepdims=True)
        decay.append(jnp.exp(cl_h) + jnp.zeros((LANES, 1), F32))
    s_new = _stack(decay) * s + _bdot(_place(kd, g), u, TN)
    y = _gated_rmsnorm(o, zs, nw)
    return (y, s_new, inv) if known_inv is None else (y, s_new)


def _gdn_in_specs(tb, heads, hp, rev_nb=None):
    groups = heads // hp

    def bb(b):
        return b if rev_nb is None else rev_nb - 1 - b

    def halo(j):
        return pl.BlockSpec((HALO, hp * LANES),
                            lambda b, h, j=j: (jnp.maximum(bb(b) * (tb // HALO) - 1, 0), j * groups + h))

    def main(j):
        return _head_cols(tb, groups, hp, j, rev_nb=rev_nb)

    return [main(0), halo(0), main(1), halo(1), main(2), halo(2), main(3),
            pl.BlockSpec((tb, LANES), lambda b, h: (bb(b), 0)),
            pl.BlockSpec((CONV_K, 3 * heads * LANES), lambda b, h: (0, 0)),
            pl.BlockSpec((8, LANES), lambda b, h: (0, 0)),
            pl.BlockSpec((1, LANES), lambda b, h: (0, 0))]


def _gdn_chunk_args(c, head0, hp, blk, heads, q_ref, qh_ref, k_ref, kh_ref, v_ref, vh_ref, z_ref, tail_ref, cw_ref, prm_ref, nw_ref):
    d = heads * LANES
    lns = [slice(p * LANES, (p + 1) * LANES) for p in range(hp)]
    rows = pl.ds(pl.multiple_of(c * CHUNK, CHUNK), CHUNK)
    prev = pl.ds(pl.multiple_of(jnp.maximum(c * CHUNK - HALO, 0), HALO), HALO)
    first = c == 0
    live = jnp.where(first & (blk == 0), 0.0, 1.0)

    def main_of(ref):
        return tuple(ref[rows, ln] for ln in lns)

    def halo_of(ref, href):
        return tuple(jnp.where(first, href[:, ln], ref[prev, ln]) * live for ln in lns)

    def cw(j):
        out = []
        for p in range(hp):
            lanes = pl.ds(pl.multiple_of(j * d + (head0 + p) * LANES, LANES), LANES)
            out.append(tuple(cw_ref[r:r + 1, lanes] for r in range(CONV_K)))
        return tuple(out)

    return (main_of(q_ref), halo_of(q_ref, qh_ref), main_of(k_ref), halo_of(k_ref, kh_ref),
            main_of(v_ref), halo_of(v_ref, vh_ref), main_of(z_ref), tail_ref[rows, :],
            cw(0), cw(1), cw(2), prm_ref[0:1, :], prm_ref[1:2, :], nw_ref[...])


def gdn_fwd(hmain, tail, conv_w, prm, nw, *, tb, hp):
    t, d4 = hmain.shape
    d = d4 // 4
    heads = d // LANES
    groups = heads // hp
    nb, nc = t // tb, tb // CHUNK
    n = hp * CHUNK

    def body(q_ref, qh_ref, k_ref, kh_ref, v_ref, vh_ref, z_ref, tail_ref, cw_ref, prm_ref, nw_ref,
             y_ref, states_ref, inv_ref, s_ref):
        b, hg = pl.program_id(0), pl.program_id(1)

        @pl.when(b == 0)
        def _():
            s_ref[hg] = jnp.zeros((hp * LANES, LANES), F32)

        def step(c, carry):
            rows = pl.ds(pl.multiple_of(c * CHUNK, CHUNK), CHUNK)
            s = s_ref[hg]
            states_ref[c, 0] = s
            args = _gdn_chunk_args(c, hg * hp, hp, b, heads, q_ref, qh_ref, k_ref, kh_ref, v_ref, vh_ref, z_ref,
                                   tail_ref, cw_ref, prm_ref, nw_ref)
            y, s_new, inv = _gdn_chunk(hg * hp, heads, None, *args, s)
            for p in range(hp):
                y_ref[rows, p * LANES:(p + 1) * LANES] = y[p * CHUNK:(p + 1) * CHUNK].astype(y_ref.dtype)
            inv_ref[c, 0] = inv.astype(inv_ref.dtype)
            s_ref[hg] = s_new
            return carry

        lax.fori_loop(0, nc, step, 0, unroll=True)

    return pl.pallas_call(
        body, name="gdn_fwd", grid=(nb, groups),
        in_specs=_gdn_in_specs(tb, heads, hp),
        out_specs=[pl.BlockSpec((tb, hp * LANES), lambda b, h: (b, h)),
                   pl.BlockSpec((nc, 1, hp * LANES, LANES), lambda b, h: (b, h, 0, 0)),
                   pl.BlockSpec((nc, 1, n, n), lambda b, h: (b, h, 0, 0))],
        out_shape=[jax.ShapeDtypeStruct((t, d), BF16),
                   jax.ShapeDtypeStruct((t // CHUNK, groups, hp * LANES, LANES), F32),
                   jax.ShapeDtypeStruct((t // CHUNK, groups, n, n), BF16)],
        scratch_shapes=[pltpu.VMEM((groups, hp * LANES, LANES), F32)],
        compiler_params=_cparams(("arbitrary", "arbitrary")),
    )(hmain, hmain, hmain, hmain, hmain, hmain, hmain, tail, conv_w, prm, nw)


def gdn_bwd(hmain, tail, conv_w, prm, nw, states, invs, dy, *, tb, hp):
    t, d4 = hmain.shape
    d = d4 // 4
    heads = d // LANES
    groups = heads // hp
    nb, nc = t // tb, tb // CHUNK

    def body(q_ref, qh_ref, k_ref, kh_ref, v_ref, vh_ref, z_ref, tail_ref, cw_ref, prm_ref, nw_ref,
             states_ref, inv_ref, dy_ref, dh_ref, dtail_ref, dcw_ref, dprm_ref, ds_ref, pend_ref):
        b, hg = pl.program_id(0), pl.program_id(1)
        blk = nb - 1 - b

        @pl.when(b == 0)
        def _():
            ds_ref[hg] = jnp.zeros((hp * LANES, LANES), F32)
            for p in range(hp):
                pend_ref[hg * hp + p] = jnp.zeros((3, HALO, LANES), F32)

        @pl.when((b == 0) & (hg == 0))
        def _():
            dcw_ref[...] = jnp.zeros_like(dcw_ref)
            dprm_ref[...] = jnp.zeros_like(dprm_ref)

        @pl.when(hg == 0)
        def _():
            dtail_ref[...] = jnp.zeros_like(dtail_ref)

        def step(i, carry):
            c = nc - 1 - i
            rows = pl.ds(pl.multiple_of(c * CHUNK, CHUNK), CHUNK)
            zpad = jnp.zeros((CHUNK - HALO, LANES), F32)
            args = _gdn_chunk_args(c, hg * hp, hp, blk, heads, q_ref, qh_ref, k_ref, kh_ref, v_ref, vh_ref, z_ref,
                                   tail_ref, cw_ref, prm_ref, nw_ref)
            dy = _stack([dy_ref[rows, p * LANES:(p + 1) * LANES].astype(F32) for p in range(hp)])
            pends = [pend_ref[hg * hp + p] for p in range(hp)]
            known_inv = inv_ref[c, 0].astype(F32)
            _, vjp = jax.vjp(functools.partial(_gdn_chunk, hg * hp, heads, known_inv), *args, states_ref[c, 0])
            (dqm, dqh, dkm, dkh, dvm, dvh, dz, dtl, dwq, dwk, dwv, dalog, ddtb, dnw, ds) = vjp((dy, ds_ref[hg]))
            for p in range(hp):
                h = hg * hp + p
                for j, (dm, dhalo) in enumerate(((dqm, dqh), (dkm, dkh), (dvm, dvh))):
                    full = dm[p] + jnp.concatenate([zpad, pends[p][j]], axis=0)
                    dh_ref[rows, pl.ds(pl.multiple_of(j * d + h * LANES, LANES), LANES)] = full.astype(dh_ref.dtype)
                    pend_ref[h, j] = dhalo[p]
                dh_ref[rows, pl.ds(pl.multiple_of(3 * d + h * LANES, LANES), LANES)] = dz[p].astype(dh_ref.dtype)
                for j, dw in enumerate((dwq, dwk, dwv)):
                    lanes = pl.ds(pl.multiple_of(j * d + h * LANES, LANES), LANES)
                    for r in range(CONV_K):
                        dcw_ref[r:r + 1, lanes] += dw[p][r]
            dtail_ref[rows, :] += dtl
            dprm_ref[0:1, :] += dalog
            dprm_ref[1:2, :] += ddtb
            dprm_ref[2:3, :] += dnw
            ds_ref[hg] = ds
            return carry

        lax.fori_loop(0, nc, step, 0, unroll=True)

    return pl.pallas_call(
        body, name="gdn_bwd", grid=(nb, groups),
        in_specs=_gdn_in_specs(tb, heads, hp, rev_nb=nb) + [
            pl.BlockSpec((nc, 1, hp * LANES, LANES), lambda b, h: (nb - 1 - b, h, 0, 0)),
            pl.BlockSpec((nc, 1, hp * CHUNK, hp * CHUNK), lambda b, h: (nb - 1 - b, h, 0, 0)),
            pl.BlockSpec((tb, hp * LANES), lambda b, h: (nb - 1 - b, h))],
        out_specs=[pl.BlockSpec((tb, d4), lambda b, h: (nb - 1 - b, 0)),
                   pl.BlockSpec((tb, LANES), lambda b, h: (nb - 1 - b, 0)),
                   pl.BlockSpec((CONV_K, 3 * d), lambda b, h: (0, 0)),
                   pl.BlockSpec((8, LANES), lambda b, h: (0, 0))],
        out_shape=[jax.ShapeDtypeStruct((t, d4), BF16),
                   jax.ShapeDtypeStruct((t, LANES), F32),
                   jax.ShapeDtypeStruct((CONV_K, 3 * d), F32),
                   jax.ShapeDtypeStruct((8, LANES), F32)],
        scratch_shapes=[pltpu.VMEM((groups, hp * LANES, LANES), F32),
                        pltpu.VMEM((heads, 3, HALO, LANES), F32)],
        compiler_params=_cparams(("arbitrary", "arbitrary")),
    )(hmain, hmain, hmain, hmain, hmain, hmain, hmain, tail, conv_w, prm, nw, states, invs, dy)


def _layer_norm(u, g, b):
    mu = jnp.mean(u, axis=1, keepdims=True)
    var = jnp.mean(jnp.square(u - mu), axis=1, keepdims=True)
    return (u - mu) * lax.rsqrt(var + LN_EPS) * g + b


def ln_fwd(u, g, b, *, tr):
    t, d = u.shape

    def body(u_ref, g_ref, b_ref, x_ref, xb_ref):
        x = _layer_norm(u_ref[...], g_ref[...], b_ref[...])
        x_ref[...] = x
        xb_ref[...] = x.astype(BF16)

    row = pl.BlockSpec((tr, d), lambda i: (i, 0))
    vec = pl.BlockSpec((1, d), lambda i: (0, 0))
    return pl.pallas_call(
        body, name="ln_fwd", grid=(t // tr,), in_specs=[row, vec, vec], out_specs=[row, row],
        out_shape=[jax.ShapeDtypeStruct((t, d), F32), jax.ShapeDtypeStruct((t, d), BF16)],
        compiler_params=_cparams(("parallel",)),
    )(u, g, b)


def ln_bwd(u, g, b, dout, *, tr):
    t, d = u.shape

    def body(u_ref, g_ref, b_ref, dout_ref, du_ref, dub_ref, dg_ref, db_ref):
        @pl.when(pl.program_id(0) == 0)
        def _():
            dg_ref[...] = jnp.zeros_like(dg_ref)
            db_ref[...] = jnp.zeros_like(db_ref)

        _, vjp = jax.vjp(_layer_norm, u_ref[...], g_ref[...], b_ref[...])
        du, dg, db = vjp(dout_ref[...])
        du_ref[...] = du
        dub_ref[...] = du.astype(BF16)
        dg_ref[0:1, :] += dg
        db_ref[0:1, :] += db

    row = pl.BlockSpec((tr, d), lambda i: (i, 0))
    vec = pl.BlockSpec((1, d), lambda i: (0, 0))
    acc = pl.BlockSpec((8, d), lambda i: (0, 0))
    return pl.pallas_call(
        body, name="ln_bwd", grid=(t // tr,), in_specs=[row, vec, vec, row], out_specs=[row, row, acc, acc],
        out_shape=[jax.ShapeDtypeStruct((t, d), F32), jax.ShapeDtypeStruct((t, d), BF16),
                   jax.ShapeDtypeStruct((8, d), F32), jax.ShapeDtypeStruct((8, d), F32)],
        compiler_params=_cparams(("arbitrary",)),
    )(u, g, b, dout)


def ln_loss_bwd(u, g, b, target, *, tr):
    t, d = u.shape

    def loss_of(uu, gg, bb, tgt):
        err = jnp.square(_layer_norm(uu, gg, bb) - tgt)
        return 0.5 * jnp.sum(jnp.mean(err, axis=1, keepdims=True), axis=0, keepdims=True)

    def body(u_ref, g_ref, b_ref, t_ref, loss_ref, du_ref, dub_ref, dg_ref, db_ref):
        @pl.when(pl.program_id(0) == 0)
        def _():
            loss_ref[...] = jnp.zeros_like(loss_ref)
            dg_ref[...] = jnp.zeros_like(dg_ref)
            db_ref[...] = jnp.zeros_like(db_ref)

        tgt = t_ref[...]
        val, vjp = jax.vjp(lambda uu, gg, bb: loss_of(uu, gg, bb, tgt), u_ref[...], g_ref[...], b_ref[...])
        du, dg, db = vjp(jnp.ones((1, 1), F32))
        loss_ref[...] += val
        du_ref[...] = du
        dub_ref[...] = du.astype(BF16)
        dg_ref[0:1, :] += dg
        db_ref[0:1, :] += db

    row = pl.BlockSpec((tr, d), lambda i: (i, 0))
    vec = pl.BlockSpec((1, d), lambda i: (0, 0))
    acc = pl.BlockSpec((8, d), lambda i: (0, 0))
    return pl.pallas_call(
        body, name="ln_loss_bwd", grid=(t // tr,), in_specs=[row, vec, vec, row],
        out_specs=[pl.BlockSpec((8, LANES), lambda i: (0, 0)), row, row, acc, acc],
        out_shape=[jax.ShapeDtypeStruct((8, LANES), F32),
                   jax.ShapeDtypeStruct((t, d), F32), jax.ShapeDtypeStruct((t, d), BF16),
                   jax.ShapeDtypeStruct((8, d), F32), jax.ShapeDtypeStruct((8, d), F32)],
        compiler_params=_cparams(("arbitrary",)),
    )(u, g, b, target)


def local_step(x, target, wa_in, alb, a_nw, wa_out, wb_t, wb_tail_t, conv_w, a_log, dt_bias, b_nw, wb_out,
               ln_g, ln_b, *, tb=256, tr=256, hp=HEADS_PER_STEP, first_after=None, b_weights=None, on_b_grads=None,
               on_a_out_grad=None, on_a_grads=None, on_mid=None, on_small_grads=None):
    t, d = x.shape
    heads = d // LANES
    tb, tr, hp = min(tb, t), min(tr, t), min(hp, heads)
    xb = x.astype(BF16)
    prm = jnp.zeros((8, LANES), F32).at[0, :heads].set(a_log[0]).at[1, :heads].set(dt_bias[0])

    ha = matmul(xb, wa_in, name="mm_a_in", after=first_after)
    ya, st_a = hgrn2_fwd(ha, alb, a_nw, tb=tb, hp=hp)
    u1 = matmul(ya, wa_out, name="mm_a_out", addend=x, alpha=DEEPNORM_ALPHA,
                after=None if on_mid is None else on_mid(ya))
    x1, x1b = ln_fwd(u1, ln_g[0:1], ln_b[0:1], tr=tr)
    if b_weights is not None:
        wb_t, wb_tail_t, wb_out, conv_w = b_weights(x1b)
    n_tail = 2 * heads
    hb = matmul(x1b, wb_t, name="mm_b_in", nt=True, n=4 * d)
    tl = matmul(x1b, wb_tail_t, name="mm_b_tail", nt=True)
    yb, st_b, inv_b = gdn_fwd(hb, tl, conv_w, prm, b_nw, tb=tb, hp=hp)
    u2 = matmul(yb, wb_out, name="mm_b_out", addend=x1, alpha=DEEPNORM_ALPHA)

    loss, du2, du2b, dg2, db2 = ln_loss_bwd(u2, ln_g[1:2], ln_b[1:2], target, tr=tr)
    d_wb_out = matmul(yb, du2b, name="mm_dwb_out", lhs_t=True, out_dtype=BF16)
    dyb = matmul(du2b, wb_out, name="mm_dyb", nt=True)
    dhb, dtl, d_conv, dprm = gdn_bwd(hb, tl, conv_w, prm, b_nw, st_b, inv_b, dyb, tb=tb, hp=hp)
    dtlb = dtl.astype(BF16)
    d_wb_t = matmul(dhb, x1b, name="mm_dwb_main", lhs_t=True, out_dtype=BF16, out_rows=4 * d + n_tail)
    d_wb_t = matmul(dtlb[:, :n_tail], x1b, name="mm_dwb_tail", lhs_t=True, out_dtype=BF16, into=(d_wb_t, 4 * d))
    sent_b = on_b_grads(d_wb_t, d_wb_out) if on_b_grads is not None else None
    dx1_tail = matmul(dtlb, wb_tail_t, name="mm_dx1_tail", addend=du2, alpha=DEEPNORM_ALPHA, after=sent_b)
    dx1 = matmul(dhb, wb_t, name="mm_dx1", addend=dx1_tail, alpha=1.0, tk=2048)
    du1, du1b, dg1, db1 = ln_bwd(u1, ln_g[0:1], ln_b[0:1], dx1, tr=tr)
    d_wa_out = matmul(ya, du1b, name="mm_dwa_out", lhs_t=True, out_dtype=BF16)
    sent_c = on_a_out_grad(d_wa_out) if on_a_out_grad is not None else None
    dya = matmul(du1b, wa_out, name="mm_dya", nt=True, after=sent_c)
    dha, d_alb, d_anw = hgrn2_bwd(ha, alb, a_nw, st_a, dya, tb=tb, hp=hp)
    small = dict(
        a_lower_bounds=d_alb, a_norm_w=d_anw[0:1], b_conv_w=d_conv,
        b_a_log=dprm[0:1, :heads], b_dt_bias=dprm[1:2, :heads], b_norm_w=dprm[2:3],
        ln_g=jnp.concatenate([dg1[0:1], dg2[0:1]], axis=0), ln_b=jnp.concatenate([db1[0:1], db2[0:1]], axis=0))
    d_wa_in = matmul(xb, dha, name="mm_dwa_in", lhs_t=True, out_dtype=BF16,
                     out_split=wa_in.shape[0] if wa_in.ndim == 3 else 1,
                     after=None if on_small_grads is None else on_small_grads(small))
    sent_a = on_a_grads(d_wa_in) if on_a_grads is not None else None
    grad_x = matmul(dha, wa_in, name="mm_dx", nt=True, addend=du1, alpha=DEEPNORM_ALPHA, after=sent_a, tk=2048)
    big = dict(a_w_in=d_wa_in, a_w_out=d_wa_out, b_w_t=d_wb_t, b_w_out=d_wb_out)
    return loss, grad_x, big, small


MESH_ID = pl.DeviceIdType.MESH


def _position():
    return lax.axis_index("x"), lax.axis_index("y"), lax.axis_index("c")


def _index_of(p):
    return 4 * p[0] + 2 * p[1] + p[2]


def all_gather(shards, *, name, space):
    n = len(shards)

    def body(*refs):
        ins, outs = refs[:n], refs[n:2 * n]
        send_sems, recv_sems, local_sems = refs[2 * n:]
        x, y, c = _position()
        me, sibling = (x, y, c), (x, y, 1 - c)
        chips = [(1 - x, y), (x, 1 - y), (1 - x, 1 - y)]

        def copy(a, k, block, to, own=False):
            dst = outs[a].at[_index_of(block)]
            return pltpu.make_async_remote_copy(
                src_ref=ins[a] if own else dst, dst_ref=dst,
                send_sem=send_sems.at[7 * a + k], recv_sem=recv_sems.at[7 * a + k],
                device_id=to, device_id_type=MESH_ID)

        mine = [pltpu.make_async_copy(ins[a], outs[a].at[_index_of(me)], local_sems.at[a]) for a in range(n)]
        for cp in mine:
            cp.start()
        first = []
        for a in range(n):
            first.append(copy(a, 0, me, sibling, own=True))
            first += [copy(a, 1 + j, me, (*chip, c), own=True) for j, chip in enumerate(chips)]
        for cp in first:
            cp.start()
        passed = []
        for j, chip in enumerate(chips):
            for a in range(n):
                copy(a, 1 + j, (*chip, c), me).wait_recv()
                fwd = copy(a, 4 + j, (*chip, c), sibling)
                fwd.start()
                passed.append(fwd)
        for a in range(n):
            copy(a, 0, sibling, me).wait_recv()
            for j, chip in enumerate(chips):
                copy(a, 4 + j, (*chip, 1 - c), me).wait_recv()
        for cp in first + passed:
            cp.wait_send()
        for cp in mine:
            cp.wait()

    spec = pl.BlockSpec(memory_space=space)
    return pl.pallas_call(
        body, name=name,
        in_specs=[spec] * n, out_specs=[spec] * n,
        out_shape=[jax.ShapeDtypeStruct((N_DEV, *s.shape), s.dtype) for s in shards],
        scratch_shapes=[pltpu.SemaphoreType.DMA((7 * n,)), pltpu.SemaphoreType.DMA((7 * n,)),
                        pltpu.SemaphoreType.DMA((n,))],
        compiler_params=pltpu.CompilerParams(vmem_limit_bytes=VMEM_LIMIT),
    )(*shards)


def all_gather_relay(shards, *, name):
    n, per = len(shards), 9

    def body(*refs):
        ins, outs = refs[:n], refs[n:2 * n]
        send_sems, recv_sems, local_sems = refs[2 * n:]
        x, y, c = _position()
        me, sibling = (x, y, c), (x, y, 1 - c)
        xn, yn, dg = (1 - x, y), (x, 1 - y), (1 - x, 1 - y)

        def parts(a):
            rows = shards[a].shape[0]
            return [(0, rows // 2), (rows // 2, rows // 2)] if rows % 32 == 0 else [(0, rows)]

        def copy(a, k, block, to, own=False, part=None):
            dst = outs[a].at[_index_of(block)]
            src = ins[a] if own else dst
            if part is not None:
                src, dst = src.at[pl.ds(*part)], dst.at[pl.ds(*part)]
            return pltpu.make_async_remote_copy(
                src_ref=src, dst_ref=dst, send_sem=send_sems.at[per * a + k], recv_sem=recv_sems.at[per * a + k],
                device_id=to, device_id_type=MESH_ID)

        mine = [pltpu.make_async_copy(ins[a], outs[a].at[_index_of(me)], local_sems.at[a]) for a in range(n)]
        for cp in mine:
            cp.start()
        started = []
        for a in range(n):
            started += [copy(a, 0, me, sibling, own=True), copy(a, 1, me, (*xn, c), own=True),
                        copy(a, 2, me, (*yn, c), own=True)]
        for cp in started:
            cp.start()

        def go(cp):
            cp.start()
            started.append(cp)

        for a in range(n):
            copy(a, 1, (*xn, c), me).wait_recv()
            go(copy(a, 3, (*xn, c), sibling))
            go(copy(a, 5, (*xn, c), (*yn, c), part=parts(a)[0]))
        for a in range(n):
            copy(a, 2, (*yn, c), me).wait_recv()
            go(copy(a, 4, (*yn, c), sibling))
            if len(parts(a)) == 2:
                go(copy(a, 6, (*yn, c), (*xn, c), part=parts(a)[1]))
        for a in range(n):
            for k, part in zip((5, 6), parts(a)):
                copy(a, k, (*dg, c), me, part=part).wait_recv()
                go(copy(a, k + 2, (*dg, c), sibling, part=part))
        for a in range(n):
            copy(a, 0, sibling, me).wait_recv()
            copy(a, 3, (*xn, 1 - c), me).wait_recv()
            copy(a, 4, (*yn, 1 - c), me).wait_recv()
            for k, part in zip((7, 8), parts(a)):
                copy(a, k, (*dg, 1 - c), me, part=part).wait_recv()
        for cp in started:
            cp.wait_send()
        for cp in mine:
            cp.wait()

    return pl.pallas_call(
        body, name=name,
        in_specs=[HBM_SPEC] * n, out_specs=[HBM_SPEC] * n,
        out_shape=[jax.ShapeDtypeStruct((N_DEV, *s.shape), s.dtype) for s in shards],
        scratch_shapes=[pltpu.SemaphoreType.DMA((per * n,)), pltpu.SemaphoreType.DMA((per * n,)),
                        pltpu.SemaphoreType.DMA((n,))],
        compiler_params=pltpu.CompilerParams(vmem_limit_bytes=VMEM_LIMIT),
    )(*shards)


HBM_SPEC = pl.BlockSpec(memory_space=pltpu.HBM)
SEM_SPEC = pl.BlockSpec(memory_space=pltpu.SEMAPHORE)


CHIP_PEERS = (4, 2, 6)
COPIES_PER_ARRAY = dict(gather=7, exchange=7, gather_near=4, gather_pass=3, reduce_pair=4, reduce_chip=3)


def _planned_copies(kind, srcs, lands, send_sems, recv_sems):
    x, y, c = _position()
    me, my_chip = _index_of((x, y, c)), 2 * x + y
    per = COPIES_PER_ARRAY[kind]
    copies = []

    def peer_of(r):
        return (x ^ ((r >> 2) & 1), y ^ ((r >> 1) & 1), c ^ (r & 1))

    def add(a, j, src, dst, peer):
        copies.append(pltpu.make_async_remote_copy(
            src_ref=src, dst_ref=dst, send_sem=send_sems.at[per * a + j], recv_sem=recv_sems.at[per * a + j],
            device_id=peer, device_id_type=MESH_ID))

    for a, land in enumerate(lands):
        src = srcs[a] if srcs else land
        if kind == "gather":
            for r in range(1, N_DEV):
                add(a, r - 1, src, land.at[me], peer_of(r))
        elif kind == "exchange":
            for r in range(1, N_DEV):
                add(a, r - 1, src.at[_index_of(peer_of(r))], land.at[me], peer_of(r))
        elif kind == "gather_near":
            for j, r in enumerate((1,) + CHIP_PEERS):
                add(a, j, src, land.at[me], peer_of(r))
        elif kind == "gather_pass":
            for j, r in enumerate(CHIP_PEERS):
                slot = _index_of(peer_of(r))
                add(a, j, land.at[slot], land.at[slot], peer_of(1))
        elif kind == "reduce_pair":
            for q in range(4):
                add(a, q, src.at[2 * q + (1 - c)], land.at[q], peer_of(1))
        elif kind == "reduce_chip":
            for j, r in enumerate(CHIP_PEERS):
                p = peer_of(r)
                add(a, j, src.at[2 * p[0] + p[1]], land.at[my_chip], p)
    return copies


def copies_start(kind, srcs, lands, *, name):
    ns, n = len(srcs), len(srcs) + len(lands)
    n_sem = COPIES_PER_ARRAY[kind] * len(lands)

    def body(*refs):
        for cp in _planned_copies(kind, refs[:ns], refs[ns:n], refs[n], refs[n + 1]):
            cp.start()
        refs[-1][...] = jnp.zeros_like(refs[-1])

    arrays = [pltpu.with_memory_space_constraint(t, pltpu.HBM) for t in (*srcs, *lands)]
    outs = pl.pallas_call(
        body, name=name,
        out_shape=(pltpu.SemaphoreType.DMA((n_sem,)), pltpu.SemaphoreType.DMA((n_sem,)),
                   *[pltpu.HBM(t.shape, t.dtype) for t in arrays], jax.ShapeDtypeStruct((8, LANES), F32)),
        in_specs=[HBM_SPEC] * n,
        out_specs=(SEM_SPEC, SEM_SPEC, *[HBM_SPEC] * n, pl.BlockSpec(memory_space=pltpu.VMEM)),
        input_output_aliases={i: 2 + i for i in range(n)},
        compiler_params=pltpu.CompilerParams(has_side_effects=pltpu.SideEffectType.DATAFLOW_SIDE_EFFECTING),
    )(*arrays)
    return (outs[0], outs[1]), list(outs[2:2 + ns]), list(outs[2 + ns:2 + n]), outs[-1]


def copies_wait(kind, sems, srcs, lands, after, *, name):
    ns, n = len(srcs), len(srcs) + len(lands)

    def body(*refs):
        for cp in _planned_copies(kind, refs[:ns], refs[ns:n], refs[n], refs[n + 1]):
            cp.wait_send()
            cp.wait_recv()

    outs = pl.pallas_call(
        body, name=name,
        out_shape=tuple(pltpu.HBM(t.shape, t.dtype) for t in (*srcs, *lands)),
        in_specs=[HBM_SPEC] * n + [SEM_SPEC, SEM_SPEC, pl.BlockSpec(memory_space=pl.ANY)],
        out_specs=tuple([HBM_SPEC] * n),
        input_output_aliases={i: i for i in range(n)},
        compiler_params=pltpu.CompilerParams(has_side_effects=pltpu.SideEffectType.DATAFLOW_SIDE_EFFECTING),
    )(*srcs, *lands, sems[0], sems[1], after)
    return list(outs[ns:])


def _landing(own, slot, slots=N_DEV):
    return lax.dynamic_update_slice_in_dim(lax.empty((slots, *own.shape), own.dtype), own[None], slot, 0)


def pair_sum(a, b, *, name, tr=256):
    p, r, c = a.shape
    tr = min(tr, r)

    def body(a_ref, b_ref, o_ref):
        o_ref[...] = (a_ref[...].astype(F32) + b_ref[...].astype(F32)).astype(o_ref.dtype)

    blk = pl.BlockSpec((1, tr, c), lambda i, j: (i, j, 0))
    return pl.pallas_call(
        body, name=name, grid=(p, r // tr), in_specs=[blk, blk], out_specs=blk,
        out_shape=jax.ShapeDtypeStruct(a.shape, a.dtype), compiler_params=_cparams(("parallel", "parallel")),
    )(a, b)


def adamw(parts, w, m, v, *, name, tr=64, tc=None):
    p, r, c = parts.shape
    tr = r if tc is not None else min(tr, r)
    tc = c if tc is None else tc
    assert r % tr == 0 and c % tc == 0
    c1 = 1.0 / (1.0 - ADAM_B1 ** ADAM_STEP)
    c2 = 1.0 / (1.0 - ADAM_B2 ** ADAM_STEP)

    def body(p_ref, w_ref, m_ref, v_ref, g_ref, d_ref, nm_ref, nv_ref):
        g = p_ref[0].astype(F32)
        for i in range(1, p):
            g = g + p_ref[i].astype(F32)
        nm = ADAM_B1 * m_ref[...] + (1.0 - ADAM_B1) * g
        nv = ADAM_B2 * v_ref[...] + (1.0 - ADAM_B2) * jnp.square(g)
        g_ref[...] = g
        nm_ref[...] = nm
        nv_ref[...] = nv
        d_ref[...] = -ADAM_LR * ((nm * c1) / (jnp.sqrt(nv * c2) + ADAM_EPS) + ADAM_WD * w_ref[...])

    blk = pl.BlockSpec((tr, tc), lambda i, j: (i, j))
    out = jax.ShapeDtypeStruct((r, c), F32)
    return pl.pallas_call(
        body, name=name, grid=(r // tr, c // tc),
        in_specs=[pl.BlockSpec((p, tr, tc), lambda i, j: (0, i, j)), blk, blk, blk],
        out_specs=[blk] * 4, out_shape=[out] * 4,
        compiler_params=_cparams(("parallel", "parallel")),
    )(parts, w, m, v)


def _pack(d, vals):
    heads = d // LANES
    vecs = jnp.zeros((8, LANES), F32)
    vecs = vecs.at[0:1].set(vals["a_norm_w"]).at[1:2, :heads].set(vals["b_a_log"])
    vecs = vecs.at[2:3, :heads].set(vals["b_dt_bias"]).at[3:4].set(vals["b_norm_w"])
    rows = [vals["a_lower_bounds"].reshape(-1, LANES), vals["ln_g"].reshape(-1, LANES),
            vals["ln_b"].reshape(-1, LANES), vecs]
    return jnp.concatenate(rows, axis=0)


def _unpack(d, packed):
    heads = d // LANES
    n3, n2 = 3 * heads, 2 * heads
    o = 0
    out = {}
    out["a_lower_bounds"] = packed[o:o + n3].reshape(3, d); o += n3
    out["ln_g"] = packed[o:o + n2].reshape(2, d); o += n2
    out["ln_b"] = packed[o:o + n2].reshape(2, d); o += n2
    out["a_norm_w"] = packed[o:o + 1]
    out["b_a_log"] = packed[o + 1:o + 2, :heads]
    out["b_dt_bias"] = packed[o + 2:o + 3, :heads]
    out["b_norm_w"] = packed[o + 3:o + 4]
    return out


ORDER = ("a_w_in", "a_lower_bounds", "a_norm_w", "a_w_out", "b_w_in", "b_conv_w", "b_a_log", "b_dt_bias", "b_norm_w",
         "b_w_out", "ln_g", "ln_b")


def kernel(x, a_w_in, a_lower_bounds, a_norm_w, a_w_out, b_w_in, b_conv_w, b_a_log, b_dt_bias, b_norm_w, b_w_out, ln_g, ln_b, loss_target, m_a_w_in, m_a_lower_bounds, m_a_norm_w, m_a_w_out, m_b_w_in, m_b_conv_w, m_b_a_log, m_b_dt_bias, m_b_norm_w, m_b_w_out, m_ln_g, m_ln_b, v_a_w_in, v_a_lower_bounds, v_a_norm_w, v_a_w_out, v_b_w_in, v_b_conv_w, v_b_a_log, v_b_dt_bias, v_b_norm_w, v_b_w_out, v_ln_g, v_ln_b):
    w = dict(a_w_in=a_w_in, a_lower_bounds=a_lower_bounds, a_norm_w=a_norm_w, a_w_out=a_w_out, b_w_in=b_w_in,
             b_conv_w=b_conv_w, b_a_log=b_a_log, b_dt_bias=b_dt_bias, b_norm_w=b_norm_w, b_w_out=b_w_out, ln_g=ln_g, ln_b=ln_b)
    m = dict(a_w_in=m_a_w_in, a_lower_bounds=m_a_lower_bounds, a_norm_w=m_a_norm_w, a_w_out=m_a_w_out, b_w_in=m_b_w_in,
             b_conv_w=m_b_conv_w, b_a_log=m_b_a_log, b_dt_bias=m_b_dt_bias, b_norm_w=m_b_norm_w, b_w_out=m_b_w_out,
             ln_g=m_ln_g, ln_b=m_ln_b)
    v = dict(a_w_in=v_a_w_in, a_lower_bounds=v_a_lower_bounds, a_norm_w=v_a_norm_w, a_w_out=v_a_w_out, b_w_in=v_b_w_in,
             b_conv_w=v_b_conv_w, b_a_log=v_b_a_log, b_dt_bias=v_b_dt_bias, b_norm_w=v_b_norm_w, b_w_out=v_b_w_out,
             ln_g=v_ln_g, ln_b=v_ln_b)
    t, d = x.shape[1], x.shape[2]
    heads = d // LANES
    n_tail = 2 * heads
    me = _index_of(_position())

    ga_in, ga_out, g_conv = all_gather_relay([a_w_in[0].astype(BF16), a_w_out[0].astype(BF16), b_conv_w[0]],
                                             name="gather_weights_a")
    b_shards = [jnp.transpose(b_w_in[0]).astype(BF16), b_w_out[0].astype(BF16)]
    flight = {}
    sems_w, b_shards, lands_w, token_w = copies_start("gather_near", b_shards, [_landing(s, me) for s in b_shards],
                                                      name="gather_weights_b_start")
    wa_out = ga_out.reshape(d, d)
    conv_w = jnp.transpose(g_conv, (1, 0, 2)).reshape(CONV_K, 3 * d)

    def on_mid(after):
        lands = copies_wait("gather_near", sems_w, b_shards, lands_w, after, name="gather_weights_b_wait")
        sems, _, lands, token = copies_start("gather_pass", [], lands, name="pass_weights_b_start")
        flight["w"] = (sems, lands)
        return token

    def b_weights(after):
        sems, lands = flight["w"]
        gb_in, gb_out = copies_wait("gather_pass", sems, [], lands, after, name="pass_weights_b_wait")
        wb_t = gb_in.reshape(4 * d + n_tail, d)
        wb_tail_t = jnp.concatenate([wb_t[4 * d:], jnp.zeros((LANES - n_tail, d), BF16)], axis=0)
        return wb_t, wb_tail_t, gb_out.reshape(d, d), conv_w

    def send_grads(key, parts):
        own = [lax.dynamic_index_in_dim(p, me, 0, keepdims=False) for p in parts]
        sems, parts, lands, token = copies_start("exchange", parts, [_landing(o, me) for o in own],
                                                 name="exchange_" + key + "_start")
        flight[key] = (sems, parts, lands)
        return token

    def on_b_grads(d_wb_t, d_out):
        return send_grads("b", [d_wb_t.reshape(N_DEV, -1, d), d_out.reshape(N_DEV, d // N_DEV, d)])

    res = {}

    def on_small_grads(small):
        conv_rows = small["b_conv_w"].reshape(-1, LANES)
        n_conv = conv_rows.shape[0]
        packed_grads = jnp.concatenate([conv_rows, _pack(d, small)], axis=0)
        (got,) = all_gather([packed_grads], name="gather_small", space=pltpu.VMEM)
        packed = adamw(got[:, n_conv:], _pack(d, w), _pack(d, m), _pack(d, v), name="adamw_small", tr=4096)
        for k, vals in zip(("grad", "delta", "new_m", "new_v"), packed):
            res[k] = _unpack(d, vals)
        shard_ch = 3 * d // N_DEV
        conv_parts = lax.dynamic_slice_in_dim(got[:, :n_conv].reshape(N_DEV, CONV_K, 3 * d), me * shard_ch, shard_ch,
                                              axis=2)
        conv_out = adamw(conv_parts, b_conv_w[0], m_b_conv_w[0], v_b_conv_w[0], name="adamw_conv")
        for k, vals in zip(("grad", "delta", "new_m", "new_v"), conv_out):
            res[k]["b_conv_w"] = vals[None]
        return conv_out[0]

    my_chip, my_core = me // 2, me % 2

    def on_a_grads(d_in):
        mine = lax.dynamic_index_in_dim(d_in.reshape(N_DEV // 2, 2, *d_in.shape[1:]), my_core, 1, keepdims=False)
        land1 = lax.empty(mine.shape, mine.dtype)
        sems, (d_in,), (land1,), token = copies_start("reduce_pair", [d_in], [land1], name="reduce_a_pair_start")
        (land1,) = copies_wait("reduce_pair", sems, [d_in], [land1], token, name="reduce_a_pair_wait")
        summed = pair_sum(mine, land1, name="reduce_a_pair_sum")
        own = lax.dynamic_index_in_dim(summed, my_chip, 0, keepdims=False)
        sems, (summed,), (land2,), token = copies_start("reduce_chip", [summed], [_landing(own, my_chip, N_DEV // 2)],
                                                        name="reduce_a_chip_start")
        flight["a"] = (sems, [summed], [land2])
        return token

    loss, grad_x, big, small = local_step(
        x[0], loss_target[0], ga_in, a_lower_bounds, a_norm_w, wa_out, None, None, None, b_a_log, b_dt_bias,
        b_norm_w, None, ln_g, ln_b, first_after=token_w, on_mid=on_mid, b_weights=b_weights, on_b_grads=on_b_grads,
        on_a_out_grad=lambda g: send_grads("c", [g.reshape(N_DEV, d // N_DEV, d)]),
        on_small_grads=on_small_grads, on_a_grads=on_a_grads)
    loss = lax.psum(loss[0, 0], ("x", "y", "c"))

    after = grad_x
    for key, kind, names in (("b", "exchange", ("b_w_in", "b_w_out")), ("c", "exchange", ("a_w_out",)),
                             ("a", "reduce_chip", ("a_w_in",))):
        sems, parts, lands = flight[key]
        recv = copies_wait(kind, sems, parts, lands, after, name="exchange_" + key + "_wait")
        for name, got_parts in zip(names, recv):
            if name == "b_w_in":
                raw = adamw(got_parts, *[jnp.transpose(t[name][0]) for t in (w, m, v)], name="adamw_" + name, tc=256)
                outs = [jnp.transpose(o) for o in raw]
            else:
                raw = outs = adamw(got_parts, w[name][0], m[name][0], v[name][0], name="adamw_" + name)
            for k, vals in zip(("grad", "delta", "new_m", "new_v"), outs):
                res[k][name] = vals[None]
            after = raw[0]

    return (loss, grad_x[None], *[res["grad"][k] for k in ORDER], *[res["delta"][k] for k in ORDER],
            *[res["new_m"][k] for k in ORDER], *[res["new_v"][k] for k in ORDER])
```

```python
import functools
import math

import jax
import jax.numpy as jnp
from jax import lax
from jax.experimental import pallas as pl
from jax.experimental.pallas import tpu as pltpu

F32 = jnp.float32
BF16 = jnp.bfloat16

N_DEV = 8
LANES = 128
CHUNK = 64
SUB = 16
HALO = 8
CONV_K = 4
DEPTH = 2
DEEPNORM_ALPHA = (2.0 * DEPTH) ** 0.25
LN_EPS = 1e-5
RMS_EPS = 1e-6
L2_EPS = 1e-6
EXP_CLAMP = 60.0
ADAM_LR = 0.001
ADAM_B1 = 0.9
ADAM_B2 = 0.999
ADAM_EPS = 1e-08
ADAM_WD = 0.01
ADAM_STEP = 10
VMEM_LIMIT = 56 * 1024 * 1024
HEADS_PER_STEP = 4

NN = ((1,), (0,))
NT = ((1,), (1,))
TN = ((0,), (0,))


def _dg(a, b, dims, precision=None):
    return lax.dot_general(a, b, (dims, ((), ())), precision=precision, preferred_element_type=F32)


def _bdot(a, b, dims):
    return _dg(a.astype(BF16), b.astype(BF16), dims)


def _split3(t):
    hi = t.astype(BF16)
    r = t - hi.astype(F32)
    mid = r.astype(BF16)
    return hi, mid, (r - mid.astype(F32)).astype(BF16)


@jax.custom_vjp
def _sdot(sel, t):
    sel = sel.astype(BF16)
    hi, mid, lo = _split3(t)
    return (_dg(sel, lo, NN) + _dg(sel, mid, NN)) + _dg(sel, hi, NN)


def _sdot_fwd(sel, t):
    return _sdot(sel, t), sel


def _sdot_bwd(sel, ct):
    selb = sel.astype(BF16)
    hi, mid, lo = _split3(ct)
    return jnp.zeros_like(sel), (_dg(selb, lo, TN) + _dg(selb, mid, TN)) + _dg(selb, hi, TN)


_sdot.defvjp(_sdot_fwd, _sdot_bwd)


@jax.custom_vjp
def _tri_inv(m):
    n = m.shape[0]
    row = lax.broadcasted_iota(jnp.int32, (n, n), 0)
    col = lax.broadcasted_iota(jnp.int32, (n, n), 1)
    inv = (col == row).astype(F32) + m
    mp = m
    for _ in range(int(math.log2(CHUNK)) - 1):
        mp = _bdot(mp, mp, NN)
        inv = inv + _bdot(inv, mp, NN)
    return inv


def _tri_inv_fwd(m):
    inv = _tri_inv(m)
    return inv, inv


def _tri_inv_bwd(inv, ct):
    return (_bdot(_bdot(inv, ct, TN), inv, NT),)


_tri_inv.defvjp(_tri_inv_fwd, _tri_inv_bwd)


@jax.custom_vjp
def _tri_inv_known(m, inv):
    return inv


_tri_inv_known.defvjp(lambda m, inv: (inv, inv), lambda inv, ct: (*_tri_inv_bwd(inv, ct), jnp.zeros_like(inv)))


@jax.custom_vjp
def _bdot_cols2(a, b1, b2):
    return _bdot_cols2_fwd(a, b1, b2)[0]


def _bdot_cols2_fwd(a, b1, b2):
    both = jnp.concatenate([b1, b2], axis=1)
    r = _bdot(a, both, NN)
    return (r[:, :b1.shape[1]], r[:, b1.shape[1]:]), (a, both)


def _bdot_cols2_bwd(res, cts):
    a, both = res
    ct = jnp.concatenate(cts, axis=1)
    d_both = _bdot(a, ct, TN)
    half = both.shape[1] // 2
    return _bdot(ct, both, NT), d_both[:, :half], d_both[:, half:]


_bdot_cols2.defvjp(_bdot_cols2_fwd, _bdot_cols2_bwd)


@functools.partial(jax.custom_vjp, nondiff_argnums=(2,))
def _shift_rows(xm, xh, back):
    r = pltpu.roll(xm, back, 0)
    row = lax.broadcasted_iota(jnp.int32, xh.shape, 0)
    top = jnp.where(row < back, pltpu.roll(xh, back, 0), r[0:HALO])
    return jnp.concatenate([top, r[HALO:]], axis=0)


def _shift_rows_fwd(xm, xh, back):
    return _shift_rows(xm, xh, back), None


def _shift_rows_bwd(back, _, ct):
    row = lax.broadcasted_iota(jnp.int32, ct.shape, 0)
    dxm = jnp.where(row < CHUNK - back, pltpu.roll(ct, CHUNK - back, 0), 0.0)
    rowh = lax.broadcasted_iota(jnp.int32, (HALO, ct.shape[1]), 0)
    dxh = jnp.where(rowh >= HALO - back, pltpu.roll(ct[0:HALO], HALO - back, 0), 0.0)
    return dxm, dxh


_shift_rows.defvjp(_shift_rows_fwd, _shift_rows_bwd)


def _silu(t):
    return t * jax.nn.sigmoid(t)


def _softplus(t):
    return jnp.where(t > 20.0, t, jnp.log1p(jnp.exp(jnp.minimum(t, 20.0))))


def _cparams(sem=None):
    kw = dict(vmem_limit_bytes=VMEM_LIMIT)
    if sem is not None:
        kw["dimension_semantics"] = sem
    return pltpu.CompilerParams(**kw)


def matmul(a, b, *, name, nt=False, lhs_t=False, n=None, addend=None, alpha=1.0, out_dtype=F32, out_split=1,
           out_rows=None, into=None, after=None, tm=1024, tn=1024, tk=4096):
    k, m = a.shape[::-1] if not lhs_t else a.shape
    b_split = b.shape[0] if b.ndim == 3 else 1
    b_rows, b_cols = b.shape[-2], b.shape[-1] * b_split
    n = (b_rows if nt else b_cols) if n is None else n
    tm, tn, tk = min(tm, m), min(tn, n), min(tk, k)
    if b_split > 1:
        part = b_cols // b_split
        tn, tk = (tn, min(tk, part)) if nt else (min(tn, part), tk)
    if out_split > 1:
        tn = min(tn, n // out_split)
    assert m % tm == 0 and n % tn == 0 and k % tk == 0, (a.shape, b.shape, nt)
    assert not (lhs_t and nt)
    nk = k // tk
    dims = TN if lhs_t else (NT if nt else NN)

    def body(*refs):
        a_ref, b_ref = refs[:2]
        add_ref = None if addend is None else refs[2]

        def finish(r, o_ref):
            if add_ref is not None:
                r = r + alpha * add_ref[...].astype(F32)
            o_ref[...] = r.astype(o_ref.dtype)

        if nk == 1:
            finish(_dg(a_ref[...], b_ref[...], dims), refs[-1])
            return
        o_ref, acc_ref = refs[-2:]
        kk = pl.program_id(2)

        @pl.when(kk == 0)
        def _():
            acc_ref[...] = _dg(a_ref[...], b_ref[...], dims)

        @pl.when(kk > 0)
        def _():
            acc_ref[...] += _dg(a_ref[...], b_ref[...], dims)

        @pl.when(kk == nk - 1)
        def _():
            finish(acc_ref[...], o_ref)

    if b_split == 1:
        b_spec = (pl.BlockSpec((tn, tk), lambda i, j, kk: (j, kk)) if nt
                  else pl.BlockSpec((tk, tn), lambda i, j, kk: (kk, j)))
    elif nt:
        per = (b_cols // b_split) // tk
        b_spec = pl.BlockSpec((None, tn, tk), lambda i, j, kk: (kk // per, j, kk % per))
    else:
        per = (b_cols // b_split) // tn
        b_spec = pl.BlockSpec((None, tk, tn), lambda i, j, kk: (j // per, kk, j % per))
    a_spec = (pl.BlockSpec((tk, tm), lambda i, j, kk: (kk, i)) if lhs_t
              else pl.BlockSpec((tm, tk), lambda i, j, kk: (i, kk)))
    in_specs = [a_spec, b_spec]
    args = [a, b]
    if addend is not None:
        in_specs.append(pl.BlockSpec((tm, tn), lambda i, j, kk: (i, j)))
        args.append(addend)
    for extra in (after, None if into is None else into[0]):
        if extra is not None:
            in_specs.append(pl.BlockSpec(memory_space=pl.ANY))
            args.append(extra)
    aliases = {}
    if into is not None:
        assert out_split == 1 and into[1] % tm == 0 and into[0].dtype == out_dtype
        row0 = into[1] // tm
        out_spec = pl.BlockSpec((tm, tn), lambda i, j, kk: (row0 + i, j))
        out_shape = jax.ShapeDtypeStruct(into[0].shape, out_dtype)
        aliases = {len(args) - 1: 0}
    elif out_split == 1:
        out_spec = pl.BlockSpec((tm, tn), lambda i, j, kk: (i, j))
        out_shape = jax.ShapeDtypeStruct((m if out_rows is None else out_rows, n), out_dtype)
    else:
        per_o = (n // out_split) // tn
        out_spec = pl.BlockSpec((None, tm, tn), lambda i, j, kk: (j // per_o, i, j % per_o))
        out_shape = jax.ShapeDtypeStruct((out_split, m, n // out_split), out_dtype)
    return pl.pallas_call(
        body, name=name, grid=(m // tm, n // tn, nk),
        in_specs=in_specs,
        out_specs=out_spec,
        out_shape=out_shape,
        input_output_aliases=aliases,
        scratch_shapes=[] if nk == 1 else [pltpu.VMEM((tm, tn), F32)],
        compiler_params=_cparams(("parallel", "parallel", "arbitrary")),
    )(*args)


def _gated_rmsnorm(o, z, nw):
    r = lax.rsqrt(jnp.mean(o * o, axis=1, keepdims=True) + RMS_EPS)
    return o * r * nw * _silu(z)


def _chunk_consts():
    row = lax.broadcasted_iota(jnp.int32, (CHUNK, CHUNK), 0)
    col = lax.broadcasted_iota(jnp.int32, (CHUNK, CHUNK), 1)
    return row, col


def _head_cols(tb, groups, hp, j, rev_nb=None):
    def bb(b):
        return b if rev_nb is None else rev_nb - 1 - b
    return pl.BlockSpec((tb, hp * LANES), lambda b, h: (bb(b), j * groups + h))


def _stack(pieces):
    return jnp.concatenate(pieces, axis=0) if len(pieces) > 1 else pieces[0]


def _place(t, g):
    if g == 1:
        return t
    head = lax.broadcasted_iota(jnp.int32, t.shape, 0) // CHUNK
    return jnp.concatenate([jnp.where(head == h, t, 0.0) for h in range(g)], axis=1)


def _hgrn2_chunk(qr, fr, iv, z, a0, a1, a2, nw, st):
    g = len(qr)
    n = g * CHUNK
    row = lax.broadcasted_iota(jnp.int32, (n, n), 0)
    col = lax.broadcasted_iota(jnp.int32, (n, n), 1)
    same = (row // CHUNK) == (col // CHUNK)
    qs, ks, gl = [], [], []
    for h in range(g):
        amax = jnp.maximum(jnp.maximum(a0[h], a1[h]), a2[h])
        e0, e1, e2 = jnp.exp(a0[h] - amax), jnp.exp(a1[h] - amax), jnp.exp(a2[h] - amax)
        lb = e0 / (e0 + e1 + e2)
        forget = lb + (1.0 - lb) * jax.nn.sigmoid(fr[h])
        qs.append(_silu(qr[h]))
        ks.append(1.0 - forget)
        gl.append(jnp.log(forget))
    q, k, glog, v, zs = _stack(qs), _stack(ks), _stack(gl), _stack(list(iv)), _stack(list(z))
    cum = _sdot((same & (col <= row)).astype(F32), glog)
    sub = (row % CHUNK) // SUB
    rowl = lax.broadcasted_iota(jnp.int32, (n, LANES), 0)
    headl, subl = rowl // CHUNK, (rowl % CHUNK) // SUB

    def row_of(r):
        picks = [jnp.sum(jnp.where(rowl == h * CHUNK + r, cum, 0.0), axis=0, keepdims=True) for h in range(g)]
        out = jnp.zeros((n, LANES), F32)
        for h in range(g):
            out = out + jnp.where(headl == h, picks[h], 0.0)
        return out, picks

    refs = [jnp.zeros((n, LANES), F32)] + [row_of(i * SUB - 1)[0] for i in range(1, CHUNK // SUB)]
    own = jnp.zeros((n, LANES), F32)
    for i, ref in enumerate(refs):
        own = own + jnp.where(subl == i, ref, 0.0)
    qt = q * jnp.exp(cum - own)
    att = jnp.zeros((n, n), F32)
    for i, ref in enumerate(refs):
        kt = k * jnp.exp(jnp.minimum(ref - cum, EXP_CLAMP))
        att = att + jnp.where(same & (sub == i) & (col <= row), _bdot(qt, kt, NT), 0.0)
    cl_rows, lasts = row_of(CHUNK - 1)
    o = _bdot(att, v, NN) + _bdot(_place(q * jnp.exp(cum), g), st, NT)
    kd = k * jnp.exp(cl_rows - cum)
    cl_wide = jnp.concatenate(lasts, axis=1) if g > 1 else lasts[0]
    st_new = st * jnp.exp(cl_wide) + _bdot(v, _place(kd, g), TN)
    return _gated_rmsnorm(o, zs, nw), st_new


def hgrn2_fwd(hproj, alb, nw, *, tb, hp):
    t, d4 = hproj.shape
    d = d4 // 4
    heads = d // LANES
    groups = heads // hp
    nb, nc = t // tb, tb // CHUNK

    def body(q_ref, f_ref, i_ref, z_ref, alb_ref, nw_ref, y_ref, states_ref, st_ref):
        b, hg = pl.program_id(0), pl.program_id(1)

        @pl.when(b == 0)
        def _():
            st_ref[hg] = jnp.zeros((LANES, hp * LANES), F32)

        nwv = nw_ref[...]
        lns = [slice(p * LANES, (p + 1) * LANES) for p in range(hp)]
        albs = []
        for r in range(3):
            albs.append(tuple(alb_ref[r:r + 1, pl.ds(pl.multiple_of((hg * hp + p) * LANES, LANES), LANES)]
                              for p in range(hp)))

        def step(c, carry):
            rows = pl.ds(pl.multiple_of(c * CHUNK, CHUNK), CHUNK)
            st = st_ref[hg]
            states_ref[c, 0] = st
            ins = [tuple(ref[rows, ln] for ln in lns) for ref in (q_ref, f_ref, i_ref, z_ref)]
            y, st_new = _hgrn2_chunk(*ins, *albs, nwv, st)
            for p in range(hp):
                y_ref[rows, lns[p]] = y[p * CHUNK:(p + 1) * CHUNK].astype(y_ref.dtype)
            st_ref[hg] = st_new
            return carry

        lax.fori_loop(0, nc, step, 0, unroll=True)

    return pl.pallas_call(
        body, name="hgrn2_fwd", grid=(nb, groups),
        in_specs=[_head_cols(tb, groups, hp, j) for j in range(4)] + [
            pl.BlockSpec((3, d), lambda b, h: (0, 0)),
            pl.BlockSpec((1, LANES), lambda b, h: (0, 0))],
        out_specs=[pl.BlockSpec((tb, hp * LANES), lambda b, h: (b, h)),
                   pl.BlockSpec((nc, 1, LANES, hp * LANES), lambda b, h: (b, h, 0, 0))],
        out_shape=[jax.ShapeDtypeStruct((t, d), BF16),
                   jax.ShapeDtypeStruct((t // CHUNK, groups, LANES, hp * LANES), F32)],
        scratch_shapes=[pltpu.VMEM((groups, LANES, hp * LANES), F32)],
        compiler_params=_cparams(("arbitrary", "arbitrary")),
    )(hproj, hproj, hproj, hproj, alb, nw)


def hgrn2_bwd(hproj, alb, nw, states, dy, *, tb, hp):
    t, d4 = hproj.shape
    d = d4 // 4
    heads = d // LANES
    groups = heads // hp
    nb, nc = t // tb, tb // CHUNK

    def body(q_ref, f_ref, i_ref, z_ref, alb_ref, nw_ref, states_ref, dy_ref,
             dh_ref, dalb_ref, dnw_ref, dst_ref):
        b, hg = pl.program_id(0), pl.program_id(1)

        @pl.when(b == 0)
        def _():
            dst_ref[hg] = jnp.zeros((LANES, hp * LANES), F32)

        @pl.when((b == 0) & (hg == 0))
        def _():
            dalb_ref[...] = jnp.zeros_like(dalb_ref)
            dnw_ref[...] = jnp.zeros_like(dnw_ref)

        nwv = nw_ref[...]
        lns = [slice(p * LANES, (p + 1) * LANES) for p in range(hp)]
        lanes_of = [pl.ds(pl.multiple_of((hg * hp + p) * LANES, LANES), LANES) for p in range(hp)]
        albs = [tuple(alb_ref[r:r + 1, lanes_of[p]] for p in range(hp)) for r in range(3)]

        def step(i, carry):
            c = nc - 1 - i
            rows = pl.ds(pl.multiple_of(c * CHUNK, CHUNK), CHUNK)
            ins = [tuple(ref[rows, ln] for ln in lns) for ref in (q_ref, f_ref, i_ref, z_ref)]
            dy = _stack([dy_ref[rows, ln].astype(F32) for ln in lns])
            _, vjp = jax.vjp(_hgrn2_chunk, *ins, *albs, nwv, states_ref[c, 0])
            dq, df, di, dz, da0, da1, da2, dnw, dst = vjp((dy, dst_ref[hg]))
            for p in range(hp):
                h = hg * hp + p
                for j, val in enumerate((dq, df, di, dz)):
                    dh_ref[rows, pl.ds(pl.multiple_of(j * d + h * LANES, LANES), LANES)] = val[p].astype(dh_ref.dtype)
                dalb_ref[0:1, lanes_of[p]] += da0[p]
                dalb_ref[1:2, lanes_of[p]] += da1[p]
                dalb_ref[2:3, lanes_of[p]] += da2[p]
            dnw_ref[0:1, :] += dnw
            dst_ref[hg] = dst
            return carry

        lax.fori_loop(0, nc, step, 0, unroll=True)

    return pl.pallas_call(
        body, name="hgrn2_bwd", grid=(nb, groups),
        in_specs=[_head_cols(tb, groups, hp, j, rev_nb=nb) for j in range(4)] + [
            pl.BlockSpec((3, d), lambda b, h: (0, 0)),
            pl.BlockSpec((1, LANES), lambda b, h: (0, 0)),
            pl.BlockSpec((nc, 1, LANES, hp * LANES), lambda b, h: (nb - 1 - b, h, 0, 0)),
            pl.BlockSpec((tb, hp * LANES), lambda b, h: (nb - 1 - b, h))],
        out_specs=[pl.BlockSpec((tb, d4), lambda b, h: (nb - 1 - b, 0)),
                   pl.BlockSpec((3, d), lambda b, h: (0, 0)),
                   pl.BlockSpec((8, LANES), lambda b, h: (0, 0))],
        out_shape=[jax.ShapeDtypeStruct((t, d4), BF16),
                   jax.ShapeDtypeStruct((3, d), F32),
                   jax.ShapeDtypeStruct((8, LANES), F32)],
        scratch_shapes=[pltpu.VMEM((groups, LANES, hp * LANES), F32)],
        compiler_params=_cparams(("arbitrary", "arbitrary")),
    )(hproj, hproj, hproj, hproj, alb, nw, states, dy)


def _conv_silu(xm, xh, w):
    acc = w[CONV_K - 1] * xm
    for j in range(CONV_K - 1):
        acc = acc + w[j] * _shift_rows(xm, xh, CONV_K - 1 - j)
    return _silu(acc)


def _l2norm(t):
    return t * lax.rsqrt(jnp.sum(t * t, axis=1, keepdims=True) + L2_EPS)


def _gdn_chunk(head0, hh, known_inv, qm, qh, km, kh, vm, vh, z, tail, wq, wk, wv, alog_row, dtb_row, nw, s):
    g = len(qm)
    n = g * CHUNK
    row = lax.broadcasted_iota(jnp.int32, (n, n), 0)
    col = lax.broadcasted_iota(jnp.int32, (n, n), 1)
    same = (row // CHUNK) == (col // CHUNK)
    lane = lax.broadcasted_iota(jnp.int32, (CHUNK, LANES), 1)
    heads_lane = lax.broadcasted_iota(jnp.int32, (1, LANES), 1)
    q = _l2norm(_stack([_conv_silu(qm[h], qh[h], wq[h]) for h in range(g)])) * (LANES ** -0.5)
    k = _l2norm(_stack([_conv_silu(km[h], kh[h], wk[h]) for h in range(g)]))
    v = _stack([_conv_silu(vm[h], vh[h], wv[h]) for h in range(g)])
    zs = _stack(list(z))
    betas, gs = [], []
    for h in range(g):
        head = head0 + h
        betas.append(jax.nn.sigmoid(jnp.sum(jnp.where(lane == head, tail, 0.0), axis=1, keepdims=True)))
        a_t = jnp.sum(jnp.where(lane == hh + head, tail, 0.0), axis=1, keepdims=True)
        alog = jnp.sum(jnp.where(heads_lane == head, alog_row, 0.0), axis=1, keepdims=True)
        dtb = jnp.sum(jnp.where(heads_lane == head, dtb_row, 0.0), axis=1, keepdims=True)
        gs.append(-jnp.exp(alog) * _softplus(a_t + dtb))
    beta, gcol = _stack(betas), _stack(gs)
    tril = (same & (col <= row)).astype(F32)
    eye = (col == row).astype(F32)
    if n % LANES == 0:
        cum1 = _sdot(tril, gcol + jnp.zeros((n, LANES), F32))
        cum = jnp.concatenate([cum1] * (n // LANES), axis=1)
        lane1 = lax.broadcasted_iota(jnp.int32, (n, LANES), 1)
        cum_c = jnp.sum(jnp.where(lane1 == 0, cum1, 0.0), axis=1, keepdims=True)
    else:
        cum = _sdot(tril, gcol + jnp.zeros((n, n), F32))
        cum_c = jnp.sum(cum * eye, axis=1, keepdims=True)
    cum_r = cum.T
    diff = cum - cum_r
    strict = same & (col < row)
    incl = same & (col <= row)
    dec_incl = jnp.exp(jnp.where(incl, diff, -1e30))
    dec_strict = jnp.where(strict, dec_incl, 0.0)
    last = same & (col % CHUNK == CHUNK - 1)
    cl_rows = jnp.sum(jnp.where(last, cum_r, 0.0), axis=1, keepdims=True)
    ecum = jnp.exp(cum_c)
    m = -(beta * _bdot(k, k, NT) * dec_strict)
    inv = _tri_inv(m) if known_inv is None else _tri_inv_known(m, known_inv)
    u0, w = _bdot_cols2(inv, beta * v, (beta * ecum) * k)
    qk = _bdot(q, k, NT) * dec_incl
    u = u0 - _bdot(_place(w, g), s, NN)
    o = _bdot(_place(q * ecum, g), s, NN) + _bdot(qk, u, NN)
    kd = k * jnp.exp(cl_rows - cum_c)
    row1 = lax.broadcasted_iota(jnp.int32, (n, 1), 0)
    decay = []
    for h in range(g):
        cl_h = jnp.sum(jnp.where(row1 == h * CHUNK + CHUNK - 1, cum_c, 0.0), axis=0, keepdims=True)
        decay.append(jnp.exp(cl_h) + jnp.zeros((LANES, 1), F32))
    s_new = _stack(decay) * s + _bdot(_place(kd, g), u, TN)
    y = _gated_rmsnorm(o, zs, nw)
    return (y, s_new, inv) if known_inv is None else (y, s_new)


def _gdn_in_specs(tb, heads, hp, rev_nb=None):
    groups = heads // hp

    def bb(b):
        return b if rev_nb is None else rev_nb - 1 - b

    def halo(j):
        return pl.BlockSpec((HALO, hp * LANES),
                            lambda b, h, j=j: (jnp.maximum(bb(b) * (tb // HALO) - 1, 0), j * groups + h))

    def main(j):
        return _head_cols(tb, groups, hp, j, rev_nb=rev_nb)

    return [main(0), halo(0), main(1), halo(1), main(2), halo(2), main(3),
            pl.BlockSpec((tb, LANES), lambda b, h: (bb(b), 0)),
            pl.BlockSpec((CONV_K, 3 * heads * LANES), lambda b, h: (0, 0)),
            pl.BlockSpec((8, LANES), lambda b, h: (0, 0)),
            pl.BlockSpec((1, LANES), lambda b, h: (0, 0))]


def _gdn_chunk_args(c, head0, hp, blk, heads, q_ref, qh_ref, k_ref, kh_ref, v_ref, vh_ref, z_ref, tail_ref, cw_ref, prm_ref, nw_ref):
    d = heads * LANES
    lns = [slice(p * LANES, (p + 1) * LANES) for p in range(hp)]
    rows = pl.ds(pl.multiple_of(c * CHUNK, CHUNK), CHUNK)
    prev = pl.ds(pl.multiple_of(jnp.maximum(c * CHUNK - HALO, 0), HALO), HALO)
    first = c == 0
    live = jnp.where(first & (blk == 0), 0.0, 1.0)

    def main_of(ref):
        return tuple(ref[rows, ln] for ln in lns)

    def halo_of(ref, href):
        return tuple(jnp.where(first, href[:, ln], ref[prev, ln]) * live for ln in lns)

    def cw(j):
        out = []
        for p in range(hp):
            lanes = pl.ds(pl.multiple_of(j * d + (head0 + p) * LANES, LANES), LANES)
            out.append(tuple(cw_ref[r:r + 1, lanes] for r in range(CONV_K)))
        return tuple(out)

    return (main_of(q_ref), halo_of(q_ref, qh_ref), main_of(k_ref), halo_of(k_ref, kh_ref),
            main_of(v_ref), halo_of(v_ref, vh_ref), main_of(z_ref), tail_ref[rows, :],
            cw(0), cw(1), cw(2), prm_ref[0:1, :], prm_ref[1:2, :], nw_ref[...])


def gdn_fwd(hmain, tail, conv_w, prm, nw, *, tb, hp):
    t, d4 = hmain.shape
    d = d4 // 4
    heads = d // LANES
    groups = heads // hp
    nb, nc = t // tb, tb // CHUNK
    n = hp * CHUNK

    def body(q_ref, qh_ref, k_ref, kh_ref, v_ref, vh_ref, z_ref, tail_ref, cw_ref, prm_ref, nw_ref,
             y_ref, states_ref, inv_ref, s_ref):
        b, hg = pl.program_id(0), pl.program_id(1)

        @pl.when(b == 0)
        def _():
            s_ref[hg] = jnp.zeros((hp * LANES, LANES), F32)

        def step(c, carry):
            rows = pl.ds(pl.multiple_of(c * CHUNK, CHUNK), CHUNK)
            s = s_ref[hg]
            states_ref[c, 0] = s
            args = _gdn_chunk_args(c, hg * hp, hp, b, heads, q_ref, qh_ref, k_ref, kh_ref, v_ref, vh_ref, z_ref,
                                   tail_ref, cw_ref, prm_ref, nw_ref)
            y, s_new, inv = _gdn_chunk(hg * hp, heads, None, *args, s)
            for p in range(hp):
                y_ref[rows, p * LANES:(p + 1) * LANES] = y[p * CHUNK:(p + 1) * CHUNK].astype(y_ref.dtype)
            inv_ref[c, 0] = inv.astype(inv_ref.dtype)
            s_ref[hg] = s_new
            return carry

        lax.fori_loop(0, nc, step, 0, unroll=True)

    return pl.pallas_call(
        body, name="gdn_fwd", grid=(nb, groups),
        in_specs=_gdn_in_specs(tb, heads, hp),
        out_specs=[pl.BlockSpec((tb, hp * LANES), lambda b, h: (b, h)),
                   pl.BlockSpec((nc, 1, hp * LANES, LANES), lambda b, h: (b, h, 0, 0)),
                   pl.BlockSpec((nc, 1, n, n), lambda b, h: (b, h, 0, 0))],
        out_shape=[jax.ShapeDtypeStruct((t, d), BF16),
                   jax.ShapeDtypeStruct((t // CHUNK, groups, hp * LANES, LANES), F32),
                   jax.ShapeDtypeStruct((t // CHUNK, groups, n, n), BF16)],
        scratch_shapes=[pltpu.VMEM((groups, hp * LANES, LANES), F32)],
        compiler_params=_cparams(("arbitrary", "arbitrary")),
    )(hmain, hmain, hmain, hmain, hmain, hmain, hmain, tail, conv_w, prm, nw)


def gdn_bwd(hmain, tail, conv_w, prm, nw, states, invs, dy, *, tb, hp):
    t, d4 = hmain.shape
    d = d4 // 4
    heads = d // LANES
    groups = heads // hp
    nb, nc = t // tb, tb // CHUNK

    def body(q_ref, qh_ref, k_ref, kh_ref, v_ref, vh_ref, z_ref, tail_ref, cw_ref, prm_ref, nw_ref,
             states_ref, inv_ref, dy_ref, dh_ref, dtail_ref, dcw_ref, dprm_ref, ds_ref, pend_ref):
        b, hg = pl.program_id(0), pl.program_id(1)
        blk = nb - 1 - b

        @pl.when(b == 0)
        def _():
            ds_ref[hg] = jnp.zeros((hp * LANES, LANES), F32)
            for p in range(hp):
                pend_ref[hg * hp + p] = jnp.zeros((3, HALO, LANES), F32)

        @pl.when((b == 0) & (hg == 0))
        def _():
            dcw_ref[...] = jnp.zeros_like(dcw_ref)
            dprm_ref[...] = jnp.zeros_like(dprm_ref)

        @pl.when(hg == 0)
        def _():
            dtail_ref[...] = jnp.zeros_like(dtail_ref)

        def step(i, carry):
            c = nc - 1 - i
            rows = pl.ds(pl.multiple_of(c * CHUNK, CHUNK), CHUNK)
            zpad = jnp.zeros((CHUNK - HALO, LANES), F32)
            args = _gdn_chunk_args(c, hg * hp, hp, blk, heads, q_ref, qh_ref, k_ref, kh_ref, v_ref, vh_ref, z_ref,
                                   tail_ref, cw_ref, prm_ref, nw_ref)
            dy = _stack([dy_ref[rows, p * LANES:(p + 1) * LANES].astype(F32) for p in range(hp)])
            pends = [pend_ref[hg * hp + p] for p in range(hp)]
            known_inv = inv_ref[c, 0].astype(F32)
            _, vjp = jax.vjp(functools.partial(_gdn_chunk, hg * hp, heads, known_inv), *args, states_ref[c, 0])
            (dqm, dqh, dkm, dkh, dvm, dvh, dz, dtl, dwq, dwk, dwv, dalog, ddtb, dnw, ds) = vjp((dy, ds_ref[hg]))
            for p in range(hp):
                h = hg * hp + p
                for j, (dm, dhalo) in enumerate(((dqm, dqh), (dkm, dkh), (dvm, dvh))):
                    full = dm[p] + jnp.concatenate([zpad, pends[p][j]], axis=0)
                    dh_ref[rows, pl.ds(pl.multiple_of(j * d + h * LANES, LANES), LANES)] = full.astype(dh_ref.dtype)
                    pend_ref[h, j] = dhalo[p]
                dh_ref[rows, pl.ds(pl.multiple_of(3 * d + h * LANES, LANES), LANES)] = dz[p].astype(dh_ref.dtype)
                for j, dw in enumerate((dwq, dwk, dwv)):
                    lanes = pl.ds(pl.multiple_of(j * d + h * LANES, LANES), LANES)
                    for r in range(CONV_K):
                        dcw_ref[r:r + 1, lanes] += dw[p][r]
            dtail_ref[rows, :] += dtl
            dprm_ref[0:1, :] += dalog
            dprm_ref[1:2, :] += ddtb
            dprm_ref[2:3, :] += dnw
            ds_ref[hg] = ds
            return carry

        lax.fori_loop(0, nc, step, 0, unroll=True)

    return pl.pallas_call(
        body, name="gdn_bwd", grid=(nb, groups),
        in_specs=_gdn_in_specs(tb, heads, hp, rev_nb=nb) + [
            pl.BlockSpec((nc, 1, hp * LANES, LANES), lambda b, h: (nb - 1 - b, h, 0, 0)),
            pl.BlockSpec((nc, 1, hp * CHUNK, hp * CHUNK), lambda b, h: (nb - 1 - b, h, 0, 0)),
            pl.BlockSpec((tb, hp * LANES), lambda b, h: (nb - 1 - b, h))],
        out_specs=[pl.BlockSpec((tb, d4), lambda b, h: (nb - 1 - b, 0)),
                   pl.BlockSpec((tb, LANES), lambda b, h: (nb - 1 - b, 0)),
                   pl.BlockSpec((CONV_K, 3 * d), lambda b, h: (0, 0)),
                   pl.BlockSpec((8, LANES), lambda b, h: (0, 0))],
        out_shape=[jax.ShapeDtypeStruct((t, d4), BF16),
                   jax.ShapeDtypeStruct((t, LANES), F32),
                   jax.ShapeDtypeStruct((CONV_K, 3 * d), F32),
                   jax.ShapeDtypeStruct((8, LANES), F32)],
        scratch_shapes=[pltpu.VMEM((groups, hp * LANES, LANES), F32),
                        pltpu.VMEM((heads, 3, HALO, LANES), F32)],
        compiler_params=_cparams(("arbitrary", "arbitrary")),
    )(hmain, hmain, hmain, hmain, hmain, hmain, hmain, tail, conv_w, prm, nw, states, invs, dy)


def _layer_norm(u, g, b):
    mu = jnp.mean(u, axis=1, keepdims=True)
    var = jnp.mean(jnp.square(u - mu), axis=1, keepdims=True)
    return (u - mu) * lax.rsqrt(var + LN_EPS) * g + b


def ln_fwd(u, g, b, *, tr):
    t, d = u.shape

    def body(u_ref, g_ref, b_ref, x_ref, xb_ref):
        x = _layer_norm(u_ref[...], g_ref[...], b_ref[...])
        x_ref[...] = x
        xb_ref[...] = x.astype(BF16)

    row = pl.BlockSpec((tr, d), lambda i: (i, 0))
    vec = pl.BlockSpec((1, d), lambda i: (0, 0))
    return pl.pallas_call(
        body, name="ln_fwd", grid=(t // tr,), in_specs=[row, vec, vec], out_specs=[row, row],
        out_shape=[jax.ShapeDtypeStruct((t, d), F32), jax.ShapeDtypeStruct((t, d), BF16)],
        compiler_params=_cparams(("parallel",)),
    )(u, g, b)


def ln_bwd(u, g, b, dout, *, tr):
    t, d = u.shape

    def body(u_ref, g_ref, b_ref, dout_ref, du_ref, dub_ref, dg_ref, db_ref):
        @pl.when(pl.program_id(0) == 0)
        def _():
            dg_ref[...] = jnp.zeros_like(dg_ref)
            db_ref[...] = jnp.zeros_like(db_ref)

        _, vjp = jax.vjp(_layer_norm, u_ref[...], g_ref[...], b_ref[...])
        du, dg, db = vjp(dout_ref[...])
        du_ref[...] = du
        dub_ref[...] = du.astype(BF16)
        dg_ref[0:1, :] += dg
        db_ref[0:1, :] += db

    row = pl.BlockSpec((tr, d), lambda i: (i, 0))
    vec = pl.BlockSpec((1, d), lambda i: (0, 0))
    acc = pl.BlockSpec((8, d), lambda i: (0, 0))
    return pl.pallas_call(
        body, name="ln_bwd", grid=(t // tr,), in_specs=[row, vec, vec, row], out_specs=[row, row, acc, acc],
        out_shape=[jax.ShapeDtypeStruct((t, d), F32), jax.ShapeDtypeStruct((t, d), BF16),
                   jax.ShapeDtypeStruct((8, d), F32), jax.ShapeDtypeStruct((8, d), F32)],
        compiler_params=_cparams(("arbitrary",)),
    )(u, g, b, dout)


def ln_loss_bwd(u, g, b, target, *, tr):
    t, d = u.shape

    def loss_of(uu, gg, bb, tgt):
        err = jnp.square(_layer_norm(uu, gg, bb) - tgt)
        return 0.5 * jnp.sum(jnp.mean(err, axis=1, keepdims=True), axis=0, keepdims=True)

    def body(u_ref, g_ref, b_ref, t_ref, loss_ref, du_ref, dub_ref, dg_ref, db_ref):
        @pl.when(pl.program_id(0) == 0)
        def _():
            loss_ref[...] = jnp.zeros_like(loss_ref)
            dg_ref[...] = jnp.zeros_like(dg_ref)
            db_ref[...] = jnp.zeros_like(db_ref)

        tgt = t_ref[...]
        val, vjp = jax.vjp(lambda uu, gg, bb: loss_of(uu, gg, bb, tgt), u_ref[...], g_ref[...], b_ref[...])
        du, dg, db = vjp(jnp.ones((1, 1), F32))
        loss_ref[...] += val
        du_ref[...] = du
        dub_ref[...] = du.astype(BF16)
        dg_ref[0:1, :] += dg
        db_ref[0:1, :] += db

    row = pl.BlockSpec((tr, d), lambda i: (i, 0))
    vec = pl.BlockSpec((1, d), lambda i: (0, 0))
    acc = pl.BlockSpec((8, d), lambda i: (0, 0))
    return pl.pallas_call(
        body, name="ln_loss_bwd", grid=(t // tr,), in_specs=[row, vec, vec, row],
        out_specs=[pl.BlockSpec((8, LANES), lambda i: (0, 0)), row, row, acc, acc],
        out_shape=[jax.ShapeDtypeStruct((8, LANES), F32),
                   jax.ShapeDtypeStruct((t, d), F32), jax.ShapeDtypeStruct((t, d), BF16),
                   jax.ShapeDtypeStruct((8, d), F32), jax.ShapeDtypeStruct((8, d), F32)],
        compiler_params=_cparams(("arbitrary",)),
    )(u, g, b, target)


def local_step(x, target, wa_in, alb, a_nw, wa_out, wb_t, wb_tail_t, conv_w, a_log, dt_bias, b_nw, wb_out,
               ln_g, ln_b, *, tb=256, tr=256, hp=HEADS_PER_STEP, first_after=None, b_weights=None, on_b_grads=None,
               on_a_out_grad=None, on_a_grads=None, on_mid=None, on_small_grads=None):
    t, d = x.shape
    heads = d // LANES
    tb, tr, hp = min(tb, t), min(tr, t), min(hp, heads)
    tb_fwd = 2 * tb if t % (2 * tb) == 0 else tb
    xb = x.astype(BF16)
    prm = jnp.zeros((8, LANES), F32).at[0, :heads].set(a_log[0]).at[1, :heads].set(dt_bias[0])

    ha = matmul(xb, wa_in, name="mm_a_in", after=first_after)
    ya, st_a = hgrn2_fwd(ha, alb, a_nw, tb=tb_fwd, hp=hp)
    u1 = matmul(ya, wa_out, name="mm_a_out", addend=x, alpha=DEEPNORM_ALPHA,
                after=None if on_mid is None else on_mid(ya))
    x1, x1b = ln_fwd(u1, ln_g[0:1], ln_b[0:1], tr=tr)
    if b_weights is not None:
        wb_t, wb_tail_t, wb_out, conv_w = b_weights(x1b)
    n_tail = 2 * heads
    hb = matmul(x1b, wb_t, name="mm_b_in", nt=True, n=4 * d)
    tl = matmul(x1b, wb_tail_t, name="mm_b_tail", nt=True)
    yb, st_b, inv_b = gdn_fwd(hb, tl, conv_w, prm, b_nw, tb=tb_fwd, hp=hp)
    u2 = matmul(yb, wb_out, name="mm_b_out", addend=x1, alpha=DEEPNORM_ALPHA)

    loss, du2, du2b, dg2, db2 = ln_loss_bwd(u2, ln_g[1:2], ln_b[1:2], target, tr=tr)
    d_wb_out = matmul(yb, du2b, name="mm_dwb_out", lhs_t=True, out_dtype=BF16)
    dyb = matmul(du2b, wb_out, name="mm_dyb", nt=True)
    dhb, dtl, d_conv, dprm = gdn_bwd(hb, tl, conv_w, prm, b_nw, st_b, inv_b, dyb, tb=tb, hp=hp)
    dtlb = dtl.astype(BF16)
    d_wb_t = matmul(dhb, x1b, name="mm_dwb_main", lhs_t=True, out_dtype=BF16, out_rows=4 * d + n_tail)
    d_wb_t = matmul(dtlb[:, :n_tail], x1b, name="mm_dwb_tail", lhs_t=True, out_dtype=BF16, into=(d_wb_t, 4 * d))
    sent_b = on_b_grads(d_wb_t, d_wb_out) if on_b_grads is not None else None
    dx1_tail = matmul(dtlb, wb_tail_t, name="mm_dx1_tail", addend=du2, alpha=DEEPNORM_ALPHA, after=sent_b)
    dx1 = matmul(dhb, wb_t, name="mm_dx1", addend=dx1_tail, alpha=1.0, tk=2048)
    du1, du1b, dg1, db1 = ln_bwd(u1, ln_g[0:1], ln_b[0:1], dx1, tr=tr)
    d_wa_out = matmul(ya, du1b, name="mm_dwa_out", lhs_t=True, out_dtype=BF16)
    sent_c = on_a_out_grad(d_wa_out) if on_a_out_grad is not None else None
    dya = matmul(du1b, wa_out, name="mm_dya", nt=True, after=sent_c)
    dha, d_alb, d_anw = hgrn2_bwd(ha, alb, a_nw, st_a, dya, tb=tb, hp=hp)
    small = dict(
        a_lower_bounds=d_alb, a_norm_w=d_anw[0:1], b_conv_w=d_conv,
        b_a_log=dprm[0:1, :heads], b_dt_bias=dprm[1:2, :heads], b_norm_w=dprm[2:3],
        ln_g=jnp.concatenate([dg1[0:1], dg2[0:1]], axis=0), ln_b=jnp.concatenate([db1[0:1], db2[0:1]], axis=0))
    d_wa_in = matmul(xb, dha, name="mm_dwa_in", lhs_t=True, out_dtype=BF16,
                     out_split=wa_in.shape[0] if wa_in.ndim == 3 else 1,
                     after=None if on_small_grads is None else on_small_grads(small))
    sent_a = on_a_grads(d_wa_in) if on_a_grads is not None else None
    grad_x = matmul(dha, wa_in, name="mm_dx", nt=True, addend=du1, alpha=DEEPNORM_ALPHA, after=sent_a, tk=2048)
    big = dict(a_w_in=d_wa_in, a_w_out=d_wa_out, b_w_t=d_wb_t, b_w_out=d_wb_out)
    return loss, grad_x, big, small


MESH_ID = pl.DeviceIdType.MESH


def _position():
    return lax.axis_index("x"), lax.axis_index("y"), lax.axis_index("c")


def _index_of(p):
    return 4 * p[0] + 2 * p[1] + p[2]


def all_gather(shards, *, name, space):
    n = len(shards)

    def body(*refs):
        ins, outs = refs[:n], refs[n:2 * n]
        send_sems, recv_sems, local_sems = refs[2 * n:]
        x, y, c = _position()
        me, sibling = (x, y, c), (x, y, 1 - c)
        chips = [(1 - x, y), (x, 1 - y), (1 - x, 1 - y)]

        def copy(a, k, block, to, own=False):
            dst = outs[a].at[_index_of(block)]
            return pltpu.make_async_remote_copy(
                src_ref=ins[a] if own else dst, dst_ref=dst,
                send_sem=send_sems.at[7 * a + k], recv_sem=recv_sems.at[7 * a + k],
                device_id=to, device_id_type=MESH_ID)

        mine = [pltpu.make_async_copy(ins[a], outs[a].at[_index_of(me)], local_sems.at[a]) for a in range(n)]
        for cp in mine:
            cp.start()
        first = []
        for a in range(n):
            first.append(copy(a, 0, me, sibling, own=True))
            first += [copy(a, 1 + j, me, (*chip, c), own=True) for j, chip in enumerate(chips)]
        for cp in first:
            cp.start()
        passed = []
        for j, chip in enumerate(chips):
            for a in range(n):
                copy(a, 1 + j, (*chip, c), me).wait_recv()
                fwd = copy(a, 4 + j, (*chip, c), sibling)
                fwd.start()
                passed.append(fwd)
        for a in range(n):
            copy(a, 0, sibling, me).wait_recv()
            for j, chip in enumerate(chips):
                copy(a, 4 + j, (*chip, 1 - c), me).wait_recv()
        for cp in first + passed:
            cp.wait_send()
        for cp in mine:
            cp.wait()

    spec = pl.BlockSpec(memory_space=space)
    return pl.pallas_call(
        body, name=name,
        in_specs=[spec] * n, out_specs=[spec] * n,
        out_shape=[jax.ShapeDtypeStruct((N_DEV, *s.shape), s.dtype) for s in shards],
        scratch_shapes=[pltpu.SemaphoreType.DMA((7 * n,)), pltpu.SemaphoreType.DMA((7 * n,)),
                        pltpu.SemaphoreType.DMA((n,))],
        compiler_params=pltpu.CompilerParams(vmem_limit_bytes=VMEM_LIMIT),
    )(*shards)


def all_gather_relay(shards, *, name):
    n, per = len(shards), 9

    def body(*refs):
        ins, outs = refs[:n], refs[n:2 * n]
        send_sems, recv_sems, local_sems = refs[2 * n:]
        x, y, c = _position()
        me, sibling = (x, y, c), (x, y, 1 - c)
        xn, yn, dg = (1 - x, y), (x, 1 - y), (1 - x, 1 - y)

        def parts(a):
            rows = shards[a].shape[0]
            return [(0, rows // 2), (rows // 2, rows // 2)] if rows % 32 == 0 else [(0, rows)]

        def copy(a, k, block, to, own=False, part=None):
            dst = outs[a].at[_index_of(block)]
            src = ins[a] if own else dst
            if part is not None:
                src, dst = src.at[pl.ds(*part)], dst.at[pl.ds(*part)]
            return pltpu.make_async_remote_copy(
                src_ref=src, dst_ref=dst, send_sem=send_sems.at[per * a + k], recv_sem=recv_sems.at[per * a + k],
                device_id=to, device_id_type=MESH_ID)

        mine = [pltpu.make_async_copy(ins[a], outs[a].at[_index_of(me)], local_sems.at[a]) for a in range(n)]
        for cp in mine:
            cp.start()
        started = []
        for a in range(n):
            started += [copy(a, 0, me, sibling, own=True), copy(a, 1, me, (*xn, c), own=True),
                        copy(a, 2, me, (*yn, c), own=True)]
        for cp in started:
            cp.start()

        def go(cp):
            cp.start()
            started.append(cp)

        for a in range(n):
            copy(a, 1, (*xn, c), me).wait_recv()
            go(copy(a, 3, (*xn, c), sibling))
            go(copy(a, 5, (*xn, c), (*yn, c), part=parts(a)[0]))
        for a in range(n):
            copy(a, 2, (*yn, c), me).wait_recv()
            go(copy(a, 4, (*yn, c), sibling))
            if len(parts(a)) == 2:
                go(copy(a, 6, (*yn, c), (*xn, c), part=parts(a)[1]))
        for a in range(n):
            for k, part in zip((5, 6), parts(a)):
                copy(a, k, (*dg, c), me, part=part).wait_recv()
                go(copy(a, k + 2, (*dg, c), sibling, part=part))
        for a in range(n):
            copy(a, 0, sibling, me).wait_recv()
            copy(a, 3, (*xn, 1 - c), me).wait_recv()
            copy(a, 4, (*yn, 1 - c), me).wait_recv()
            for k, part in zip((7, 8), parts(a)):
                copy(a, k, (*dg, 1 - c), me, part=part).wait_recv()
        for cp in started:
            cp.wait_send()
        for cp in mine:
            cp.wait()

    return pl.pallas_call(
        body, name=name,
        in_specs=[HBM_SPEC] * n, out_specs=[HBM_SPEC] * n,
        out_shape=[jax.ShapeDtypeStruct((N_DEV, *s.shape), s.dtype) for s in shards],
        scratch_shapes=[pltpu.SemaphoreType.DMA((per * n,)), pltpu.SemaphoreType.DMA((per * n,)),
                        pltpu.SemaphoreType.DMA((n,))],
        compiler_params=pltpu.CompilerParams(vmem_limit_bytes=VMEM_LIMIT),
    )(*shards)


HBM_SPEC = pl.BlockSpec(memory_space=pltpu.HBM)
SEM_SPEC = pl.BlockSpec(memory_space=pltpu.SEMAPHORE)


CHIP_PEERS = (4, 2, 6)
COPIES_PER_ARRAY = dict(gather=7, exchange=7, gather_near=4, gather_pass=3, reduce_pair=4, reduce_chip=3)


def _planned_copies(kind, srcs, lands, send_sems, recv_sems):
    x, y, c = _position()
    me, my_chip = _index_of((x, y, c)), 2 * x + y
    per = COPIES_PER_ARRAY[kind]
    copies = []

    def peer_of(r):
        return (x ^ ((r >> 2) & 1), y ^ ((r >> 1) & 1), c ^ (r & 1))

    def add(a, j, src, dst, peer):
        copies.append(pltpu.make_async_remote_copy(
            src_ref=src, dst_ref=dst, send_sem=send_sems.at[per * a + j], recv_sem=recv_sems.at[per * a + j],
            device_id=peer, device_id_type=MESH_ID))

    for a, land in enumerate(lands):
        src = srcs[a] if srcs else land
        if kind == "gather":
            for r in range(1, N_DEV):
                add(a, r - 1, src, land.at[me], peer_of(r))
        elif kind == "exchange":
            for r in range(1, N_DEV):
                add(a, r - 1, src.at[_index_of(peer_of(r))], land.at[me], peer_of(r))
        elif kind == "gather_near":
            for j, r in enumerate((1,) + CHIP_PEERS):
                add(a, j, src, land.at[me], peer_of(r))
        elif kind == "gather_pass":
            for j, r in enumerate(CHIP_PEERS):
                slot = _index_of(peer_of(r))
                add(a, j, land.at[slot], land.at[slot], peer_of(1))
        elif kind == "reduce_pair":
            for q in range(4):
                add(a, q, src.at[2 * q + (1 - c)], land.at[q], peer_of(1))
        elif kind == "reduce_chip":
            for j, r in enumerate(CHIP_PEERS):
                p = peer_of(r)
                add(a, j, src.at[2 * p[0] + p[1]], land.at[my_chip], p)
    return copies


def copies_start(kind, srcs, lands, *, name):
    ns, n = len(srcs), len(srcs) + len(lands)
    n_sem = COPIES_PER_ARRAY[kind] * len(lands)

    def body(*refs):
        for cp in _planned_copies(kind, refs[:ns], refs[ns:n], refs[n], refs[n + 1]):
            cp.start()
        refs[-1][...] = jnp.zeros_like(refs[-1])

    arrays = [pltpu.with_memory_space_constraint(t, pltpu.HBM) for t in (*srcs, *lands)]
    outs = pl.pallas_call(
        body, name=name,
        out_shape=(pltpu.SemaphoreType.DMA((n_sem,)), pltpu.SemaphoreType.DMA((n_sem,)),
                   *[pltpu.HBM(t.shape, t.dtype) for t in arrays], jax.ShapeDtypeStruct((8, LANES), F32)),
        in_specs=[HBM_SPEC] * n,
        out_specs=(SEM_SPEC, SEM_SPEC, *[HBM_SPEC] * n, pl.BlockSpec(memory_space=pltpu.VMEM)),
        input_output_aliases={i: 2 + i for i in range(n)},
        compiler_params=pltpu.CompilerParams(has_side_effects=pltpu.SideEffectType.DATAFLOW_SIDE_EFFECTING),
    )(*arrays)
    return (outs[0], outs[1]), list(outs[2:2 + ns]), list(outs[2 + ns:2 + n]), outs[-1]


def copies_wait(kind, sems, srcs, lands, after, *, name):
    ns, n = len(srcs), len(srcs) + len(lands)

    def body(*refs):
        for cp in _planned_copies(kind, refs[:ns], refs[ns:n], refs[n], refs[n + 1]):
            cp.wait_send()
            cp.wait_recv()

    outs = pl.pallas_call(
        body, name=name,
        out_shape=tuple(pltpu.HBM(t.shape, t.dtype) for t in (*srcs, *lands)),
        in_specs=[HBM_SPEC] * n + [SEM_SPEC, SEM_SPEC, pl.BlockSpec(memory_space=pl.ANY)],
        out_specs=tuple([HBM_SPEC] * n),
        input_output_aliases={i: i for i in range(n)},
        compiler_params=pltpu.CompilerParams(has_side_effects=pltpu.SideEffectType.DATAFLOW_SIDE_EFFECTING),
    )(*srcs, *lands, sems[0], sems[1], after)
    return list(outs[ns:])


def _landing(own, slot, slots=N_DEV):
    return lax.dynamic_update_slice_in_dim(lax.empty((slots, *own.shape), own.dtype), own[None], slot, 0)


def pair_sum(a, b, *, name, tr=256):
    p, r, c = a.shape
    tr = min(tr, r)

    def body(a_ref, b_ref, o_ref):
        o_ref[...] = (a_ref[...].astype(F32) + b_ref[...].astype(F32)).astype(o_ref.dtype)

    blk = pl.BlockSpec((1, tr, c), lambda i, j: (i, j, 0))
    return pl.pallas_call(
        body, name=name, grid=(p, r // tr), in_specs=[blk, blk], out_specs=blk,
        out_shape=jax.ShapeDtypeStruct(a.shape, a.dtype), compiler_params=_cparams(("parallel", "parallel")),
    )(a, b)


def adamw(parts, w, m, v, *, name, tr=64, tc=None):
    p, r, c = parts.shape
    tr = r if tc is not None else min(tr, r)
    tc = c if tc is None else tc
    assert r % tr == 0 and c % tc == 0
    c1 = 1.0 / (1.0 - ADAM_B1 ** ADAM_STEP)
    c2 = 1.0 / (1.0 - ADAM_B2 ** ADAM_STEP)

    def body(p_ref, w_ref, m_ref, v_ref, g_ref, d_ref, nm_ref, nv_ref):
        g = p_ref[0].astype(F32)
        for i in range(1, p):
            g = g + p_ref[i].astype(F32)
        nm = ADAM_B1 * m_ref[...] + (1.0 - ADAM_B1) * g
        nv = ADAM_B2 * v_ref[...] + (1.0 - ADAM_B2) * jnp.square(g)
        g_ref[...] = g
        nm_ref[...] = nm
        nv_ref[...] = nv
        d_ref[...] = -ADAM_LR * ((nm * c1) / (jnp.sqrt(nv * c2) + ADAM_EPS) + ADAM_WD * w_ref[...])

    blk = pl.BlockSpec((tr, tc), lambda i, j: (i, j))
    out = jax.ShapeDtypeStruct((r, c), F32)
    return pl.pallas_call(
        body, name=name, grid=(r // tr, c // tc),
        in_specs=[pl.BlockSpec((p, tr, tc), lambda i, j: (0, i, j)), blk, blk, blk],
        out_specs=[blk] * 4, out_shape=[out] * 4,
        compiler_params=_cparams(("parallel", "parallel")),
    )(parts, w, m, v)


def _pack(d, vals):
    heads = d // LANES
    vecs = jnp.zeros((8, LANES), F32)
    vecs = vecs.at[0:1].set(vals["a_norm_w"]).at[1:2, :heads].set(vals["b_a_log"])
    vecs = vecs.at[2:3, :heads].set(vals["b_dt_bias"]).at[3:4].set(vals["b_norm_w"])
    rows = [vals["a_lower_bounds"].reshape(-1, LANES), vals["ln_g"].reshape(-1, LANES),
            vals["ln_b"].reshape(-1, LANES), vecs]
    return jnp.concatenate(rows, axis=0)


def _unpack(d, packed):
    heads = d // LANES
    n3, n2 = 3 * heads, 2 * heads
    o = 0
    out = {}
    out["a_lower_bounds"] = packed[o:o + n3].reshape(3, d); o += n3
    out["ln_g"] = packed[o:o + n2].reshape(2, d); o += n2
    out["ln_b"] = packed[o:o + n2].reshape(2, d); o += n2
    out["a_norm_w"] = packed[o:o + 1]
    out["b_a_log"] = packed[o + 1:o + 2, :heads]
    out["b_dt_bias"] = packed[o + 2:o + 3, :heads]
    out["b_norm_w"] = packed[o + 3:o + 4]
    return out


ORDER = ("a_w_in", "a_lower_bounds", "a_norm_w", "a_w_out", "b_w_in", "b_conv_w", "b_a_log", "b_dt_bias", "b_norm_w",
         "b_w_out", "ln_g", "ln_b")


def kernel(x, a_w_in, a_lower_bounds, a_norm_w, a_w_out, b_w_in, b_conv_w, b_a_log, b_dt_bias, b_norm_w, b_w_out, ln_g, ln_b, loss_target, m_a_w_in, m_a_lower_bounds, m_a_norm_w, m_a_w_out, m_b_w_in, m_b_conv_w, m_b_a_log, m_b_dt_bias, m_b_norm_w, m_b_w_out, m_ln_g, m_ln_b, v_a_w_in, v_a_lower_bounds, v_a_norm_w, v_a_w_out, v_b_w_in, v_b_conv_w, v_b_a_log, v_b_dt_bias, v_b_norm_w, v_b_w_out, v_ln_g, v_ln_b):
    w = dict(a_w_in=a_w_in, a_lower_bounds=a_lower_bounds, a_norm_w=a_norm_w, a_w_out=a_w_out, b_w_in=b_w_in,
             b_conv_w=b_conv_w, b_a_log=b_a_log, b_dt_bias=b_dt_bias, b_norm_w=b_norm_w, b_w_out=b_w_out, ln_g=ln_g, ln_b=ln_b)
    m = dict(a_w_in=m_a_w_in, a_lower_bounds=m_a_lower_bounds, a_norm_w=m_a_norm_w, a_w_out=m_a_w_out, b_w_in=m_b_w_in,
             b_conv_w=m_b_conv_w, b_a_log=m_b_a_log, b_dt_bias=m_b_dt_bias, b_norm_w=m_b_norm_w, b_w_out=m_b_w_out,
             ln_g=m_ln_g, ln_b=m_ln_b)
    v = dict(a_w_in=v_a_w_in, a_lower_bounds=v_a_lower_bounds, a_norm_w=v_a_norm_w, a_w_out=v_a_w_out, b_w_in=v_b_w_in,
             b_conv_w=v_b_conv_w, b_a_log=v_b_a_log, b_dt_bias=v_b_dt_bias, b_norm_w=v_b_norm_w, b_w_out=v_b_w_out,
             ln_g=v_ln_g, ln_b=v_ln_b)
    t, d = x.shape[1], x.shape[2]
    heads = d // LANES
    n_tail = 2 * heads
    me = _index_of(_position())

    ga_in, ga_out, g_conv = all_gather_relay([a_w_in[0].astype(BF16), a_w_out[0].astype(BF16), b_conv_w[0]],
                                             name="gather_weights_a")
    b_shards = [jnp.transpose(b_w_in[0]).astype(BF16), b_w_out[0].astype(BF16)]
    flight = {}
    sems_w, b_shards, lands_w, token_w = copies_start("gather_near", b_shards, [_landing(s, me) for s in b_shards],
                                                      name="gather_weights_b_start")
    wa_out = ga_out.reshape(d, d)
    conv_w = jnp.transpose(g_conv, (1, 0, 2)).reshape(CONV_K, 3 * d)

    def on_mid(after):
        lands = copies_wait("gather_near", sems_w, b_shards, lands_w, after, name="gather_weights_b_wait")
        sems, _, lands, token = copies_start("gather_pass", [], lands, name="pass_weights_b_start")
        flight["w"] = (sems, lands)
        return token

    def b_weights(after):
        sems, lands = flight["w"]
        gb_in, gb_out = copies_wait("gather_pass", sems, [], lands, after, name="pass_weights_b_wait")
        wb_t = gb_in.reshape(4 * d + n_tail, d)
        wb_tail_t = jnp.concatenate([wb_t[4 * d:], jnp.zeros((LANES - n_tail, d), BF16)], axis=0)
        return wb_t, wb_tail_t, gb_out.reshape(d, d), conv_w

    def send_grads(key, parts):
        own = [lax.dynamic_index_in_dim(p, me, 0, keepdims=False) for p in parts]
        sems, parts, lands, token = copies_start("exchange", parts, [_landing(o, me) for o in own],
                                                 name="exchange_" + key + "_start")
        flight[key] = (sems, parts, lands)
        return token

    def on_b_grads(d_wb_t, d_out):
        return send_grads("b", [d_wb_t.reshape(N_DEV, -1, d), d_out.reshape(N_DEV, d // N_DEV, d)])

    res = {}

    def on_small_grads(small):
        conv_rows = small["b_conv_w"].reshape(-1, LANES)
        n_conv = conv_rows.shape[0]
        packed_grads = jnp.concatenate([conv_rows, _pack(d, small)], axis=0)
        (got,) = all_gather([packed_grads], name="gather_small", space=pltpu.VMEM)
        packed = adamw(got[:, n_conv:], _pack(d, w), _pack(d, m), _pack(d, v), name="adamw_small", tr=4096)
        for k, vals in zip(("grad", "delta", "new_m", "new_v"), packed):
            res[k] = _unpack(d, vals)
        shard_ch = 3 * d // N_DEV
        conv_parts = lax.dynamic_slice_in_dim(got[:, :n_conv].reshape(N_DEV, CONV_K, 3 * d), me * shard_ch, shard_ch,
                                              axis=2)
        conv_out = adamw(conv_parts, b_conv_w[0], m_b_conv_w[0], v_b_conv_w[0], name="adamw_conv")
        for k, vals in zip(("grad", "delta", "new_m", "new_v"), conv_out):
            res[k]["b_conv_w"] = vals[None]
        return conv_out[0]

    my_chip, my_core = me // 2, me % 2

    def on_a_grads(d_in):
        mine = lax.dynamic_index_in_dim(d_in.reshape(N_DEV // 2, 2, *d_in.shape[1:]), my_core, 1, keepdims=False)
        land1 = lax.empty(mine.shape, mine.dtype)
        sems, (d_in,), (land1,), token = copies_start("reduce_pair", [d_in], [land1], name="reduce_a_pair_start")
        (land1,) = copies_wait("reduce_pair", sems, [d_in], [land1], token, name="reduce_a_pair_wait")
        summed = pair_sum(mine, land1, name="reduce_a_pair_sum")
        own = lax.dynamic_index_in_dim(summed, my_chip, 0, keepdims=False)
        sems, (summed,), (land2,), token = copies_start("reduce_chip", [summed], [_landing(own, my_chip, N_DEV // 2)],
                                                        name="reduce_a_chip_start")
        flight["a"] = (sems, [summed], [land2])
        return token

    loss, grad_x, big, small = local_step(
        x[0], loss_target[0], ga_in, a_lower_bounds, a_norm_w, wa_out, None, None, None, b_a_log, b_dt_bias,
        b_norm_w, None, ln_g, ln_b, first_after=token_w, on_mid=on_mid, b_weights=b_weights, on_b_grads=on_b_grads,
        on_a_out_grad=lambda g: send_grads("c", [g.reshape(N_DEV, d // N_DEV, d)]),
        on_small_grads=on_small_grads, on_a_grads=on_a_grads)
    loss = lax.psum(loss[0, 0], ("x", "y", "c"))

    after = grad_x
    for key, kind, names in (("b", "exchange", ("b_w_in", "b_w_out")), ("c", "exchange", ("a_w_out",)),
                             ("a", "reduce_chip", ("a_w_in",))):
        sems, parts, lands = flight[key]
        recv = copies_wait(kind, sems, parts, lands, after, name="exchange_" + key + "_wait")
        for name, got_parts in zip(names, recv):
            if name == "b_w_in":
                raw = adamw(got_parts, *[jnp.transpose(t[name][0]) for t in (w, m, v)], name="adamw_" + name, tc=256)
                outs = [jnp.transpose(o) for o in raw]
            else:
                raw = outs = adamw(got_parts, w[name][0], m[name][0], v[name][0], name="adamw_" + name)
            for k, vals in zip(("grad", "delta", "new_m", "new_v"), outs):
                res[k][name] = vals[None]
            after = raw[0]

    return (loss, grad_x[None], *[res["grad"][k] for k in ORDER], *[res["delta"][k] for k in ORDER],
            *[res["new_m"][k] for k in ORDER], *[res["new_v"][k] for k in ORDER])
```
